```python
import math
import jax, jax.numpy as jnp
from jax import lax
import numpy as np

D_MODEL = 1024
BATCH = 2
SEQ = 16384
DEPTH = 2
DEC_BATCH = 8
DEC_SEQ = 8192
PAST_LEN = 128

N_META = 16
D_MIX = D_MODEL
S5_P = 16
S5_WIDTH = 256
S5_GROUPS = S5_WIDTH // S5_P
S5_N = 64
RET_HEADS = 6
RET_DIM = 64
RET_WIDTH = RET_HEADS * RET_DIM
GLA_HEADS = 4
GLA_DV = 96
GLA_DK = GLA_DV // 2
GLA_QK = GLA_HEADS * GLA_DK
GLA_WIDTH = GLA_HEADS * GLA_DV
GLA_GATE_RANK = 16
GLA_TAU = 16.0
CHUNK = 64
META_PAD = CHUNK - N_META
ROPE_BASE = 10000.0
D_FF = 2816
N_EXPERTS = 8
TOP_K = 2
N_DENSE = (DEPTH + 1) // 2
N_MOE = DEPTH // 2
NORM_EPS = 1e-5
PROJ_WIDTHS = (S5_WIDTH, RET_WIDTH, RET_WIDTH, RET_WIDTH, RET_WIDTH, GLA_QK, GLA_QK, GLA_WIDTH, GLA_WIDTH, GLA_GATE_RANK, GLA_GATE_RANK)
D_PROJ = sum(PROJ_WIDTHS)

kernel_name = "hymba_s5_retention_gla_encoder"

F32 = jnp.float32


def _rmsnorm(x, g):
    xf = x.astype(F32)
    y = xf * lax.rsqrt(jnp.mean(xf * xf, axis=-1, keepdims=True) + NORM_EPS) * g.astype(F32)
    return y.astype(x.dtype)


def _head_rms(x):
    return x * lax.rsqrt(jnp.mean(x * x, axis=-1, keepdims=True) + NORM_EPS)


def _rotary(x, pos):
    half = x.shape[-1] // 2
    inv = ROPE_BASE ** (-jnp.arange(half, dtype=F32) / half)
    ang = pos[:, None] * inv[None, :]
    cos = jnp.cos(ang)[None, :, None, :]
    sin = jnp.sin(ang)[None, :, None, :]
    x1, x2 = x[..., :half], x[..., half:]
    return jnp.concatenate([x1 * cos - x2 * sin, x1 * sin + x2 * cos], axis=-1)


def _pad_front(x):
    pads = ((0, 0), (META_PAD, 0)) + ((0, 0),) * (x.ndim - 2)
    return jnp.pad(x, pads)


def _complex_affine_combine(e1, e2):
    a1r, a1i, b1r, b1i = e1
    a2r, a2i, b2r, b2i = e2
    return (a2r * a1r - a2i * a1i,
            a2r * a1i + a2i * a1r,
            a2r * b1r - a2i * b1i + b2r,
            a2r * b1i + a2i * b1r + b2i)


def _s5_mixer(u, lam_re, lam_im, log_dt, b_re, b_im, c_re, c_im, d, w_glu):
    bsz, L, _ = u.shape
    uf = u.astype(F32).reshape(bsz, L, S5_GROUPS, S5_P)
    dt = jnp.exp(log_dt.astype(F32))[..., None]
    lr = lam_re.astype(F32)
    li = lam_im.astype(F32)
    mag = jnp.exp(lr * dt)
    a_re = mag * jnp.cos(li * dt)
    a_im = mag * jnp.sin(li * dt)
    den = lr * lr + li * li
    nr = a_re - 1.0
    ni = a_im
    coef_re = (nr * lr + ni * li) / den
    coef_im = (ni * lr - nr * li) / den
    br = b_re.astype(F32)
    bi = b_im.astype(F32)
    bb_re = coef_re[..., None] * br - coef_im[..., None] * bi
    bb_im = coef_re[..., None] * bi + coef_im[..., None] * br
    y = d.astype(F32) * uf
    for dr in range(2):
        bu_re = jnp.einsum('blgp,gnp->blgn', uf, bb_re[dr])
        bu_im = jnp.einsum('blgp,gnp->blgn', uf, bb_im[dr])
        ar = jnp.broadcast_to(a_re[dr], bu_re.shape)
        ai = jnp.broadcast_to(a_im[dr], bu_re.shape)
        _, _, h_re, h_im = lax.associative_scan(_complex_affine_combine, (ar, ai, bu_re, bu_im), axis=1, reverse=(dr == 1))
        y = y + jnp.einsum('blgn,gpn->blgp', h_re, c_re[dr].astype(F32)) - jnp.einsum('blgn,gpn->blgp', h_im, c_im[dr].astype(F32))
    y = jax.nn.gelu(y.reshape(bsz, L, S5_WIDTH))
    out = y * jax.nn.sigmoid(y @ w_glu.astype(F32))
    return out.astype(u.dtype)


def _chunk_scan(kv, dec, reverse):
    def step(s, inp):
        kv_c, d_c = inp
        return d_c * s + kv_c, s
    _, states = lax.scan(step, jnp.zeros_like(kv[0]), (kv, dec), reverse=reverse)
    return states


def _bidir_chunked_linear_attention(q, k, v, log_f, log_b):
    bsz, L, H, _ = q.shape
    dv = v.shape[-1]
    nc = L // CHUNK

    def chunks(t):
        return t.reshape(bsz, nc, CHUNK, H, t.shape[-1])

    q, k, v, log_f, log_b = (chunks(t) for t in (q, k, v, log_f, log_b))
    mid = CHUNK // 2
    idx = jnp.arange(CHUNK)
    lower = idx[:, None] >= idx[None, :]
    upper = idx[:, None] <= idx[None, :]
    cf = jnp.cumsum(log_f, axis=2)
    tf = cf[:, :, -1:]
    mf = cf[:, :, mid:mid + 1]
    sf = jnp.einsum('bncht,bnsht->bnhcs', q * jnp.exp(cf - mf), k * jnp.exp(mf - cf))
    cb = jnp.cumsum(log_b, axis=2) - log_b
    tb = cb[:, :, -1:] + log_b[:, :, -1:]
    mb = cb[:, :, mid:mid + 1]
    sb = jnp.einsum('bncht,bnsht->bnhcs', q * jnp.exp(mb - cb), k * jnp.exp(cb - mb))
    scores = jnp.where(lower, sf, 0.0) + jnp.where(upper, sb, 0.0)
    o = jnp.einsum('bnhcs,bnshv->bnchv', scores, v)
    kv_f = jnp.einsum('bnsht,bnshv->nbhtv', k * jnp.exp(tf - cf), v)
    kv_b = jnp.einsum('bnsht,bnshv->nbhtv', k * jnp.exp(cb), v)
    dec_f = jnp.exp(tf[:, :, 0]).transpose(1, 0, 2, 3)[..., None]
    dec_b = jnp.exp(tb[:, :, 0]).transpose(1, 0, 2, 3)[..., None]
    st_f = _chunk_scan(kv_f, dec_f, False)
    st_b = _chunk_scan(kv_b, dec_b, True)
    o = o + jnp.einsum('bncht,nbhtv->bnchv', q * jnp.exp(cf), st_f)
    o = o + jnp.einsum('bncht,nbhtv->bnchv', q * jnp.exp(tb - cb), st_b)
    return o.reshape(bsz, L, H, dv)


def _retention(q, k, v, g, pos):
    bsz, L, _ = q.shape
    shp = (bsz, L, RET_HEADS, RET_DIM)
    qf = _rotary(q.astype(F32).reshape(shp), pos)
    kf = _rotary(k.astype(F32).reshape(shp), pos) * (RET_DIM ** -0.5)
    vf = v.astype(F32).reshape(shp)
    log_gamma = jnp.log1p(-jnp.exp2(-5.0 - jnp.arange(RET_HEADS, dtype=F32)))
    ld = jnp.broadcast_to(log_gamma[:, None], (bsz, L + META_PAD, RET_HEADS, 1))
    o = _bidir_chunked_linear_attention(_pad_front(qf), _pad_front(kf), _pad_front(vf), ld, ld)[:, META_PAD:]
    o = o - jnp.sum(qf * kf, axis=-1, keepdims=True) * vf
    o = _head_rms(o) * jax.nn.silu(g.astype(F32).reshape(shp))
    return o.reshape(bsz, L, RET_WIDTH).astype(q.dtype)


def _gla(q, k, v, g, lr_f, lr_b, w_f, b_f, w_b, b_b):
    bsz, L, _ = q.shape
    kshp = (bsz, L, GLA_HEADS, GLA_DK)
    vshp = (bsz, L, GLA_HEADS, GLA_DV)
    qf = q.astype(F32).reshape(kshp) * (GLA_DK ** -0.5)
    kf = k.astype(F32).reshape(kshp)
    vf = v.astype(F32).reshape(vshp)
    log_f = (jax.nn.log_sigmoid(lr_f.astype(F32) @ w_f.astype(F32) + b_f.astype(F32)) / GLA_TAU).reshape(kshp)
    log_b = (jax.nn.log_sigmoid(lr_b.astype(F32) @ w_b.astype(F32) + b_b.astype(F32)) / GLA_TAU).reshape(kshp)
    o = _bidir_chunked_linear_attention(_pad_front(qf), _pad_front(kf), _pad_front(vf), _pad_front(log_f), _pad_front(log_b))[:, META_PAD:]
    o = _head_rms(o) * jax.nn.silu(g.astype(F32).reshape(vshp))
    return o.reshape(bsz, L, GLA_WIDTH).astype(q.dtype)


def _swiglu(x, wg, wu, wd):
    return (jax.nn.silu(x @ wg) * (x @ wu)) @ wd


def _moe(x, router_w, wg, wu, wd):
    logits = (x @ router_w).astype(F32)
    top_v, top_i = lax.top_k(logits, TOP_K)
    w = jax.nn.softmax(top_v, axis=-1)
    gates = jnp.sum(jax.nn.one_hot(top_i, N_EXPERTS, dtype=F32) * w[..., None], axis=-2)
    out = jnp.zeros_like(x)
    for e in range(N_EXPERTS):
        out = out + gates[..., e:e + 1].astype(x.dtype) * _swiglu(x, wg[e], wu[e], wd[e])
    return out


def _trunk(x, meta_tokens, norm_mix, w_in, s5_lambda_re, s5_lambda_im, s5_log_dt, s5_b_re, s5_b_im, s5_c_re, s5_c_im, s5_d, s5_w_glu, gla_w_gate_f, gla_b_gate_f, gla_w_gate_b, gla_b_gate_b, w_out, norm_ffn, ffn_w_gate, ffn_w_up, ffn_w_down, router_w, moe_w_gate, moe_w_up, moe_w_down, norm_final):
    bsz = x.shape[0]
    meta = jnp.broadcast_to(meta_tokens.astype(x.dtype)[None], (bsz, N_META, D_MODEL))
    h = jnp.concatenate([meta, x], axis=1)
    L = h.shape[1]
    pos = jnp.arange(L, dtype=F32)
    offsets = np.cumsum(PROJ_WIDTHS)[:-1].tolist()
    for li in range(DEPTH):
        hn = _rmsnorm(h, norm_mix[li])
        proj = hn @ w_in[li]
        u_s5, rq, rk, rv, rg, gq, gk, gv, gg, glf, glb = jnp.split(proj, offsets, axis=-1)
        y_a = _s5_mixer(u_s5, s5_lambda_re[li], s5_lambda_im[li], s5_log_dt[li], s5_b_re[li], s5_b_im[li], s5_c_re[li], s5_c_im[li], s5_d[li], s5_w_glu[li])
        y_b = _retention(rq, rk, rv, rg, pos)
        y_c = _gla(gq, gk, gv, gg, glf, glb, gla_w_gate_f[li], gla_b_gate_f[li], gla_w_gate_b[li], gla_b_gate_b[li])
        h = h + jnp.concatenate([y_a, y_b, y_c], axis=-1) @ w_out[li]
        hn = _rmsnorm(h, norm_ffn[li])
        if li % 2 == 0:
            j = li // 2
            h = h + _swiglu(hn, ffn_w_gate[j], ffn_w_up[j], ffn_w_down[j])
        else:
            j = li // 2
            h = h + _moe(hn, router_w[j], moe_w_gate[j], moe_w_up[j], moe_w_down[j])
    h = _rmsnorm(h, norm_final)
    return h[:, N_META:]


def setup_inputs(seed: int = 0) -> dict:
    key = jax.random.key(seed)
    ks = jax.random.split(key, 32)

    def nrm(k, shape, scale):
        return scale * jax.random.normal(k, shape, F32)

    G, N, P = S5_GROUPS, S5_N, S5_P
    lam_re = -0.5 * jnp.exp(nrm(ks[5], (DEPTH, 2, G, N), 0.02))
    lam_im = jnp.broadcast_to(jnp.pi * jnp.arange(N, dtype=F32), (DEPTH, 2, G, N))
    log_dt = jax.random.uniform(ks[6], (DEPTH, 2, G), F32, minval=math.log(1e-3), maxval=math.log(1e-1))
    return {
        "x_prompt": nrm(ks[0], (BATCH, SEQ, D_MODEL), 1.0),
        "x_sample": nrm(ks[1], (DEC_BATCH, DEC_SEQ, D_MODEL), 1.0),
        "meta_tokens": nrm(ks[2], (N_META, D_MODEL), 1.0),
        "norm_mix": 1.0 + nrm(ks[3], (DEPTH, D_MODEL), 0.01),
        "w_in": nrm(ks[4], (DEPTH, D_MODEL, D_PROJ), D_MODEL ** -0.5),
        "s5_lambda_re": lam_re,
        "s5_lambda_im": lam_im,
        "s5_log_dt": log_dt,
        "s5_b_re": nrm(ks[7], (DEPTH, 2, G, N, P), (2 * P) ** -0.5),
        "s5_b_im": nrm(ks[8], (DEPTH, 2, G, N, P), (2 * P) ** -0.5),
        "s5_c_re": nrm(ks[9], (DEPTH, 2, G, P, N), (2 * N) ** -0.5),
        "s5_c_im": nrm(ks[10], (DEPTH, 2, G, P, N), (2 * N) ** -0.5),
        "s5_d": nrm(ks[11], (DEPTH, G, P), 1.0),
        "s5_w_glu": nrm(ks[12], (DEPTH, S5_WIDTH, S5_WIDTH), S5_WIDTH ** -0.5),
        "gla_w_gate_f": nrm(ks[13], (DEPTH, GLA_GATE_RANK, GLA_QK), GLA_GATE_RANK ** -0.5),
        "gla_b_gate_f": nrm(ks[14], (DEPTH, GLA_QK), 0.01),
        "gla_w_gate_b": nrm(ks[15], (DEPTH, GLA_GATE_RANK, GLA_QK), GLA_GATE_RANK ** -0.5),
        "gla_b_gate_b": nrm(ks[16], (DEPTH, GLA_QK), 0.01),
        "w_out": nrm(ks[17], (DEPTH, D_MIX, D_MODEL), D_MIX ** -0.5),
        "norm_ffn": 1.0 + nrm(ks[18], (DEPTH, D_MODEL), 0.01),
        "ffn_w_gate": nrm(ks[19], (N_DENSE, D_MODEL, D_FF), D_MODEL ** -0.5),
        "ffn_w_up": nrm(ks[20], (N_DENSE, D_MODEL, D_FF), D_MODEL ** -0.5),
        "ffn_w_down": nrm(ks[21], (N_DENSE, D_FF, D_MODEL), D_FF ** -0.5),
        "router_w": nrm(ks[22], (N_MOE, D_MODEL, N_EXPERTS), D_MODEL ** -0.5),
        "moe_w_gate": nrm(ks[23], (N_MOE, N_EXPERTS, D_MODEL, D_FF), D_MODEL ** -0.5),
        "moe_w_up": nrm(ks[24], (N_MOE, N_EXPERTS, D_MODEL, D_FF), D_MODEL ** -0.5),
        "moe_w_down": nrm(ks[25], (N_MOE, N_EXPERTS, D_FF, D_MODEL), D_FF ** -0.5),
        "norm_final": 1.0 + nrm(ks[26], (D_MODEL,), 0.01),
    }


def reference(x_prompt, x_sample, meta_tokens, norm_mix, w_in, s5_lambda_re, s5_lambda_im, s5_log_dt, s5_b_re, s5_b_im, s5_c_re, s5_c_im, s5_d, s5_w_glu, gla_w_gate_f, gla_b_gate_f, gla_w_gate_b, gla_b_gate_b, w_out, norm_ffn, ffn_w_gate, ffn_w_up, ffn_w_down, router_w, moe_w_gate, moe_w_up, moe_w_down, norm_final):
    y_prompt = _trunk(x_prompt, meta_tokens, norm_mix, w_in, s5_lambda_re, s5_lambda_im, s5_log_dt, s5_b_re, s5_b_im, s5_c_re, s5_c_im, s5_d, s5_w_glu, gla_w_gate_f, gla_b_gate_f, gla_w_gate_b, gla_b_gate_b, w_out, norm_ffn, ffn_w_gate, ffn_w_up, ffn_w_down, router_w, moe_w_gate, moe_w_up, moe_w_down, norm_final)
    y_sample = _trunk(x_sample, meta_tokens, norm_mix, w_in, s5_lambda_re, s5_lambda_im, s5_log_dt, s5_b_re, s5_b_im, s5_c_re, s5_c_im, s5_d, s5_w_glu, gla_w_gate_f, gla_b_gate_f, gla_w_gate_b, gla_b_gate_b, w_out, norm_ffn, ffn_w_gate, ffn_w_up, ffn_w_down, router_w, moe_w_gate, moe_w_up, moe_w_down, norm_final)
    return (y_prompt, y_sample)
```

```python
import functools
import math

import jax
import jax.numpy as jnp
from jax import lax
from jax.experimental import pallas as pl
from jax.experimental.pallas import tpu as pltpu

F32 = jnp.float32
BF16 = jnp.bfloat16

D_MODEL = 1024
N_META = 16
S5_P = 16
S5_WIDTH = 256
S5_GROUPS = 16
S5_N = 64
RET_HEADS = 6
RET_DIM = 64
RET_WIDTH = 384
GLA_HEADS = 4
GLA_DV = 96
GLA_DK = 48
GLA_QK = 192
GLA_WIDTH = 384
GLA_GATE_RANK = 16
GLA_TAU = 16.0
ROPE_BASE = 10000.0
D_FF = 2816
N_EXPERTS = 8
NORM_EPS = 1e-5

CHUNK = 64
FRONT = 256
PAD = FRONT - N_META
ROW_TILE = 512
LA_GROUP = 4
LANES = 128
GLA_DK_PAD = 64
GLA_DV_PAD = 128
VMEM_LIMIT = 56 * 1024 * 1024


def _params(*sem):
    return pltpu.CompilerParams(dimension_semantics=sem, vmem_limit_bytes=VMEM_LIMIT)


def _const_spec(shape):
    nd = len(shape)
    return pl.BlockSpec(shape, lambda *_: (0,) * nd)


def _sigmoid(x):
    return 1.0 / (1.0 + jnp.exp(-x))


def _silu(x):
    return x * _sigmoid(x)


def _gelu_tanh(x):
    c = math.sqrt(2.0 / math.pi)
    return 0.5 * x * (1.0 + jnp.tanh(c * (x + 0.044715 * (x * x * x))))


def _log_sigmoid(z):
    return jnp.minimum(z, 0.0) - jnp.log(1.0 + jnp.exp(-jnp.abs(z)))


def _split_dot(a, b_bf16, dims=None):
    hi = a.astype(BF16)
    lo = (a - hi.astype(F32)).astype(BF16)
    if dims is None:
        return (jnp.dot(hi, b_bf16, preferred_element_type=F32)
                + jnp.dot(lo, b_bf16, preferred_element_type=F32))
    return (lax.dot_general(hi, b_bf16, dims, preferred_element_type=F32)
            + lax.dot_general(lo, b_bf16, dims, preferred_element_type=F32))


_C_U = 0
_C_RQ = 256
_C_RQR = 640
_C_RK = 1024
_C_RKR = 1408
_C_RV = 1792
_C_RG = 2176
_C_GQ = 2560
_C_GK = 2816
_C_GV = 3072
_C_GG = 3584
_C_GL = 4096
_C_END = 4224


def _pack_w_in(w):
    o = 0
    u = w[:, o:o + 256]; o += 256
    rq = w[:, o:o + 384]; o += 384
    rk = w[:, o:o + 384]; o += 384
    rv = w[:, o:o + 384]; o += 384
    rg = w[:, o:o + 384]; o += 384
    gq = w[:, o:o + 192]; o += 192
    gk = w[:, o:o + 192]; o += 192
    gv = w[:, o:o + 384]; o += 384
    gg = w[:, o:o + 384]; o += 384
    glf = w[:, o:o + 16]; o += 16
    glb = w[:, o:o + 16]; o += 16

    def rot(m):
        m = m.reshape(D_MODEL, RET_HEADS, 2, RET_DIM // 2)
        return jnp.stack([-m[:, :, 1], m[:, :, 0]], axis=2).reshape(D_MODEL, RET_WIDTH)

    def padh(m, d, dp):
        m = m.reshape(D_MODEL, GLA_HEADS, d)
        return jnp.pad(m, ((0, 0), (0, 0), (0, dp - d))).reshape(D_MODEL, GLA_HEADS * dp)

    gl = jnp.pad(jnp.concatenate([glf, glb], axis=1), ((0, 0), (0, LANES - 2 * GLA_GATE_RANK)))
    cat = jnp.concatenate([
        u, rq, rot(rq), rk, rot(rk), rv, rg,
        padh(gq, GLA_DK, GLA_DK_PAD), padh(gk, GLA_DK, GLA_DK_PAD),
        padh(gv, GLA_DV, GLA_DV_PAD), padh(gg, GLA_DV, GLA_DV_PAD), gl], axis=1)
    return cat.astype(BF16)


def _pack_gate(w_f, b_f, w_b, b_b):
    def padh(m):
        m = m.reshape(m.shape[0], GLA_HEADS, GLA_DK)
        return jnp.pad(m, ((0, 0), (0, 0), (0, GLA_DK_PAD - GLA_DK))).reshape(m.shape[0], GLA_HEADS * GLA_DK_PAD)
    r = GLA_GATE_RANK
    w = jnp.zeros((LANES, 2 * GLA_HEADS * GLA_DK_PAD), F32)
    w = w.at[0:r, 0:256].set(padh(w_f.astype(F32)))
    w = w.at[r:2 * r, 256:512].set(padh(w_b.astype(F32)))
    b = jnp.concatenate([padh(b_f.astype(F32)[None]), padh(b_b.astype(F32)[None])], axis=1)
    return w.astype(BF16), b


def _inproj_kernel(h_ref, g_ref, w_ref, cos_ref, sin_ref, wgate_ref, bgate_ref,
                   u_ref, rq_ref, rk_ref, rv_ref, rg_ref, gq_ref, gk_ref, gv_ref, gg_ref, lf_ref, lb_ref,
                   *, batch, lp):
    tm = h_ref.shape[0]
    x = h_ref[...]
    ms = jnp.mean(x * x, axis=-1, keepdims=True)
    row = pl.program_id(0) * tm + lax.broadcasted_iota(jnp.int32, (tm, 1), 0)
    valid = jnp.ones((tm, 1), F32)
    for b in range(batch):
        valid = jnp.where((row >= b * lp) & (row < b * lp + PAD), 0.0, valid)
    hn = (x * (lax.rsqrt(ms + NORM_EPS) * valid) * g_ref[...]).astype(BF16)

    def proj(lo, hi):
        return jnp.dot(hn, w_ref[:, lo:hi], preferred_element_type=F32)

    u_ref[...] = proj(_C_U, _C_RQ).astype(BF16)
    cos = cos_ref[...]
    sin = sin_ref[...]
    rq_ref[...] = (proj(_C_RQ, _C_RQR) * cos + proj(_C_RQR, _C_RK) * sin).astype(BF16)
    rk_ref[...] = ((proj(_C_RK, _C_RKR) * cos + proj(_C_RKR, _C_RV) * sin) * (RET_DIM ** -0.5)).astype(BF16)
    rv_ref[...] = proj(_C_RV, _C_RG).astype(BF16)
    rg_ref[...] = proj(_C_RG, _C_GQ).astype(BF16)
    gq_ref[...] = (proj(_C_GQ, _C_GK) * (GLA_DK ** -0.5)).astype(BF16)
    gk_ref[...] = proj(_C_GK, _C_GV).astype(BF16)
    gv_ref[...] = proj(_C_GV, _C_GG).astype(BF16)
    gg_ref[...] = proj(_C_GG, _C_GL).astype(BF16)
    codes = proj(_C_GL, _C_END).astype(BF16)
    z = jnp.dot(codes, wgate_ref[...], preferred_element_type=F32) + bgate_ref[...]
    ls = _log_sigmoid(z) * (1.0 / GLA_TAU)
    lf_ref[...] = ls[:, 0:256]
    lb_ref[...] = ls[:, 256:512]


def _inproj(h, gamma, w, cos, sin, wgate, bgate, batch, lp):
    r = h.shape[0]
    tm = ROW_TILE
    widths = (256, 384, 384, 384, 384, 256, 256, 512, 512, 256, 256)
    dtypes = (BF16,) * 9 + (F32, F32)

    def rows(wd):
        return pl.BlockSpec((tm, wd), lambda i: (i, 0))

    return pl.pallas_call(
        functools.partial(_inproj_kernel, batch=batch, lp=lp),
        grid=(r // tm,),
        in_specs=[rows(D_MODEL), _const_spec((1, D_MODEL)), _const_spec((D_MODEL, _C_END)),
                  rows(RET_WIDTH), rows(RET_WIDTH), _const_spec((LANES, 512)), _const_spec((1, 512))],
        out_specs=[rows(wd) for wd in widths],
        out_shape=[jax.ShapeDtypeStruct((r, wd), dt) for wd, dt in zip(widths, dtypes)],
        compiler_params=_params("parallel"),
        name="inproj",
    )(h, gamma, w, cos, sin, wgate, bgate)


def _s5_tables(lam_re, lam_im, log_dt, b_re, b_im, c_re, c_im, d):
    hp = lax.Precision.HIGHEST
    c = CHUNK
    g_, n_, p_ = S5_GROUPS, S5_N, S5_P
    dt = jnp.exp(log_dt.astype(F32))[..., None]
    lr = lam_re.astype(F32)
    li = lam_im.astype(F32)
    e = lr * dt
    th = li * dt
    mag = jnp.exp(e)
    a_re = mag * jnp.cos(th)
    a_im = mag * jnp.sin(th)
    den = lr * lr + li * li
    nr = a_re - 1.0
    ni = a_im
    coef_re = (nr * lr + ni * li) / den
    coef_im = (ni * lr - nr * li) / den
    br = b_re.astype(F32)
    bi = b_im.astype(F32)
    bb_re = coef_re[..., None] * br - coef_im[..., None] * bi
    bb_im = coef_re[..., None] * bi + coef_im[..., None] * br
    cr = c_re.astype(F32)
    ci = c_im.astype(F32)
    tau = jnp.arange(c + 1, dtype=F32)[:, None, None, None]
    pw_mag = jnp.exp(tau * e[None])
    pw_re = pw_mag * jnp.cos(tau * th[None])
    pw_im = pw_mag * jnp.sin(tau * th[None])

    z_re = cr[None] * pw_re[:, :, :, None, :] - ci[None] * pw_im[:, :, :, None, :]
    z_im = cr[None] * pw_im[:, :, :, None, :] + ci[None] * pw_re[:, :, :, None, :]
    kern = (jnp.einsum('tdgqn,dgnp->tdgqp', z_re[:c], bb_re, precision=hp)
            - jnp.einsum('tdgqn,dgnp->tdgqp', z_im[:c], bb_im, precision=hp))
    kf = kern[:, 0]
    kb = kern[:, 1]
    eye = jnp.eye(p_, dtype=F32)
    kf = kf.at[0].add(d.astype(F32)[:, :, None] * eye[None])
    idx = jnp.arange(c)
    diff = idx[None, :] - idx[:, None]
    tf = jnp.where((diff >= 0)[:, :, None, None, None], kf[jnp.clip(diff, 0, c - 1)], 0.0)
    tb = jnp.where((diff <= 0)[:, :, None, None, None], kb[jnp.clip(-diff, 0, c - 1)], 0.0)
    tt = jnp.transpose(tf + tb, (2, 0, 4, 1, 3)).reshape(g_, c * p_, c * p_).astype(BF16)

    pf_re = pw_re[c - 1 - idx, 0]
    pf_im = pw_im[c - 1 - idx, 0]
    pb_re = pw_re[idx, 1]
    pb_im = pw_im[idx, 1]

    def m_of(p_re_, p_im_, dr):
        m_re = p_re_[:, :, :, None] * bb_re[dr][None] - p_im_[:, :, :, None] * bb_im[dr][None]
        m_im = p_re_[:, :, :, None] * bb_im[dr][None] + p_im_[:, :, :, None] * bb_re[dr][None]
        to = lambda m: jnp.transpose(m, (1, 0, 3, 2)).reshape(g_, c * p_, n_)
        return to(m_re), to(m_im)

    mf_re, mf_im = m_of(pf_re, pf_im, 0)
    mb_re, mb_im = m_of(pb_re, pb_im, 1)
    m4 = jnp.stack([mf_re, mf_im, mb_re, mb_im], axis=1)
    m4 = m4.reshape(g_ // 2, 2, 4, c * p_, n_)
    mpair = jnp.zeros((g_ // 2, 2, c * p_, 4, 2, n_), F32)
    for gi in range(2):
        mpair = mpair.at[:, gi, :, :, gi, :].set(jnp.transpose(m4[:, gi], (0, 2, 1, 3)))
    mpair = mpair.reshape(g_ // 2, 2 * c * p_, 4 * 2 * n_).astype(BF16)

    def n_of(tsel, dr):
        w_re = z_re[tsel, dr]
        w_im = z_im[tsel, dr]
        to = lambda m: jnp.transpose(m, (1, 3, 0, 2)).reshape(g_, n_, c * p_)
        return to(w_re), to(-w_im)

    nf_re, nf_im = n_of(idx + 1, 0)
    nb_re, nb_im = n_of(c - idx, 1)
    n4 = jnp.stack([nf_re, nf_im, nb_re, nb_im], axis=1)
    n4 = n4.reshape(g_ // 2, 2, 4, n_, c * p_)
    npair = jnp.zeros((g_ // 2, 4, 2, n_, 2, c * p_), F32)
    for gi in range(2):
        npair = npair.at[:, :, gi, :, gi, :].set(n4[:, gi])
    npair = npair.reshape(g_ // 2, 4 * 2 * n_, 2 * c * p_).astype(BF16)

    dec = jnp.stack([pw_re[c, 0], pw_im[c, 0], pw_re[c, 1], pw_im[c, 1]], axis=0)
    dec = dec.reshape(4, 1, g_ * n_)
    return tt, mpair, npair, dec


def _s5_state_kernel(u_ref, m_ref, o0, o1, o2, o3):
    s = jnp.dot(u_ref[...], m_ref[...], preferred_element_type=F32)
    o0[...] = s[:, 0:128]
    o1[...] = s[:, 128:256]
    o2[...] = s[:, 256:384]
    o3[...] = s[:, 384:512]


def _s5_state(u, mpair):
    ncb = u.shape[0]
    return pl.pallas_call(
        _s5_state_kernel,
        grid=(S5_GROUPS // 2,),
        in_specs=[pl.BlockSpec((ncb, 2048), lambda j: (0, j)),
                  pl.BlockSpec((None, 2048, 512), lambda j: (j, 0, 0))],
        out_specs=[pl.BlockSpec((ncb, LANES), lambda j: (0, j))] * 4,
        out_shape=[jax.ShapeDtypeStruct((ncb, 1024), F32)] * 4,
        compiler_params=_params("parallel"),
        name="s5_state",
    )(u, mpair)


def _s5_scan_kernel(sfr, sfi, sbr, sbi, dfr, dfi, dbr, dbi, hfr, hfi, hbr, hbi):
    nc, b, _ = sfr.shape
    a_fr = dfr[...]
    a_fi = dfi[...]
    a_br = dbr[...]
    a_bi = dbi[...]
    zero = jnp.zeros((b, LANES), F32)

    def fwd(c, carry):
        hr, hi = carry
        hfr[c] = hr
        hfi[c] = hi
        return (a_fr * hr - a_fi * hi + sfr[c], a_fr * hi + a_fi * hr + sfi[c])

    def bwd(i, carry):
        c = nc - 1 - i
        hr, hi = carry
        hbr[c] = hr
        hbi[c] = hi
        return (a_br * hr - a_bi * hi + sbr[c], a_br * hi + a_bi * hr + sbi[c])

    lax.fori_loop(0, nc, fwd, (zero, zero))
    lax.fori_loop(0, nc, bwd, (zero, zero))


def _s5_scan(s4, dec, nc, b):
    s4 = [s.reshape(nc, b, 1024) for s in s4]
    blk = pl.BlockSpec((nc, b, LANES), lambda j: (0, 0, j))
    dblk = pl.BlockSpec((None, 1, LANES), lambda j: (0, 0, j))
    dspecs = [pl.BlockSpec((None, 1, LANES), functools.partial(lambda j, k: (k, 0, j), k=k)) for k in range(4)]
    del dblk
    outs = pl.pallas_call(
        _s5_scan_kernel,
        grid=(S5_GROUPS // 2,),
        in_specs=[blk] * 4 + dspecs,
        out_specs=[blk] * 4,
        out_shape=[jax.ShapeDtypeStruct((nc, b, 1024), F32)] * 4,
        compiler_params=_params("parallel"),
        name="s5_scan",
    )(*s4, dec, dec, dec, dec)
    return [o.reshape(nc * b, 1024) for o in outs]


def _s5_out_kernel(u_ref, tt_ref, h0, h1, h2, h3, n_ref, y_ref):
    u = u_ref[...]
    y0 = jnp.dot(u[:, 0:1024], tt_ref[0], preferred_element_type=F32)
    y1 = jnp.dot(u[:, 1024:2048], tt_ref[1], preferred_element_type=F32)
    hcat = jnp.concatenate([h0[...], h1[...], h2[...], h3[...]], axis=1).astype(BF16)
    yh = jnp.dot(hcat, n_ref[...], preferred_element_type=F32)
    y_ref[...] = (jnp.concatenate([y0, y1], axis=1) + yh).astype(BF16)


def _s5_out(u, tt, h4, npair):
    ncb = u.shape[0]
    rt = ncb if ncb <= 640 else ncb // 2
    hblk = pl.BlockSpec((rt, LANES), lambda j, i: (i, j))
    return pl.pallas_call(
        _s5_out_kernel,
        grid=(S5_GROUPS // 2, ncb // rt),
        in_specs=[pl.BlockSpec((rt, 2048), lambda j, i: (i, j)),
                  pl.BlockSpec((2, 1024, 1024), lambda j, i: (j, 0, 0)),
                  hblk, hblk, hblk, hblk,
                  pl.BlockSpec((None, 512, 2048), lambda j, i: (j, 0, 0))],
        out_specs=pl.BlockSpec((rt, 2048), lambda j, i: (i, j)),
        out_shape=jax.ShapeDtypeStruct((ncb, 16 * 1024), BF16),
        compiler_params=_params("parallel", "parallel"),
        name="s5_out",
    )(u, tt, *h4, npair)


def _s5_mixer(u, tables, batch, lp):
    tt, mpair, npair, dec = tables
    nc = lp // CHUNK
    g_, p_ = S5_GROUPS, S5_P
    uc = u.reshape(batch, nc, CHUNK, g_, p_)
    uc = jnp.transpose(uc, (1, 0, 3, 2, 4)).reshape(nc * batch, g_ * CHUNK * p_)
    s4 = _s5_state(uc, mpair)
    h4 = _s5_scan(s4, dec, nc, batch)
    y = _s5_out(uc, tt, h4, npair)
    y = y.reshape(nc, batch, g_, CHUNK, p_)
    return jnp.transpose(y, (1, 0, 3, 2, 4)).reshape(batch * lp, g_ * p_)


def _cumsum_rows(x):
    c = x.shape[0]
    r = lax.broadcasted_iota(jnp.int32, (c, c), 0)
    s = lax.broadcasted_iota(jnp.int32, (c, c), 1)
    tri = jnp.where(s <= r, 1.0, 0.0).astype(BF16)
    hi = x.astype(BF16)
    lo = (x - hi.astype(F32)).astype(BF16)
    return jnp.dot(tri, hi, preferred_element_type=F32) + jnp.dot(tri, lo, preferred_element_type=F32)


def _head_masks(npair, vp):
    kl = lax.broadcasted_iota(jnp.int32, (1, LANES), 1)
    km = [(kl < 64), (kl >= 64)]
    vl = lax.broadcasted_iota(jnp.int32, (1, vp), 1)
    vm = [(vl < vp // 2), (vl >= vp // 2)]
    vrow = lax.broadcasted_iota(jnp.int32, (vp, LANES), 0)
    kcol = lax.broadcasted_iota(jnp.int32, (vp, LANES), 1)
    bd = (vrow >= vp // 2) == (kcol >= 64)
    return km, vm, bd


def _la_bwd_kernel(k_ref, v_ref, lb_ref, sb_ref, st_ref, *, npair, vp, const_decay):
    g = LA_GROUP
    c = CHUNK

    @pl.when(pl.program_id(1) == 0)
    def _():
        st_ref[...] = jnp.zeros_like(st_ref)

    km, vm, bd = _head_masks(npair, vp)
    if const_decay:
        lg = lb_ref[...]
        i = lax.broadcasted_iota(jnp.int32, (c, 1), 0).astype(F32)
        e_cb_const = jnp.exp(i * lg)
        dec_const = jnp.exp(float(c) * lg)
    for gi in reversed(range(g)):
        rows = slice(gi * c, (gi + 1) * c)
        kk = k_ref[rows, :].astype(F32)
        vv = v_ref[rows, :]
        if const_decay:
            e_cb, dec = e_cb_const, dec_const
        else:
            lb = lb_ref[rows, :]
            cbi = _cumsum_rows(lb)
            e_cb = jnp.exp(cbi - lb)
            dec = jnp.exp(cbi[c - 1:c, :])
        kb2 = (kk * e_cb).astype(BF16)
        for p in range(npair):
            st = st_ref[p]
            sb_ref[gi, p] = st.astype(BF16)
            upd = lax.dot_general(vv[:, p * vp:(p + 1) * vp], kb2[:, p * LANES:(p + 1) * LANES],
                                  (((0,), (0,)), ((), ())), preferred_element_type=F32)
            st_ref[p] = st * dec[:, p * LANES:(p + 1) * LANES] + jnp.where(bd, upd, 0.0)


def _la_fwd_kernel(q_ref, k_ref, v_ref, lf_ref, lb_ref, sb_ref, gate_ref, bdv_ref, o_ref, st_ref,
                   *, npair, vp, const_decay, strict_upper, dv):
    g = LA_GROUP
    c = CHUNK

    @pl.when(pl.program_id(1) == 0)
    def _():
        st_ref[...] = jnp.zeros_like(st_ref)

    km, vm, bd = _head_masks(npair, vp)
    ri = lax.broadcasted_iota(jnp.int32, (c, LANES), 0)
    cj = lax.broadcasted_iota(jnp.int32, (c, LANES), 1) & (c - 1)
    lower = ri >= cj
    upper = (ri < cj) if strict_upper else (ri <= cj)
    mid = c // 2

    def terms(lf, lb, cfi, cbi):
        cf = cfi
        cb = cbi - lb
        mf = cf[mid:mid + 1, :]
        mb = cb[mid:mid + 1, :]
        tf = cf[c - 1:c, :]
        tb = cbi[c - 1:c, :]
        return dict(qf=jnp.exp(cf - mf), kf=jnp.exp(mf - cf), qb=jnp.exp(mb - cb), kb=jnp.exp(cb - mb),
                    qf2=jnp.exp(cf), qb2=jnp.exp(tb - cb), kf2=jnp.exp(tf - cf), dec=jnp.exp(tf))

    if const_decay:
        lg = lf_ref[...]
        i1 = lax.broadcasted_iota(jnp.int32, (c, 1), 0).astype(F32) + 1.0
        cfi_c = i1 * lg
        t_const = terms(lg, lg, cfi_c, cfi_c)

    outs = []
    for gi in range(g):
        rows = slice(gi * c, (gi + 1) * c)
        qq = q_ref[rows, :].astype(F32)
        kk = k_ref[rows, :].astype(F32)
        vv = v_ref[rows, :]
        if const_decay:
            t = t_const
        else:
            lf = lf_ref[rows, :]
            lb = lb_ref[rows, :]
            w = lf.shape[1]
            cs = _cumsum_rows(jnp.concatenate([lf, lb], axis=1))
            t = terms(lf, lb, cs[:, :w], cs[:, w:])
        qf = (qq * t["qf"]).astype(BF16)
        kf = kk * t["kf"]
        qb = (qq * t["qb"]).astype(BF16)
        kb = kk * t["kb"]
        q2 = jnp.concatenate([qq * t["qf2"], qq * t["qb2"]], axis=0).astype(BF16)
        kf2 = (kk * t["kf2"]).astype(BF16)
        pair_out = []
        for p in range(npair):
            ks = slice(p * LANES, (p + 1) * LANES)
            vsl = slice(p * vp, (p + 1) * vp)
            kfp = kf[:, ks]
            kbp = kb[:, ks]
            k2f = jnp.concatenate([jnp.where(km[0], kfp, 0.0), jnp.where(km[1], kfp, 0.0)], axis=0).astype(BF16)
            k2b = jnp.concatenate([jnp.where(km[0], kbp, 0.0), jnp.where(km[1], kbp, 0.0)], axis=0).astype(BF16)
            nt = (((1,), (1,)), ((), ()))
            sf = lax.dot_general(qf[:, ks], k2f, nt, preferred_element_type=F32)
            sb = lax.dot_general(qb[:, ks], k2b, nt, preferred_element_type=F32)
            sc = (jnp.where(lower, sf, 0.0) + jnp.where(upper, sb, 0.0)).astype(BF16)
            vp_ = vv[:, vsl]
            zero = jnp.zeros_like(vp_)
            v2 = jnp.concatenate([jnp.where(vm[0], vp_, zero), jnp.where(vm[1], vp_, zero)], axis=0)
            o = jnp.dot(sc, v2, preferred_element_type=F32)
            st = st_ref[p]
            o = o + lax.dot_general(q2[0:c, ks], st.astype(BF16), nt, preferred_element_type=F32)
            o = o + lax.dot_general(q2[c:2 * c, ks], sb_ref[gi, p], nt, preferred_element_type=F32)
            upd = lax.dot_general(vp_, kf2[:, ks], (((0,), (0,)), ((), ())), preferred_element_type=F32)
            st_ref[p] = st * t["dec"][:, ks] + jnp.where(bd, upd, 0.0)
            pair_out.append(o)
        outs.append(jnp.concatenate(pair_out, axis=1) if npair > 1 else pair_out[0])
    o = jnp.concatenate(outs, axis=0)
    ms = _split_dot(o * o, bdv_ref[...]) * (1.0 / dv)
    gate = gate_ref[...].astype(F32)
    o_ref[...] = (o * lax.rsqrt(ms + NORM_EPS) * _silu(gate)).astype(BF16)


def _linear_attention(q, k, v, lf, lb, gate, batch, lp, *, npair, vp, const_decay, strict_upper, dv):
    gc = LA_GROUP * CHUNK
    nblk = lp // gc
    nc = lp // CHUNK
    wk = npair * LANES
    wv = npair * vp
    q3 = q.reshape(batch, lp, wk)
    k3 = k.reshape(batch, lp, wk)
    v3 = v.reshape(batch, lp, wv)
    g3 = gate.reshape(batch, lp, wv)
    if const_decay:
        lf3, lb3 = lf, lb
        dspec_f = pl.BlockSpec((1, wk), lambda b, j: (0, 0))
        dspec_r = dspec_f
    else:
        lf3 = lf.reshape(batch, lp, wk)
        lb3 = lb.reshape(batch, lp, wk)
        dspec_f = pl.BlockSpec((None, gc, wk), lambda b, j: (b, j, 0))
        dspec_r = pl.BlockSpec((None, gc, wk), lambda b, j: (b, nblk - 1 - j, 0))

    sb = pl.pallas_call(
        functools.partial(_la_bwd_kernel, npair=npair, vp=vp, const_decay=const_decay),
        grid=(batch, nblk),
        in_specs=[pl.BlockSpec((None, gc, wk), lambda b, j: (b, nblk - 1 - j, 0)),
                  pl.BlockSpec((None, gc, wv), lambda b, j: (b, nblk - 1 - j, 0)),
                  dspec_r],
        out_specs=pl.BlockSpec((None, LA_GROUP, npair, vp, LANES), lambda b, j: (b, nblk - 1 - j, 0, 0, 0)),
        out_shape=jax.ShapeDtypeStruct((batch, nc, npair, vp, LANES), BF16),
        scratch_shapes=[pltpu.VMEM((npair, vp, LANES), F32)],
        compiler_params=_params("parallel", "arbitrary"),
        name="la_bwd_states",
    )(k3, v3, lb3)

    vr = lax.broadcasted_iota(jnp.int32, (wv, wv), 0) // (vp // 2)
    vc = lax.broadcasted_iota(jnp.int32, (wv, wv), 1) // (vp // 2)
    bdv = (vr == vc).astype(BF16)
    o = pl.pallas_call(
        functools.partial(_la_fwd_kernel, npair=npair, vp=vp, const_decay=const_decay,
                          strict_upper=strict_upper, dv=dv),
        grid=(batch, nblk),
        in_specs=[pl.BlockSpec((None, gc, wk), lambda b, j: (b, j, 0)),
                  pl.BlockSpec((None, gc, wk), lambda b, j: (b, j, 0)),
                  pl.BlockSpec((None, gc, wv), lambda b, j: (b, j, 0)),
                  dspec_f, dspec_f,
                  pl.BlockSpec((None, LA_GROUP, npair, vp, LANES), lambda b, j: (b, j, 0, 0, 0)),
                  pl.BlockSpec((None, gc, wv), lambda b, j: (b, j, 0)),
                  pl.BlockSpec((wv, wv), lambda b, j: (0, 0))],
        out_specs=pl.BlockSpec((None, gc, wv), lambda b, j: (b, j, 0)),
        out_shape=jax.ShapeDtypeStruct((batch, lp, wv), BF16),
        scratch_shapes=[pltpu.VMEM((npair, vp, LANES), F32)],
        compiler_params=_params("parallel", "arbitrary"),
        name="la_fwd",
    )(q3, k3, v3, lf3, lb3, sb, g3, bdv)
    return o.reshape(batch * lp, wv)


def _outproj_kernel(ya_ref, yb_ref, yc_ref, h_ref, wglu_ref, wout_ref, g_ref, *rest, with_router):
    if with_router:
        wr_ref, hout_ref, hn_ref, gates_ref = rest
    else:
        hout_ref, hn_ref = rest
    ya = _gelu_tanh(ya_ref[...].astype(F32))
    gl = jnp.dot(ya.astype(BF16), wglu_ref[...], preferred_element_type=F32)
    ya = (ya * _sigmoid(gl)).astype(BF16)
    y = jnp.concatenate([ya, yb_ref[...], yc_ref[...]], axis=1)
    h = h_ref[...] + jnp.dot(y, wout_ref[...], preferred_element_type=F32)
    hout_ref[...] = h
    ms = jnp.mean(h * h, axis=-1, keepdims=True)
    hn = h * lax.rsqrt(ms + NORM_EPS) * g_ref[...]
    hn_ref[...] = hn.astype(BF16)
    if with_router:
        logits = jnp.dot(hn, wr_ref[...], preferred_element_type=F32, precision=lax.Precision.HIGHEST)
        lane = lax.broadcasted_iota(jnp.int32, logits.shape, 1)
        neg = jnp.float32(-jnp.inf)
        logits = jnp.where(lane < N_EXPERTS, logits, neg)
        v1 = jnp.max(logits, axis=-1, keepdims=True)
        i1 = jnp.min(jnp.where(logits == v1, lane, LANES), axis=-1, keepdims=True)
        m1 = lane == i1
        l2 = jnp.where(m1, neg, logits)
        v2 = jnp.max(l2, axis=-1, keepdims=True)
        i2 = jnp.min(jnp.where(l2 == v2, lane, LANES), axis=-1, keepdims=True)
        m2 = lane == i2
        e = jnp.exp(v2 - v1)
        w1 = 1.0 / (1.0 + e)
        gates_ref[...] = jnp.where(m1, w1, 0.0) + jnp.where(m2, e * w1, 0.0)


def _outproj(ya, yb, yc, h, wglu, wout, gamma, wr=None):
    r = h.shape[0]
    tm = ROW_TILE
    with_router = wr is not None

    def rows(wd):
        return pl.BlockSpec((tm, wd), lambda i: (i, 0))

    in_specs = [rows(256), rows(384), rows(512), rows(D_MODEL), _const_spec((256, 256)),
                _const_spec((1152, D_MODEL)), _const_spec((1, D_MODEL))]
    out_specs = [rows(D_MODEL), rows(D_MODEL)]
    out_shape = [jax.ShapeDtypeStruct((r, D_MODEL), F32), jax.ShapeDtypeStruct((r, D_MODEL), BF16)]
    args = [ya, yb, yc, h, wglu, wout, gamma]
    if with_router:
        in_specs.append(_const_spec((D_MODEL, LANES)))
        out_specs.append(rows(LANES))
        out_shape.append(jax.ShapeDtypeStruct((r, LANES), F32))
        args.append(wr)
    return pl.pallas_call(
        functools.partial(_outproj_kernel, with_router=with_router),
        grid=(r // tm,),
        in_specs=in_specs, out_specs=out_specs, out_shape=out_shape,
        input_output_aliases={3: 0},
        compiler_params=_params("parallel"),
        name="outproj_router" if with_router else "outproj",
    )(*args)


FF_TILE = 1408


def _ffn_kernel(hn_ref, h_ref, wg_ref, wu_ref, wd_ref, o_ref, acc_ref):
    j = pl.program_id(1)
    hn = hn_ref[...]
    a = _silu(jnp.dot(hn, wg_ref[...], preferred_element_type=F32)) * jnp.dot(hn, wu_ref[...], preferred_element_type=F32)
    y = jnp.dot(a.astype(BF16), wd_ref[...], preferred_element_type=F32)

    @pl.when(j == 0)
    def _():
        acc_ref[...] = h_ref[...] + y

    @pl.when(j != 0)
    def _():
        acc_ref[...] += y

    @pl.when(j == pl.num_programs(1) - 1)
    def _():
        o_ref[...] = acc_ref[...]


def _ffn(hn, h, wg, wu, wd):
    r = h.shape[0]
    tm = ROW_TILE
    nf = D_FF // FF_TILE
    return pl.pallas_call(
        _ffn_kernel,
        grid=(r // tm, nf),
        in_specs=[pl.BlockSpec((tm, D_MODEL), lambda i, j: (i, 0)),
                  pl.BlockSpec((tm, D_MODEL), lambda i, j: (i, 0)),
                  pl.BlockSpec((D_MODEL, FF_TILE), lambda i, j: (0, j)),
                  pl.BlockSpec((D_MODEL, FF_TILE), lambda i, j: (0, j)),
                  pl.BlockSpec((FF_TILE, D_MODEL), lambda i, j: (j, 0))],
        out_specs=pl.BlockSpec((tm, D_MODEL), lambda i, j: (i, 0)),
        out_shape=jax.ShapeDtypeStruct((r, D_MODEL), F32),
        scratch_shapes=[pltpu.VMEM((tm, D_MODEL), F32)],
        input_output_aliases={1: 0},
        compiler_params=_params("parallel", "arbitrary"),
        name="ffn",
    )(hn, h, wg, wu, wd)


def _moe_kernel(hn_ref, gates_ref, h_ref, wg_ref, wu_ref, wd_ref, o_ref, acc_ref):
    e = pl.program_id(1)
    j = pl.program_id(2)
    hn = hn_ref[...]
    a = _silu(jnp.dot(hn, wg_ref[...], preferred_element_type=F32)) * jnp.dot(hn, wu_ref[...], preferred_element_type=F32)
    y = jnp.dot(a.astype(BF16), wd_ref[...], preferred_element_type=F32)
    gates = gates_ref[...]
    lane = lax.broadcasted_iota(jnp.int32, gates.shape, 1)
    ge = jnp.sum(jnp.where(lane == e, gates, 0.0), axis=-1, keepdims=True)
    first = (e == 0) & (j == 0)

    @pl.when(first)
    def _():
        acc_ref[...] = h_ref[...] + ge * y

    @pl.when(jnp.logical_not(first))
    def _():
        acc_ref[...] += ge * y

    @pl.when((e == pl.num_programs(1) - 1) & (j == pl.num_programs(2) - 1))
    def _():
        o_ref[...] = acc_ref[...]


def _moe(hn, gates, h, wg, wu, wd):
    r = h.shape[0]
    tm = ROW_TILE
    nf = D_FF // FF_TILE
    return pl.pallas_call(
        _moe_kernel,
        grid=(r // tm, N_EXPERTS, nf),
        in_specs=[pl.BlockSpec((tm, D_MODEL), lambda i, e, j: (i, 0)),
                  pl.BlockSpec((tm, LANES), lambda i, e, j: (i, 0)),
                  pl.BlockSpec((tm, D_MODEL), lambda i, e, j: (i, 0)),
                  pl.BlockSpec((None, D_MODEL, FF_TILE), lambda i, e, j: (e, 0, j)),
                  pl.BlockSpec((None, D_MODEL, FF_TILE), lambda i, e, j: (e, 0, j)),
                  pl.BlockSpec((None, FF_TILE, D_MODEL), lambda i, e, j: (e, j, 0))],
        out_specs=pl.BlockSpec((tm, D_MODEL), lambda i, e, j: (i, 0)),
        out_shape=jax.ShapeDtypeStruct((r, D_MODEL), F32),
        scratch_shapes=[pltpu.VMEM((tm, D_MODEL), F32)],
        input_output_aliases={2: 0},
        compiler_params=_params("parallel", "arbitrary", "arbitrary"),
        name="moe",
    )(hn, gates, h, wg, wu, wd)


def _final_norm_kernel(h_ref, g_ref, o_ref):
    h = h_ref[...]
    ms = jnp.mean(h * h, axis=-1, keepdims=True)
    o_ref[...] = h * lax.rsqrt(ms + NORM_EPS) * g_ref[...]


def _final_norm(h, gamma):
    r = h.shape[0]
    tm = ROW_TILE
    return pl.pallas_call(
        _final_norm_kernel,
        grid=(r // tm,),
        in_specs=[pl.BlockSpec((tm, D_MODEL), lambda i: (i, 0)), _const_spec((1, D_MODEL))],
        out_specs=pl.BlockSpec((tm, D_MODEL), lambda i: (i, 0)),
        out_shape=jax.ShapeDtypeStruct((r, D_MODEL), F32),
        compiler_params=_params("parallel"),
        name="final_norm",
    )(h, gamma)


def _rope_tables(lp):
    half = RET_DIM // 2
    pos = jnp.arange(lp, dtype=F32) - float(PAD)
    inv = ROPE_BASE ** (-jnp.arange(half, dtype=F32) / half)
    ang = pos[:, None] * inv[None, :]
    cos = jnp.tile(jnp.cos(ang), (1, 2 * RET_HEADS))
    sin = jnp.tile(jnp.sin(ang), (1, 2 * RET_HEADS))
    return cos, sin


def _prep_layer(li, p):
    w_out = p["w_out"][li].astype(F32)
    wc = w_out[640:1024].reshape(GLA_HEADS, GLA_DV, D_MODEL)
    wc = jnp.pad(wc, ((0, 0), (0, GLA_DV_PAD - GLA_DV), (0, 0))).reshape(GLA_HEADS * GLA_DV_PAD, D_MODEL)
    wgate, bgate = _pack_gate(p["gla_w_gate_f"][li], p["gla_b_gate_f"][li], p["gla_w_gate_b"][li], p["gla_b_gate_b"][li])
    return dict(
        norm_mix=p["norm_mix"][li].astype(F32)[None],
        w_in=_pack_w_in(p["w_in"][li].astype(F32)),
        wgate=wgate, bgate=bgate,
        s5=_s5_tables(p["s5_lambda_re"][li], p["s5_lambda_im"][li], p["s5_log_dt"][li], p["s5_b_re"][li],
                      p["s5_b_im"][li], p["s5_c_re"][li], p["s5_c_im"][li], p["s5_d"][li]),
        w_glu=p["s5_w_glu"][li].astype(BF16),
        w_out=jnp.concatenate([w_out[0:640], wc], axis=0).astype(BF16),
        norm_ffn=p["norm_ffn"][li].astype(F32)[None],
    )


def _trunk(x, meta_tokens, layers, ffn, moe, norm_final, depth):
    batch, seq, _ = x.shape
    lp = seq + FRONT
    r = batch * lp
    meta = jnp.broadcast_to(meta_tokens.astype(F32)[None], (batch, N_META, D_MODEL))
    h = jnp.concatenate([jnp.zeros((batch, PAD, D_MODEL), F32), meta, x.astype(F32)], axis=1).reshape(r, D_MODEL)
    cos, sin = _rope_tables(lp)
    cos = jnp.tile(cos, (batch, 1))
    sin = jnp.tile(sin, (batch, 1))
    log_gamma = jnp.log1p(-jnp.exp2(-5.0 - jnp.arange(RET_HEADS, dtype=F32)))
    lg = jnp.repeat(log_gamma, RET_DIM)[None]
    for li in range(depth):
        lw = layers[li]
        u, rq, rk, rv, rg, gq, gk, gv, gg, lf, lb = _inproj(
            h, lw["norm_mix"], lw["w_in"], cos, sin, lw["wgate"], lw["bgate"], batch, lp)
        ya = _s5_mixer(u, lw["s5"], batch, lp)
        yb = _linear_attention(rq, rk, rv, lg, lg, rg, batch, lp, npair=RET_HEADS // 2, vp=LANES,
                               const_decay=True, strict_upper=True, dv=RET_DIM)
        yc = _linear_attention(gq, gk, gv, lf, lb, gg, batch, lp, npair=GLA_HEADS // 2, vp=2 * GLA_DV_PAD,
                               const_decay=False, strict_upper=False, dv=GLA_DV)
        j = li // 2
        if li % 2 == 0:
            h, hn = _outproj(ya, yb, yc, h, lw["w_glu"], lw["w_out"], lw["norm_ffn"])
            h = _ffn(hn, h, ffn["wg"][j], ffn["wu"][j], ffn["wd"][j])
        else:
            h, hn, gates = _outproj(ya, yb, yc, h, lw["w_glu"], lw["w_out"], lw["norm_ffn"], moe["wr"][j])
            h = _moe(hn, gates, h, moe["wg"][j], moe["wu"][j], moe["wd"][j])
    out = _final_norm(h, norm_final.astype(F32)[None])
    return out.reshape(batch, lp, D_MODEL)[:, FRONT:]


def kernel(x_prompt, x_sample, meta_tokens, norm_mix, w_in, s5_lambda_re, s5_lambda_im, s5_log_dt, s5_b_re, s5_b_im, s5_c_re, s5_c_im, s5_d, s5_w_glu, gla_w_gate_f, gla_b_gate_f, gla_w_gate_b, gla_b_gate_b, w_out, norm_ffn, ffn_w_gate, ffn_w_up, ffn_w_down, router_w, moe_w_gate, moe_w_up, moe_w_down, norm_final):
    depth = w_in.shape[0]
    p = dict(norm_mix=norm_mix, w_in=w_in, s5_lambda_re=s5_lambda_re, s5_lambda_im=s5_lambda_im,
             s5_log_dt=s5_log_dt, s5_b_re=s5_b_re, s5_b_im=s5_b_im, s5_c_re=s5_c_re, s5_c_im=s5_c_im, s5_d=s5_d,
             s5_w_glu=s5_w_glu, gla_w_gate_f=gla_w_gate_f, gla_b_gate_f=gla_b_gate_f, gla_w_gate_b=gla_w_gate_b,
             gla_b_gate_b=gla_b_gate_b, w_out=w_out, norm_ffn=norm_ffn)
    layers = [_prep_layer(li, p) for li in range(depth)]
    ffn = dict(wg=ffn_w_gate.astype(BF16), wu=ffn_w_up.astype(BF16), wd=ffn_w_down.astype(BF16))
    wr = jnp.pad(router_w.astype(F32), ((0, 0), (0, 0), (0, LANES - N_EXPERTS)))
    moe = dict(wr=wr, wg=moe_w_gate.astype(BF16), wu=moe_w_up.astype(BF16), wd=moe_w_down.astype(BF16))
    y_prompt = _trunk(x_prompt, meta_tokens, layers, ffn, moe, norm_final, depth)
    y_sample = _trunk(x_sample, meta_tokens, layers, ffn, moe, norm_final, depth)
    return (y_prompt, y_sample)
```

```python
import functools
import math

import jax
import jax.numpy as jnp
from jax import lax
from jax.experimental import pallas as pl
from jax.experimental.pallas import tpu as pltpu

F32 = jnp.float32
BF16 = jnp.bfloat16

D_MODEL = 1024
N_META = 16
S5_P = 16
S5_WIDTH = 256
S5_GROUPS = 16
S5_N = 64
RET_HEADS = 6
RET_DIM = 64
RET_WIDTH = 384
GLA_HEADS = 4
GLA_DV = 96
GLA_DK = 48
GLA_QK = 192
GLA_WIDTH = 384
GLA_GATE_RANK = 16
GLA_TAU = 16.0
ROPE_BASE = 10000.0
D_FF = 2816
N_EXPERTS = 8
NORM_EPS = 1e-5

CHUNK = 64
FRONT = 256
PAD = FRONT - N_META
ROW_TILE = 512
LA_GROUP = 4
LANES = 128
GLA_DK_PAD = 64
GLA_DV_PAD = 128
VMEM_LIMIT = 56 * 1024 * 1024


def _params(*sem):
    return pltpu.CompilerParams(dimension_semantics=sem, vmem_limit_bytes=VMEM_LIMIT)


def _const_spec(shape):
    nd = len(shape)
    return pl.BlockSpec(shape, lambda *_: (0,) * nd)


def _sigmoid(x):
    return 1.0 / (1.0 + jnp.exp(-x))


def _silu(x):
    return x * _sigmoid(x)


def _gelu_tanh(x):
    c = math.sqrt(2.0 / math.pi)
    return 0.5 * x * (1.0 + jnp.tanh(c * (x + 0.044715 * (x * x * x))))


def _log_sigmoid(z):
    return jnp.minimum(z, 0.0) - jnp.log(1.0 + jnp.exp(-jnp.abs(z)))


def _split_dot(a, b_bf16, dims=None):
    hi = a.astype(BF16)
    lo = (a - hi.astype(F32)).astype(BF16)
    if dims is None:
        return (jnp.dot(hi, b_bf16, preferred_element_type=F32)
                + jnp.dot(lo, b_bf16, preferred_element_type=F32))
    return (lax.dot_general(hi, b_bf16, dims, preferred_element_type=F32)
            + lax.dot_general(lo, b_bf16, dims, preferred_element_type=F32))


_C_U = 0
_C_RQ = 256
_C_RQR = 640
_C_RK = 1024
_C_RKR = 1408
_C_RV = 1792
_C_RG = 2176
_C_GQ = 2560
_C_GK = 2816
_C_GV = 3072
_C_GG = 3584
_C_GL = 4096
_C_END = 4224


def _pack_w_in(w):
    o = 0
    u = w[:, o:o + 256]; o += 256
    rq = w[:, o:o + 384]; o += 384
    rk = w[:, o:o + 384]; o += 384
    rv = w[:, o:o + 384]; o += 384
    rg = w[:, o:o + 384]; o += 384
    gq = w[:, o:o + 192]; o += 192
    gk = w[:, o:o + 192]; o += 192
    gv = w[:, o:o + 384]; o += 384
    gg = w[:, o:o + 384]; o += 384
    glf = w[:, o:o + 16]; o += 16
    glb = w[:, o:o + 16]; o += 16

    def rot(m):
        m = m.reshape(D_MODEL, RET_HEADS, 2, RET_DIM // 2)
        return jnp.stack([-m[:, :, 1], m[:, :, 0]], axis=2).reshape(D_MODEL, RET_WIDTH)

    def padh(m, d, dp):
        m = m.reshape(D_MODEL, GLA_HEADS, d)
        return jnp.pad(m, ((0, 0), (0, 0), (0, dp - d))).reshape(D_MODEL, GLA_HEADS * dp)

    gl = jnp.pad(jnp.concatenate([glf, glb], axis=1), ((0, 0), (0, LANES - 2 * GLA_GATE_RANK)))
    cat = jnp.concatenate([
        u, rq, rot(rq), rk, rot(rk), rv, rg,
        padh(gq, GLA_DK, GLA_DK_PAD), padh(gk, GLA_DK, GLA_DK_PAD),
        padh(gv, GLA_DV, GLA_DV_PAD), padh(gg, GLA_DV, GLA_DV_PAD), gl], axis=1)
    return cat.astype(BF16)


def _pack_gate(w_f, b_f, w_b, b_b):
    def padh(m):
        m = m.reshape(m.shape[0], GLA_HEADS, GLA_DK)
        return jnp.pad(m, ((0, 0), (0, 0), (0, GLA_DK_PAD - GLA_DK))).reshape(m.shape[0], GLA_HEADS * GLA_DK_PAD)
    r = GLA_GATE_RANK
    w = jnp.zeros((LANES, 2 * GLA_HEADS * GLA_DK_PAD), F32)
    w = w.at[0:r, 0:256].set(padh(w_f.astype(F32)))
    w = w.at[r:2 * r, 256:512].set(padh(w_b.astype(F32)))
    b = jnp.concatenate([padh(b_f.astype(F32)[None]), padh(b_b.astype(F32)[None])], axis=1)
    return w.astype(BF16), b


def _inproj_kernel(h_ref, g_ref, w_ref, cos_ref, sin_ref, wgate_ref, bgate_ref,
                   u_ref, rq_ref, rk_ref, rv_ref, rg_ref, gq_ref, gk_ref, gv_ref, gg_ref, lf_ref, lb_ref,
                   *, batch, lp):
    tm = h_ref.shape[0]
    x = h_ref[...]
    ms = jnp.mean(x * x, axis=-1, keepdims=True)
    row = pl.program_id(0) * tm + lax.broadcasted_iota(jnp.int32, (tm, 1), 0)
    valid = jnp.ones((tm, 1), F32)
    for b in range(batch):
        valid = jnp.where((row >= b * lp) & (row < b * lp + PAD), 0.0, valid)
    hn = (x * (lax.rsqrt(ms + NORM_EPS) * valid) * g_ref[...]).astype(BF16)

    def proj(lo, hi):
        return jnp.dot(hn, w_ref[:, lo:hi], preferred_element_type=F32)

    u_ref[...] = proj(_C_U, _C_RQ).astype(BF16)
    cos = cos_ref[...]
    sin = sin_ref[...]
    rq_ref[...] = (proj(_C_RQ, _C_RQR) * cos + proj(_C_RQR, _C_RK) * sin).astype(BF16)
    rk_ref[...] = ((proj(_C_RK, _C_RKR) * cos + proj(_C_RKR, _C_RV) * sin) * (RET_DIM ** -0.5)).astype(BF16)
    rv_ref[...] = proj(_C_RV, _C_RG).astype(BF16)
    rg_ref[...] = proj(_C_RG, _C_GQ).astype(BF16)
    gq_ref[...] = (proj(_C_GQ, _C_GK) * (GLA_DK ** -0.5)).astype(BF16)
    gk_ref[...] = proj(_C_GK, _C_GV).astype(BF16)
    gv_ref[...] = proj(_C_GV, _C_GG).astype(BF16)
    gg_ref[...] = proj(_C_GG, _C_GL).astype(BF16)
    codes = proj(_C_GL, _C_END).astype(BF16)
    z = jnp.dot(codes, wgate_ref[...], preferred_element_type=F32) + bgate_ref[...]
    ls = _log_sigmoid(z) * (1.0 / GLA_TAU)
    lf_ref[...] = ls[:, 0:256]
    lb_ref[...] = ls[:, 256:512]


def _inproj(h, gamma, w, cos, sin, wgate, bgate, batch, lp):
    r = h.shape[0]
    tm = ROW_TILE
    widths = (256, 384, 384, 384, 384, 256, 256, 512, 512, 256, 256)
    dtypes = (BF16,) * 9 + (F32, F32)

    def rows(wd):
        return pl.BlockSpec((tm, wd), lambda i: (i, 0))

    return pl.pallas_call(
        functools.partial(_inproj_kernel, batch=batch, lp=lp),
        grid=(r // tm,),
        in_specs=[rows(D_MODEL), _const_spec((1, D_MODEL)), _const_spec((D_MODEL, _C_END)),
                  rows(RET_WIDTH), rows(RET_WIDTH), _const_spec((LANES, 512)), _const_spec((1, 512))],
        out_specs=[rows(wd) for wd in widths],
        out_shape=[jax.ShapeDtypeStruct((r, wd), dt) for wd, dt in zip(widths, dtypes)],
        compiler_params=_params("parallel"),
        name="inproj",
    )(h, gamma, w, cos, sin, wgate, bgate)


def _s5_toeplitz_kernel(pwr_ref, pwi_ref, cer_ref, cei_ref, bbr_ref, bbi_ref, d_ref, tt_ref):
    hp = lax.Precision.HIGHEST
    w = CHUNK * S5_P
    lane = lax.broadcasted_iota(jnp.int32, (S5_P, w), 1)
    krow = []
    for dr in range(2):
        pr, pi, cr, ci = pwr_ref[dr], pwi_ref[dr], cer_ref[dr], cei_ref[dr]
        zr = pr * cr - pi * ci
        zi = pr * ci + pi * cr
        krow.append(jnp.dot(bbr_ref[dr], zr, precision=hp, preferred_element_type=F32)
                    - jnp.dot(bbi_ref[dr], zi, precision=hp, preferred_element_type=F32))
    kf = krow[0] + d_ref[...]
    kb = krow[1]
    for s in range(CHUNK):
        right = S5_P * s
        left = S5_P * (CHUNK - 1 - s)
        a = kf if right == 0 else jnp.where(lane >= right, pltpu.roll(kf, right, 1), 0.0)
        b = kb if left == 0 else jnp.where(lane < w - left, pltpu.roll(kb, w - left, 1), 0.0)
        tt_ref[S5_P * s:S5_P * (s + 1), :] = (a + b).astype(BF16)


def _s5_toeplitz(pwr, pwi, cer, cei, bbr, bbi, d_e):
    w = CHUNK * S5_P
    big = pl.BlockSpec((None, 2, S5_N, w), lambda g: (g, 0, 0, 0))
    small = pl.BlockSpec((None, 2, S5_P, S5_N), lambda g: (g, 0, 0, 0))
    return pl.pallas_call(
        _s5_toeplitz_kernel,
        grid=(S5_GROUPS,),
        in_specs=[big, big, big, big, small, small, pl.BlockSpec((None, S5_P, w), lambda g: (g, 0, 0))],
        out_specs=pl.BlockSpec((None, w, w), lambda g: (g, 0, 0)),
        out_shape=jax.ShapeDtypeStruct((S5_GROUPS, w, w), BF16),
        compiler_params=_params("parallel"),
        name="s5_toeplitz",
    )(pwr, pwi, cer, cei, bbr, bbi, d_e)


def _s5_tables(lam_re, lam_im, log_dt, b_re, b_im, c_re, c_im, d):
    c = CHUNK
    g_, n_, p_ = S5_GROUPS, S5_N, S5_P
    dt = jnp.exp(log_dt.astype(F32))[..., None]
    lr = lam_re.astype(F32)
    li = lam_im.astype(F32)
    e = lr * dt
    th = li * dt
    mag = jnp.exp(e)
    a_re = mag * jnp.cos(th)
    a_im = mag * jnp.sin(th)
    den = lr * lr + li * li
    nr = a_re - 1.0
    ni = a_im
    coef_re = (nr * lr + ni * li) / den
    coef_im = (ni * lr - nr * li) / den
    br = b_re.astype(F32)
    bi = b_im.astype(F32)
    bb_re = coef_re[..., None] * br - coef_im[..., None] * bi
    bb_im = coef_re[..., None] * bi + coef_im[..., None] * br
    cr = c_re.astype(F32)
    ci = c_im.astype(F32)
    tau = jnp.arange(c + 1, dtype=F32)[:, None, None, None]
    pw_mag = jnp.exp(tau * e[None])
    pw_re = pw_mag * jnp.cos(tau * th[None])
    pw_im = pw_mag * jnp.sin(tau * th[None])

    z_re = cr[None] * pw_re[:, :, :, None, :] - ci[None] * pw_im[:, :, :, None, :]
    z_im = cr[None] * pw_im[:, :, :, None, :] + ci[None] * pw_re[:, :, :, None, :]
    idx = jnp.arange(c)

    def expand_pw(pw):
        both = jnp.stack([pw[:c, 0], pw[c - 1 - idx, 1]], axis=0)
        both = jnp.transpose(both, (2, 0, 3, 1))
        return jnp.broadcast_to(both[..., None], (g_, 2, n_, c, p_)).reshape(g_, 2, n_, c * p_)

    def expand_c(cm):
        cm = jnp.transpose(cm, (1, 0, 3, 2))
        return jnp.broadcast_to(cm[:, :, :, None, :], (g_, 2, n_, c, p_)).reshape(g_, 2, n_, c * p_)

    bbt_re = jnp.transpose(bb_re, (1, 0, 3, 2))
    bbt_im = jnp.transpose(bb_im, (1, 0, 3, 2))
    d_e = jnp.zeros((g_, p_, c * p_), F32).at[:, :, :p_].set(d.astype(F32)[:, :, None] * jnp.eye(p_, dtype=F32))
    tt = _s5_toeplitz(expand_pw(pw_re), expand_pw(pw_im), expand_c(cr), expand_c(ci), bbt_re, bbt_im, d_e)

    pf_re = pw_re[c - 1 - idx, 0]
    pf_im = pw_im[c - 1 - idx, 0]
    pb_re = pw_re[idx, 1]
    pb_im = pw_im[idx, 1]

    def m_of(p_re_, p_im_, dr):
        m_re = p_re_[:, :, :, None] * bb_re[dr][None] - p_im_[:, :, :, None] * bb_im[dr][None]
        m_im = p_re_[:, :, :, None] * bb_im[dr][None] + p_im_[:, :, :, None] * bb_re[dr][None]
        to = lambda m: jnp.transpose(m, (1, 0, 3, 2)).reshape(g_, c * p_, n_)
        return to(m_re), to(m_im)

    mf_re, mf_im = m_of(pf_re, pf_im, 0)
    mb_re, mb_im = m_of(pb_re, pb_im, 1)
    m4 = jnp.stack([mf_re, mf_im, mb_re, mb_im], axis=1)
    m4 = m4.reshape(g_ // 2, 2, 4, c * p_, n_)
    mpair = jnp.zeros((g_ // 2, 2, c * p_, 4, 2, n_), F32)
    for gi in range(2):
        mpair = mpair.at[:, gi, :, :, gi, :].set(jnp.transpose(m4[:, gi], (0, 2, 1, 3)))
    mpair = mpair.reshape(g_ // 2, 2 * c * p_, 4 * 2 * n_).astype(BF16)

    def n_of(tsel, dr):
        w_re = z_re[tsel, dr]
        w_im = z_im[tsel, dr]
        to = lambda m: jnp.transpose(m, (1, 3, 0, 2)).reshape(g_, n_, c * p_)
        return to(w_re), to(-w_im)

    nf_re, nf_im = n_of(idx + 1, 0)
    nb_re, nb_im = n_of(c - idx, 1)
    n4 = jnp.stack([nf_re, nf_im, nb_re, nb_im], axis=1)
    n4 = n4.reshape(g_ // 2, 2, 4, n_, c * p_)
    npair = jnp.zeros((g_ // 2, 4, 2, n_, 2, c * p_), F32)
    for gi in range(2):
        npair = npair.at[:, :, gi, :, gi, :].set(n4[:, gi])
    npair = npair.reshape(g_ // 2, 4 * 2 * n_, 2 * c * p_).astype(BF16)

    dec = jnp.stack([pw_re[c, 0], pw_im[c, 0], pw_re[c, 1], pw_im[c, 1]], axis=0)
    dec = dec.reshape(4, 1, g_ * n_)
    return tt, mpair, npair, dec


def _s5_state_kernel(u_ref, m_ref, o0, o1, o2, o3):
    s = jnp.dot(u_ref[...], m_ref[...], preferred_element_type=F32)
    o0[...] = s[:, 0:128]
    o1[...] = s[:, 128:256]
    o2[...] = s[:, 256:384]
    o3[...] = s[:, 384:512]


def _s5_state(u, mpair):
    ncb = u.shape[0]
    return pl.pallas_call(
        _s5_state_kernel,
        grid=(S5_GROUPS // 2,),
        in_specs=[pl.BlockSpec((ncb, 2048), lambda j: (0, j)),
                  pl.BlockSpec((None, 2048, 512), lambda j: (j, 0, 0))],
        out_specs=[pl.BlockSpec((ncb, LANES), lambda j: (0, j))] * 4,
        out_shape=[jax.ShapeDtypeStruct((ncb, 1024), F32)] * 4,
        compiler_params=_params("parallel"),
        name="s5_state",
    )(u, mpair)


def _s5_scan_kernel(sfr, sfi, sbr, sbi, dfr, dfi, dbr, dbi, hfr, hfi, hbr, hbi):
    nc, b, _ = sfr.shape
    a_fr = dfr[...]
    a_fi = dfi[...]
    a_br = dbr[...]
    a_bi = dbi[...]
    zero = jnp.zeros((b, LANES), F32)

    def fwd(c, carry):
        hr, hi = carry
        hfr[c] = hr
        hfi[c] = hi
        return (a_fr * hr - a_fi * hi + sfr[c], a_fr * hi + a_fi * hr + sfi[c])

    def bwd(i, carry):
        c = nc - 1 - i
        hr, hi = carry
        hbr[c] = hr
        hbi[c] = hi
        return (a_br * hr - a_bi * hi + sbr[c], a_br * hi + a_bi * hr + sbi[c])

    lax.fori_loop(0, nc, fwd, (zero, zero))
    lax.fori_loop(0, nc, bwd, (zero, zero))


def _s5_scan(s4, dec, nc, b):
    s4 = [s.reshape(nc, b, 1024) for s in s4]
    blk = pl.BlockSpec((nc, b, LANES), lambda j: (0, 0, j))
    dblk = pl.BlockSpec((None, 1, LANES), lambda j: (0, 0, j))
    dspecs = [pl.BlockSpec((None, 1, LANES), functools.partial(lambda j, k: (k, 0, j), k=k)) for k in range(4)]
    del dblk
    outs = pl.pallas_call(
        _s5_scan_kernel,
        grid=(S5_GROUPS // 2,),
        in_specs=[blk] * 4 + dspecs,
        out_specs=[blk] * 4,
        out_shape=[jax.ShapeDtypeStruct((nc, b, 1024), F32)] * 4,
        compiler_params=_params("parallel"),
        name="s5_scan",
    )(*s4, dec, dec, dec, dec)
    return [o.reshape(nc * b, 1024) for o in outs]


def _s5_out_kernel(u_ref, tt_ref, h0, h1, h2, h3, n_ref, y_ref):
    u = u_ref[...]
    y0 = jnp.dot(u[:, 0:1024], tt_ref[0], preferred_element_type=F32)
    y1 = jnp.dot(u[:, 1024:2048], tt_ref[1], preferred_element_type=F32)
    hcat = jnp.concatenate([h0[...], h1[...], h2[...], h3[...]], axis=1).astype(BF16)
    yh = jnp.dot(hcat, n_ref[...], preferred_element_type=F32)
    y_ref[...] = (jnp.concatenate([y0, y1], axis=1) + yh).astype(BF16)


def _s5_out(u, tt, h4, npair):
    ncb = u.shape[0]
    rt = ncb if ncb <= 640 else ncb // 2
    hblk = pl.BlockSpec((rt, LANES), lambda j, i: (i, j))
    return pl.pallas_call(
        _s5_out_kernel,
        grid=(S5_GROUPS // 2, ncb // rt),
        in_specs=[pl.BlockSpec((rt, 2048), lambda j, i: (i, j)),
                  pl.BlockSpec((2, 1024, 1024), lambda j, i: (j, 0, 0)),
                  hblk, hblk, hblk, hblk,
                  pl.BlockSpec((None, 512, 2048), lambda j, i: (j, 0, 0))],
        out_specs=pl.BlockSpec((rt, 2048), lambda j, i: (i, j)),
        out_shape=jax.ShapeDtypeStruct((ncb, 16 * 1024), BF16),
        compiler_params=_params("parallel", "parallel"),
        name="s5_out",
    )(u, tt, *h4, npair)


def _s5_mixer(u, tables, batch, lp):
    tt, mpair, npair, dec = tables
    nc = lp // CHUNK
    g_, p_ = S5_GROUPS, S5_P
    uc = u.reshape(batch, nc, CHUNK, g_, p_)
    uc = jnp.transpose(uc, (1, 0, 3, 2, 4)).reshape(nc * batch, g_ * CHUNK * p_)
    s4 = _s5_state(uc, mpair)
    h4 = _s5_scan(s4, dec, nc, batch)
    y = _s5_out(uc, tt, h4, npair)
    y = y.reshape(nc, batch, g_, CHUNK, p_)
    return jnp.transpose(y, (1, 0, 3, 2, 4)).reshape(batch * lp, g_ * p_)


def _cumsum_rows(x):
    c = x.shape[0]
    r = lax.broadcasted_iota(jnp.int32, (c, c), 0)
    s = lax.broadcasted_iota(jnp.int32, (c, c), 1)
    tri = jnp.where(s <= r, 1.0, 0.0).astype(BF16)
    hi = x.astype(BF16)
    lo = (x - hi.astype(F32)).astype(BF16)
    return jnp.dot(tri, hi, preferred_element_type=F32) + jnp.dot(tri, lo, preferred_element_type=F32)


def _head_masks(npair, vp):
    kl = lax.broadcasted_iota(jnp.int32, (1, LANES), 1)
    km = [(kl < 64), (kl >= 64)]
    vl = lax.broadcasted_iota(jnp.int32, (1, vp), 1)
    vm = [(vl < vp // 2), (vl >= vp // 2)]
    vrow = lax.broadcasted_iota(jnp.int32, (vp, LANES), 0)
    kcol = lax.broadcasted_iota(jnp.int32, (vp, LANES), 1)
    bd = (vrow >= vp // 2) == (kcol >= 64)
    return km, vm, bd


def _la_bwd_kernel(k_ref, v_ref, lb_ref, sb_ref, st_ref, *, npair, vp, const_decay):
    g = LA_GROUP
    c = CHUNK

    @pl.when(pl.program_id(1) == 0)
    def _():
        st_ref[...] = jnp.zeros_like(st_ref)

    km, vm, bd = _head_masks(npair, vp)
    if const_decay:
        lg = lb_ref[...]
        i = lax.broadcasted_iota(jnp.int32, (c, 1), 0).astype(F32)
        e_cb_const = jnp.exp(i * lg)
        dec_const = jnp.exp(float(c) * lg)
    for gi in reversed(range(g)):
        rows = slice(gi * c, (gi + 1) * c)
        kk = k_ref[rows, :].astype(F32)
        vv = v_ref[rows, :]
        if const_decay:
            e_cb, dec = e_cb_const, dec_const
        else:
            lb = lb_ref[rows, :]
            cbi = _cumsum_rows(lb)
            e_cb = jnp.exp(cbi - lb)
            dec = jnp.exp(cbi[c - 1:c, :])
        kb2 = (kk * e_cb).astype(BF16)
        for p in range(npair):
            st = st_ref[p]
            sb_ref[gi, p] = st.astype(BF16)
            upd = lax.dot_general(vv[:, p * vp:(p + 1) * vp], kb2[:, p * LANES:(p + 1) * LANES],
                                  (((0,), (0,)), ((), ())), preferred_element_type=F32)
            st_ref[p] = st * dec[:, p * LANES:(p + 1) * LANES] + jnp.where(bd, upd, 0.0)


def _la_fwd_kernel(q_ref, k_ref, v_ref, lf_ref, lb_ref, sb_ref, gate_ref, bdv_ref, o_ref, st_ref,
                   *, npair, vp, const_decay, strict_upper, dv):
    g = LA_GROUP
    c = CHUNK

    @pl.when(pl.program_id(1) == 0)
    def _():
        st_ref[...] = jnp.zeros_like(st_ref)

    km, vm, bd = _head_masks(npair, vp)
    ri = lax.broadcasted_iota(jnp.int32, (c, LANES), 0)
    cj = lax.broadcasted_iota(jnp.int32, (c, LANES), 1) & (c - 1)
    lower = ri >= cj
    upper = (ri < cj) if strict_upper else (ri <= cj)
    mid = c // 2

    def terms(lf, lb, cfi, cbi):
        cf = cfi
        cb = cbi - lb
        mf = cf[mid:mid + 1, :]
        mb = cb[mid:mid + 1, :]
        tf = cf[c - 1:c, :]
        tb = cbi[c - 1:c, :]
        return dict(qf=jnp.exp(cf - mf), kf=jnp.exp(mf - cf), qb=jnp.exp(mb - cb), kb=jnp.exp(cb - mb),
                    qf2=jnp.exp(cf), qb2=jnp.exp(tb - cb), kf2=jnp.exp(tf - cf), dec=jnp.exp(tf))

    if const_decay:
        lg = lf_ref[...]
        i1 = lax.broadcasted_iota(jnp.int32, (c, 1), 0).astype(F32) + 1.0
        cfi_c = i1 * lg
        t_const = terms(lg, lg, cfi_c, cfi_c)

    outs = []
    for gi in range(g):
        rows = slice(gi * c, (gi + 1) * c)
        qq = q_ref[rows, :].astype(F32)
        kk = k_ref[rows, :].astype(F32)
        vv = v_ref[rows, :]
        if const_decay:
            t = t_const
        else:
            lf = lf_ref[rows, :]
            lb = lb_ref[rows, :]
            w = lf.shape[1]
            cs = _cumsum_rows(jnp.concatenate([lf, lb], axis=1))
            t = terms(lf, lb, cs[:, :w], cs[:, w:])
        qf = (qq * t["qf"]).astype(BF16)
        kf = kk * t["kf"]
        qb = (qq * t["qb"]).astype(BF16)
        kb = kk * t["kb"]
        q2 = jnp.concatenate([qq * t["qf2"], qq * t["qb2"]], axis=0).astype(BF16)
        kf2 = (kk * t["kf2"]).astype(BF16)
        pair_out = []
        for p in range(npair):
            ks = slice(p * LANES, (p + 1) * LANES)
            vsl = slice(p * vp, (p + 1) * vp)
            kfp = kf[:, ks]
            kbp = kb[:, ks]
            k2f = jnp.concatenate([jnp.where(km[0], kfp, 0.0), jnp.where(km[1], kfp, 0.0)], axis=0).astype(BF16)
            k2b = jnp.concatenate([jnp.where(km[0], kbp, 0.0), jnp.where(km[1], kbp, 0.0)], axis=0).astype(BF16)
            nt = (((1,), (1,)), ((), ()))
            sf = lax.dot_general(qf[:, ks], k2f, nt, preferred_element_type=F32)
            sb = lax.dot_general(qb[:, ks], k2b, nt, preferred_element_type=F32)
            sc = (jnp.where(lower, sf, 0.0) + jnp.where(upper, sb, 0.0)).astype(BF16)
            vp_ = vv[:, vsl]
            zero = jnp.zeros_like(vp_)
            v2 = jnp.concatenate([jnp.where(vm[0], vp_, zero), jnp.where(vm[1], vp_, zero)], axis=0)
            o = jnp.dot(sc, v2, preferred_element_type=F32)
            st = st_ref[p]
            o = o + lax.dot_general(q2[0:c, ks], st.astype(BF16), nt, preferred_element_type=F32)
            o = o + lax.dot_general(q2[c:2 * c, ks], sb_ref[gi, p], nt, preferred_element_type=F32)
            upd = lax.dot_general(vp_, kf2[:, ks], (((0,), (0,)), ((), ())), preferred_element_type=F32)
            st_ref[p] = st * t["dec"][:, ks] + jnp.where(bd, upd, 0.0)
            pair_out.append(o)
        outs.append(jnp.concatenate(pair_out, axis=1) if npair > 1 else pair_out[0])
    o = jnp.concatenate(outs, axis=0)
    ms = _split_dot(o * o, bdv_ref[...]) * (1.0 / dv)
    gate = gate_ref[...].astype(F32)
    o_ref[...] = (o * lax.rsqrt(ms + NORM_EPS) * _silu(gate)).astype(BF16)


def _linear_attention(q, k, v, lf, lb, gate, batch, lp, *, npair, vp, const_decay, strict_upper, dv):
    gc = LA_GROUP * CHUNK
    nblk = lp // gc
    nc = lp // CHUNK
    wk = npair * LANES
    wv = npair * vp
    q3 = q.reshape(batch, lp, wk)
    k3 = k.reshape(batch, lp, wk)
    v3 = v.reshape(batch, lp, wv)
    g3 = gate.reshape(batch, lp, wv)
    if const_decay:
        lf3, lb3 = lf, lb
        dspec_f = pl.BlockSpec((1, wk), lambda b, j: (0, 0))
        dspec_r = dspec_f
    else:
        lf3 = lf.reshape(batch, lp, wk)
        lb3 = lb.reshape(batch, lp, wk)
        dspec_f = pl.BlockSpec((None, gc, wk), lambda b, j: (b, j, 0))
        dspec_r = pl.BlockSpec((None, gc, wk), lambda b, j: (b, nblk - 1 - j, 0))

    sb = pl.pallas_call(
        functools.partial(_la_bwd_kernel, npair=npair, vp=vp, const_decay=const_decay),
        grid=(batch, nblk),
        in_specs=[pl.BlockSpec((None, gc, wk), lambda b, j: (b, nblk - 1 - j, 0)),
                  pl.BlockSpec((None, gc, wv), lambda b, j: (b, nblk - 1 - j, 0)),
                  dspec_r],
        out_specs=pl.BlockSpec((None, LA_GROUP, npair, vp, LANES), lambda b, j: (b, nblk - 1 - j, 0, 0, 0)),
        out_shape=jax.ShapeDtypeStruct((batch, nc, npair, vp, LANES), BF16),
        scratch_shapes=[pltpu.VMEM((npair, vp, LANES), F32)],
        compiler_params=_params("parallel", "arbitrary"),
        name="la_bwd_states",
    )(k3, v3, lb3)

    vr = lax.broadcasted_iota(jnp.int32, (wv, wv), 0) // (vp // 2)
    vc = lax.broadcasted_iota(jnp.int32, (wv, wv), 1) // (vp // 2)
    bdv = (vr == vc).astype(BF16)
    o = pl.pallas_call(
        functools.partial(_la_fwd_kernel, npair=npair, vp=vp, const_decay=const_decay,
                          strict_upper=strict_upper, dv=dv),
        grid=(batch, nblk),
        in_specs=[pl.BlockSpec((None, gc, wk), lambda b, j: (b, j, 0)),
                  pl.BlockSpec((None, gc, wk), lambda b, j: (b, j, 0)),
                  pl.BlockSpec((None, gc, wv), lambda b, j: (b, j, 0)),
                  dspec_f, dspec_f,
                  pl.BlockSpec((None, LA_GROUP, npair, vp, LANES), lambda b, j: (b, j, 0, 0, 0)),
                  pl.BlockSpec((None, gc, wv), lambda b, j: (b, j, 0)),
                  pl.BlockSpec((wv, wv), lambda b, j: (0, 0))],
        out_specs=pl.BlockSpec((None, gc, wv), lambda b, j: (b, j, 0)),
        out_shape=jax.ShapeDtypeStruct((batch, lp, wv), BF16),
        scratch_shapes=[pltpu.VMEM((npair, vp, LANES), F32)],
        compiler_params=_params("parallel", "arbitrary"),
        name="la_fwd",
    )(q3, k3, v3, lf3, lb3, sb, g3, bdv)
    return o.reshape(batch * lp, wv)


_RT_E1, _RT_E2, _RT_W1, _RT_W2, _RT_R1, _RT_R2 = range(6)


def _mix_out(ya_ref, yb_ref, yc_ref, h_ref, wglu_ref, wout_ref, g_ref):
    ya = _gelu_tanh(ya_ref[...].astype(F32))
    gl = jnp.dot(ya.astype(BF16), wglu_ref[...], preferred_element_type=F32)
    ya = (ya * _sigmoid(gl)).astype(BF16)
    y = jnp.concatenate([ya, yb_ref[...], yc_ref[...]], axis=1)
    h = h_ref[...] + jnp.dot(y, wout_ref[...], preferred_element_type=F32)
    ms = jnp.mean(h * h, axis=-1, keepdims=True)
    return h, h * lax.rsqrt(ms + NORM_EPS) * g_ref[...]


def _outproj_kernel(ya_ref, yb_ref, yc_ref, h_ref, wglu_ref, wout_ref, g_ref, hout_ref, hn_ref):
    h, hn = _mix_out(ya_ref, yb_ref, yc_ref, h_ref, wglu_ref, wout_ref, g_ref)
    hout_ref[...] = h
    hn_ref[...] = hn.astype(BF16)


def _outproj_router_kernel(ya_ref, yb_ref, yc_ref, h_ref, wglu_ref, wout_ref, g_ref, wrh_ref, wrl_ref,
                           hout_ref, hn_ref, route_ref, cnt_ref, carry_ref):
    @pl.when(pl.program_id(0) == 0)
    def _():
        carry_ref[...] = jnp.zeros_like(carry_ref)

    h, hn = _mix_out(ya_ref, yb_ref, yc_ref, h_ref, wglu_ref, wout_ref, g_ref)
    hout_ref[...] = h
    hn_ref[...] = hn
    hi = hn.astype(BF16)
    lo = (hn - hi.astype(F32)).astype(BF16)
    wrh = wrh_ref[...]
    logits = (jnp.dot(hi, wrh, preferred_element_type=F32) + jnp.dot(lo, wrh, preferred_element_type=F32)
              + jnp.dot(hi, wrl_ref[...], preferred_element_type=F32))
    tm = logits.shape[0]
    lane = lax.broadcasted_iota(jnp.int32, logits.shape, 1)
    neg = jnp.float32(-jnp.inf)
    logits = jnp.where(lane < N_EXPERTS, logits, neg)
    v1 = jnp.max(logits, axis=-1, keepdims=True)
    i1 = jnp.min(jnp.where(logits == v1, lane, LANES), axis=-1, keepdims=True)
    m1 = lane == i1
    l2 = jnp.where(m1, neg, logits)
    v2 = jnp.max(l2, axis=-1, keepdims=True)
    i2 = jnp.min(jnp.where(l2 == v2, lane, LANES), axis=-1, keepdims=True)
    m2 = lane == i2
    e = jnp.exp(v2 - v1)
    w1 = 1.0 / (1.0 + e)
    w2 = e * w1
    chosen = jnp.where(m1 | m2, 1.0, 0.0)
    rr = lax.broadcasted_iota(jnp.int32, (tm, tm), 0)
    ss = lax.broadcasted_iota(jnp.int32, (tm, tm), 1)
    before = jnp.where(ss < rr, 1.0, 0.0).astype(BF16)
    prefix = jnp.dot(before, chosen.astype(BF16), preferred_element_type=F32) + carry_ref[...]
    r1 = jnp.sum(jnp.where(m1, prefix, 0.0), axis=-1, keepdims=True)
    r2 = jnp.sum(jnp.where(m2, prefix, 0.0), axis=-1, keepdims=True)
    total = carry_ref[...] + jnp.sum(chosen, axis=0, keepdims=True)
    carry_ref[...] = total
    cnt_ref[...] = total
    rec = jnp.zeros_like(logits)
    for ln, val in ((_RT_E1, i1.astype(F32)), (_RT_E2, i2.astype(F32)), (_RT_W1, w1), (_RT_W2, w2),
                    (_RT_R1, r1), (_RT_R2, r2)):
        rec = jnp.where(lane == ln, val, rec)
    route_ref[...] = rec


def _outproj(ya, yb, yc, h, wglu, wout, gamma, wr=None):
    r = h.shape[0]
    tm = ROW_TILE
    with_router = wr is not None

    def rows(wd):
        return pl.BlockSpec((tm, wd), lambda i: (i, 0))

    in_specs = [rows(256), rows(384), rows(512), rows(D_MODEL), _const_spec((256, 256)),
                _const_spec((1152, D_MODEL)), _const_spec((1, D_MODEL))]
    args = [ya, yb, yc, h, wglu, wout, gamma]
    if not with_router:
        return pl.pallas_call(
            _outproj_kernel,
            grid=(r // tm,),
            in_specs=in_specs, out_specs=[rows(D_MODEL), rows(D_MODEL)],
            out_shape=[jax.ShapeDtypeStruct((r, D_MODEL), F32), jax.ShapeDtypeStruct((r, D_MODEL), BF16)],
            input_output_aliases={3: 0},
            compiler_params=_params("parallel"),
            name="outproj",
        )(*args)
    return pl.pallas_call(
        _outproj_router_kernel,
        grid=(r // tm,),
        in_specs=in_specs + [_const_spec((D_MODEL, LANES)), _const_spec((D_MODEL, LANES))],
        out_specs=[rows(D_MODEL), rows(D_MODEL), rows(LANES), _const_spec((1, LANES))],
        out_shape=[jax.ShapeDtypeStruct((r, D_MODEL), F32), jax.ShapeDtypeStruct((r, D_MODEL), F32),
                   jax.ShapeDtypeStruct((r, LANES), F32), jax.ShapeDtypeStruct((1, LANES), F32)],
        scratch_shapes=[pltpu.VMEM((1, LANES), F32)],
        input_output_aliases={3: 0},
        compiler_params=_params("arbitrary"),
        name="outproj_router",
    )(*args, wr[0], wr[1])


FF_TILE = 1408


def _ffn_kernel(hn_ref, h_ref, wg_ref, wu_ref, wd_ref, o_ref, acc_ref):
    j = pl.program_id(1)
    hn = hn_ref[...]
    a = _silu(jnp.dot(hn, wg_ref[...], preferred_element_type=F32)) * jnp.dot(hn, wu_ref[...], preferred_element_type=F32)
    y = jnp.dot(a.astype(BF16), wd_ref[...], preferred_element_type=F32)

    @pl.when(j == 0)
    def _():
        acc_ref[...] = h_ref[...] + y

    @pl.when(j != 0)
    def _():
        acc_ref[...] += y

    @pl.when(j == pl.num_programs(1) - 1)
    def _():
        o_ref[...] = acc_ref[...]


def _ffn(hn, h, wg, wu, wd):
    r = h.shape[0]
    tm = ROW_TILE
    nf = D_FF // FF_TILE
    return pl.pallas_call(
        _ffn_kernel,
        grid=(r // tm, nf),
        in_specs=[pl.BlockSpec((tm, D_MODEL), lambda i, j: (i, 0)),
                  pl.BlockSpec((tm, D_MODEL), lambda i, j: (i, 0)),
                  pl.BlockSpec((D_MODEL, FF_TILE), lambda i, j: (0, j)),
                  pl.BlockSpec((D_MODEL, FF_TILE), lambda i, j: (0, j)),
                  pl.BlockSpec((FF_TILE, D_MODEL), lambda i, j: (j, 0))],
        out_specs=pl.BlockSpec((tm, D_MODEL), lambda i, j: (i, 0)),
        out_shape=jax.ShapeDtypeStruct((r, D_MODEL), F32),
        scratch_shapes=[pltpu.VMEM((tm, D_MODEL), F32)],
        input_output_aliases={1: 0},
        compiler_params=_params("parallel", "arbitrary"),
        name="ffn",
    )(hn, h, wg, wu, wd)


MOE_BLOCK = 512
GATHER_TILE = 256


def _route_meta(route, cnt, r):
    bm = MOE_BLOCK
    nb = 2 * r // bm + N_EXPERTS
    e1 = route[:, _RT_E1].astype(jnp.int32)
    e2 = route[:, _RT_E2].astype(jnp.int32)
    counts = cnt[0, :N_EXPERTS].astype(jnp.int32)
    padded = ((counts + bm - 1) // bm) * bm
    ends = jnp.cumsum(padded)
    starts = ends - padded
    pos1 = starts[e1] + route[:, _RT_R1].astype(jnp.int32)
    pos2 = starts[e2] + route[:, _RT_R2].astype(jnp.int32)
    n_used = (ends[-1] // bm).astype(jnp.int32)
    blk = jnp.arange(nb, dtype=jnp.int32)
    blk = jnp.minimum(blk, n_used - 1)
    block_expert = jnp.sum((blk[:, None] * bm >= ends[None, :]).astype(jnp.int32), axis=1)
    block_expert = jnp.minimum(block_expert, N_EXPERTS - 1).astype(jnp.int32)
    return pos1, pos2, block_expert, n_used.reshape(1), nb


def _dispatch_kernel(p1_ref, p2_ref, hn_ref, xs_in_ref, xs_ref, sem):
    del xs_in_ref
    n = hn_ref.shape[0]

    def issue(r, c):
        src = hn_ref.at[pl.ds(r, 1), :]
        pltpu.make_async_copy(src, xs_ref.at[pl.ds(p1_ref[0, 0, r], 1), :], sem).start()
        pltpu.make_async_copy(src, xs_ref.at[pl.ds(p2_ref[0, 0, r], 1), :], sem).start()
        return c

    lax.fori_loop(0, n, issue, 0, unroll=8)

    def drain(r, c):
        row = pltpu.make_async_copy(hn_ref.at[pl.ds(0, 1), :], xs_ref.at[pl.ds(0, 1), :], sem)
        row.wait()
        row.wait()
        return c

    lax.fori_loop(0, n, drain, 0, unroll=8)


def _dispatch(hn, pos1, pos2, nb):
    r = hn.shape[0]
    t = GATHER_TILE
    p = nb * MOE_BLOCK
    idx = pl.BlockSpec((1, 1, t), lambda i: (i, 0, 0), memory_space=pltpu.SMEM)
    return pl.pallas_call(
        _dispatch_kernel,
        grid=(r // t,),
        in_specs=[idx, idx, pl.BlockSpec((t, D_MODEL), lambda i: (i, 0)), pl.BlockSpec(memory_space=pl.ANY)],
        out_specs=pl.BlockSpec(memory_space=pl.ANY),
        out_shape=jax.ShapeDtypeStruct((p, D_MODEL), F32),
        scratch_shapes=[pltpu.SemaphoreType.DMA(())],
        input_output_aliases={3: 0},
        compiler_params=_params("arbitrary"),
        name="moe_dispatch",
    )(pos1.reshape(r // t, 1, t), pos2.reshape(r // t, 1, t), hn, jnp.zeros((p, D_MODEL), F32))


def _moe_ffn_kernel(be_ref, nu_ref, x_ref, wg_ref, wu_ref, wd_ref, o_ref, acc_ref):
    del be_ref
    b = pl.program_id(0)
    j = pl.program_id(1)
    last = j == pl.num_programs(1) - 1
    used = b < nu_ref[0]

    @pl.when(used)
    def _():
        x = x_ref[...].astype(BF16)
        a = _silu(jnp.dot(x, wg_ref[...], preferred_element_type=F32)) * jnp.dot(x, wu_ref[...], preferred_element_type=F32)
        y = jnp.dot(a.astype(BF16), wd_ref[...], preferred_element_type=F32)

        @pl.when(j == 0)
        def _():
            acc_ref[...] = y

        @pl.when(j != 0)
        def _():
            acc_ref[...] += y

        @pl.when(last)
        def _():
            o_ref[...] = acc_ref[...]

    @pl.when(jnp.logical_not(used) & last)
    def _():
        o_ref[...] = jnp.zeros_like(o_ref)


def _moe_ffn(xs, block_expert, n_used, wg, wu, wd):
    p = xs.shape[0]
    bm = MOE_BLOCK
    nf = D_FF // FF_TILE
    grid_spec = pltpu.PrefetchScalarGridSpec(
        num_scalar_prefetch=2,
        grid=(p // bm, nf),
        in_specs=[pl.BlockSpec((bm, D_MODEL), lambda b, j, be, nu: (b, 0)),
                  pl.BlockSpec((None, D_MODEL, FF_TILE), lambda b, j, be, nu: (be[b], 0, j)),
                  pl.BlockSpec((None, D_MODEL, FF_TILE), lambda b, j, be, nu: (be[b], 0, j)),
                  pl.BlockSpec((None, FF_TILE, D_MODEL), lambda b, j, be, nu: (be[b], j, 0))],
        out_specs=pl.BlockSpec((bm, D_MODEL), lambda b, j, be, nu: (b, 0)),
        scratch_shapes=[pltpu.VMEM((bm, D_MODEL), F32)])
    return pl.pallas_call(
        _moe_ffn_kernel,
        grid_spec=grid_spec,
        out_shape=jax.ShapeDtypeStruct((p, D_MODEL), F32),
        compiler_params=_params("arbitrary", "arbitrary"),
        name="moe_ffn",
    )(block_expert, n_used, xs, wg, wu, wd)


def _combine_kernel(p1_ref, p2_ref, route_ref, h_ref, g_ref, ys_ref, o_ref, buf1, buf2, sems, *, final):
    n = h_ref.shape[0]
    half = n // 2

    def issue(k):
        def body(r, c):
            pltpu.make_async_copy(ys_ref.at[pl.ds(p1_ref[0, 0, r], 1), :], buf1.at[pl.ds(r, 1), :], sems.at[k]).start()
            pltpu.make_async_copy(ys_ref.at[pl.ds(p2_ref[0, 0, r], 1), :], buf2.at[pl.ds(r, 1), :], sems.at[k]).start()
            return c
        lax.fori_loop(k * half, (k + 1) * half, body, 0, unroll=8)

    def drain(k):
        def body(r, c):
            row = pltpu.make_async_copy(ys_ref.at[pl.ds(0, 1), :], buf1.at[pl.ds(0, 1), :], sems.at[k])
            row.wait()
            row.wait()
            return c
        lax.fori_loop(0, half, body, 0, unroll=8)

    issue(0)
    issue(1)
    for k in range(2):
        drain(k)
        rows = slice(k * half, (k + 1) * half)
        rt = route_ref[rows, :]
        lane = lax.broadcasted_iota(jnp.int32, rt.shape, 1)
        w1 = jnp.sum(jnp.where(lane == _RT_W1, rt, 0.0), axis=-1, keepdims=True)
        w2 = jnp.sum(jnp.where(lane == _RT_W2, rt, 0.0), axis=-1, keepdims=True)
        h = h_ref[rows, :] + w1 * buf1[rows, :] + w2 * buf2[rows, :]
        if final:
            ms = jnp.mean(h * h, axis=-1, keepdims=True)
            h = h * lax.rsqrt(ms + NORM_EPS) * g_ref[...]
        o_ref[rows, :] = h


def _combine(ys, pos1, pos2, route, h, gamma, batch, lp, final):
    r = h.shape[0]
    t = GATHER_TILE
    p1 = pos1.reshape(r // t, 1, t)
    p2 = pos2.reshape(r // t, 1, t)
    scratch = [pltpu.VMEM((t, D_MODEL), F32), pltpu.VMEM((t, D_MODEL), F32), pltpu.SemaphoreType.DMA((2,))]
    if final:
        per_seq = lp // t
        skip = FRONT // t
        rb = lambda b, i: b * per_seq + skip + i
        grid = (batch, per_seq - skip)
        idx = pl.BlockSpec((1, 1, t), lambda b, i: (rb(b, i), 0, 0), memory_space=pltpu.SMEM)
        in_specs = [idx, idx, pl.BlockSpec((t, LANES), lambda b, i: (rb(b, i), 0)),
                    pl.BlockSpec((t, D_MODEL), lambda b, i: (rb(b, i), 0)),
                    pl.BlockSpec((1, D_MODEL), lambda b, i: (0, 0)), pl.BlockSpec(memory_space=pl.ANY)]
        out_specs = pl.BlockSpec((None, t, D_MODEL), lambda b, i: (b, i, 0))
        out_shape = jax.ShapeDtypeStruct((batch, lp - FRONT, D_MODEL), F32)
        sem = ("arbitrary", "arbitrary")
    else:
        grid = (r // t,)
        idx = pl.BlockSpec((1, 1, t), lambda i: (i, 0, 0), memory_space=pltpu.SMEM)
        in_specs = [idx, idx, pl.BlockSpec((t, LANES), lambda i: (i, 0)), pl.BlockSpec((t, D_MODEL), lambda i: (i, 0)),
                    pl.BlockSpec((1, D_MODEL), lambda i: (0, 0)), pl.BlockSpec(memory_space=pl.ANY)]
        out_specs = pl.BlockSpec((t, D_MODEL), lambda i: (i, 0))
        out_shape = jax.ShapeDtypeStruct((r, D_MODEL), F32)
        sem = ("arbitrary",)
    return pl.pallas_call(
        functools.partial(_combine_kernel, final=final),
        grid=grid, in_specs=in_specs, out_specs=out_specs, out_shape=out_shape, scratch_shapes=scratch,
        compiler_params=_params(*sem),
        name="moe_combine_final" if final else "moe_combine",
    )(p1, p2, route, h, gamma, ys)


def _final_norm_kernel(h_ref, g_ref, o_ref):
    h = h_ref[...]
    ms = jnp.mean(h * h, axis=-1, keepdims=True)
    o_ref[...] = h * lax.rsqrt(ms + NORM_EPS) * g_ref[...]


def _final_norm(h, gamma, batch, lp):
    t = GATHER_TILE
    per_seq = lp // t
    skip = FRONT // t
    return pl.pallas_call(
        _final_norm_kernel,
        grid=(batch, per_seq - skip),
        in_specs=[pl.BlockSpec((t, D_MODEL), lambda b, i: (b * per_seq + skip + i, 0)),
                  pl.BlockSpec((1, D_MODEL), lambda b, i: (0, 0))],
        out_specs=pl.BlockSpec((None, t, D_MODEL), lambda b, i: (b, i, 0)),
        out_shape=jax.ShapeDtypeStruct((batch, lp - FRONT, D_MODEL), F32),
        compiler_params=_params("parallel", "parallel"),
        name="final_norm",
    )(h, gamma)


def _rope_tables(lp):
    half = RET_DIM // 2
    pos = jnp.arange(lp, dtype=F32) - float(PAD)
    inv = ROPE_BASE ** (-jnp.arange(half, dtype=F32) / half)
    ang = pos[:, None] * inv[None, :]
    cos = jnp.tile(jnp.cos(ang), (1, 2 * RET_HEADS))
    sin = jnp.tile(jnp.sin(ang), (1, 2 * RET_HEADS))
    return cos, sin


def _prep_layer(li, p):
    w_out = p["w_out"][li].astype(F32)
    wc = w_out[640:1024].reshape(GLA_HEADS, GLA_DV, D_MODEL)
    wc = jnp.pad(wc, ((0, 0), (0, GLA_DV_PAD - GLA_DV), (0, 0))).reshape(GLA_HEADS * GLA_DV_PAD, D_MODEL)
    wgate, bgate = _pack_gate(p["gla_w_gate_f"][li], p["gla_b_gate_f"][li], p["gla_w_gate_b"][li], p["gla_b_gate_b"][li])
    return dict(
        norm_mix=p["norm_mix"][li].astype(F32)[None],
        w_in=_pack_w_in(p["w_in"][li].astype(F32)),
        wgate=wgate, bgate=bgate,
        s5=_s5_tables(p["s5_lambda_re"][li], p["s5_lambda_im"][li], p["s5_log_dt"][li], p["s5_b_re"][li],
                      p["s5_b_im"][li], p["s5_c_re"][li], p["s5_c_im"][li], p["s5_d"][li]),
        w_glu=p["s5_w_glu"][li].astype(BF16),
        w_out=jnp.concatenate([w_out[0:640], wc], axis=0).astype(BF16),
        norm_ffn=p["norm_ffn"][li].astype(F32)[None],
    )


def _trunk(x, meta_tokens, layers, ffn, moe, norm_final, depth):
    batch, seq, _ = x.shape
    lp = seq + FRONT
    r = batch * lp
    meta = jnp.broadcast_to(meta_tokens.astype(F32)[None], (batch, N_META, D_MODEL))
    h = jnp.concatenate([jnp.zeros((batch, PAD, D_MODEL), F32), meta, x.astype(F32)], axis=1).reshape(r, D_MODEL)
    cos, sin = _rope_tables(lp)
    cos = jnp.tile(cos, (batch, 1))
    sin = jnp.tile(sin, (batch, 1))
    log_gamma = jnp.log1p(-jnp.exp2(-5.0 - jnp.arange(RET_HEADS, dtype=F32)))
    lg = jnp.repeat(log_gamma, RET_DIM)[None]
    gamma_final = norm_final.astype(F32)[None]
    for li in range(depth):
        lw = layers[li]
        u, rq, rk, rv, rg, gq, gk, gv, gg, lf, lb = _inproj(
            h, lw["norm_mix"], lw["w_in"], cos, sin, lw["wgate"], lw["bgate"], batch, lp)
        ya = _s5_mixer(u, lw["s5"], batch, lp)
        yb = _linear_attention(rq, rk, rv, lg, lg, rg, batch, lp, npair=RET_HEADS // 2, vp=LANES,
                               const_decay=True, strict_upper=True, dv=RET_DIM)
        yc = _linear_attention(gq, gk, gv, lf, lb, gg, batch, lp, npair=GLA_HEADS // 2, vp=2 * GLA_DV_PAD,
                               const_decay=False, strict_upper=False, dv=GLA_DV)
        j = li // 2
        last = li == depth - 1
        if li % 2 == 0:
            h, hn = _outproj(ya, yb, yc, h, lw["w_glu"], lw["w_out"], lw["norm_ffn"])
            h = _ffn(hn, h, ffn["wg"][j], ffn["wu"][j], ffn["wd"][j])
        else:
            h, hn, route, cnt = _outproj(ya, yb, yc, h, lw["w_glu"], lw["w_out"], lw["norm_ffn"], moe["wr"][j])
            pos1, pos2, block_expert, n_used, nb = _route_meta(route, cnt, r)
            xs = _dispatch(hn, pos1, pos2, nb)
            ys = _moe_ffn(xs, block_expert, n_used, moe["wg"][j], moe["wu"][j], moe["wd"][j])
            h = _combine(ys, pos1, pos2, route, h, gamma_final, batch, lp, final=last)
            if last:
                return h
    return _final_norm(h, gamma_final, batch, lp)


def kernel(x_prompt, x_sample, meta_tokens, norm_mix, w_in, s5_lambda_re, s5_lambda_im, s5_log_dt, s5_b_re, s5_b_im, s5_c_re, s5_c_im, s5_d, s5_w_glu, gla_w_gate_f, gla_b_gate_f, gla_w_gate_b, gla_b_gate_b, w_out, norm_ffn, ffn_w_gate, ffn_w_up, ffn_w_down, router_w, moe_w_gate, moe_w_up, moe_w_down, norm_final):
    depth = w_in.shape[0]
    p = dict(norm_mix=norm_mix, w_in=w_in, s5_lambda_re=s5_lambda_re, s5_lambda_im=s5_lambda_im,
             s5_log_dt=s5_log_dt, s5_b_re=s5_b_re, s5_b_im=s5_b_im, s5_c_re=s5_c_re, s5_c_im=s5_c_im, s5_d=s5_d,
             s5_w_glu=s5_w_glu, gla_w_gate_f=gla_w_gate_f, gla_b_gate_f=gla_b_gate_f, gla_w_gate_b=gla_w_gate_b,
             gla_b_gate_b=gla_b_gate_b, w_out=w_out, norm_ffn=norm_ffn)
    layers = [_prep_layer(li, p) for li in range(depth)]
    ffn = dict(wg=ffn_w_gate.astype(BF16), wu=ffn_w_up.astype(BF16), wd=ffn_w_down.astype(BF16))
    wr = jnp.pad(router_w.astype(F32), ((0, 0), (0, 0), (0, LANES - N_EXPERTS)))
    wr_hi = wr.astype(BF16)
    wr_lo = (wr - wr_hi.astype(F32)).astype(BF16)
    wr = [(wr_hi[j], wr_lo[j]) for j in range(wr.shape[0])]
    moe = dict(wr=wr, wg=moe_w_gate.astype(BF16), wu=moe_w_up.astype(BF16), wd=moe_w_down.astype(BF16))
    y_prompt = _trunk(x_prompt, meta_tokens, layers, ffn, moe, norm_final, depth)
    y_sample = _trunk(x_sample, meta_tokens, layers, ffn, moe, norm_final, depth)
    return (y_prompt, y_sample)
```

```python
import functools
import math

import jax
import jax.numpy as jnp
from jax import lax
from jax.experimental import pallas as pl
from jax.experimental.pallas import tpu as pltpu

F32 = jnp.float32
BF16 = jnp.bfloat16

D_MODEL = 1024
N_META = 16
S5_P = 16
S5_WIDTH = 256
S5_GROUPS = 16
S5_N = 64
RET_HEADS = 6
RET_DIM = 64
RET_WIDTH = 384
GLA_HEADS = 4
GLA_DV = 96
GLA_DK = 48
GLA_QK = 192
GLA_WIDTH = 384
GLA_GATE_RANK = 16
GLA_TAU = 16.0
ROPE_BASE = 10000.0
D_FF = 2816
N_EXPERTS = 8
NORM_EPS = 1e-5

CHUNK = 64
FRONT = 256
PAD = FRONT - N_META
ROW_TILE = 512
LA_GROUP = 4
LANES = 128
GLA_DK_PAD = 64
GLA_DV_PAD = 128
VMEM_LIMIT = 56 * 1024 * 1024


def _params(*sem):
    return pltpu.CompilerParams(dimension_semantics=sem, vmem_limit_bytes=VMEM_LIMIT)


def _const_spec(shape):
    nd = len(shape)
    return pl.BlockSpec(shape, lambda *_: (0,) * nd)


def _sigmoid(x):
    return 1.0 / (1.0 + jnp.exp(-x))


def _silu(x):
    return x * _sigmoid(x)


def _gelu_tanh(x):
    c = math.sqrt(2.0 / math.pi)
    return 0.5 * x * (1.0 + jnp.tanh(c * (x + 0.044715 * (x * x * x))))


def _log_sigmoid(z):
    return jnp.minimum(z, 0.0) - jnp.log(1.0 + jnp.exp(-jnp.abs(z)))


def _split_dot(a, b_bf16, dims=None):
    hi = a.astype(BF16)
    lo = (a - hi.astype(F32)).astype(BF16)
    if dims is None:
        return (jnp.dot(hi, b_bf16, preferred_element_type=F32)
                + jnp.dot(lo, b_bf16, preferred_element_type=F32))
    return (lax.dot_general(hi, b_bf16, dims, preferred_element_type=F32)
            + lax.dot_general(lo, b_bf16, dims, preferred_element_type=F32))


_C_U = 0
_C_RQ = 256
_C_RQR = 640
_C_RK = 1024
_C_RKR = 1408
_C_RV = 1792
_C_RG = 2176
_C_GQ = 2560
_C_GK = 2816
_C_GV = 3072
_C_GG = 3584
_C_GL = 4096
_C_END = 4224


def _pack_w_in(w):
    o = 0
    u = w[:, o:o + 256]; o += 256
    rq = w[:, o:o + 384]; o += 384
    rk = w[:, o:o + 384]; o += 384
    rv = w[:, o:o + 384]; o += 384
    rg = w[:, o:o + 384]; o += 384
    gq = w[:, o:o + 192]; o += 192
    gk = w[:, o:o + 192]; o += 192
    gv = w[:, o:o + 384]; o += 384
    gg = w[:, o:o + 384]; o += 384
    glf = w[:, o:o + 16]; o += 16
    glb = w[:, o:o + 16]; o += 16

    def rot(m):
        m = m.reshape(D_MODEL, RET_HEADS, 2, RET_DIM // 2)
        return jnp.stack([-m[:, :, 1], m[:, :, 0]], axis=2).reshape(D_MODEL, RET_WIDTH)

    def padh(m, d, dp):
        m = m.reshape(D_MODEL, GLA_HEADS, d)
        return jnp.pad(m, ((0, 0), (0, 0), (0, dp - d))).reshape(D_MODEL, GLA_HEADS * dp)

    gl = jnp.pad(jnp.concatenate([glf, glb], axis=1), ((0, 0), (0, LANES - 2 * GLA_GATE_RANK)))
    cat = jnp.concatenate([
        u, rq, rot(rq), rk, rot(rk), rv, rg,
        padh(gq, GLA_DK, GLA_DK_PAD), padh(gk, GLA_DK, GLA_DK_PAD),
        padh(gv, GLA_DV, GLA_DV_PAD), padh(gg, GLA_DV, GLA_DV_PAD), gl], axis=1)
    return cat.astype(BF16)


def _pack_gate(w_f, b_f, w_b, b_b):
    def padh(m):
        m = m.reshape(m.shape[0], GLA_HEADS, GLA_DK)
        return jnp.pad(m, ((0, 0), (0, 0), (0, GLA_DK_PAD - GLA_DK))).reshape(m.shape[0], GLA_HEADS * GLA_DK_PAD)
    r = GLA_GATE_RANK
    w = jnp.zeros((LANES, 2 * GLA_HEADS * GLA_DK_PAD), F32)
    w = w.at[0:r, 0:256].set(padh(w_f.astype(F32)))
    w = w.at[r:2 * r, 256:512].set(padh(w_b.astype(F32)))
    b = jnp.concatenate([padh(b_f.astype(F32)[None]), padh(b_b.astype(F32)[None])], axis=1)
    return w.astype(BF16), b


def _inproj_kernel(h_ref, g_ref, w_ref, cos_ref, sin_ref, wgate_ref, bgate_ref,
                   u_ref, rq_ref, rk_ref, rv_ref, rg_ref, gq_ref, gk_ref, gv_ref, gg_ref, lf_ref, lb_ref,
                   *, batch, lp):
    tm = h_ref.shape[0]
    x = h_ref[...]
    ms = jnp.mean(x * x, axis=-1, keepdims=True)
    row = pl.program_id(0) * tm + lax.broadcasted_iota(jnp.int32, (tm, 1), 0)
    valid = jnp.ones((tm, 1), F32)
    for b in range(batch):
        valid = jnp.where((row >= b * lp) & (row < b * lp + PAD), 0.0, valid)
    hn = (x * (lax.rsqrt(ms + NORM_EPS) * valid) * g_ref[...]).astype(BF16)

    def proj(lo, hi):
        return jnp.dot(hn, w_ref[:, lo:hi], preferred_element_type=F32)

    u_ref[...] = proj(_C_U, _C_RQ)
    cos = cos_ref[...]
    sin = sin_ref[...]
    rq_ref[...] = (proj(_C_RQ, _C_RQR) * cos + proj(_C_RQR, _C_RK) * sin).astype(BF16)
    rk_ref[...] = ((proj(_C_RK, _C_RKR) * cos + proj(_C_RKR, _C_RV) * sin) * (RET_DIM ** -0.5)).astype(BF16)
    rv_ref[...] = proj(_C_RV, _C_RG).astype(BF16)
    rg_ref[...] = proj(_C_RG, _C_GQ).astype(BF16)
    gq_ref[...] = (proj(_C_GQ, _C_GK) * (GLA_DK ** -0.5)).astype(BF16)
    gk_ref[...] = proj(_C_GK, _C_GV).astype(BF16)
    gv_ref[...] = proj(_C_GV, _C_GG).astype(BF16)
    gg_ref[...] = proj(_C_GG, _C_GL).astype(BF16)
    codes = proj(_C_GL, _C_END).astype(BF16)
    z = jnp.dot(codes, wgate_ref[...], preferred_element_type=F32) + bgate_ref[...]
    ls = _log_sigmoid(z) * (1.0 / GLA_TAU)
    lf_ref[...] = ls[:, 0:256]
    lb_ref[...] = ls[:, 256:512]


def _inproj(h, gamma, w, cos, sin, wgate, bgate, batch, lp):
    r = h.shape[0]
    tm = ROW_TILE
    widths = (256, 384, 384, 384, 384, 256, 256, 512, 512, 256, 256)
    dtypes = (F32,) + (BF16,) * 8 + (F32, F32)

    def rows(wd):
        return pl.BlockSpec((tm, wd), lambda i: (i, 0))

    return pl.pallas_call(
        functools.partial(_inproj_kernel, batch=batch, lp=lp),
        grid=(r // tm,),
        in_specs=[rows(D_MODEL), _const_spec((1, D_MODEL)), _const_spec((D_MODEL, _C_END)),
                  rows(RET_WIDTH), rows(RET_WIDTH), _const_spec((LANES, 512)), _const_spec((1, 512))],
        out_specs=[rows(wd) for wd in widths],
        out_shape=[jax.ShapeDtypeStruct((r, wd), dt) for wd, dt in zip(widths, dtypes)],
        compiler_params=_params("parallel"),
        name="inproj",
    )(h, gamma, w, cos, sin, wgate, bgate)


def _s5_toeplitz_kernel(pwr_ref, pwi_ref, cer_ref, cei_ref, bbr_ref, bbi_ref, d_ref, tt_ref):
    hp = lax.Precision.HIGHEST
    w = CHUNK * S5_P
    lane = lax.broadcasted_iota(jnp.int32, (S5_P, w), 1)
    krow = []
    for dr in range(2):
        pr, pi, cr, ci = pwr_ref[dr], pwi_ref[dr], cer_ref[dr], cei_ref[dr]
        zr = pr * cr - pi * ci
        zi = pr * ci + pi * cr
        krow.append(jnp.dot(bbr_ref[dr], zr, precision=hp, preferred_element_type=F32)
                    - jnp.dot(bbi_ref[dr], zi, precision=hp, preferred_element_type=F32))
    kf = krow[0] + d_ref[...]
    kb = krow[1]
    for s in range(CHUNK):
        right = S5_P * s
        left = S5_P * (CHUNK - 1 - s)
        a = kf if right == 0 else jnp.where(lane >= right, pltpu.roll(kf, right, 1), 0.0)
        b = kb if left == 0 else jnp.where(lane < w - left, pltpu.roll(kb, w - left, 1), 0.0)
        tt_ref[S5_P * s:S5_P * (s + 1), :] = (a + b).astype(BF16)


def _s5_toeplitz(pwr, pwi, cer, cei, bbr, bbi, d_e):
    w = CHUNK * S5_P
    big = pl.BlockSpec((None, 2, S5_N, w), lambda g: (g, 0, 0, 0))
    small = pl.BlockSpec((None, 2, S5_P, S5_N), lambda g: (g, 0, 0, 0))
    return pl.pallas_call(
        _s5_toeplitz_kernel,
        grid=(S5_GROUPS,),
        in_specs=[big, big, big, big, small, small, pl.BlockSpec((None, S5_P, w), lambda g: (g, 0, 0))],
        out_specs=pl.BlockSpec((None, w, w), lambda g: (g, 0, 0)),
        out_shape=jax.ShapeDtypeStruct((S5_GROUPS, w, w), BF16),
        compiler_params=_params("parallel"),
        name="s5_toeplitz",
    )(pwr, pwi, cer, cei, bbr, bbi, d_e)


def _s5_tables(lam_re, lam_im, log_dt, b_re, b_im, c_re, c_im, d):
    c = CHUNK
    g_, n_, p_ = S5_GROUPS, S5_N, S5_P
    dt = jnp.exp(log_dt.astype(F32))[..., None]
    lr = lam_re.astype(F32)
    li = lam_im.astype(F32)
    e = lr * dt
    th = li * dt
    mag = jnp.exp(e)
    a_re = mag * jnp.cos(th)
    a_im = mag * jnp.sin(th)
    den = lr * lr + li * li
    nr = a_re - 1.0
    ni = a_im
    coef_re = (nr * lr + ni * li) / den
    coef_im = (ni * lr - nr * li) / den
    br = b_re.astype(F32)
    bi = b_im.astype(F32)
    bb_re = coef_re[..., None] * br - coef_im[..., None] * bi
    bb_im = coef_re[..., None] * bi + coef_im[..., None] * br
    cr = c_re.astype(F32)
    ci = c_im.astype(F32)
    tau = jnp.arange(c + 1, dtype=F32)[:, None, None, None]
    pw_mag = jnp.exp(tau * e[None])
    pw_re = pw_mag * jnp.cos(tau * th[None])
    pw_im = pw_mag * jnp.sin(tau * th[None])

    z_re = cr[None] * pw_re[:, :, :, None, :] - ci[None] * pw_im[:, :, :, None, :]
    z_im = cr[None] * pw_im[:, :, :, None, :] + ci[None] * pw_re[:, :, :, None, :]
    idx = jnp.arange(c)

    def expand_pw(pw):
        both = jnp.stack([pw[:c, 0], pw[c - 1 - idx, 1]], axis=0)
        both = jnp.transpose(both, (2, 0, 3, 1))
        return jnp.broadcast_to(both[..., None], (g_, 2, n_, c, p_)).reshape(g_, 2, n_, c * p_)

    def expand_c(cm):
        cm = jnp.transpose(cm, (1, 0, 3, 2))
        return jnp.broadcast_to(cm[:, :, :, None, :], (g_, 2, n_, c, p_)).reshape(g_, 2, n_, c * p_)

    bbt_re = jnp.transpose(bb_re, (1, 0, 3, 2))
    bbt_im = jnp.transpose(bb_im, (1, 0, 3, 2))
    d_e = jnp.zeros((g_, p_, c * p_), F32).at[:, :, :p_].set(d.astype(F32)[:, :, None] * jnp.eye(p_, dtype=F32))
    tt = _s5_toeplitz(expand_pw(pw_re), expand_pw(pw_im), expand_c(cr), expand_c(ci), bbt_re, bbt_im, d_e)

    pf_re = pw_re[c - 1 - idx, 0]
    pf_im = pw_im[c - 1 - idx, 0]
    pb_re = pw_re[idx, 1]
    pb_im = pw_im[idx, 1]

    def m_of(p_re_, p_im_, dr):
        m_re = p_re_[:, :, :, None] * bb_re[dr][None] - p_im_[:, :, :, None] * bb_im[dr][None]
        m_im = p_re_[:, :, :, None] * bb_im[dr][None] + p_im_[:, :, :, None] * bb_re[dr][None]
        to = lambda m: jnp.transpose(m, (1, 0, 3, 2)).reshape(g_, c * p_, n_)
        return to(m_re), to(m_im)

    mf_re, mf_im = m_of(pf_re, pf_im, 0)
    mb_re, mb_im = m_of(pb_re, pb_im, 1)
    m4 = jnp.stack([mf_re, mf_im, mb_re, mb_im], axis=1)
    m4 = m4.reshape(g_ // 2, 2, 4, c * p_, n_)
    mpair = jnp.zeros((g_ // 2, 2, c * p_, 4, 2, n_), F32)
    for gi in range(2):
        mpair = mpair.at[:, gi, :, :, gi, :].set(jnp.transpose(m4[:, gi], (0, 2, 1, 3)))
    mpair = mpair.reshape(g_ // 2, 2 * c * p_, 4 * 2 * n_).astype(BF16)

    def n_of(tsel, dr):
        w_re = z_re[tsel, dr]
        w_im = z_im[tsel, dr]
        to = lambda m: jnp.transpose(m, (1, 3, 0, 2)).reshape(g_, n_, c * p_)
        return to(w_re), to(-w_im)

    nf_re, nf_im = n_of(idx + 1, 0)
    nb_re, nb_im = n_of(c - idx, 1)
    n4 = jnp.stack([nf_re, nf_im, nb_re, nb_im], axis=1)
    n4 = n4.reshape(g_ // 2, 2, 4, n_, c * p_)
    npair = jnp.zeros((g_ // 2, 4, 2, n_, 2, c * p_), F32)
    for gi in range(2):
        npair = npair.at[:, :, gi, :, gi, :].set(n4[:, gi])
    npair = npair.reshape(g_ // 2, 4 * 2 * n_, 2 * c * p_).astype(BF16)

    dec = jnp.stack([pw_re[c, 0], pw_im[c, 0], pw_re[c, 1], pw_im[c, 1]], axis=0)
    dec = dec.reshape(4, 1, g_ * n_)
    return tt, mpair, npair, dec


S5_PITCH = CHUNK + 4
S5_HALVES = S5_WIDTH // LANES
S5_GPH = LANES // S5_P


def _s5_chunk_block(nch):
    best = 8
    for cand in range(8, 113, 8):
        if nch % cand == 0:
            best = cand
    return best


def _lane_window(cb, k):
    lane = lax.broadcasted_iota(jnp.int32, (cb, LANES), 1)
    return (lane >= S5_P * k) & (lane < S5_P * (k + 1))


def _s5_in_kernel(u_hbm, m_ref, ucat_ref, o0, o1, o2, o3, xpad, uall, sem):
    i = pl.program_id(0)
    j = pl.program_id(1)
    cb = uall.shape[1]

    @pl.when(j == 0)
    def _():
        def chunk_copy(c, h):
            return pltpu.make_async_copy(
                u_hbm.at[pl.ds((i * cb + c) * CHUNK, CHUNK), pl.ds(h * LANES, LANES)],
                xpad.at[h, pl.ds(c * S5_PITCH, CHUNK), :], sem)

        def start(c, carry):
            for h in range(S5_HALVES):
                chunk_copy(c, h).start()
            return carry

        def wait(c, carry):
            for h in range(S5_HALVES):
                chunk_copy(c, h).wait()
            return carry

        lax.fori_loop(0, cb, start, 0)
        lax.fori_loop(0, cb, wait, 0)

        def dest_tile(jt, carry):
            col = pl.multiple_of(jt * LANES, LANES)
            for g in range(S5_GROUPS):
                h, go = divmod(g, S5_GPH)
                acc = jnp.zeros((cb, LANES), F32)
                for k in range(S5_GPH):
                    src = xpad[h, pl.ds(jt * S5_GPH + k, cb, stride=S5_PITCH), :]
                    shift = (S5_P * (k - go)) % LANES
                    if shift:
                        src = pltpu.roll(src, shift, 1)
                    acc = jnp.where(_lane_window(cb, k), src, acc)
                uall[g, :, pl.ds(col, LANES)] = acc
            return carry

        lax.fori_loop(0, CHUNK // S5_GPH, dest_tile, 0)

    u2 = jnp.concatenate([uall[2 * j], uall[2 * j + 1]], axis=1)
    ucat_ref[...] = u2
    s = jnp.dot(u2.astype(BF16), m_ref[...], preferred_element_type=F32)
    o0[...] = s[:, 0:128]
    o1[...] = s[:, 128:256]
    o2[...] = s[:, 256:384]
    o3[...] = s[:, 384:512]


def _s5_in(u, mpair):
    nch = u.shape[0] // CHUNK
    cb = _s5_chunk_block(nch)
    w = CHUNK * S5_P
    return pl.pallas_call(
        _s5_in_kernel,
        grid=(nch // cb, S5_GROUPS // 2),
        in_specs=[pl.BlockSpec(memory_space=pl.ANY),
                  pl.BlockSpec((None, 2 * w, 512), lambda i, j: (j, 0, 0))],
        out_specs=[pl.BlockSpec((cb, 2 * w), lambda i, j: (i, j))] + [pl.BlockSpec((cb, LANES), lambda i, j: (i, j))] * 4,
        out_shape=[jax.ShapeDtypeStruct((nch, S5_GROUPS * w), F32)] + [jax.ShapeDtypeStruct((nch, 1024), F32)] * 4,
        scratch_shapes=[pltpu.VMEM((S5_HALVES, cb * S5_PITCH, LANES), F32), pltpu.VMEM((S5_GROUPS, cb, w), F32),
                        pltpu.SemaphoreType.DMA(())],
        compiler_params=_params("arbitrary", "arbitrary"),
        name="s5_in",
    )(u, mpair)


def _s5_scan_kernel(sfr, sfi, sbr, sbi, dfr, dfi, dbr, dbi, hfr, hfi, hbr, hbi):
    nc, b, _ = sfr.shape
    a_fr = dfr[...]
    a_fi = dfi[...]
    a_br = dbr[...]
    a_bi = dbi[...]
    zero = jnp.zeros((b, LANES), F32)

    def fwd(c, carry):
        hr, hi = carry
        hfr[c] = hr
        hfi[c] = hi
        return (a_fr * hr - a_fi * hi + sfr[c], a_fr * hi + a_fi * hr + sfi[c])

    def bwd(i, carry):
        c = nc - 1 - i
        hr, hi = carry
        hbr[c] = hr
        hbi[c] = hi
        return (a_br * hr - a_bi * hi + sbr[c], a_br * hi + a_bi * hr + sbi[c])

    lax.fori_loop(0, nc, fwd, (zero, zero))
    lax.fori_loop(0, nc, bwd, (zero, zero))


def _s5_scan(s4, dec, nc, b):
    blk = pl.BlockSpec((nc, b, LANES), lambda j: (0, 0, j))
    dspecs = [pl.BlockSpec((None, 1, LANES), functools.partial(lambda j, k: (k, 0, j), k=k)) for k in range(4)]
    return pl.pallas_call(
        _s5_scan_kernel,
        grid=(S5_GROUPS // 2,),
        in_specs=[blk] * 4 + dspecs,
        out_specs=[blk] * 4,
        out_shape=[jax.ShapeDtypeStruct((nc, b, 1024), F32)] * 4,
        compiler_params=_params("parallel"),
        name="s5_scan",
    )(*s4, dec, dec, dec, dec)


def _s5_out_kernel(u_ref, tt_ref, h0, h1, h2, h3, n_ref, y_hbm, yall, ypad, sem):
    i = pl.program_id(0)
    j = pl.program_id(1)
    cb = yall.shape[1]
    w = CHUNK * S5_P
    u = u_ref[...].astype(BF16)
    y0 = jnp.dot(u[:, 0:w], tt_ref[0], preferred_element_type=F32)
    y1 = jnp.dot(u[:, w:2 * w], tt_ref[1], preferred_element_type=F32)
    hcat = jnp.concatenate([h0[...], h1[...], h2[...], h3[...]], axis=1).astype(BF16)
    yh = jnp.dot(hcat, n_ref[...], preferred_element_type=F32)
    yall[2 * j] = y0 + yh[:, 0:w]
    yall[2 * j + 1] = y1 + yh[:, w:2 * w]

    @pl.when(j == pl.num_programs(1) - 1)
    def _():
        def src_tile(jt, carry):
            col = pl.multiple_of(jt * LANES, LANES)
            for k in range(S5_GPH):
                for h in range(S5_HALVES):
                    acc = jnp.zeros((cb, LANES), F32)
                    for go in range(S5_GPH):
                        src = yall[h * S5_GPH + go, :, pl.ds(col, LANES)]
                        shift = (S5_P * (go - k)) % LANES
                        if shift:
                            src = pltpu.roll(src, shift, 1)
                        acc = jnp.where(_lane_window(cb, go), src, acc)
                    ypad[h, pl.ds(jt * S5_GPH + k, cb, stride=S5_PITCH), :] = acc
            return carry

        lax.fori_loop(0, CHUNK // S5_GPH, src_tile, 0)

        def chunk_copy(c, h):
            return pltpu.make_async_copy(
                ypad.at[h, pl.ds(c * S5_PITCH, CHUNK), :],
                y_hbm.at[pl.ds((i * cb + c) * CHUNK, CHUNK), pl.ds(h * LANES, LANES)], sem)

        def start(c, carry):
            for h in range(S5_HALVES):
                chunk_copy(c, h).start()
            return carry

        def wait(c, carry):
            for h in range(S5_HALVES):
                chunk_copy(c, h).wait()
            return carry

        lax.fori_loop(0, cb, start, 0)
        lax.fori_loop(0, cb, wait, 0)


def _s5_out(ucat, tt, h4, npair):
    nch = ucat.shape[0]
    cb = _s5_chunk_block(nch)
    w = CHUNK * S5_P
    hblk = pl.BlockSpec((cb, LANES), lambda i, j: (i, j))
    return pl.pallas_call(
        _s5_out_kernel,
        grid=(nch // cb, S5_GROUPS // 2),
        in_specs=[pl.BlockSpec((cb, 2 * w), lambda i, j: (i, j)),
                  pl.BlockSpec((2, w, w), lambda i, j: (j, 0, 0)),
                  hblk, hblk, hblk, hblk,
                  pl.BlockSpec((None, 512, 2 * w), lambda i, j: (j, 0, 0))],
        out_specs=pl.BlockSpec(memory_space=pl.ANY),
        out_shape=jax.ShapeDtypeStruct((nch * CHUNK, S5_WIDTH), F32),
        scratch_shapes=[pltpu.VMEM((S5_GROUPS, cb, w), F32), pltpu.VMEM((S5_HALVES, cb * S5_PITCH, LANES), F32),
                        pltpu.SemaphoreType.DMA(())],
        compiler_params=_params("arbitrary", "arbitrary"),
        name="s5_out",
    )(ucat, tt, *h4, npair)


def _s5_mixer(u, tables, batch, lp):
    tt, mpair, npair, dec = tables
    nc = lp // CHUNK
    ucat, *s4 = _s5_in(u, mpair)
    s4 = [jnp.transpose(s.reshape(batch, nc, 1024), (1, 0, 2)) for s in s4]
    h4 = _s5_scan(s4, dec, nc, batch)
    h4 = [jnp.transpose(h, (1, 0, 2)).reshape(batch * nc, 1024) for h in h4]
    return _s5_out(ucat, tt, h4, npair)


def _cumsum_chunks(x):
    n = x.shape[0]
    r = lax.broadcasted_iota(jnp.int32, (n, n), 0)
    s = lax.broadcasted_iota(jnp.int32, (n, n), 1)
    tri = jnp.where((s <= r) & ((s // CHUNK) == (r // CHUNK)), 1.0, 0.0).astype(BF16)
    hi = x.astype(BF16)
    lo = (x - hi.astype(F32)).astype(BF16)
    return jnp.dot(tri, hi, preferred_element_type=F32) + jnp.dot(tri, lo, preferred_element_type=F32)


def _head_masks(npair, vp):
    kl = lax.broadcasted_iota(jnp.int32, (1, LANES), 1)
    km = [(kl < 64), (kl >= 64)]
    vl = lax.broadcasted_iota(jnp.int32, (1, vp), 1)
    vm = [(vl < vp // 2), (vl >= vp // 2)]
    vrow = lax.broadcasted_iota(jnp.int32, (vp, LANES), 0)
    kcol = lax.broadcasted_iota(jnp.int32, (vp, LANES), 1)
    bd = (vrow >= vp // 2) == (kcol >= 64)
    return km, vm, bd


def _la_bwd_kernel(k_ref, v_ref, lb_ref, sb_ref, st_ref, *, npair, vp, nsub):
    g = LA_GROUP
    c = CHUNK
    gc = g * c

    @pl.when(pl.program_id(1) == 0)
    def _():
        st_ref[...] = jnp.zeros_like(st_ref)

    km, vm, bd = _head_masks(npair, vp)

    def sub_block(t, carry):
        sub = nsub - 1 - t
        base = pl.multiple_of(sub * gc, gc)
        kblk = k_ref[pl.ds(base, gc), :].astype(F32)
        vblk = v_ref[pl.ds(base, gc), :]
        lblk = lb_ref[pl.ds(base, gc), :]
        cblk = _cumsum_chunks(lblk)
        kb2blk = (kblk * jnp.exp(cblk - lblk)).astype(BF16)
        states = [st_ref[p] for p in range(npair)]
        for gi in reversed(range(g)):
            rows = slice(gi * c, (gi + 1) * c)
            kb2 = kb2blk[rows, :]
            dec = jnp.exp(cblk[gi * c + c - 1:(gi + 1) * c, :])
            vv = vblk[rows, :]
            for p in range(npair):
                sb_ref[sub * g + gi, p] = states[p].astype(BF16)
                upd = lax.dot_general(vv[:, p * vp:(p + 1) * vp], kb2[:, p * LANES:(p + 1) * LANES],
                                      (((0,), (0,)), ((), ())), preferred_element_type=F32)
                states[p] = states[p] * dec[:, p * LANES:(p + 1) * LANES] + jnp.where(bd, upd, 0.0)
        for p in range(npair):
            st_ref[p] = states[p]
        return carry

    lax.fori_loop(0, nsub, sub_block, 0)


def _head_rms_gate(o, gate, bdv, dv):
    ms = _split_dot(o * o, bdv) * (1.0 / dv)
    return (o * lax.rsqrt(ms + NORM_EPS) * _silu(gate.astype(F32))).astype(BF16)


def _la_fwd_kernel(q_ref, k_ref, v_ref, lf_ref, lb_ref, sb_ref, gate_ref, bdv_ref, o_ref, st_ref,
                   *, npair, vp, dv, nsub):
    g = LA_GROUP
    c = CHUNK
    gc = g * c

    @pl.when(pl.program_id(1) == 0)
    def _():
        st_ref[...] = jnp.zeros_like(st_ref)

    km, vm, bd = _head_masks(npair, vp)
    ri = lax.broadcasted_iota(jnp.int32, (c, LANES), 0)
    cj = lax.broadcasted_iota(jnp.int32, (c, LANES), 1) & (c - 1)
    lower = ri >= cj
    upper = ri <= cj
    mid = c // 2
    nt = (((1,), (1,)), ((), ()))
    tn = (((0,), (0,)), ((), ()))

    def sub_block(sub, carry):
        base = pl.multiple_of(sub * gc, gc)
        qblk = q_ref[pl.ds(base, gc), :].astype(F32)
        kblk = k_ref[pl.ds(base, gc), :].astype(F32)
        vblk = v_ref[pl.ds(base, gc), :]
        lfblk = lf_ref[pl.ds(base, gc), :]
        lbblk = lb_ref[pl.ds(base, gc), :]
        w = lfblk.shape[1]
        csblk = _cumsum_chunks(jnp.concatenate([lfblk, lbblk], axis=1))
        states = [st_ref[p] for p in range(npair)]
        outs = []
        for gi in range(g):
            rows = slice(gi * c, (gi + 1) * c)
            qq = qblk[rows, :]
            kk = kblk[rows, :]
            vv = vblk[rows, :]
            cf = csblk[rows, :w]
            cbi = csblk[rows, w:]
            cb = cbi - lbblk[rows, :]
            mf = cf[mid:mid + 1, :]
            mb = cb[mid:mid + 1, :]
            tf = cf[c - 1:c, :]
            tb = cbi[c - 1:c, :]
            qf = (qq * jnp.exp(cf - mf)).astype(BF16)
            kf = kk * jnp.exp(mf - cf)
            qb = (qq * jnp.exp(mb - cb)).astype(BF16)
            kb = kk * jnp.exp(cb - mb)
            q2 = jnp.concatenate([qq * jnp.exp(cf), qq * jnp.exp(tb - cb)], axis=0).astype(BF16)
            kf2 = (kk * jnp.exp(tf - cf)).astype(BF16)
            dec = jnp.exp(tf)
            pair_out = []
            for p in range(npair):
                ks = slice(p * LANES, (p + 1) * LANES)
                vsl = slice(p * vp, (p + 1) * vp)
                kfp = kf[:, ks]
                kbp = kb[:, ks]
                k2f = jnp.concatenate([jnp.where(km[0], kfp, 0.0), jnp.where(km[1], kfp, 0.0)], axis=0).astype(BF16)
                k2b = jnp.concatenate([jnp.where(km[0], kbp, 0.0), jnp.where(km[1], kbp, 0.0)], axis=0).astype(BF16)
                sf = lax.dot_general(qf[:, ks], k2f, nt, preferred_element_type=F32)
                sb = lax.dot_general(qb[:, ks], k2b, nt, preferred_element_type=F32)
                sc = (jnp.where(lower, sf, 0.0) + jnp.where(upper, sb, 0.0)).astype(BF16)
                vp_ = vv[:, vsl]
                zero = jnp.zeros_like(vp_)
                v2 = jnp.concatenate([jnp.where(vm[0], vp_, zero), jnp.where(vm[1], vp_, zero)], axis=0)
                o = jnp.dot(sc, v2, preferred_element_type=F32)
                st = states[p]
                o = o + lax.dot_general(q2[0:c, ks], st.astype(BF16), nt, preferred_element_type=F32)
                o = o + lax.dot_general(q2[c:2 * c, ks], sb_ref[sub * g + gi, p], nt, preferred_element_type=F32)
                upd = lax.dot_general(vp_, kf2[:, ks], tn, preferred_element_type=F32)
                states[p] = st * dec[:, ks] + jnp.where(bd, upd, 0.0)
                pair_out.append(o)
            outs.append(jnp.concatenate(pair_out, axis=1) if npair > 1 else pair_out[0])
        for p in range(npair):
            st_ref[p] = states[p]
        o = jnp.concatenate(outs, axis=0)
        o_ref[pl.ds(base, gc), :] = _head_rms_gate(o, gate_ref[pl.ds(base, gc), :], bdv_ref[...], dv)
        return carry

    lax.fori_loop(0, nsub, sub_block, 0)


def _head_block_ones(wv, seg):
    vr = lax.broadcasted_iota(jnp.int32, (wv, wv), 0) // seg
    vc = lax.broadcasted_iota(jnp.int32, (wv, wv), 1) // seg
    return (vr == vc).astype(BF16)


def _sub_blocks(lp, rows):
    n = lp // rows
    for cand in (5, 4, 3, 2):
        if n % cand == 0:
            return cand
    return 1


def _linear_attention(q, k, v, lf, lb, gate, batch, lp, *, npair, vp, dv):
    gc = LA_GROUP * CHUNK
    nsub = _sub_blocks(lp, gc)
    rows = nsub * gc
    nblk = lp // rows
    nc = lp // CHUNK
    wk = npair * LANES
    wv = npair * vp
    q3 = q.reshape(batch, lp, wk)
    k3 = k.reshape(batch, lp, wk)
    v3 = v.reshape(batch, lp, wv)
    g3 = gate.reshape(batch, lp, wv)
    lf3 = lf.reshape(batch, lp, wk)
    lb3 = lb.reshape(batch, lp, wk)
    fwd = lambda b, j: (b, j, 0)
    rev = lambda b, j: (b, nblk - 1 - j, 0)
    sblk = (None, nsub * LA_GROUP, npair, vp, LANES)

    sb = pl.pallas_call(
        functools.partial(_la_bwd_kernel, npair=npair, vp=vp, nsub=nsub),
        grid=(batch, nblk),
        in_specs=[pl.BlockSpec((None, rows, wk), rev), pl.BlockSpec((None, rows, wv), rev),
                  pl.BlockSpec((None, rows, wk), rev)],
        out_specs=pl.BlockSpec(sblk, lambda b, j: (b, nblk - 1 - j, 0, 0, 0)),
        out_shape=jax.ShapeDtypeStruct((batch, nc, npair, vp, LANES), BF16),
        scratch_shapes=[pltpu.VMEM((npair, vp, LANES), F32)],
        compiler_params=_params("parallel", "arbitrary"),
        name="la_bwd_states",
    )(k3, v3, lb3)

    o = pl.pallas_call(
        functools.partial(_la_fwd_kernel, npair=npair, vp=vp, dv=dv, nsub=nsub),
        grid=(batch, nblk),
        in_specs=[pl.BlockSpec((None, rows, wk), fwd), pl.BlockSpec((None, rows, wk), fwd),
                  pl.BlockSpec((None, rows, wv), fwd), pl.BlockSpec((None, rows, wk), fwd),
                  pl.BlockSpec((None, rows, wk), fwd),
                  pl.BlockSpec(sblk, lambda b, j: (b, j, 0, 0, 0)),
                  pl.BlockSpec((None, rows, wv), fwd),
                  pl.BlockSpec((wv, wv), lambda b, j: (0, 0))],
        out_specs=pl.BlockSpec((None, rows, wv), fwd),
        out_shape=jax.ShapeDtypeStruct((batch, lp, wv), BF16),
        scratch_shapes=[pltpu.VMEM((npair, vp, LANES), F32)],
        compiler_params=_params("parallel", "arbitrary"),
        name="la_fwd",
    )(q3, k3, v3, lf3, lb3, sb, g3, _head_block_ones(wv, vp // 2))
    return o.reshape(batch * lp, wv)


RET_CHUNK = 256


def _ret_decay_terms(lg_ref):
    c = RET_CHUNK
    lg = lg_ref[...]
    i = lax.broadcasted_iota(jnp.int32, (c, 1), 0).astype(F32)
    return dict(q_f=jnp.exp((i + 1.0) * lg), q_b=jnp.exp((float(c) - i) * lg),
                k_f=jnp.exp((float(c - 1) - i) * lg), k_b=jnp.exp(i * lg), dec=jnp.exp(float(c) * lg))


def _ret_bwd_kernel(k_ref, v_ref, lg_ref, sb_ref, st_ref, *, npair, nsub):
    c = RET_CHUNK
    tn = (((0,), (0,)), ((), ()))

    @pl.when(pl.program_id(1) == 0)
    def _():
        st_ref[...] = jnp.zeros_like(st_ref)

    _, _, bd = _head_masks(npair, LANES)
    t = _ret_decay_terms(lg_ref)

    def sub_block(it, carry):
        sub = nsub - 1 - it
        base = pl.multiple_of(sub * c, c)
        kb = (k_ref[pl.ds(base, c), :].astype(F32) * t["k_b"]).astype(BF16)
        vv = v_ref[pl.ds(base, c), :]
        for p in range(npair):
            ks = slice(p * LANES, (p + 1) * LANES)
            st = st_ref[p]
            sb_ref[sub, p] = st.astype(BF16)
            upd = lax.dot_general(vv[:, ks], kb[:, ks], tn, preferred_element_type=F32)
            st_ref[p] = st * t["dec"][:, ks] + jnp.where(bd, upd, 0.0)
        return carry

    lax.fori_loop(0, nsub, sub_block, 0)


def _ret_fwd_kernel(q_ref, k_ref, v_ref, lg_ref, lgd_ref, sb_ref, gate_ref, bdv_ref, o_ref, st_ref,
                    *, npair, nsub, dv):
    c = RET_CHUNK
    nt = (((1,), (1,)), ((), ()))
    tn = (((0,), (0,)), ((), ()))

    @pl.when(pl.program_id(1) == 0)
    def _():
        st_ref[...] = jnp.zeros_like(st_ref)

    km, vm, bd = _head_masks(npair, LANES)
    t = _ret_decay_terms(lg_ref)
    ri = lax.broadcasted_iota(jnp.int32, (c, 2 * c), 0)
    cj = lax.broadcasted_iota(jnp.int32, (c, 2 * c), 1) & (c - 1)
    dist = jnp.abs(ri - cj).astype(F32)
    dmask = [jnp.exp(dist * lgd_ref[p]) for p in range(npair)]

    def sub_block(sub, carry):
        base = pl.multiple_of(sub * c, c)
        qb16 = q_ref[pl.ds(base, c), :]
        kb16 = k_ref[pl.ds(base, c), :]
        vv = v_ref[pl.ds(base, c), :]
        qq = qb16.astype(F32)
        kk = kb16.astype(F32)
        q2 = jnp.concatenate([qq * t["q_f"], qq * t["q_b"]], axis=0).astype(BF16)
        kf2 = (kk * t["k_f"]).astype(BF16)
        zero = jnp.zeros((c, LANES), BF16)
        pair_out = []
        for p in range(npair):
            ks = slice(p * LANES, (p + 1) * LANES)
            kp = kb16[:, ks]
            vp_ = vv[:, ks]
            k2 = jnp.concatenate([jnp.where(km[0], kp, zero), jnp.where(km[1], kp, zero)], axis=0)
            v2 = jnp.concatenate([jnp.where(vm[0], vp_, zero), jnp.where(vm[1], vp_, zero)], axis=0)
            s = lax.dot_general(qb16[:, ks], k2, nt, preferred_element_type=F32)
            o = jnp.dot((s * dmask[p]).astype(BF16), v2, preferred_element_type=F32)
            st = st_ref[p]
            o = o + lax.dot_general(q2[0:c, ks], st.astype(BF16), nt, preferred_element_type=F32)
            o = o + lax.dot_general(q2[c:2 * c, ks], sb_ref[sub, p], nt, preferred_element_type=F32)
            upd = lax.dot_general(vp_, kf2[:, ks], tn, preferred_element_type=F32)
            st_ref[p] = st * t["dec"][:, ks] + jnp.where(bd, upd, 0.0)
            pair_out.append(o)
        o = jnp.concatenate(pair_out, axis=1)
        o_ref[pl.ds(base, c), :] = _head_rms_gate(o, gate_ref[pl.ds(base, c), :], bdv_ref[...], dv)
        return carry

    lax.fori_loop(0, nsub, sub_block, 0)


def _retention(q, k, v, gate, batch, lp):
    c = RET_CHUNK
    npair = RET_HEADS // 2
    w = npair * LANES
    nsub = _sub_blocks(lp, c)
    rows = nsub * c
    nblk = lp // rows
    log_gamma = jnp.log1p(-jnp.exp2(-5.0 - jnp.arange(RET_HEADS, dtype=F32)))
    lg = jnp.repeat(log_gamma, RET_DIM)[None]
    lgd = jnp.repeat(log_gamma, c).reshape(npair, 1, 2 * c)
    q3, k3, v3, g3 = (a.reshape(batch, lp, w) for a in (q, k, v, gate))
    fwd = lambda b, j: (b, j, 0)
    rev = lambda b, j: (b, nblk - 1 - j, 0)
    sblk = (None, nsub, npair, LANES, LANES)
    lgspec = pl.BlockSpec((1, w), lambda b, j: (0, 0))

    sb = pl.pallas_call(
        functools.partial(_ret_bwd_kernel, npair=npair, nsub=nsub),
        grid=(batch, nblk),
        in_specs=[pl.BlockSpec((None, rows, w), rev), pl.BlockSpec((None, rows, w), rev), lgspec],
        out_specs=pl.BlockSpec(sblk, lambda b, j: (b, nblk - 1 - j, 0, 0, 0)),
        out_shape=jax.ShapeDtypeStruct((batch, lp // c, npair, LANES, LANES), BF16),
        scratch_shapes=[pltpu.VMEM((npair, LANES, LANES), F32)],
        compiler_params=_params("parallel", "arbitrary"),
        name="ret_bwd_states",
    )(k3, v3, lg)

    o = pl.pallas_call(
        functools.partial(_ret_fwd_kernel, npair=npair, nsub=nsub, dv=RET_DIM),
        grid=(batch, nblk),
        in_specs=[pl.BlockSpec((None, rows, w), fwd), pl.BlockSpec((None, rows, w), fwd),
                  pl.BlockSpec((None, rows, w), fwd), lgspec,
                  pl.BlockSpec((npair, 1, 2 * c), lambda b, j: (0, 0, 0)),
                  pl.BlockSpec(sblk, lambda b, j: (b, j, 0, 0, 0)),
                  pl.BlockSpec((None, rows, w), fwd),
                  pl.BlockSpec((w, w), lambda b, j: (0, 0))],
        out_specs=pl.BlockSpec((None, rows, w), fwd),
        out_shape=jax.ShapeDtypeStruct((batch, lp, w), BF16),
        scratch_shapes=[pltpu.VMEM((npair, LANES, LANES), F32)],
        compiler_params=_params("parallel", "arbitrary"),
        name="ret_fwd",
    )(q3, k3, v3, lg, lgd, sb, g3, _head_block_ones(w, RET_DIM))
    return o.reshape(batch * lp, w)


_RT_E1, _RT_E2, _RT_W1, _RT_W2, _RT_R1, _RT_R2 = range(6)


def _mix_out(ya_ref, yb_ref, yc_ref, h_ref, wglu_ref, wout_ref, g_ref):
    ya = _gelu_tanh(ya_ref[...].astype(F32))
    gl = jnp.dot(ya.astype(BF16), wglu_ref[...], preferred_element_type=F32)
    ya = (ya * _sigmoid(gl)).astype(BF16)
    y = jnp.concatenate([ya, yb_ref[...], yc_ref[...]], axis=1)
    h = h_ref[...] + jnp.dot(y, wout_ref[...], preferred_element_type=F32)
    ms = jnp.mean(h * h, axis=-1, keepdims=True)
    return h, h * lax.rsqrt(ms + NORM_EPS) * g_ref[...]


def _outproj_kernel(ya_ref, yb_ref, yc_ref, h_ref, wglu_ref, wout_ref, g_ref, hout_ref, hn_ref):
    h, hn = _mix_out(ya_ref, yb_ref, yc_ref, h_ref, wglu_ref, wout_ref, g_ref)
    hout_ref[...] = h
    hn_ref[...] = hn.astype(BF16)


def _outproj_router_kernel(ya_ref, yb_ref, yc_ref, h_ref, wglu_ref, wout_ref, g_ref, wrh_ref, wrl_ref,
                           hout_ref, hn_ref, route_ref, cnt_ref, carry_ref):
    @pl.when(pl.program_id(0) == 0)
    def _():
        carry_ref[...] = jnp.zeros_like(carry_ref)

    h, hn = _mix_out(ya_ref, yb_ref, yc_ref, h_ref, wglu_ref, wout_ref, g_ref)
    hout_ref[...] = h
    hn_ref[...] = hn
    hi = hn.astype(BF16)
    lo = (hn - hi.astype(F32)).astype(BF16)
    wrh = wrh_ref[...]
    logits = (jnp.dot(hi, wrh, preferred_element_type=F32) + jnp.dot(lo, wrh, preferred_element_type=F32)
              + jnp.dot(hi, wrl_ref[...], preferred_element_type=F32))
    tm = logits.shape[0]
    lane = lax.broadcasted_iota(jnp.int32, logits.shape, 1)
    neg = jnp.float32(-jnp.inf)
    logits = jnp.where(lane < N_EXPERTS, logits, neg)
    v1 = jnp.max(logits, axis=-1, keepdims=True)
    i1 = jnp.min(jnp.where(logits == v1, lane, LANES), axis=-1, keepdims=True)
    m1 = lane == i1
    l2 = jnp.where(m1, neg, logits)
    v2 = jnp.max(l2, axis=-1, keepdims=True)
    i2 = jnp.min(jnp.where(l2 == v2, lane, LANES), axis=-1, keepdims=True)
    m2 = lane == i2
    e = jnp.exp(v2 - v1)
    w1 = 1.0 / (1.0 + e)
    w2 = e * w1
    chosen = jnp.where(m1 | m2, 1.0, 0.0)
    rr = lax.broadcasted_iota(jnp.int32, (tm, tm), 0)
    ss = lax.broadcasted_iota(jnp.int32, (tm, tm), 1)
    before = jnp.where(ss < rr, 1.0, 0.0).astype(BF16)
    prefix = jnp.dot(before, chosen.astype(BF16), preferred_element_type=F32) + carry_ref[...]
    r1 = jnp.sum(jnp.where(m1, prefix, 0.0), axis=-1, keepdims=True)
    r2 = jnp.sum(jnp.where(m2, prefix, 0.0), axis=-1, keepdims=True)
    total = carry_ref[...] + jnp.sum(chosen, axis=0, keepdims=True)
    carry_ref[...] = total
    cnt_ref[...] = total
    rec = jnp.zeros_like(logits)
    for ln, val in ((_RT_E1, i1.astype(F32)), (_RT_E2, i2.astype(F32)), (_RT_W1, w1), (_RT_W2, w2),
                    (_RT_R1, r1), (_RT_R2, r2)):
        rec = jnp.where(lane == ln, val, rec)
    route_ref[...] = rec


def _outproj(ya, yb, yc, h, wglu, wout, gamma, wr=None):
    r = h.shape[0]
    tm = ROW_TILE
    with_router = wr is not None

    def rows(wd):
        return pl.BlockSpec((tm, wd), lambda i: (i, 0))

    in_specs = [rows(256), rows(384), rows(512), rows(D_MODEL), _const_spec((256, 256)),
                _const_spec((1152, D_MODEL)), _const_spec((1, D_MODEL))]
    args = [ya, yb, yc, h, wglu, wout, gamma]
    if not with_router:
        return pl.pallas_call(
            _outproj_kernel,
            grid=(r // tm,),
            in_specs=in_specs, out_specs=[rows(D_MODEL), rows(D_MODEL)],
            out_shape=[jax.ShapeDtypeStruct((r, D_MODEL), F32), jax.ShapeDtypeStruct((r, D_MODEL), BF16)],
            input_output_aliases={3: 0},
            compiler_params=_params("parallel"),
            name="outproj",
        )(*args)
    return pl.pallas_call(
        _outproj_router_kernel,
        grid=(r // tm,),
        in_specs=in_specs + [_const_spec((D_MODEL, LANES)), _const_spec((D_MODEL, LANES))],
        out_specs=[rows(D_MODEL), rows(D_MODEL), rows(LANES), _const_spec((1, LANES))],
        out_shape=[jax.ShapeDtypeStruct((r, D_MODEL), F32), jax.ShapeDtypeStruct((r, D_MODEL), F32),
                   jax.ShapeDtypeStruct((r, LANES), F32), jax.ShapeDtypeStruct((1, LANES), F32)],
        scratch_shapes=[pltpu.VMEM((1, LANES), F32)],
        input_output_aliases={3: 0},
        compiler_params=_params("arbitrary"),
        name="outproj_router",
    )(*args, wr[0], wr[1])


FF_TILE = 1408


def _ffn_kernel(hn_ref, h_ref, wg_ref, wu_ref, wd_ref, o_ref, acc_ref):
    j = pl.program_id(1)
    hn = hn_ref[...]
    a = _silu(jnp.dot(hn, wg_ref[...], preferred_element_type=F32)) * jnp.dot(hn, wu_ref[...], preferred_element_type=F32)
    y = jnp.dot(a.astype(BF16), wd_ref[...], preferred_element_type=F32)

    @pl.when(j == 0)
    def _():
        acc_ref[...] = h_ref[...] + y

    @pl.when(j != 0)
    def _():
        acc_ref[...] += y

    @pl.when(j == pl.num_programs(1) - 1)
    def _():
        o_ref[...] = acc_ref[...]


def _ffn(hn, h, wg, wu, wd):
    r = h.shape[0]
    tm = ROW_TILE
    nf = D_FF // FF_TILE
    return pl.pallas_call(
        _ffn_kernel,
        grid=(r // tm, nf),
        in_specs=[pl.BlockSpec((tm, D_MODEL), lambda i, j: (i, 0)),
                  pl.BlockSpec((tm, D_MODEL), lambda i, j: (i, 0)),
                  pl.BlockSpec((D_MODEL, FF_TILE), lambda i, j: (0, j)),
                  pl.BlockSpec((D_MODEL, FF_TILE), lambda i, j: (0, j)),
                  pl.BlockSpec((FF_TILE, D_MODEL), lambda i, j: (j, 0))],
        out_specs=pl.BlockSpec((tm, D_MODEL), lambda i, j: (i, 0)),
        out_shape=jax.ShapeDtypeStruct((r, D_MODEL), F32),
        scratch_shapes=[pltpu.VMEM((tm, D_MODEL), F32)],
        input_output_aliases={1: 0},
        compiler_params=_params("parallel", "arbitrary"),
        name="ffn",
    )(hn, h, wg, wu, wd)


MOE_BLOCK = 512
GATHER_TILE = 256


def _route_meta(route, cnt, r):
    bm = MOE_BLOCK
    nb = 2 * r // bm + N_EXPERTS
    e1 = route[:, _RT_E1].astype(jnp.int32)
    e2 = route[:, _RT_E2].astype(jnp.int32)
    counts = cnt[0, :N_EXPERTS].astype(jnp.int32)
    padded = ((counts + bm - 1) // bm) * bm
    ends = jnp.cumsum(padded)
    starts = ends - padded
    pos1 = starts[e1] + route[:, _RT_R1].astype(jnp.int32)
    pos2 = starts[e2] + route[:, _RT_R2].astype(jnp.int32)
    n_used = (ends[-1] // bm).astype(jnp.int32)
    blk = jnp.arange(nb, dtype=jnp.int32)
    blk = jnp.minimum(blk, n_used - 1)
    block_expert = jnp.sum((blk[:, None] * bm >= ends[None, :]).astype(jnp.int32), axis=1)
    block_expert = jnp.minimum(block_expert, N_EXPERTS - 1).astype(jnp.int32)
    return pos1, pos2, block_expert, n_used.reshape(1), nb


def _dispatch_kernel(p1_ref, p2_ref, hn_ref, xs_in_ref, xs_ref, sem):
    del xs_in_ref
    n = hn_ref.shape[0]

    def issue(r, c):
        src = hn_ref.at[pl.ds(r, 1), :]
        pltpu.make_async_copy(src, xs_ref.at[pl.ds(p1_ref[0, 0, r], 1), :], sem).start()
        pltpu.make_async_copy(src, xs_ref.at[pl.ds(p2_ref[0, 0, r], 1), :], sem).start()
        return c

    lax.fori_loop(0, n, issue, 0, unroll=8)

    def drain(r, c):
        row = pltpu.make_async_copy(hn_ref.at[pl.ds(0, 1), :], xs_ref.at[pl.ds(0, 1), :], sem)
        row.wait()
        row.wait()
        return c

    lax.fori_loop(0, n, drain, 0, unroll=8)


def _dispatch(hn, pos1, pos2, nb):
    r = hn.shape[0]
    t = GATHER_TILE
    p = nb * MOE_BLOCK
    idx = pl.BlockSpec((1, 1, t), lambda i: (i, 0, 0), memory_space=pltpu.SMEM)
    return pl.pallas_call(
        _dispatch_kernel,
        grid=(r // t,),
        in_specs=[idx, idx, pl.BlockSpec((t, D_MODEL), lambda i: (i, 0)), pl.BlockSpec(memory_space=pl.ANY)],
        out_specs=pl.BlockSpec(memory_space=pl.ANY),
        out_shape=jax.ShapeDtypeStruct((p, D_MODEL), F32),
        scratch_shapes=[pltpu.SemaphoreType.DMA(())],
        input_output_aliases={3: 0},
        compiler_params=_params("arbitrary"),
        name="moe_dispatch",
    )(pos1.reshape(r // t, 1, t), pos2.reshape(r // t, 1, t), hn, jnp.zeros((p, D_MODEL), F32))


def _moe_ffn_kernel(be_ref, nu_ref, x_ref, wg_ref, wu_ref, wd_ref, o_ref, acc_ref):
    del be_ref
    b = pl.program_id(0)
    j = pl.program_id(1)
    last = j == pl.num_programs(1) - 1
    used = b < nu_ref[0]

    @pl.when(used)
    def _():
        x = x_ref[...].astype(BF16)
        a = _silu(jnp.dot(x, wg_ref[...], preferred_element_type=F32)) * jnp.dot(x, wu_ref[...], preferred_element_type=F32)
        y = jnp.dot(a.astype(BF16), wd_ref[...], preferred_element_type=F32)

        @pl.when(j == 0)
        def _():
            acc_ref[...] = y

        @pl.when(j != 0)
        def _():
            acc_ref[...] += y

        @pl.when(last)
        def _():
            o_ref[...] = acc_ref[...]

    @pl.when(jnp.logical_not(used) & last)
    def _():
        o_ref[...] = jnp.zeros_like(o_ref)


def _moe_ffn(xs, block_expert, n_used, wg, wu, wd):
    p = xs.shape[0]
    bm = MOE_BLOCK
    nf = D_FF // FF_TILE
    grid_spec = pltpu.PrefetchScalarGridSpec(
        num_scalar_prefetch=2,
        grid=(p // bm, nf),
        in_specs=[pl.BlockSpec((bm, D_MODEL), lambda b, j, be, nu: (b, 0)),
                  pl.BlockSpec((None, D_MODEL, FF_TILE), lambda b, j, be, nu: (be[b], 0, j)),
                  pl.BlockSpec((None, D_MODEL, FF_TILE), lambda b, j, be, nu: (be[b], 0, j)),
                  pl.BlockSpec((None, FF_TILE, D_MODEL), lambda b, j, be, nu: (be[b], j, 0))],
        out_specs=pl.BlockSpec((bm, D_MODEL), lambda b, j, be, nu: (b, 0)),
        scratch_shapes=[pltpu.VMEM((bm, D_MODEL), F32)])
    return pl.pallas_call(
        _moe_ffn_kernel,
        grid_spec=grid_spec,
        out_shape=jax.ShapeDtypeStruct((p, D_MODEL), F32),
        compiler_params=_params("arbitrary", "arbitrary"),
        name="moe_ffn",
    )(block_expert, n_used, xs, wg, wu, wd)


def _combine_kernel(p1_ref, p2_ref, route_ref, h_ref, g_ref, ys_ref, o_ref, buf1, buf2, sems, *, final):
    n = h_ref.shape[0]
    half = n // 2

    def issue(k):
        def body(r, c):
            pltpu.make_async_copy(ys_ref.at[pl.ds(p1_ref[0, 0, r], 1), :], buf1.at[pl.ds(r, 1), :], sems.at[k]).start()
            pltpu.make_async_copy(ys_ref.at[pl.ds(p2_ref[0, 0, r], 1), :], buf2.at[pl.ds(r, 1), :], sems.at[k]).start()
            return c
        lax.fori_loop(k * half, (k + 1) * half, body, 0, unroll=8)

    def drain(k):
        def body(r, c):
            row = pltpu.make_async_copy(ys_ref.at[pl.ds(0, 1), :], buf1.at[pl.ds(0, 1), :], sems.at[k])
            row.wait()
            row.wait()
            return c
        lax.fori_loop(0, half, body, 0, unroll=8)

    issue(0)
    issue(1)
    for k in range(2):
        drain(k)
        rows = slice(k * half, (k + 1) * half)
        rt = route_ref[rows, :]
        lane = lax.broadcasted_iota(jnp.int32, rt.shape, 1)
        w1 = jnp.sum(jnp.where(lane == _RT_W1, rt, 0.0), axis=-1, keepdims=True)
        w2 = jnp.sum(jnp.where(lane == _RT_W2, rt, 0.0), axis=-1, keepdims=True)
        h = h_ref[rows, :] + w1 * buf1[rows, :] + w2 * buf2[rows, :]
        if final:
            ms = jnp.mean(h * h, axis=-1, keepdims=True)
            h = h * lax.rsqrt(ms + NORM_EPS) * g_ref[...]
        o_ref[rows, :] = h


def _combine(ys, pos1, pos2, route, h, gamma, batch, lp, final):
    r = h.shape[0]
    t = GATHER_TILE
    p1 = pos1.reshape(r // t, 1, t)
    p2 = pos2.reshape(r // t, 1, t)
    scratch = [pltpu.VMEM((t, D_MODEL), F32), pltpu.VMEM((t, D_MODEL), F32), pltpu.SemaphoreType.DMA((2,))]
    if final:
        per_seq = lp // t
        skip = FRONT // t
        rb = lambda b, i: b * per_seq + skip + i
        grid = (batch, per_seq - skip)
        idx = pl.BlockSpec((1, 1, t), lambda b, i: (rb(b, i), 0, 0), memory_space=pltpu.SMEM)
        in_specs = [idx, idx, pl.BlockSpec((t, LANES), lambda b, i: (rb(b, i), 0)),
                    pl.BlockSpec((t, D_MODEL), lambda b, i: (rb(b, i), 0)),
                    pl.BlockSpec((1, D_MODEL), lambda b, i: (0, 0)), pl.BlockSpec(memory_space=pl.ANY)]
        out_specs = pl.BlockSpec((None, t, D_MODEL), lambda b, i: (b, i, 0))
        out_shape = jax.ShapeDtypeStruct((batch, lp - FRONT, D_MODEL), F32)
        sem = ("arbitrary", "arbitrary")
    else:
        grid = (r // t,)
        idx = pl.BlockSpec((1, 1, t), lambda i: (i, 0, 0), memory_space=pltpu.SMEM)
        in_specs = [idx, idx, pl.BlockSpec((t, LANES), lambda i: (i, 0)), pl.BlockSpec((t, D_MODEL), lambda i: (i, 0)),
                    pl.BlockSpec((1, D_MODEL), lambda i: (0, 0)), pl.BlockSpec(memory_space=pl.ANY)]
        out_specs = pl.BlockSpec((t, D_MODEL), lambda i: (i, 0))
        out_shape = jax.ShapeDtypeStruct((r, D_MODEL), F32)
        sem = ("arbitrary",)
    return pl.pallas_call(
        functools.partial(_combine_kernel, final=final),
        grid=grid, in_specs=in_specs, out_specs=out_specs, out_shape=out_shape, scratch_shapes=scratch,
        compiler_params=_params(*sem),
        name="moe_combine_final" if final else "moe_combine",
    )(p1, p2, route, h, gamma, ys)


def _final_norm_kernel(h_ref, g_ref, o_ref):
    h = h_ref[...]
    ms = jnp.mean(h * h, axis=-1, keepdims=True)
    o_ref[...] = h * lax.rsqrt(ms + NORM_EPS) * g_ref[...]


def _final_norm(h, gamma, batch, lp):
    t = GATHER_TILE
    per_seq = lp // t
    skip = FRONT // t
    return pl.pallas_call(
        _final_norm_kernel,
        grid=(batch, per_seq - skip),
        in_specs=[pl.BlockSpec((t, D_MODEL), lambda b, i: (b * per_seq + skip + i, 0)),
                  pl.BlockSpec((1, D_MODEL), lambda b, i: (0, 0))],
        out_specs=pl.BlockSpec((None, t, D_MODEL), lambda b, i: (b, i, 0)),
        out_shape=jax.ShapeDtypeStruct((batch, lp - FRONT, D_MODEL), F32),
        compiler_params=_params("parallel", "parallel"),
        name="final_norm",
    )(h, gamma)


def _rope_tables(lp):
    half = RET_DIM // 2
    pos = jnp.arange(lp, dtype=F32) - float(PAD)
    inv = ROPE_BASE ** (-jnp.arange(half, dtype=F32) / half)
    ang = pos[:, None] * inv[None, :]
    cos = jnp.tile(jnp.cos(ang), (1, 2 * RET_HEADS))
    sin = jnp.tile(jnp.sin(ang), (1, 2 * RET_HEADS))
    return cos, sin


def _prep_layer(li, p):
    w_out = p["w_out"][li].astype(F32)
    wc = w_out[640:1024].reshape(GLA_HEADS, GLA_DV, D_MODEL)
    wc = jnp.pad(wc, ((0, 0), (0, GLA_DV_PAD - GLA_DV), (0, 0))).reshape(GLA_HEADS * GLA_DV_PAD, D_MODEL)
    wgate, bgate = _pack_gate(p["gla_w_gate_f"][li], p["gla_b_gate_f"][li], p["gla_w_gate_b"][li], p["gla_b_gate_b"][li])
    return dict(
        norm_mix=p["norm_mix"][li].astype(F32)[None],
        w_in=_pack_w_in(p["w_in"][li].astype(F32)),
        wgate=wgate, bgate=bgate,
        s5=_s5_tables(p["s5_lambda_re"][li], p["s5_lambda_im"][li], p["s5_log_dt"][li], p["s5_b_re"][li],
                      p["s5_b_im"][li], p["s5_c_re"][li], p["s5_c_im"][li], p["s5_d"][li]),
        w_glu=p["s5_w_glu"][li].astype(BF16),
        w_out=jnp.concatenate([w_out[0:640], wc], axis=0).astype(BF16),
        norm_ffn=p["norm_ffn"][li].astype(F32)[None],
    )


def _trunk(x, meta_tokens, layers, ffn, moe, norm_final, depth):
    batch, seq, _ = x.shape
    lp = seq + FRONT
    r = batch * lp
    meta = jnp.broadcast_to(meta_tokens.astype(F32)[None], (batch, N_META, D_MODEL))
    h = jnp.concatenate([jnp.zeros((batch, PAD, D_MODEL), F32), meta, x.astype(F32)], axis=1).reshape(r, D_MODEL)
    cos, sin = _rope_tables(lp)
    cos = jnp.tile(cos, (batch, 1))
    sin = jnp.tile(sin, (batch, 1))
    gamma_final = norm_final.astype(F32)[None]
    for li in range(depth):
        lw = layers[li]
        u, rq, rk, rv, rg, gq, gk, gv, gg, lf, lb = _inproj(
            h, lw["norm_mix"], lw["w_in"], cos, sin, lw["wgate"], lw["bgate"], batch, lp)
        ya = _s5_mixer(u, lw["s5"], batch, lp)
        yb = _retention(rq, rk, rv, rg, batch, lp)
        yc = _linear_attention(gq, gk, gv, lf, lb, gg, batch, lp, npair=GLA_HEADS // 2, vp=2 * GLA_DV_PAD, dv=GLA_DV)
        j = li // 2
        last = li == depth - 1
        if li % 2 == 0:
            h, hn = _outproj(ya, yb, yc, h, lw["w_glu"], lw["w_out"], lw["norm_ffn"])
            h = _ffn(hn, h, ffn["wg"][j], ffn["wu"][j], ffn["wd"][j])
        else:
            h, hn, route, cnt = _outproj(ya, yb, yc, h, lw["w_glu"], lw["w_out"], lw["norm_ffn"], moe["wr"][j])
            pos1, pos2, block_expert, n_used, nb = _route_meta(route, cnt, r)
            xs = _dispatch(hn, pos1, pos2, nb)
            ys = _moe_ffn(xs, block_expert, n_used, moe["wg"][j], moe["wu"][j], moe["wd"][j])
            h = _combine(ys, pos1, pos2, route, h, gamma_final, batch, lp, final=last)
            if last:
                return h
    return _final_norm(h, gamma_final, batch, lp)


def kernel(x_prompt, x_sample, meta_tokens, norm_mix, w_in, s5_lambda_re, s5_lambda_im, s5_log_dt, s5_b_re, s5_b_im, s5_c_re, s5_c_im, s5_d, s5_w_glu, gla_w_gate_f, gla_b_gate_f, gla_w_gate_b, gla_b_gate_b, w_out, norm_ffn, ffn_w_gate, ffn_w_up, ffn_w_down, router_w, moe_w_gate, moe_w_up, moe_w_down, norm_final):
    depth = w_in.shape[0]
    p = dict(norm_mix=norm_mix, w_in=w_in, s5_lambda_re=s5_lambda_re, s5_lambda_im=s5_lambda_im,
             s5_log_dt=s5_log_dt, s5_b_re=s5_b_re, s5_b_im=s5_b_im, s5_c_re=s5_c_re, s5_c_im=s5_c_im, s5_d=s5_d,
             s5_w_glu=s5_w_glu, gla_w_gate_f=gla_w_gate_f, gla_b_gate_f=gla_b_gate_f, gla_w_gate_b=gla_w_gate_b,
             gla_b_gate_b=gla_b_gate_b, w_out=w_out, norm_ffn=norm_ffn)
    layers = [_prep_layer(li, p) for li in range(depth)]
    ffn = dict(wg=ffn_w_gate.astype(BF16), wu=ffn_w_up.astype(BF16), wd=ffn_w_down.astype(BF16))
    wr = jnp.pad(router_w.astype(F32), ((0, 0), (0, 0), (0, LANES - N_EXPERTS)))
    wr_hi = wr.astype(BF16)
    wr_lo = (wr - wr_hi.astype(F32)).astype(BF16)
    wr = [(wr_hi[j], wr_lo[j]) for j in range(wr.shape[0])]
    moe = dict(wr=wr, wg=moe_w_gate.astype(BF16), wu=moe_w_up.astype(BF16), wd=moe_w_down.astype(BF16))
    y_prompt = _trunk(x_prompt, meta_tokens, layers, ffn, moe, norm_final, depth)
    y_sample = _trunk(x_sample, meta_tokens, layers, ffn, moe, norm_final, depth)
    return (y_prompt, y_sample)
```

```python
import functools
import math

import jax
import jax.numpy as jnp
from jax import lax
from jax.experimental import pallas as pl
from jax.experimental.pallas import tpu as pltpu

F32 = jnp.float32
BF16 = jnp.bfloat16

D_MODEL = 1024
N_META = 16
S5_P = 16
S5_WIDTH = 256
S5_GROUPS = 16
S5_N = 64
RET_HEADS = 6
RET_DIM = 64
RET_WIDTH = 384
GLA_HEADS = 4
GLA_DV = 96
GLA_DK = 48
GLA_QK = 192
GLA_WIDTH = 384
GLA_GATE_RANK = 16
GLA_TAU = 16.0
ROPE_BASE = 10000.0
D_FF = 2816
N_EXPERTS = 8
NORM_EPS = 1e-5

CHUNK = 64
FRONT = 256
PAD = FRONT - N_META
ROW_TILE = 512
LA_GROUP = 4
LANES = 128
GLA_DK_PAD = 64
GLA_DV_PAD = 128
VMEM_LIMIT = 56 * 1024 * 1024


def _params(*sem):
    return pltpu.CompilerParams(dimension_semantics=sem, vmem_limit_bytes=VMEM_LIMIT)


def _const_spec(shape):
    nd = len(shape)
    return pl.BlockSpec(shape, lambda *_: (0,) * nd)


def _sigmoid(x):
    return 1.0 / (1.0 + jnp.exp(-x))


def _silu(x):
    return x * _sigmoid(x)


def _gelu_tanh(x):
    c = math.sqrt(2.0 / math.pi)
    return 0.5 * x * (1.0 + jnp.tanh(c * (x + 0.044715 * (x * x * x))))


def _log_sigmoid(z):
    return jnp.minimum(z, 0.0) - jnp.log(1.0 + jnp.exp(-jnp.abs(z)))


def _split_dot(a, b_bf16, dims=None):
    hi = a.astype(BF16)
    lo = (a - hi.astype(F32)).astype(BF16)
    if dims is None:
        return (jnp.dot(hi, b_bf16, preferred_element_type=F32)
                + jnp.dot(lo, b_bf16, preferred_element_type=F32))
    return (lax.dot_general(hi, b_bf16, dims, preferred_element_type=F32)
            + lax.dot_general(lo, b_bf16, dims, preferred_element_type=F32))


_C_U = 0
_C_RQ = 256
_C_RK = 640
_C_RV = 1024
_C_RG = 1408
_C_GQ = 1792
_C_GK = 2048
_C_GV = 2304
_C_GG = 2816
_C_GL = 3328
_C_END = 3456


def _pack_w_in(w):
    o = 0
    u = w[:, o:o + 256]; o += 256
    rq = w[:, o:o + 384]; o += 384
    rk = w[:, o:o + 384]; o += 384
    rv = w[:, o:o + 384]; o += 384
    rg = w[:, o:o + 384]; o += 384
    gq = w[:, o:o + 192]; o += 192
    gk = w[:, o:o + 192]; o += 192
    gv = w[:, o:o + 384]; o += 384
    gg = w[:, o:o + 384]; o += 384
    glf = w[:, o:o + 16]; o += 16
    glb = w[:, o:o + 16]; o += 16

    def padh(m, d, dp):
        m = m.reshape(D_MODEL, GLA_HEADS, d)
        return jnp.pad(m, ((0, 0), (0, 0), (0, dp - d))).reshape(D_MODEL, GLA_HEADS * dp)

    gl = jnp.pad(jnp.concatenate([glf, glb], axis=1), ((0, 0), (0, LANES - 2 * GLA_GATE_RANK)))
    cat = jnp.concatenate([
        u, rq, rk, rv, rg,
        padh(gq, GLA_DK, GLA_DK_PAD), padh(gk, GLA_DK, GLA_DK_PAD),
        padh(gv, GLA_DV, GLA_DV_PAD), padh(gg, GLA_DV, GLA_DV_PAD), gl], axis=1)
    return cat.astype(BF16)


def _pack_gate(w_f, b_f, w_b, b_b):
    def padh(m):
        m = m.reshape(m.shape[0], GLA_HEADS, GLA_DK)
        return jnp.pad(m, ((0, 0), (0, 0), (0, GLA_DK_PAD - GLA_DK))).reshape(m.shape[0], GLA_HEADS * GLA_DK_PAD)
    r = GLA_GATE_RANK
    w = jnp.zeros((LANES, 2 * GLA_HEADS * GLA_DK_PAD), F32)
    w = w.at[0:r, 0:256].set(padh(w_f.astype(F32)))
    w = w.at[r:2 * r, 256:512].set(padh(w_b.astype(F32)))
    b = jnp.concatenate([padh(b_f.astype(F32)[None]), padh(b_b.astype(F32)[None])], axis=1)
    return w.astype(BF16), b


def _inproj_kernel(h_ref, g_ref, w_ref, cos_ref, sin_ref, wgate_ref, bgate_ref,
                   u_ref, rq_ref, rk_ref, rv_ref, rg_ref, gq_ref, gk_ref, gv_ref, gg_ref, lf_ref, lb_ref,
                   *, batch, lp):
    tm = h_ref.shape[0]
    x = h_ref[...]
    ms = jnp.mean(x * x, axis=-1, keepdims=True)
    row = pl.program_id(0) * tm + lax.broadcasted_iota(jnp.int32, (tm, 1), 0)
    valid = jnp.ones((tm, 1), F32)
    for b in range(batch):
        valid = jnp.where((row >= b * lp) & (row < b * lp + PAD), 0.0, valid)
    hn = (x * (lax.rsqrt(ms + NORM_EPS) * valid) * g_ref[...]).astype(BF16)

    def proj(lo, hi):
        return jnp.dot(hn, w_ref[:, lo:hi], preferred_element_type=F32)

    u_ref[...] = proj(_C_U, _C_RQ)
    cos = cos_ref[...]
    sin = sin_ref[...]
    half = RET_DIM // 2
    first_half = (lax.broadcasted_iota(jnp.int32, (1, RET_WIDTH), 1) & (RET_DIM - 1)) < half

    def rope(x):
        rot = jnp.where(first_half, -pltpu.roll(x, RET_WIDTH - half, 1), pltpu.roll(x, half, 1))
        return x * cos + rot * sin

    rq_ref[...] = rope(proj(_C_RQ, _C_RK)).astype(BF16)
    rk_ref[...] = (rope(proj(_C_RK, _C_RV)) * (RET_DIM ** -0.5)).astype(BF16)
    rv_ref[...] = proj(_C_RV, _C_RG).astype(BF16)
    rg_ref[...] = proj(_C_RG, _C_GQ).astype(BF16)
    gq_ref[...] = (proj(_C_GQ, _C_GK) * (GLA_DK ** -0.5)).astype(BF16)
    gk_ref[...] = proj(_C_GK, _C_GV).astype(BF16)
    gv_ref[...] = proj(_C_GV, _C_GG).astype(BF16)
    gg_ref[...] = proj(_C_GG, _C_GL).astype(BF16)
    codes = proj(_C_GL, _C_END).astype(BF16)
    z = jnp.dot(codes, wgate_ref[...], preferred_element_type=F32) + bgate_ref[...]
    ls = _log_sigmoid(z) * (1.0 / GLA_TAU)
    lf_ref[...] = ls[:, 0:256]
    lb_ref[...] = ls[:, 256:512]


def _inproj(h, gamma, w, cos, sin, wgate, bgate, batch, lp):
    r = h.shape[0]
    tm = ROW_TILE
    widths = (256, 384, 384, 384, 384, 256, 256, 512, 512, 256, 256)
    dtypes = (F32,) + (BF16,) * 8 + (F32, F32)

    def rows(wd):
        return pl.BlockSpec((tm, wd), lambda i: (i, 0))

    return pl.pallas_call(
        functools.partial(_inproj_kernel, batch=batch, lp=lp),
        grid=(r // tm,),
        in_specs=[rows(D_MODEL), _const_spec((1, D_MODEL)), _const_spec((D_MODEL, _C_END)),
                  rows(RET_WIDTH), rows(RET_WIDTH), _const_spec((LANES, 512)), _const_spec((1, 512))],
        out_specs=[rows(wd) for wd in widths],
        out_shape=[jax.ShapeDtypeStruct((r, wd), dt) for wd, dt in zip(widths, dtypes)],
        compiler_params=_params("parallel"),
        name="inproj",
    )(h, gamma, w, cos, sin, wgate, bgate)


def _s5_toeplitz_kernel(pwr_ref, pwi_ref, cer_ref, cei_ref, bbr_ref, bbi_ref, d_ref, tt_ref):
    hp = lax.Precision.HIGHEST
    w = CHUNK * S5_P
    lane = lax.broadcasted_iota(jnp.int32, (S5_P, w), 1)
    krow = []
    for dr in range(2):
        pr, pi, cr, ci = pwr_ref[dr], pwi_ref[dr], cer_ref[dr], cei_ref[dr]
        zr = pr * cr - pi * ci
        zi = pr * ci + pi * cr
        krow.append(jnp.dot(bbr_ref[dr], zr, precision=hp, preferred_element_type=F32)
                    - jnp.dot(bbi_ref[dr], zi, precision=hp, preferred_element_type=F32))
    kf = krow[0] + d_ref[...]
    kb = krow[1]
    for s in range(CHUNK):
        right = S5_P * s
        left = S5_P * (CHUNK - 1 - s)
        a = kf if right == 0 else jnp.where(lane >= right, pltpu.roll(kf, right, 1), 0.0)
        b = kb if left == 0 else jnp.where(lane < w - left, pltpu.roll(kb, w - left, 1), 0.0)
        tt_ref[S5_P * s:S5_P * (s + 1), :] = (a + b).astype(BF16)


def _s5_toeplitz(pwr, pwi, cer, cei, bbr, bbi, d_e):
    w = CHUNK * S5_P
    big = pl.BlockSpec((None, 2, S5_N, w), lambda g: (g, 0, 0, 0))
    small = pl.BlockSpec((None, 2, S5_P, S5_N), lambda g: (g, 0, 0, 0))
    return pl.pallas_call(
        _s5_toeplitz_kernel,
        grid=(S5_GROUPS,),
        in_specs=[big, big, big, big, small, small, pl.BlockSpec((None, S5_P, w), lambda g: (g, 0, 0))],
        out_specs=pl.BlockSpec((None, w, w), lambda g: (g, 0, 0)),
        out_shape=jax.ShapeDtypeStruct((S5_GROUPS, w, w), BF16),
        compiler_params=_params("parallel"),
        name="s5_toeplitz",
    )(pwr, pwi, cer, cei, bbr, bbi, d_e)


def _s5_tables(lam_re, lam_im, log_dt, b_re, b_im, c_re, c_im, d):
    c = CHUNK
    g_, n_, p_ = S5_GROUPS, S5_N, S5_P
    dt = jnp.exp(log_dt.astype(F32))[..., None]
    lr = lam_re.astype(F32)
    li = lam_im.astype(F32)
    e = lr * dt
    th = li * dt
    mag = jnp.exp(e)
    a_re = mag * jnp.cos(th)
    a_im = mag * jnp.sin(th)
    den = lr * lr + li * li
    nr = a_re - 1.0
    ni = a_im
    coef_re = (nr * lr + ni * li) / den
    coef_im = (ni * lr - nr * li) / den
    br = b_re.astype(F32)
    bi = b_im.astype(F32)
    bb_re = coef_re[..., None] * br - coef_im[..., None] * bi
    bb_im = coef_re[..., None] * bi + coef_im[..., None] * br
    cr = c_re.astype(F32)
    ci = c_im.astype(F32)
    tau = jnp.arange(c + 1, dtype=F32)[:, None, None, None]
    pw_mag = jnp.exp(tau * e[None])
    pw_re = pw_mag * jnp.cos(tau * th[None])
    pw_im = pw_mag * jnp.sin(tau * th[None])

    z_re = cr[None] * pw_re[:, :, :, None, :] - ci[None] * pw_im[:, :, :, None, :]
    z_im = cr[None] * pw_im[:, :, :, None, :] + ci[None] * pw_re[:, :, :, None, :]
    idx = jnp.arange(c)

    def expand_pw(pw):
        both = jnp.stack([pw[:c, 0], pw[c - 1 - idx, 1]], axis=0)
        both = jnp.transpose(both, (2, 0, 3, 1))
        return jnp.broadcast_to(both[..., None], (g_, 2, n_, c, p_)).reshape(g_, 2, n_, c * p_)

    def expand_c(cm):
        cm = jnp.transpose(cm, (1, 0, 3, 2))
        return jnp.broadcast_to(cm[:, :, :, None, :], (g_, 2, n_, c, p_)).reshape(g_, 2, n_, c * p_)

    bbt_re = jnp.transpose(bb_re, (1, 0, 3, 2))
    bbt_im = jnp.transpose(bb_im, (1, 0, 3, 2))
    d_e = jnp.zeros((g_, p_, c * p_), F32).at[:, :, :p_].set(d.astype(F32)[:, :, None] * jnp.eye(p_, dtype=F32))
    tt = _s5_toeplitz(expand_pw(pw_re), expand_pw(pw_im), expand_c(cr), expand_c(ci), bbt_re, bbt_im, d_e)

    pf_re = pw_re[c - 1 - idx, 0]
    pf_im = pw_im[c - 1 - idx, 0]
    pb_re = pw_re[idx, 1]
    pb_im = pw_im[idx, 1]

    def m_of(p_re_, p_im_, dr):
        m_re = p_re_[:, :, :, None] * bb_re[dr][None] - p_im_[:, :, :, None] * bb_im[dr][None]
        m_im = p_re_[:, :, :, None] * bb_im[dr][None] + p_im_[:, :, :, None] * bb_re[dr][None]
        to = lambda m: jnp.transpose(m, (1, 0, 3, 2)).reshape(g_, c * p_, n_)
        return to(m_re), to(m_im)

    mf_re, mf_im = m_of(pf_re, pf_im, 0)
    mb_re, mb_im = m_of(pb_re, pb_im, 1)
    m4 = jnp.stack([mf_re, mf_im, mb_re, mb_im], axis=1)
    m4 = m4.reshape(g_ // 2, 2, 4, c * p_, n_)
    mz = jnp.zeros_like(m4[:, 0])

    def m_rows(blocks):
        return jnp.transpose(jnp.concatenate(blocks, axis=-1), (0, 2, 1, 3)).reshape(g_ // 2, c * p_, 8 * n_)

    mpair = jnp.concatenate([m_rows([m4[:, 0], mz]), m_rows([mz, m4[:, 1]])], axis=1).astype(BF16)

    def n_of(tsel, dr):
        w_re = z_re[tsel, dr]
        w_im = z_im[tsel, dr]
        to = lambda m: jnp.transpose(m, (1, 3, 0, 2)).reshape(g_, n_, c * p_)
        return to(w_re), to(-w_im)

    nf_re, nf_im = n_of(idx + 1, 0)
    nb_re, nb_im = n_of(c - idx, 1)
    n4 = jnp.stack([nf_re, nf_im, nb_re, nb_im], axis=1)
    n4 = n4.reshape(g_ // 2, 2, 4, n_, c * p_)
    nz = jnp.zeros_like(n4[:, 0])
    npair = jnp.stack([jnp.concatenate([n4[:, 0], nz], axis=-1), jnp.concatenate([nz, n4[:, 1]], axis=-1)], axis=2)
    npair = npair.reshape(g_ // 2, 4 * 2 * n_, 2 * c * p_).astype(BF16)

    dec = jnp.stack([pw_re[c, 0], pw_im[c, 0], pw_re[c, 1], pw_im[c, 1]], axis=0)
    dec = dec.reshape(4, 1, g_ * n_)
    return tt, mpair, npair, dec


S5_PITCH = CHUNK + 4
S5_HALVES = S5_WIDTH // LANES
S5_GPH = LANES // S5_P


def _s5_chunk_block(nch):
    best = 8
    for cand in range(8, 113, 8):
        if nch % cand == 0:
            best = cand
    return best


def _block_transpose8(tiles):
    lane = lax.broadcasted_iota(jnp.int32, tiles[0].shape, 1)
    tiles = list(tiles)
    for dist in (4, 2, 1):
        width = S5_P * dist
        low = (lane & (2 * width - 1)) < width
        for k in range(S5_GPH):
            if k & dist:
                continue
            a, b = tiles[k], tiles[k + dist]
            tiles[k] = jnp.where(low, a, pltpu.roll(b, width, 1))
            tiles[k + dist] = jnp.where(low, pltpu.roll(a, LANES - width, 1), b)
    return tiles


def _s5_in_kernel(u_hbm, m_ref, ucat_ref, o0, o1, o2, o3, xpad, uall, sem):
    i = pl.program_id(0)
    j = pl.program_id(1)
    cb = uall.shape[1]

    @pl.when(j == 0)
    def _():
        def chunk_copy(c, h):
            return pltpu.make_async_copy(
                u_hbm.at[pl.ds((i * cb + c) * CHUNK, CHUNK), pl.ds(h * LANES, LANES)],
                xpad.at[h, pl.ds(c * S5_PITCH, CHUNK), :], sem)

        def start(c, carry):
            for h in range(S5_HALVES):
                chunk_copy(c, h).start()
            return carry

        def wait(c, carry):
            for h in range(S5_HALVES):
                chunk_copy(c, h).wait()
            return carry

        lax.fori_loop(0, cb, start, 0)
        lax.fori_loop(0, cb, wait, 0)

        def dest_tile(jt, carry):
            col = pl.multiple_of(jt * LANES, LANES)
            for h in range(S5_HALVES):
                by_token = [xpad[h, pl.ds(jt * S5_GPH + k, cb, stride=S5_PITCH), :] for k in range(S5_GPH)]
                for go, tile in enumerate(_block_transpose8(by_token)):
                    uall[h * S5_GPH + go, :, pl.ds(col, LANES)] = tile
            return carry

        lax.fori_loop(0, CHUNK // S5_GPH, dest_tile, 0)

    u2 = jnp.concatenate([uall[2 * j], uall[2 * j + 1]], axis=1)
    ucat_ref[...] = u2
    s = jnp.dot(u2.astype(BF16), m_ref[...], preferred_element_type=F32)
    o0[...] = s[:, 0:128]
    o1[...] = s[:, 128:256]
    o2[...] = s[:, 256:384]
    o3[...] = s[:, 384:512]


def _s5_in(u, mpair):
    nch = u.shape[0] // CHUNK
    cb = _s5_chunk_block(nch)
    w = CHUNK * S5_P
    return pl.pallas_call(
        _s5_in_kernel,
        grid=(nch // cb, S5_GROUPS // 2),
        in_specs=[pl.BlockSpec(memory_space=pl.ANY),
                  pl.BlockSpec((None, 2 * w, 512), lambda i, j: (j, 0, 0))],
        out_specs=[pl.BlockSpec((cb, 2 * w), lambda i, j: (i, j))] + [pl.BlockSpec((cb, LANES), lambda i, j: (i, j))] * 4,
        out_shape=[jax.ShapeDtypeStruct((nch, S5_GROUPS * w), F32)] + [jax.ShapeDtypeStruct((nch, 1024), F32)] * 4,
        scratch_shapes=[pltpu.VMEM((S5_HALVES, cb * S5_PITCH, LANES), F32), pltpu.VMEM((S5_GROUPS, cb, w), F32),
                        pltpu.SemaphoreType.DMA(())],
        compiler_params=_params("arbitrary", "arbitrary"),
        name="s5_in",
    )(u, mpair)


def _s5_scan_kernel(sfr, sfi, sbr, sbi, dfr, dfi, dbr, dbi, hfr, hfi, hbr, hbi):
    nc, b, _ = sfr.shape
    a_fr = dfr[...]
    a_fi = dfi[...]
    a_br = dbr[...]
    a_bi = dbi[...]
    zero = jnp.zeros((b, LANES), F32)

    def fwd(c, carry):
        hr, hi = carry
        hfr[c] = hr
        hfi[c] = hi
        return (a_fr * hr - a_fi * hi + sfr[c], a_fr * hi + a_fi * hr + sfi[c])

    def bwd(i, carry):
        c = nc - 1 - i
        hr, hi = carry
        hbr[c] = hr
        hbi[c] = hi
        return (a_br * hr - a_bi * hi + sbr[c], a_br * hi + a_bi * hr + sbi[c])

    lax.fori_loop(0, nc, fwd, (zero, zero))
    lax.fori_loop(0, nc, bwd, (zero, zero))


def _s5_scan(s4, dec, nc, b):
    blk = pl.BlockSpec((nc, b, LANES), lambda j: (0, 0, j))
    dspecs = [pl.BlockSpec((None, 1, LANES), functools.partial(lambda j, k: (k, 0, j), k=k)) for k in range(4)]
    return pl.pallas_call(
        _s5_scan_kernel,
        grid=(S5_GROUPS // 2,),
        in_specs=[blk] * 4 + dspecs,
        out_specs=[blk] * 4,
        out_shape=[jax.ShapeDtypeStruct((nc, b, 1024), F32)] * 4,
        compiler_params=_params("parallel"),
        name="s5_scan",
    )(*s4, dec, dec, dec, dec)


def _s5_out_kernel(u_ref, tt_ref, h0, h1, h2, h3, n_ref, y_hbm, yall, ypad, sem):
    i = pl.program_id(0)
    j = pl.program_id(1)
    cb = yall.shape[1]
    w = CHUNK * S5_P
    u = u_ref[...].astype(BF16)
    y0 = jnp.dot(u[:, 0:w], tt_ref[0], preferred_element_type=F32)
    y1 = jnp.dot(u[:, w:2 * w], tt_ref[1], preferred_element_type=F32)
    hcat = jnp.concatenate([h0[...], h1[...], h2[...], h3[...]], axis=1).astype(BF16)
    yh = jnp.dot(hcat, n_ref[...], preferred_element_type=F32)
    yall[2 * j] = y0 + yh[:, 0:w]
    yall[2 * j + 1] = y1 + yh[:, w:2 * w]

    @pl.when(j == pl.num_programs(1) - 1)
    def _():
        def src_tile(jt, carry):
            col = pl.multiple_of(jt * LANES, LANES)
            for h in range(S5_HALVES):
                by_group = [yall[h * S5_GPH + go, :, pl.ds(col, LANES)] for go in range(S5_GPH)]
                for k, tile in enumerate(_block_transpose8(by_group)):
                    ypad[h, pl.ds(jt * S5_GPH + k, cb, stride=S5_PITCH), :] = tile
            return carry

        lax.fori_loop(0, CHUNK // S5_GPH, src_tile, 0)

        def chunk_copy(c, h):
            return pltpu.make_async_copy(
                ypad.at[h, pl.ds(c * S5_PITCH, CHUNK), :],
                y_hbm.at[pl.ds((i * cb + c) * CHUNK, CHUNK), pl.ds(h * LANES, LANES)], sem)

        def start(c, carry):
            for h in range(S5_HALVES):
                chunk_copy(c, h).start()
            return carry

        def wait(c, carry):
            for h in range(S5_HALVES):
                chunk_copy(c, h).wait()
            return carry

        lax.fori_loop(0, cb, start, 0)
        lax.fori_loop(0, cb, wait, 0)


def _s5_out(ucat, tt, h4, npair):
    nch = ucat.shape[0]
    cb = _s5_chunk_block(nch)
    w = CHUNK * S5_P
    hblk = pl.BlockSpec((cb, LANES), lambda i, j: (i, j))
    return pl.pallas_call(
        _s5_out_kernel,
        grid=(nch // cb, S5_GROUPS // 2),
        in_specs=[pl.BlockSpec((cb, 2 * w), lambda i, j: (i, j)),
                  pl.BlockSpec((2, w, w), lambda i, j: (j, 0, 0)),
                  hblk, hblk, hblk, hblk,
                  pl.BlockSpec((None, 512, 2 * w), lambda i, j: (j, 0, 0))],
        out_specs=pl.BlockSpec(memory_space=pl.ANY),
        out_shape=jax.ShapeDtypeStruct((nch * CHUNK, S5_WIDTH), F32),
        scratch_shapes=[pltpu.VMEM((S5_GROUPS, cb, w), F32), pltpu.VMEM((S5_HALVES, cb * S5_PITCH, LANES), F32),
                        pltpu.SemaphoreType.DMA(())],
        compiler_params=_params("arbitrary", "arbitrary"),
        name="s5_out",
    )(ucat, tt, *h4, npair)


def _s5_mixer(u, tables, batch, lp):
    tt, mpair, npair, dec = tables
    nc = lp // CHUNK
    ucat, *s4 = _s5_in(u, mpair)
    s4 = [jnp.transpose(s.reshape(batch, nc, 1024), (1, 0, 2)) for s in s4]
    h4 = _s5_scan(s4, dec, nc, batch)
    h4 = [jnp.transpose(h, (1, 0, 2)).reshape(batch * nc, 1024) for h in h4]
    return _s5_out(ucat, tt, h4, npair)


def _cumsum_chunks(x):
    n = x.shape[0]
    r = lax.broadcasted_iota(jnp.int32, (n, n), 0)
    s = lax.broadcasted_iota(jnp.int32, (n, n), 1)
    tri = jnp.where((s <= r) & ((s // CHUNK) == (r // CHUNK)), 1.0, 0.0).astype(BF16)
    hi = x.astype(BF16)
    lo = (x - hi.astype(F32)).astype(BF16)
    return jnp.dot(tri, hi, preferred_element_type=F32) + jnp.dot(tri, lo, preferred_element_type=F32)


def _head_masks(npair, vp):
    kl = lax.broadcasted_iota(jnp.int32, (1, LANES), 1)
    km = [(kl < 64), (kl >= 64)]
    vl = lax.broadcasted_iota(jnp.int32, (1, vp), 1)
    vm = [(vl < vp // 2), (vl >= vp // 2)]
    vrow = lax.broadcasted_iota(jnp.int32, (vp, LANES), 0)
    kcol = lax.broadcasted_iota(jnp.int32, (vp, LANES), 1)
    bd = (vrow >= vp // 2) == (kcol >= 64)
    return km, vm, bd


def _la_bwd_kernel(k_ref, v_ref, lb_ref, sb_ref, st_ref, *, npair, vp, nsub):
    g = LA_GROUP
    c = CHUNK
    gc = g * c

    @pl.when(pl.program_id(1) == 0)
    def _():
        st_ref[...] = jnp.zeros_like(st_ref)

    km, vm, bd = _head_masks(npair, vp)

    def sub_block(t, carry):
        sub = nsub - 1 - t
        base = pl.multiple_of(sub * gc, gc)
        kblk = k_ref[pl.ds(base, gc), :].astype(F32)
        vblk = v_ref[pl.ds(base, gc), :]
        lblk = lb_ref[pl.ds(base, gc), :]
        cblk = _cumsum_chunks(lblk)
        kb2blk = (kblk * jnp.exp(cblk - lblk)).astype(BF16)
        states = [st_ref[p] for p in range(npair)]
        for gi in reversed(range(g)):
            rows = slice(gi * c, (gi + 1) * c)
            kb2 = kb2blk[rows, :]
            dec = jnp.exp(cblk[gi * c + c - 1:(gi + 1) * c, :])
            vv = vblk[rows, :]
            for p in range(npair):
                sb_ref[sub * g + gi, p] = states[p].astype(BF16)
                upd = lax.dot_general(vv[:, p * vp:(p + 1) * vp], kb2[:, p * LANES:(p + 1) * LANES],
                                      (((0,), (0,)), ((), ())), preferred_element_type=F32)
                states[p] = states[p] * dec[:, p * LANES:(p + 1) * LANES] + jnp.where(bd, upd, 0.0)
        for p in range(npair):
            st_ref[p] = states[p]
        return carry

    lax.fori_loop(0, nsub, sub_block, 0)


def _head_rms_gate(o, gate, bdv, dv):
    ms = _split_dot(o * o, bdv) * (1.0 / dv)
    return (o * lax.rsqrt(ms + NORM_EPS) * _silu(gate.astype(F32))).astype(BF16)


def _la_fwd_kernel(q_ref, k_ref, v_ref, lf_ref, lb_ref, sb_ref, gate_ref, bdv_ref, o_ref, st_ref,
                   *, npair, vp, dv, nsub):
    g = LA_GROUP
    c = CHUNK
    gc = g * c

    @pl.when(pl.program_id(1) == 0)
    def _():
        st_ref[...] = jnp.zeros_like(st_ref)

    km, vm, bd = _head_masks(npair, vp)
    ri = lax.broadcasted_iota(jnp.int32, (c, LANES), 0)
    cj = lax.broadcasted_iota(jnp.int32, (c, LANES), 1) & (c - 1)
    lower = ri >= cj
    upper = ri <= cj
    mid = c // 2
    nt = (((1,), (1,)), ((), ()))
    tn = (((0,), (0,)), ((), ()))

    def sub_block(sub, carry):
        base = pl.multiple_of(sub * gc, gc)
        qblk = q_ref[pl.ds(base, gc), :].astype(F32)
        kblk = k_ref[pl.ds(base, gc), :].astype(F32)
        vblk = v_ref[pl.ds(base, gc), :]
        lfblk = lf_ref[pl.ds(base, gc), :]
        lbblk = lb_ref[pl.ds(base, gc), :]
        w = lfblk.shape[1]
        csblk = _cumsum_chunks(jnp.concatenate([lfblk, lbblk], axis=1))
        states = [st_ref[p] for p in range(npair)]
        outs = []
        for gi in range(g):
            rows = slice(gi * c, (gi + 1) * c)
            qq = qblk[rows, :]
            kk = kblk[rows, :]
            vv = vblk[rows, :]
            cf = csblk[rows, :w]
            cbi = csblk[rows, w:]
            cb = cbi - lbblk[rows, :]
            mf = cf[mid:mid + 1, :]
            mb = cb[mid:mid + 1, :]
            tf = cf[c - 1:c, :]
            tb = cbi[c - 1:c, :]
            qf = (qq * jnp.exp(cf - mf)).astype(BF16)
            kf = kk * jnp.exp(mf - cf)
            qb = (qq * jnp.exp(mb - cb)).astype(BF16)
            kb = kk * jnp.exp(cb - mb)
            q2 = jnp.concatenate([qq * jnp.exp(cf), qq * jnp.exp(tb - cb)], axis=0).astype(BF16)
            kf2 = (kk * jnp.exp(tf - cf)).astype(BF16)
            dec = jnp.exp(tf)
            pair_out = []
            for p in range(npair):
                ks = slice(p * LANES, (p + 1) * LANES)
                vsl = slice(p * vp, (p + 1) * vp)
                kfp = kf[:, ks]
                kbp = kb[:, ks]
                k2f = jnp.concatenate([jnp.where(km[0], kfp, 0.0), jnp.where(km[1], kfp, 0.0)], axis=0).astype(BF16)
                k2b = jnp.concatenate([jnp.where(km[0], kbp, 0.0), jnp.where(km[1], kbp, 0.0)], axis=0).astype(BF16)
                sf = lax.dot_general(qf[:, ks], k2f, nt, preferred_element_type=F32)
                sb = lax.dot_general(qb[:, ks], k2b, nt, preferred_element_type=F32)
                sc = (jnp.where(lower, sf, 0.0) + jnp.where(upper, sb, 0.0)).astype(BF16)
                vp_ = vv[:, vsl]
                zero = jnp.zeros_like(vp_)
                v2 = jnp.concatenate([jnp.where(vm[0], vp_, zero), jnp.where(vm[1], vp_, zero)], axis=0)
                o = jnp.dot(sc, v2, preferred_element_type=F32)
                st = states[p]
                o = o + lax.dot_general(q2[0:c, ks], st.astype(BF16), nt, preferred_element_type=F32)
                o = o + lax.dot_general(q2[c:2 * c, ks], sb_ref[sub * g + gi, p], nt, preferred_element_type=F32)
                upd = lax.dot_general(vp_, kf2[:, ks], tn, preferred_element_type=F32)
                states[p] = st * dec[:, ks] + jnp.where(bd, upd, 0.0)
                pair_out.append(o)
            outs.append(jnp.concatenate(pair_out, axis=1) if npair > 1 else pair_out[0])
        for p in range(npair):
            st_ref[p] = states[p]
        o = jnp.concatenate(outs, axis=0)
        o_ref[pl.ds(base, gc), :] = _head_rms_gate(o, gate_ref[pl.ds(base, gc), :], bdv_ref[...], dv)
        return carry

    lax.fori_loop(0, nsub, sub_block, 0)


def _head_block_ones(wv, seg):
    vr = lax.broadcasted_iota(jnp.int32, (wv, wv), 0) // seg
    vc = lax.broadcasted_iota(jnp.int32, (wv, wv), 1) // seg
    return (vr == vc).astype(BF16)


def _sub_blocks(lp, rows):
    n = lp // rows
    for cand in (5, 4, 3, 2):
        if n % cand == 0:
            return cand
    return 1


def _linear_attention(q, k, v, lf, lb, gate, batch, lp, *, npair, vp, dv):
    gc = LA_GROUP * CHUNK
    nsub = _sub_blocks(lp, gc)
    rows = nsub * gc
    nblk = lp // rows
    nc = lp // CHUNK
    wk = npair * LANES
    wv = npair * vp
    q3 = q.reshape(batch, lp, wk)
    k3 = k.reshape(batch, lp, wk)
    v3 = v.reshape(batch, lp, wv)
    g3 = gate.reshape(batch, lp, wv)
    lf3 = lf.reshape(batch, lp, wk)
    lb3 = lb.reshape(batch, lp, wk)
    fwd = lambda b, j: (b, j, 0)
    rev = lambda b, j: (b, nblk - 1 - j, 0)
    sblk = (None, nsub * LA_GROUP, npair, vp, LANES)

    sb = pl.pallas_call(
        functools.partial(_la_bwd_kernel, npair=npair, vp=vp, nsub=nsub),
        grid=(batch, nblk),
        in_specs=[pl.BlockSpec((None, rows, wk), rev), pl.BlockSpec((None, rows, wv), rev),
                  pl.BlockSpec((None, rows, wk), rev)],
        out_specs=pl.BlockSpec(sblk, lambda b, j: (b, nblk - 1 - j, 0, 0, 0)),
        out_shape=jax.ShapeDtypeStruct((batch, nc, npair, vp, LANES), BF16),
        scratch_shapes=[pltpu.VMEM((npair, vp, LANES), F32)],
        compiler_params=_params("parallel", "arbitrary"),
        name="la_bwd_states",
    )(k3, v3, lb3)

    o = pl.pallas_call(
        functools.partial(_la_fwd_kernel, npair=npair, vp=vp, dv=dv, nsub=nsub),
        grid=(batch, nblk),
        in_specs=[pl.BlockSpec((None, rows, wk), fwd), pl.BlockSpec((None, rows, wk), fwd),
                  pl.BlockSpec((None, rows, wv), fwd), pl.BlockSpec((None, rows, wk), fwd),
                  pl.BlockSpec((None, rows, wk), fwd),
                  pl.BlockSpec(sblk, lambda b, j: (b, j, 0, 0, 0)),
                  pl.BlockSpec((None, rows, wv), fwd),
                  pl.BlockSpec((wv, wv), lambda b, j: (0, 0))],
        out_specs=pl.BlockSpec((None, rows, wv), fwd),
        out_shape=jax.ShapeDtypeStruct((batch, lp, wv), BF16),
        scratch_shapes=[pltpu.VMEM((npair, vp, LANES), F32)],
        compiler_params=_params("parallel", "arbitrary"),
        name="la_fwd",
    )(q3, k3, v3, lf3, lb3, sb, g3, _head_block_ones(wv, vp // 2))
    return o.reshape(batch * lp, wv)


RET_CHUNK = 256


def _ret_decay_terms(lg_ref):
    c = RET_CHUNK
    lg = lg_ref[...]
    i = lax.broadcasted_iota(jnp.int32, (c, 1), 0).astype(F32)
    return dict(q_f=jnp.exp((i + 1.0) * lg), q_b=jnp.exp((float(c) - i) * lg),
                k_f=jnp.exp((float(c - 1) - i) * lg), k_b=jnp.exp(i * lg), dec=jnp.exp(float(c) * lg))


def _ret_bwd_kernel(k_ref, v_ref, lg_ref, sb_ref, st_ref, *, npair, nsub):
    c = RET_CHUNK
    tn = (((0,), (0,)), ((), ()))

    @pl.when(pl.program_id(1) == 0)
    def _():
        st_ref[...] = jnp.zeros_like(st_ref)

    _, _, bd = _head_masks(npair, LANES)
    t = _ret_decay_terms(lg_ref)

    def sub_block(it, carry):
        sub = nsub - 1 - it
        base = pl.multiple_of(sub * c, c)
        kb = (k_ref[pl.ds(base, c), :].astype(F32) * t["k_b"]).astype(BF16)
        vv = v_ref[pl.ds(base, c), :]
        for p in range(npair):
            ks = slice(p * LANES, (p + 1) * LANES)
            st = st_ref[p]
            sb_ref[sub, p] = st.astype(BF16)
            upd = lax.dot_general(vv[:, ks], kb[:, ks], tn, preferred_element_type=F32)
            st_ref[p] = st * t["dec"][:, ks] + jnp.where(bd, upd, 0.0)
        return carry

    lax.fori_loop(0, nsub, sub_block, 0)


def _ret_fwd_kernel(q_ref, k_ref, v_ref, lg_ref, dmask_ref, sb_ref, gate_ref, bdv_ref, o_ref, st_ref,
                    *, npair, nsub, dv):
    c = RET_CHUNK
    nt = (((1,), (1,)), ((), ()))
    tn = (((0,), (0,)), ((), ()))

    @pl.when(pl.program_id(1) == 0)
    def _():
        st_ref[...] = jnp.zeros_like(st_ref)

    km, vm, bd = _head_masks(npair, LANES)
    t = _ret_decay_terms(lg_ref)

    def sub_block(sub, carry):
        base = pl.multiple_of(sub * c, c)
        qb16 = q_ref[pl.ds(base, c), :]
        kb16 = k_ref[pl.ds(base, c), :]
        vv = v_ref[pl.ds(base, c), :]
        qq = qb16.astype(F32)
        kk = kb16.astype(F32)
        q2 = jnp.concatenate([qq * t["q_f"], qq * t["q_b"]], axis=0).astype(BF16)
        kf2 = (kk * t["k_f"]).astype(BF16)
        zero = jnp.zeros((c, LANES), BF16)
        pair_out = []
        for p in range(npair):
            ks = slice(p * LANES, (p + 1) * LANES)
            kp = kb16[:, ks]
            vp_ = vv[:, ks]
            k2 = jnp.concatenate([jnp.where(km[0], kp, zero), jnp.where(km[1], kp, zero)], axis=0)
            v2 = jnp.concatenate([jnp.where(vm[0], vp_, zero), jnp.where(vm[1], vp_, zero)], axis=0)
            s = lax.dot_general(qb16[:, ks], k2, nt, preferred_element_type=F32)
            o = jnp.dot((s * dmask_ref[p]).astype(BF16), v2, preferred_element_type=F32)
            st = st_ref[p]
            o = o + lax.dot_general(q2[0:c, ks], st.astype(BF16), nt, preferred_element_type=F32)
            o = o + lax.dot_general(q2[c:2 * c, ks], sb_ref[sub, p], nt, preferred_element_type=F32)
            upd = lax.dot_general(vp_, kf2[:, ks], tn, preferred_element_type=F32)
            st_ref[p] = st * t["dec"][:, ks] + jnp.where(bd, upd, 0.0)
            pair_out.append(o)
        o = jnp.concatenate(pair_out, axis=1)
        o_ref[pl.ds(base, c), :] = _head_rms_gate(o, gate_ref[pl.ds(base, c), :], bdv_ref[...], dv)
        return carry

    lax.fori_loop(0, nsub, sub_block, 0)


def _retention(q, k, v, gate, batch, lp):
    c = RET_CHUNK
    npair = RET_HEADS // 2
    w = npair * LANES
    nsub = _sub_blocks(lp, c)
    rows = nsub * c
    nblk = lp // rows
    log_gamma = jnp.log1p(-jnp.exp2(-5.0 - jnp.arange(RET_HEADS, dtype=F32)))
    lg = jnp.repeat(log_gamma, RET_DIM)[None]
    dist = jnp.abs(jnp.arange(c)[:, None] - (jnp.arange(2 * c) % c)[None, :]).astype(F32)
    dmask = jnp.exp(dist[None] * jnp.repeat(log_gamma, c).reshape(npair, 1, 2 * c))
    q3, k3, v3, g3 = (a.reshape(batch, lp, w) for a in (q, k, v, gate))
    fwd = lambda b, j: (b, j, 0)
    rev = lambda b, j: (b, nblk - 1 - j, 0)
    sblk = (None, nsub, npair, LANES, LANES)
    lgspec = pl.BlockSpec((1, w), lambda b, j: (0, 0))

    sb = pl.pallas_call(
        functools.partial(_ret_bwd_kernel, npair=npair, nsub=nsub),
        grid=(batch, nblk),
        in_specs=[pl.BlockSpec((None, rows, w), rev), pl.BlockSpec((None, rows, w), rev), lgspec],
        out_specs=pl.BlockSpec(sblk, lambda b, j: (b, nblk - 1 - j, 0, 0, 0)),
        out_shape=jax.ShapeDtypeStruct((batch, lp // c, npair, LANES, LANES), BF16),
        scratch_shapes=[pltpu.VMEM((npair, LANES, LANES), F32)],
        compiler_params=_params("parallel", "arbitrary"),
        name="ret_bwd_states",
    )(k3, v3, lg)

    o = pl.pallas_call(
        functools.partial(_ret_fwd_kernel, npair=npair, nsub=nsub, dv=RET_DIM),
        grid=(batch, nblk),
        in_specs=[pl.BlockSpec((None, rows, w), fwd), pl.BlockSpec((None, rows, w), fwd),
                  pl.BlockSpec((None, rows, w), fwd), lgspec,
                  pl.BlockSpec((npair, c, 2 * c), lambda b, j: (0, 0, 0)),
                  pl.BlockSpec(sblk, lambda b, j: (b, j, 0, 0, 0)),
                  pl.BlockSpec((None, rows, w), fwd),
                  pl.BlockSpec((w, w), lambda b, j: (0, 0))],
        out_specs=pl.BlockSpec((None, rows, w), fwd),
        out_shape=jax.ShapeDtypeStruct((batch, lp, w), BF16),
        scratch_shapes=[pltpu.VMEM((npair, LANES, LANES), F32)],
        compiler_params=_params("parallel", "arbitrary"),
        name="ret_fwd",
    )(q3, k3, v3, lg, dmask, sb, g3, _head_block_ones(w, RET_DIM))
    return o.reshape(batch * lp, w)


PACKED = D_MODEL // 2


def _pack_rows(x):
    bits = lax.bitcast_convert_type(x.astype(BF16).astype(F32), jnp.int32)
    return bits[:, :PACKED] | lax.shift_right_logical(bits[:, PACKED:], 16)


def _unpack_rows(p):
    hi = lax.bitcast_convert_type(p & jnp.int32(-65536), F32)
    lo = lax.bitcast_convert_type(lax.shift_left(p, jnp.int32(16)), F32)
    return jnp.concatenate([hi, lo], axis=1)


_RT_E1, _RT_E2, _RT_W1, _RT_W2, _RT_R1, _RT_R2 = range(6)


def _mix_out(ya_ref, yb_ref, yc_ref, h_ref, wglu_ref, wout_ref, g_ref):
    ya = _gelu_tanh(ya_ref[...].astype(F32))
    gl = jnp.dot(ya.astype(BF16), wglu_ref[...], preferred_element_type=F32)
    ya = (ya * _sigmoid(gl)).astype(BF16)
    y = jnp.concatenate([ya, yb_ref[...], yc_ref[...]], axis=1)
    h = h_ref[...] + jnp.dot(y, wout_ref[...], preferred_element_type=F32)
    ms = jnp.mean(h * h, axis=-1, keepdims=True)
    return h, h * lax.rsqrt(ms + NORM_EPS) * g_ref[...]


def _outproj_kernel(ya_ref, yb_ref, yc_ref, h_ref, wglu_ref, wout_ref, g_ref, hout_ref, hn_ref):
    h, hn = _mix_out(ya_ref, yb_ref, yc_ref, h_ref, wglu_ref, wout_ref, g_ref)
    hout_ref[...] = h
    hn_ref[...] = hn.astype(BF16)


def _outproj_router_kernel(ya_ref, yb_ref, yc_ref, h_ref, wglu_ref, wout_ref, g_ref, wrh_ref, wrl_ref, before_ref,
                           hout_ref, hn_ref, route_ref, cnt_ref, carry_ref):
    @pl.when(pl.program_id(0) == 0)
    def _():
        carry_ref[...] = jnp.zeros_like(carry_ref)

    h, hn = _mix_out(ya_ref, yb_ref, yc_ref, h_ref, wglu_ref, wout_ref, g_ref)
    hout_ref[...] = h
    hn_ref[...] = _pack_rows(hn)
    hi = hn.astype(BF16)
    lo = (hn - hi.astype(F32)).astype(BF16)
    wrh = wrh_ref[...]
    logits = (jnp.dot(hi, wrh, preferred_element_type=F32) + jnp.dot(lo, wrh, preferred_element_type=F32)
              + jnp.dot(hi, wrl_ref[...], preferred_element_type=F32))
    lane = lax.broadcasted_iota(jnp.int32, logits.shape, 1)
    neg = jnp.float32(-jnp.inf)
    logits = jnp.where(lane < N_EXPERTS, logits, neg)
    v1 = jnp.max(logits, axis=-1, keepdims=True)
    i1 = jnp.min(jnp.where(logits == v1, lane, LANES), axis=-1, keepdims=True)
    m1 = lane == i1
    l2 = jnp.where(m1, neg, logits)
    v2 = jnp.max(l2, axis=-1, keepdims=True)
    i2 = jnp.min(jnp.where(l2 == v2, lane, LANES), axis=-1, keepdims=True)
    m2 = lane == i2
    e = jnp.exp(v2 - v1)
    w1 = 1.0 / (1.0 + e)
    w2 = e * w1
    chosen = jnp.where(m1 | m2, 1.0, 0.0)
    prefix = jnp.dot(before_ref[...], chosen.astype(BF16), preferred_element_type=F32) + carry_ref[...]
    r1 = jnp.sum(jnp.where(m1, prefix, 0.0), axis=-1, keepdims=True)
    r2 = jnp.sum(jnp.where(m2, prefix, 0.0), axis=-1, keepdims=True)
    total = carry_ref[...] + jnp.sum(chosen, axis=0, keepdims=True)
    carry_ref[...] = total
    cnt_ref[...] = total
    rec = jnp.zeros_like(logits)
    for ln, val in ((_RT_E1, i1.astype(F32)), (_RT_E2, i2.astype(F32)), (_RT_W1, w1), (_RT_W2, w2),
                    (_RT_R1, r1), (_RT_R2, r2)):
        rec = jnp.where(lane == ln, val, rec)
    route_ref[...] = rec


def _outproj(ya, yb, yc, h, wglu, wout, gamma, wr=None):
    r = h.shape[0]
    tm = ROW_TILE
    with_router = wr is not None

    def rows(wd):
        return pl.BlockSpec((tm, wd), lambda i: (i, 0))

    in_specs = [rows(256), rows(384), rows(512), rows(D_MODEL), _const_spec((256, 256)),
                _const_spec((1152, D_MODEL)), _const_spec((1, D_MODEL))]
    args = [ya, yb, yc, h, wglu, wout, gamma]
    if not with_router:
        return pl.pallas_call(
            _outproj_kernel,
            grid=(r // tm,),
            in_specs=in_specs, out_specs=[rows(D_MODEL), rows(D_MODEL)],
            out_shape=[jax.ShapeDtypeStruct((r, D_MODEL), F32), jax.ShapeDtypeStruct((r, D_MODEL), BF16)],
            input_output_aliases={3: 0},
            compiler_params=_params("parallel"),
            name="outproj",
        )(*args)
    earlier = (lax.broadcasted_iota(jnp.int32, (tm, tm), 1) < lax.broadcasted_iota(jnp.int32, (tm, tm), 0)).astype(BF16)
    return pl.pallas_call(
        _outproj_router_kernel,
        grid=(r // tm,),
        in_specs=in_specs + [_const_spec((D_MODEL, LANES)), _const_spec((D_MODEL, LANES)), _const_spec((tm, tm))],
        out_specs=[rows(D_MODEL), rows(PACKED), rows(LANES), _const_spec((1, LANES))],
        out_shape=[jax.ShapeDtypeStruct((r, D_MODEL), F32), jax.ShapeDtypeStruct((r, PACKED), jnp.int32),
                   jax.ShapeDtypeStruct((r, LANES), F32), jax.ShapeDtypeStruct((1, LANES), F32)],
        scratch_shapes=[pltpu.VMEM((1, LANES), F32)],
        input_output_aliases={3: 0},
        compiler_params=_params("arbitrary"),
        name="outproj_router",
    )(*args, wr[0], wr[1], earlier)


FF_TILE = 1408


def _ffn_kernel(hn_ref, h_ref, wg_ref, wu_ref, wd_ref, o_ref, acc_ref):
    j = pl.program_id(1)
    hn = hn_ref[...]
    a = _silu(jnp.dot(hn, wg_ref[...], preferred_element_type=F32)) * jnp.dot(hn, wu_ref[...], preferred_element_type=F32)
    y = jnp.dot(a.astype(BF16), wd_ref[...], preferred_element_type=F32)

    @pl.when(j == 0)
    def _():
        acc_ref[...] = h_ref[...] + y

    @pl.when(j != 0)
    def _():
        acc_ref[...] += y

    @pl.when(j == pl.num_programs(1) - 1)
    def _():
        o_ref[...] = acc_ref[...]


def _ffn(hn, h, wg, wu, wd):
    r = h.shape[0]
    tm = ROW_TILE
    nf = D_FF // FF_TILE
    return pl.pallas_call(
        _ffn_kernel,
        grid=(r // tm, nf),
        in_specs=[pl.BlockSpec((tm, D_MODEL), lambda i, j: (i, 0)),
                  pl.BlockSpec((tm, D_MODEL), lambda i, j: (i, 0)),
                  pl.BlockSpec((D_MODEL, FF_TILE), lambda i, j: (0, j)),
                  pl.BlockSpec((D_MODEL, FF_TILE), lambda i, j: (0, j)),
                  pl.BlockSpec((FF_TILE, D_MODEL), lambda i, j: (j, 0))],
        out_specs=pl.BlockSpec((tm, D_MODEL), lambda i, j: (i, 0)),
        out_shape=jax.ShapeDtypeStruct((r, D_MODEL), F32),
        scratch_shapes=[pltpu.VMEM((tm, D_MODEL), F32)],
        input_output_aliases={1: 0},
        compiler_params=_params("parallel", "arbitrary"),
        name="ffn",
    )(hn, h, wg, wu, wd)


MOE_BLOCK = 512
GATHER_TILE = 256


def _route_meta(route, cnt, r):
    bm = MOE_BLOCK
    nb = 2 * r // bm + N_EXPERTS
    e1 = route[:, _RT_E1].astype(jnp.int32)
    e2 = route[:, _RT_E2].astype(jnp.int32)
    counts = cnt[0, :N_EXPERTS].astype(jnp.int32)
    padded = ((counts + bm - 1) // bm) * bm
    ends = jnp.cumsum(padded)
    starts = ends - padded
    pos1 = starts[e1] + route[:, _RT_R1].astype(jnp.int32)
    pos2 = starts[e2] + route[:, _RT_R2].astype(jnp.int32)
    n_used = (ends[-1] // bm).astype(jnp.int32)
    blk = jnp.arange(nb, dtype=jnp.int32)
    blk = jnp.minimum(blk, n_used - 1)
    block_expert = jnp.sum((blk[:, None] * bm >= ends[None, :]).astype(jnp.int32), axis=1)
    block_expert = jnp.minimum(block_expert, N_EXPERTS - 1).astype(jnp.int32)
    return pos1, pos2, block_expert, n_used.reshape(1), nb


def _dispatch_kernel(p1_ref, p2_ref, hn_ref, xs_in_ref, xs_ref, sem):
    del xs_in_ref
    n = hn_ref.shape[0]

    def issue(r, c):
        src = hn_ref.at[pl.ds(r, 1), :]
        pltpu.make_async_copy(src, xs_ref.at[pl.ds(p1_ref[0, 0, r], 1), :], sem).start()
        pltpu.make_async_copy(src, xs_ref.at[pl.ds(p2_ref[0, 0, r], 1), :], sem).start()
        return c

    lax.fori_loop(0, n, issue, 0, unroll=8)

    def drain(r, c):
        row = pltpu.make_async_copy(hn_ref.at[pl.ds(0, 1), :], xs_ref.at[pl.ds(0, 1), :], sem)
        row.wait()
        row.wait()
        return c

    lax.fori_loop(0, n, drain, 0, unroll=8)


def _dispatch(hn, pos1, pos2, nb):
    r = hn.shape[0]
    t = GATHER_TILE
    p = nb * MOE_BLOCK
    idx = pl.BlockSpec((1, 1, t), lambda i: (i, 0, 0), memory_space=pltpu.SMEM)
    return pl.pallas_call(
        _dispatch_kernel,
        grid=(r // t,),
        in_specs=[idx, idx, pl.BlockSpec((t, PACKED), lambda i: (i, 0)), pl.BlockSpec(memory_space=pl.ANY)],
        out_specs=pl.BlockSpec(memory_space=pl.ANY),
        out_shape=jax.ShapeDtypeStruct((p, PACKED), jnp.int32),
        scratch_shapes=[pltpu.SemaphoreType.DMA(())],
        input_output_aliases={3: 0},
        compiler_params=_params("arbitrary"),
        name="moe_dispatch",
    )(pos1.reshape(r // t, 1, t), pos2.reshape(r // t, 1, t), hn, jnp.zeros((p, PACKED), jnp.int32))


def _moe_ffn_kernel(be_ref, nu_ref, x_ref, wg_ref, wu_ref, wd_ref, o_ref, acc_ref):
    del be_ref
    b = pl.program_id(0)
    j = pl.program_id(1)
    last = j == pl.num_programs(1) - 1
    used = b < nu_ref[0]

    @pl.when(used)
    def _():
        x = _unpack_rows(x_ref[...]).astype(BF16)
        a = _silu(jnp.dot(x, wg_ref[...], preferred_element_type=F32)) * jnp.dot(x, wu_ref[...], preferred_element_type=F32)
        y = jnp.dot(a.astype(BF16), wd_ref[...], preferred_element_type=F32)

        @pl.when(j == 0)
        def _():
            acc_ref[...] = y

        @pl.when(j != 0)
        def _():
            acc_ref[...] += y

        @pl.when(last)
        def _():
            o_ref[...] = _pack_rows(acc_ref[...])

    @pl.when(jnp.logical_not(used) & last)
    def _():
        o_ref[...] = jnp.zeros_like(o_ref)


def _moe_ffn(xs, block_expert, n_used, wg, wu, wd):
    p = xs.shape[0]
    bm = MOE_BLOCK
    nf = D_FF // FF_TILE
    grid_spec = pltpu.PrefetchScalarGridSpec(
        num_scalar_prefetch=2,
        grid=(p // bm, nf),
        in_specs=[pl.BlockSpec((bm, PACKED), lambda b, j, be, nu: (b, 0)),
                  pl.BlockSpec((None, D_MODEL, FF_TILE), lambda b, j, be, nu: (be[b], 0, j)),
                  pl.BlockSpec((None, D_MODEL, FF_TILE), lambda b, j, be, nu: (be[b], 0, j)),
                  pl.BlockSpec((None, FF_TILE, D_MODEL), lambda b, j, be, nu: (be[b], j, 0))],
        out_specs=pl.BlockSpec((bm, PACKED), lambda b, j, be, nu: (b, 0)),
        scratch_shapes=[pltpu.VMEM((bm, D_MODEL), F32)])
    return pl.pallas_call(
        _moe_ffn_kernel,
        grid_spec=grid_spec,
        out_shape=jax.ShapeDtypeStruct((p, PACKED), jnp.int32),
        compiler_params=_params("arbitrary", "arbitrary"),
        name="moe_ffn",
    )(block_expert, n_used, xs, wg, wu, wd)


def _combine_kernel(p1_ref, p2_ref, route_ref, h_ref, g_ref, ys_ref, o_ref, buf1, buf2, sems, *, final):
    n = h_ref.shape[0]
    half = n // 2

    def issue(k):
        def body(r, c):
            pltpu.make_async_copy(ys_ref.at[pl.ds(p1_ref[0, 0, r], 1), :], buf1.at[pl.ds(r, 1), :], sems.at[k]).start()
            pltpu.make_async_copy(ys_ref.at[pl.ds(p2_ref[0, 0, r], 1), :], buf2.at[pl.ds(r, 1), :], sems.at[k]).start()
            return c
        lax.fori_loop(k * half, (k + 1) * half, body, 0, unroll=8)

    def drain(k):
        def body(r, c):
            row = pltpu.make_async_copy(ys_ref.at[pl.ds(0, 1), :], buf1.at[pl.ds(0, 1), :], sems.at[k])
            row.wait()
            row.wait()
            return c
        lax.fori_loop(0, half, body, 0, unroll=8)

    issue(0)
    issue(1)
    for k in range(2):
        drain(k)
        rows = slice(k * half, (k + 1) * half)
        rt = route_ref[rows, :]
        lane = lax.broadcasted_iota(jnp.int32, rt.shape, 1)
        w1 = jnp.sum(jnp.where(lane == _RT_W1, rt, 0.0), axis=-1, keepdims=True)
        w2 = jnp.sum(jnp.where(lane == _RT_W2, rt, 0.0), axis=-1, keepdims=True)
        h = h_ref[rows, :] + w1 * _unpack_rows(buf1[rows, :]) + w2 * _unpack_rows(buf2[rows, :])
        if final:
            ms = jnp.mean(h * h, axis=-1, keepdims=True)
            h = h * lax.rsqrt(ms + NORM_EPS) * g_ref[...]
        o_ref[rows, :] = h


def _combine(ys, pos1, pos2, route, h, gamma, batch, lp, final):
    r = h.shape[0]
    t = GATHER_TILE
    p1 = pos1.reshape(r // t, 1, t)
    p2 = pos2.reshape(r // t, 1, t)
    scratch = [pltpu.VMEM((t, PACKED), jnp.int32), pltpu.VMEM((t, PACKED), jnp.int32), pltpu.SemaphoreType.DMA((2,))]
    if final:
        per_seq = lp // t
        skip = FRONT // t
        rb = lambda b, i: b * per_seq + skip + i
        grid = (batch, per_seq - skip)
        idx = pl.BlockSpec((1, 1, t), lambda b, i: (rb(b, i), 0, 0), memory_space=pltpu.SMEM)
        in_specs = [idx, idx, pl.BlockSpec((t, LANES), lambda b, i: (rb(b, i), 0)),
                    pl.BlockSpec((t, D_MODEL), lambda b, i: (rb(b, i), 0)),
                    pl.BlockSpec((1, D_MODEL), lambda b, i: (0, 0)), pl.BlockSpec(memory_space=pl.ANY)]
        out_specs = pl.BlockSpec((None, t, D_MODEL), lambda b, i: (b, i, 0))
        out_shape = jax.ShapeDtypeStruct((batch, lp - FRONT, D_MODEL), F32)
        sem = ("arbitrary", "arbitrary")
    else:
        grid = (r // t,)
        idx = pl.BlockSpec((1, 1, t), lambda i: (i, 0, 0), memory_space=pltpu.SMEM)
        in_specs = [idx, idx, pl.BlockSpec((t, LANES), lambda i: (i, 0)), pl.BlockSpec((t, D_MODEL), lambda i: (i, 0)),
                    pl.BlockSpec((1, D_MODEL), lambda i: (0, 0)), pl.BlockSpec(memory_space=pl.ANY)]
        out_specs = pl.BlockSpec((t, D_MODEL), lambda i: (i, 0))
        out_shape = jax.ShapeDtypeStruct((r, D_MODEL), F32)
        sem = ("arbitrary",)
    return pl.pallas_call(
        functools.partial(_combine_kernel, final=final),
        grid=grid, in_specs=in_specs, out_specs=out_specs, out_shape=out_shape, scratch_shapes=scratch,
        compiler_params=_params(*sem),
        name="moe_combine_final" if final else "moe_combine",
    )(p1, p2, route, h, gamma, ys)


def _final_norm_kernel(h_ref, g_ref, o_ref):
    h = h_ref[...]
    ms = jnp.mean(h * h, axis=-1, keepdims=True)
    o_ref[...] = h * lax.rsqrt(ms + NORM_EPS) * g_ref[...]


def _final_norm(h, gamma, batch, lp):
    t = GATHER_TILE
    per_seq = lp // t
    skip = FRONT // t
    return pl.pallas_call(
        _final_norm_kernel,
        grid=(batch, per_seq - skip),
        in_specs=[pl.BlockSpec((t, D_MODEL), lambda b, i: (b * per_seq + skip + i, 0)),
                  pl.BlockSpec((1, D_MODEL), lambda b, i: (0, 0))],
        out_specs=pl.BlockSpec((None, t, D_MODEL), lambda b, i: (b, i, 0)),
        out_shape=jax.ShapeDtypeStruct((batch, lp - FRONT, D_MODEL), F32),
        compiler_params=_params("parallel", "parallel"),
        name="final_norm",
    )(h, gamma)


def _rope_tables(lp):
    half = RET_DIM // 2
    pos = jnp.arange(lp, dtype=F32) - float(PAD)
    inv = ROPE_BASE ** (-jnp.arange(half, dtype=F32) / half)
    ang = pos[:, None] * inv[None, :]
    cos = jnp.tile(jnp.cos(ang), (1, 2 * RET_HEADS))
    sin = jnp.tile(jnp.sin(ang), (1, 2 * RET_HEADS))
    return cos, sin


def _prep_layer(li, p):
    w_out = p["w_out"][li].astype(F32)
    wc = w_out[640:1024].reshape(GLA_HEADS, GLA_DV, D_MODEL)
    wc = jnp.pad(wc, ((0, 0), (0, GLA_DV_PAD - GLA_DV), (0, 0))).reshape(GLA_HEADS * GLA_DV_PAD, D_MODEL)
    wgate, bgate = _pack_gate(p["gla_w_gate_f"][li], p["gla_b_gate_f"][li], p["gla_w_gate_b"][li], p["gla_b_gate_b"][li])
    return dict(
        norm_mix=p["norm_mix"][li].astype(F32)[None],
        w_in=_pack_w_in(p["w_in"][li].astype(F32)),
        wgate=wgate, bgate=bgate,
        s5=_s5_tables(p["s5_lambda_re"][li], p["s5_lambda_im"][li], p["s5_log_dt"][li], p["s5_b_re"][li],
                      p["s5_b_im"][li], p["s5_c_re"][li], p["s5_c_im"][li], p["s5_d"][li]),
        w_glu=p["s5_w_glu"][li].astype(BF16),
        w_out=jnp.concatenate([w_out[0:640], wc], axis=0).astype(BF16),
        norm_ffn=p["norm_ffn"][li].astype(F32)[None],
    )


def _trunk(x, meta_tokens, layers, ffn, moe, norm_final, depth):
    batch, seq, _ = x.shape
    lp = seq + FRONT
    r = batch * lp
    meta = jnp.broadcast_to(meta_tokens.astype(F32)[None], (batch, N_META, D_MODEL))
    h = jnp.concatenate([jnp.zeros((batch, PAD, D_MODEL), F32), meta, x.astype(F32)], axis=1).reshape(r, D_MODEL)
    cos, sin = _rope_tables(lp)
    cos = jnp.tile(cos, (batch, 1))
    sin = jnp.tile(sin, (batch, 1))
    gamma_final = norm_final.astype(F32)[None]
    for li in range(depth):
        lw = layers[li]
        u, rq, rk, rv, rg, gq, gk, gv, gg, lf, lb = _inproj(
            h, lw["norm_mix"], lw["w_in"], cos, sin, lw["wgate"], lw["bgate"], batch, lp)
        ya = _s5_mixer(u, lw["s5"], batch, lp)
        yb = _retention(rq, rk, rv, rg, batch, lp)
        yc = _linear_attention(gq, gk, gv, lf, lb, gg, batch, lp, npair=GLA_HEADS // 2, vp=2 * GLA_DV_PAD, dv=GLA_DV)
        j = li // 2
        last = li == depth - 1
        if li % 2 == 0:
            h, hn = _outproj(ya, yb, yc, h, lw["w_glu"], lw["w_out"], lw["norm_ffn"])
            h = _ffn(hn, h, ffn["wg"][j], ffn["wu"][j], ffn["wd"][j])
        else:
            h, hn, route, cnt = _outproj(ya, yb, yc, h, lw["w_glu"], lw["w_out"], lw["norm_ffn"], moe["wr"][j])
            pos1, pos2, block_expert, n_used, nb = _route_meta(route, cnt, r)
            xs = _dispatch(hn, pos1, pos2, nb)
            ys = _moe_ffn(xs, block_expert, n_used, moe["wg"][j], moe["wu"][j], moe["wd"][j])
            h = _combine(ys, pos1, pos2, route, h, gamma_final, batch, lp, final=last)
            if last:
                return h
    return _final_norm(h, gamma_final, batch, lp)


def kernel(x_prompt, x_sample, meta_tokens, norm_mix, w_in, s5_lambda_re, s5_lambda_im, s5_log_dt, s5_b_re, s5_b_im, s5_c_re, s5_c_im, s5_d, s5_w_glu, gla_w_gate_f, gla_b_gate_f, gla_w_gate_b, gla_b_gate_b, w_out, norm_ffn, ffn_w_gate, ffn_w_up, ffn_w_down, router_w, moe_w_gate, moe_w_up, moe_w_down, norm_final):
    depth = w_in.shape[0]
    p = dict(norm_mix=norm_mix, w_in=w_in, s5_lambda_re=s5_lambda_re, s5_lambda_im=s5_lambda_im,
             s5_log_dt=s5_log_dt, s5_b_re=s5_b_re, s5_b_im=s5_b_im, s5_c_re=s5_c_re, s5_c_im=s5_c_im, s5_d=s5_d,
             s5_w_glu=s5_w_glu, gla_w_gate_f=gla_w_gate_f, gla_b_gate_f=gla_b_gate_f, gla_w_gate_b=gla_w_gate_b,
             gla_b_gate_b=gla_b_gate_b, w_out=w_out, norm_ffn=norm_ffn)
    layers = [_prep_layer(li, p) for li in range(depth)]
    ffn = dict(wg=ffn_w_gate.astype(BF16), wu=ffn_w_up.astype(BF16), wd=ffn_w_down.astype(BF16))
    wr = jnp.pad(router_w.astype(F32), ((0, 0), (0, 0), (0, LANES - N_EXPERTS)))
    wr_hi = wr.astype(BF16)
    wr_lo = (wr - wr_hi.astype(F32)).astype(BF16)
    wr = [(wr_hi[j], wr_lo[j]) for j in range(wr.shape[0])]
    moe = dict(wr=wr, wg=moe_w_gate.astype(BF16), wu=moe_w_up.astype(BF16), wd=moe_w_down.astype(BF16))
    y_prompt = _trunk(x_prompt, meta_tokens, layers, ffn, moe, norm_final, depth)
    y_sample = _trunk(x_sample, meta_tokens, layers, ffn, moe, norm_final, depth)
    return (y_prompt, y_sample)
```

```python
import functools
import math

import jax
import jax.numpy as jnp
from jax import lax
from jax.experimental import pallas as pl
from jax.experimental.pallas import tpu as pltpu

F32 = jnp.float32
BF16 = jnp.bfloat16

D_MODEL = 1024
N_META = 16
S5_P = 16
S5_WIDTH = 256
S5_GROUPS = 16
S5_N = 64
RET_HEADS = 6
RET_DIM = 64
RET_WIDTH = 384
GLA_HEADS = 4
GLA_DV = 96
GLA_DK = 48
GLA_QK = 192
GLA_WIDTH = 384
GLA_GATE_RANK = 16
GLA_TAU = 16.0
ROPE_BASE = 10000.0
D_FF = 2816
N_EXPERTS = 8
NORM_EPS = 1e-5

CHUNK = 64
FRONT = 256
PAD = FRONT - N_META
ROW_TILE = 512
LA_GROUP = 4
LANES = 128
GLA_DK_PAD = 64
GLA_DV_PAD = 128
VMEM_LIMIT = 56 * 1024 * 1024


def _params(*sem):
    return pltpu.CompilerParams(dimension_semantics=sem, vmem_limit_bytes=VMEM_LIMIT)


def _const_spec(shape):
    nd = len(shape)
    return pl.BlockSpec(shape, lambda *_: (0,) * nd)


def _sigmoid(x):
    return 1.0 / (1.0 + jnp.exp(-x))


def _silu(x):
    return x * _sigmoid(x)


def _gelu_tanh(x):
    c = math.sqrt(2.0 / math.pi)
    return 0.5 * x * (1.0 + jnp.tanh(c * (x + 0.044715 * (x * x * x))))


def _log_sigmoid(z):
    return jnp.minimum(z, 0.0) - jnp.log(1.0 + jnp.exp(-jnp.abs(z)))


def _split_dot(a, b_bf16, dims=None):
    hi = a.astype(BF16)
    lo = (a - hi.astype(F32)).astype(BF16)
    if dims is None:
        return (jnp.dot(hi, b_bf16, preferred_element_type=F32)
                + jnp.dot(lo, b_bf16, preferred_element_type=F32))
    return (lax.dot_general(hi, b_bf16, dims, preferred_element_type=F32)
            + lax.dot_general(lo, b_bf16, dims, preferred_element_type=F32))


_C_U = 0
_C_RQ = 256
_C_RK = 640
_C_RV = 1024
_C_RG = 1408
_C_GQ = 1792
_C_GK = 2048
_C_GV = 2304
_C_GG = 2816
_C_GL = 3328
_C_END = 3456


def _pack_w_in(w):
    o = 0
    u = w[:, o:o + 256]; o += 256
    rq = w[:, o:o + 384]; o += 384
    rk = w[:, o:o + 384]; o += 384
    rv = w[:, o:o + 384]; o += 384
    rg = w[:, o:o + 384]; o += 384
    gq = w[:, o:o + 192]; o += 192
    gk = w[:, o:o + 192]; o += 192
    gv = w[:, o:o + 384]; o += 384
    gg = w[:, o:o + 384]; o += 384
    glf = w[:, o:o + 16]; o += 16
    glb = w[:, o:o + 16]; o += 16

    def padh(m, d, dp):
        m = m.reshape(D_MODEL, GLA_HEADS, d)
        return jnp.pad(m, ((0, 0), (0, 0), (0, dp - d))).reshape(D_MODEL, GLA_HEADS * dp)

    gl = jnp.pad(jnp.concatenate([glf, glb], axis=1), ((0, 0), (0, LANES - 2 * GLA_GATE_RANK)))
    cat = jnp.concatenate([
        u, rq, rk, rv, rg,
        padh(gq, GLA_DK, GLA_DK_PAD), padh(gk, GLA_DK, GLA_DK_PAD),
        padh(gv, GLA_DV, GLA_DV_PAD), padh(gg, GLA_DV, GLA_DV_PAD), gl], axis=1)
    return cat.astype(BF16)


def _pack_gate(w_f, b_f, w_b, b_b):
    def padh(m):
        m = m.reshape(m.shape[0], GLA_HEADS, GLA_DK)
        return jnp.pad(m, ((0, 0), (0, 0), (0, GLA_DK_PAD - GLA_DK))).reshape(m.shape[0], GLA_HEADS * GLA_DK_PAD)
    r = GLA_GATE_RANK
    w = jnp.zeros((LANES, 2 * GLA_HEADS * GLA_DK_PAD), F32)
    w = w.at[0:r, 0:256].set(padh(w_f.astype(F32)))
    w = w.at[r:2 * r, 256:512].set(padh(w_b.astype(F32)))
    b = jnp.concatenate([padh(b_f.astype(F32)[None]), padh(b_b.astype(F32)[None])], axis=1)
    return w.astype(BF16), b


def _inproj_kernel(h_ref, g_ref, w_ref, cos_ref, sin_ref, wgate_ref, bgate_ref,
                   u_ref, rq_ref, rk_ref, rv_ref, rg_ref, gq_ref, gk_ref, gv_ref, gg_ref, lf_ref, lb_ref,
                   *, batch, lp):
    tm = h_ref.shape[0]
    x = h_ref[...]
    ms = jnp.mean(x * x, axis=-1, keepdims=True)
    row = pl.program_id(0) * tm + lax.broadcasted_iota(jnp.int32, (tm, 1), 0)
    valid = jnp.ones((tm, 1), F32)
    for b in range(batch):
        valid = jnp.where((row >= b * lp) & (row < b * lp + PAD), 0.0, valid)
    hn = (x * (lax.rsqrt(ms + NORM_EPS) * valid) * g_ref[...]).astype(BF16)

    full = jnp.dot(hn, w_ref[...], preferred_element_type=F32)

    def proj(lo, hi):
        return full[:, lo:hi]

    u_ref[...] = proj(_C_U, _C_RQ)
    cos = cos_ref[...]
    sin = sin_ref[...]
    half = RET_DIM // 2
    first_half = (lax.broadcasted_iota(jnp.int32, (1, RET_WIDTH), 1) & (RET_DIM - 1)) < half

    def rope(x):
        rot = jnp.where(first_half, -pltpu.roll(x, RET_WIDTH - half, 1), pltpu.roll(x, half, 1))
        return x * cos + rot * sin

    rq_ref[...] = rope(proj(_C_RQ, _C_RK)).astype(BF16)
    rk_ref[...] = (rope(proj(_C_RK, _C_RV)) * (RET_DIM ** -0.5)).astype(BF16)
    rv_ref[...] = proj(_C_RV, _C_RG).astype(BF16)
    rg_ref[...] = proj(_C_RG, _C_GQ).astype(BF16)
    gq_ref[...] = (proj(_C_GQ, _C_GK) * (GLA_DK ** -0.5)).astype(BF16)
    gk_ref[...] = proj(_C_GK, _C_GV).astype(BF16)
    gv_ref[...] = proj(_C_GV, _C_GG).astype(BF16)
    gg_ref[...] = proj(_C_GG, _C_GL).astype(BF16)
    codes = proj(_C_GL, _C_END).astype(BF16)
    z = jnp.dot(codes, wgate_ref[...], preferred_element_type=F32) + bgate_ref[...]
    ls = _log_sigmoid(z) * (1.0 / GLA_TAU)
    lf_ref[...] = ls[:, 0:256]
    lb_ref[...] = ls[:, 256:512]


def _inproj(h, gamma, w, cos, sin, wgate, bgate, batch, lp):
    r = h.shape[0]
    tm = ROW_TILE
    widths = (256, 384, 384, 384, 384, 256, 256, 512, 512, 256, 256)
    dtypes = (F32,) + (BF16,) * 8 + (F32, F32)

    def rows(wd):
        return pl.BlockSpec((tm, wd), lambda i: (i, 0))

    return pl.pallas_call(
        functools.partial(_inproj_kernel, batch=batch, lp=lp),
        grid=(r // tm,),
        in_specs=[rows(D_MODEL), _const_spec((1, D_MODEL)), _const_spec((D_MODEL, _C_END)),
                  rows(RET_WIDTH), rows(RET_WIDTH), _const_spec((LANES, 512)), _const_spec((1, 512))],
        out_specs=[rows(wd) for wd in widths],
        out_shape=[jax.ShapeDtypeStruct((r, wd), dt) for wd, dt in zip(widths, dtypes)],
        compiler_params=_params("parallel"),
        name="inproj",
    )(h, gamma, w, cos, sin, wgate, bgate)


def _s5_toeplitz_kernel(pwr_ref, pwi_ref, cer_ref, cei_ref, bbr_ref, bbi_ref, d_ref, tt_ref):
    hp = lax.Precision.HIGHEST
    w = CHUNK * S5_P
    lane = lax.broadcasted_iota(jnp.int32, (S5_P, w), 1)
    krow = []
    for dr in range(2):
        pr, pi, cr, ci = pwr_ref[dr], pwi_ref[dr], cer_ref[dr], cei_ref[dr]
        zr = pr * cr - pi * ci
        zi = pr * ci + pi * cr
        krow.append(jnp.dot(bbr_ref[dr], zr, precision=hp, preferred_element_type=F32)
                    - jnp.dot(bbi_ref[dr], zi, precision=hp, preferred_element_type=F32))
    kf = krow[0] + d_ref[...]
    kb = krow[1]
    for s in range(CHUNK):
        right = S5_P * s
        left = S5_P * (CHUNK - 1 - s)
        a = kf if right == 0 else jnp.where(lane >= right, pltpu.roll(kf, right, 1), 0.0)
        b = kb if left == 0 else jnp.where(lane < w - left, pltpu.roll(kb, w - left, 1), 0.0)
        tt_ref[S5_P * s:S5_P * (s + 1), :] = (a + b).astype(BF16)


def _s5_toeplitz(pwr, pwi, cer, cei, bbr, bbi, d_e):
    w = CHUNK * S5_P
    big = pl.BlockSpec((None, 2, S5_N, w), lambda g: (g, 0, 0, 0))
    small = pl.BlockSpec((None, 2, S5_P, S5_N), lambda g: (g, 0, 0, 0))
    return pl.pallas_call(
        _s5_toeplitz_kernel,
        grid=(S5_GROUPS,),
        in_specs=[big, big, big, big, small, small, pl.BlockSpec((None, S5_P, w), lambda g: (g, 0, 0))],
        out_specs=pl.BlockSpec((None, w, w), lambda g: (g, 0, 0)),
        out_shape=jax.ShapeDtypeStruct((S5_GROUPS, w, w), BF16),
        compiler_params=_params("parallel"),
        name="s5_toeplitz",
    )(pwr, pwi, cer, cei, bbr, bbi, d_e)


def _s5_tables(lam_re, lam_im, log_dt, b_re, b_im, c_re, c_im, d):
    c = CHUNK
    g_, n_, p_ = S5_GROUPS, S5_N, S5_P
    dt = jnp.exp(log_dt.astype(F32))[..., None]
    lr = lam_re.astype(F32)
    li = lam_im.astype(F32)
    e = lr * dt
    th = li * dt
    mag = jnp.exp(e)
    a_re = mag * jnp.cos(th)
    a_im = mag * jnp.sin(th)
    den = lr * lr + li * li
    nr = a_re - 1.0
    ni = a_im
    coef_re = (nr * lr + ni * li) / den
    coef_im = (ni * lr - nr * li) / den
    br = b_re.astype(F32)
    bi = b_im.astype(F32)
    bb_re = coef_re[..., None] * br - coef_im[..., None] * bi
    bb_im = coef_re[..., None] * bi + coef_im[..., None] * br
    cr = c_re.astype(F32)
    ci = c_im.astype(F32)
    tau = jnp.arange(c + 1, dtype=F32)[:, None, None, None]
    pw_mag = jnp.exp(tau * e[None])
    pw_re = pw_mag * jnp.cos(tau * th[None])
    pw_im = pw_mag * jnp.sin(tau * th[None])

    z_re = cr[None] * pw_re[:, :, :, None, :] - ci[None] * pw_im[:, :, :, None, :]
    z_im = cr[None] * pw_im[:, :, :, None, :] + ci[None] * pw_re[:, :, :, None, :]
    idx = jnp.arange(c)

    def expand_pw(pw):
        both = jnp.stack([pw[:c, 0], pw[c - 1 - idx, 1]], axis=0)
        both = jnp.transpose(both, (2, 0, 3, 1))
        return jnp.broadcast_to(both[..., None], (g_, 2, n_, c, p_)).reshape(g_, 2, n_, c * p_)

    def expand_c(cm):
        cm = jnp.transpose(cm, (1, 0, 3, 2))
        return jnp.broadcast_to(cm[:, :, :, None, :], (g_, 2, n_, c, p_)).reshape(g_, 2, n_, c * p_)

    bbt_re = jnp.transpose(bb_re, (1, 0, 3, 2))
    bbt_im = jnp.transpose(bb_im, (1, 0, 3, 2))
    d_e = jnp.zeros((g_, p_, c * p_), F32).at[:, :, :p_].set(d.astype(F32)[:, :, None] * jnp.eye(p_, dtype=F32))
    tt = _s5_toeplitz(expand_pw(pw_re), expand_pw(pw_im), expand_c(cr), expand_c(ci), bbt_re, bbt_im, d_e)

    pf_re = pw_re[c - 1 - idx, 0]
    pf_im = pw_im[c - 1 - idx, 0]
    pb_re = pw_re[idx, 1]
    pb_im = pw_im[idx, 1]

    def m_of(p_re_, p_im_, dr):
        m_re = p_re_[:, :, :, None] * bb_re[dr][None] - p_im_[:, :, :, None] * bb_im[dr][None]
        m_im = p_re_[:, :, :, None] * bb_im[dr][None] + p_im_[:, :, :, None] * bb_re[dr][None]
        to = lambda m: jnp.transpose(m, (1, 0, 3, 2)).reshape(g_, c * p_, n_)
        return to(m_re), to(m_im)

    mf_re, mf_im = m_of(pf_re, pf_im, 0)
    mb_re, mb_im = m_of(pb_re, pb_im, 1)
    m4 = jnp.stack([mf_re, mf_im, mb_re, mb_im], axis=1)
    m4 = m4.reshape(g_ // 2, 2, 4, c * p_, n_)
    mz = jnp.zeros_like(m4[:, 0])

    def m_rows(blocks):
        return jnp.transpose(jnp.concatenate(blocks, axis=-1), (0, 2, 1, 3)).reshape(g_ // 2, c * p_, 8 * n_)

    mpair = jnp.concatenate([m_rows([m4[:, 0], mz]), m_rows([mz, m4[:, 1]])], axis=1).astype(BF16)

    def n_of(tsel, dr):
        w_re = z_re[tsel, dr]
        w_im = z_im[tsel, dr]
        to = lambda m: jnp.transpose(m, (1, 3, 0, 2)).reshape(g_, n_, c * p_)
        return to(w_re), to(-w_im)

    nf_re, nf_im = n_of(idx + 1, 0)
    nb_re, nb_im = n_of(c - idx, 1)
    n4 = jnp.stack([nf_re, nf_im, nb_re, nb_im], axis=1)
    n4 = n4.reshape(g_ // 2, 2, 4, n_, c * p_)
    nz = jnp.zeros_like(n4[:, 0])
    npair = jnp.stack([jnp.concatenate([n4[:, 0], nz], axis=-1), jnp.concatenate([nz, n4[:, 1]], axis=-1)], axis=2)
    npair = npair.reshape(g_ // 2, 4 * 2 * n_, 2 * c * p_).astype(BF16)

    dec = jnp.stack([pw_re[c, 0], pw_im[c, 0], pw_re[c, 1], pw_im[c, 1]], axis=0)
    dec = dec.reshape(4, 1, g_ * n_)
    return tt, mpair, npair, dec


S5_PITCH = CHUNK + 4
S5_HALVES = S5_WIDTH // LANES
S5_GPH = LANES // S5_P


def _s5_chunk_block(nch):
    best = 8
    for cand in range(8, 113, 8):
        if nch % cand == 0:
            best = cand
    return best


def _block_transpose8(tiles):
    lane = lax.broadcasted_iota(jnp.int32, tiles[0].shape, 1)
    tiles = list(tiles)
    for dist in (4, 2, 1):
        width = S5_P * dist
        low = (lane & (2 * width - 1)) < width
        for k in range(S5_GPH):
            if k & dist:
                continue
            a, b = tiles[k], tiles[k + dist]
            tiles[k] = jnp.where(low, a, pltpu.roll(b, width, 1))
            tiles[k + dist] = jnp.where(low, pltpu.roll(a, LANES - width, 1), b)
    return tiles


def _s5_in_kernel(u_hbm, m_ref, ucat_ref, o0, o1, o2, o3, xpad, uall, sem):
    i = pl.program_id(0)
    j = pl.program_id(1)
    cb = uall.shape[1]

    @pl.when(j == 0)
    def _():
        def chunk_copy(c, h):
            return pltpu.make_async_copy(
                u_hbm.at[pl.ds((i * cb + c) * CHUNK, CHUNK), pl.ds(h * LANES, LANES)],
                xpad.at[h, pl.ds(c * S5_PITCH, CHUNK), :], sem)

        def start(c, carry):
            for h in range(S5_HALVES):
                chunk_copy(c, h).start()
            return carry

        def wait(c, carry):
            for h in range(S5_HALVES):
                chunk_copy(c, h).wait()
            return carry

        lax.fori_loop(0, cb, start, 0)
        lax.fori_loop(0, cb, wait, 0)

        def dest_tile(jt, carry):
            col = pl.multiple_of(jt * LANES, LANES)
            for h in range(S5_HALVES):
                by_token = [xpad[h, pl.ds(jt * S5_GPH + k, cb, stride=S5_PITCH), :] for k in range(S5_GPH)]
                for go, tile in enumerate(_block_transpose8(by_token)):
                    uall[h * S5_GPH + go, :, pl.ds(col, LANES)] = tile
            return carry

        lax.fori_loop(0, CHUNK // S5_GPH, dest_tile, 0)

    u2 = jnp.concatenate([uall[2 * j], uall[2 * j + 1]], axis=1)
    ucat_ref[...] = u2
    s = jnp.dot(u2.astype(BF16), m_ref[...], preferred_element_type=F32)
    o0[...] = s[:, 0:128]
    o1[...] = s[:, 128:256]
    o2[...] = s[:, 256:384]
    o3[...] = s[:, 384:512]


def _s5_in(u, mpair):
    nch = u.shape[0] // CHUNK
    cb = _s5_chunk_block(nch)
    w = CHUNK * S5_P
    return pl.pallas_call(
        _s5_in_kernel,
        grid=(nch // cb, S5_GROUPS // 2),
        in_specs=[pl.BlockSpec(memory_space=pl.ANY),
                  pl.BlockSpec((None, 2 * w, 512), lambda i, j: (j, 0, 0))],
        out_specs=[pl.BlockSpec((cb, 2 * w), lambda i, j: (i, j))] + [pl.BlockSpec((cb, LANES), lambda i, j: (i, j))] * 4,
        out_shape=[jax.ShapeDtypeStruct((nch, S5_GROUPS * w), F32)] + [jax.ShapeDtypeStruct((nch, 1024), F32)] * 4,
        scratch_shapes=[pltpu.VMEM((S5_HALVES, cb * S5_PITCH, LANES), F32), pltpu.VMEM((S5_GROUPS, cb, w), F32),
                        pltpu.SemaphoreType.DMA(())],
        compiler_params=_params("arbitrary", "arbitrary"),
        name="s5_in",
    )(u, mpair)


def _s5_scan_kernel(sfr, sfi, sbr, sbi, dfr, dfi, dbr, dbi, hfr, hfi, hbr, hbi):
    nc, b, _ = sfr.shape
    a_fr = dfr[...]
    a_fi = dfi[...]
    a_br = dbr[...]
    a_bi = dbi[...]
    zero = jnp.zeros((b, LANES), F32)

    def fwd(c, carry):
        hr, hi = carry
        hfr[c] = hr
        hfi[c] = hi
        return (a_fr * hr - a_fi * hi + sfr[c], a_fr * hi + a_fi * hr + sfi[c])

    def bwd(i, carry):
        c = nc - 1 - i
        hr, hi = carry
        hbr[c] = hr
        hbi[c] = hi
        return (a_br * hr - a_bi * hi + sbr[c], a_br * hi + a_bi * hr + sbi[c])

    lax.fori_loop(0, nc, fwd, (zero, zero))
    lax.fori_loop(0, nc, bwd, (zero, zero))


def _s5_scan(s4, dec, nc, b):
    blk = pl.BlockSpec((nc, b, LANES), lambda j: (0, 0, j))
    dspecs = [pl.BlockSpec((None, 1, LANES), functools.partial(lambda j, k: (k, 0, j), k=k)) for k in range(4)]
    return pl.pallas_call(
        _s5_scan_kernel,
        grid=(S5_GROUPS // 2,),
        in_specs=[blk] * 4 + dspecs,
        out_specs=[blk] * 4,
        out_shape=[jax.ShapeDtypeStruct((nc, b, 1024), F32)] * 4,
        compiler_params=_params("parallel"),
        name="s5_scan",
    )(*s4, dec, dec, dec, dec)


def _s5_out_kernel(u_ref, tt_ref, h0, h1, h2, h3, n_ref, y_hbm, yall, ypad, sem):
    i = pl.program_id(0)
    j = pl.program_id(1)
    cb = yall.shape[1]
    w = CHUNK * S5_P
    u = u_ref[...].astype(BF16)
    y0 = jnp.dot(u[:, 0:w], tt_ref[0], preferred_element_type=F32)
    y1 = jnp.dot(u[:, w:2 * w], tt_ref[1], preferred_element_type=F32)
    hcat = jnp.concatenate([h0[...], h1[...], h2[...], h3[...]], axis=1).astype(BF16)
    yh = jnp.dot(hcat, n_ref[...], preferred_element_type=F32)
    yall[2 * j] = y0 + yh[:, 0:w]
    yall[2 * j + 1] = y1 + yh[:, w:2 * w]

    @pl.when(j == pl.num_programs(1) - 1)
    def _():
        def src_tile(jt, carry):
            col = pl.multiple_of(jt * LANES, LANES)
            for h in range(S5_HALVES):
                by_group = [yall[h * S5_GPH + go, :, pl.ds(col, LANES)] for go in range(S5_GPH)]
                for k, tile in enumerate(_block_transpose8(by_group)):
                    ypad[h, pl.ds(jt * S5_GPH + k, cb, stride=S5_PITCH), :] = tile
            return carry

        lax.fori_loop(0, CHUNK // S5_GPH, src_tile, 0)

        def chunk_copy(c, h):
            return pltpu.make_async_copy(
                ypad.at[h, pl.ds(c * S5_PITCH, CHUNK), :],
                y_hbm.at[pl.ds((i * cb + c) * CHUNK, CHUNK), pl.ds(h * LANES, LANES)], sem)

        def start(c, carry):
            for h in range(S5_HALVES):
                chunk_copy(c, h).start()
            return carry

        def wait(c, carry):
            for h in range(S5_HALVES):
                chunk_copy(c, h).wait()
            return carry

        lax.fori_loop(0, cb, start, 0)
        lax.fori_loop(0, cb, wait, 0)


def _s5_out(ucat, tt, h4, npair):
    nch = ucat.shape[0]
    cb = _s5_chunk_block(nch)
    w = CHUNK * S5_P
    hblk = pl.BlockSpec((cb, LANES), lambda i, j: (i, j))
    return pl.pallas_call(
        _s5_out_kernel,
        grid=(nch // cb, S5_GROUPS // 2),
        in_specs=[pl.BlockSpec((cb, 2 * w), lambda i, j: (i, j)),
                  pl.BlockSpec((2, w, w), lambda i, j: (j, 0, 0)),
                  hblk, hblk, hblk, hblk,
                  pl.BlockSpec((None, 512, 2 * w), lambda i, j: (j, 0, 0))],
        out_specs=pl.BlockSpec(memory_space=pl.ANY),
        out_shape=jax.ShapeDtypeStruct((nch * CHUNK, S5_WIDTH), F32),
        scratch_shapes=[pltpu.VMEM((S5_GROUPS, cb, w), F32), pltpu.VMEM((S5_HALVES, cb * S5_PITCH, LANES), F32),
                        pltpu.SemaphoreType.DMA(())],
        compiler_params=_params("arbitrary", "arbitrary"),
        name="s5_out",
    )(ucat, tt, *h4, npair)


def _s5_mixer(u, tables, batch, lp):
    tt, mpair, npair, dec = tables
    nc = lp // CHUNK
    ucat, *s4 = _s5_in(u, mpair)
    s4 = [jnp.transpose(s.reshape(batch, nc, 1024), (1, 0, 2)) for s in s4]
    h4 = _s5_scan(s4, dec, nc, batch)
    h4 = [jnp.transpose(h, (1, 0, 2)).reshape(batch * nc, 1024) for h in h4]
    return _s5_out(ucat, tt, h4, npair)


def _cumsum_chunks(x):
    n = x.shape[0]
    r = lax.broadcasted_iota(jnp.int32, (n, n), 0)
    s = lax.broadcasted_iota(jnp.int32, (n, n), 1)
    tri = jnp.where((s <= r) & ((s // CHUNK) == (r // CHUNK)), 1.0, 0.0).astype(BF16)
    hi = x.astype(BF16)
    lo = (x - hi.astype(F32)).astype(BF16)
    return jnp.dot(tri, hi, preferred_element_type=F32) + jnp.dot(tri, lo, preferred_element_type=F32)


def _head_masks(npair, vp):
    kl = lax.broadcasted_iota(jnp.int32, (1, LANES), 1)
    km = [(kl < 64), (kl >= 64)]
    vl = lax.broadcasted_iota(jnp.int32, (1, vp), 1)
    vm = [(vl < vp // 2), (vl >= vp // 2)]
    vrow = lax.broadcasted_iota(jnp.int32, (vp, LANES), 0)
    kcol = lax.broadcasted_iota(jnp.int32, (vp, LANES), 1)
    bd = (vrow >= vp // 2) == (kcol >= 64)
    return km, vm, bd


def _la_bwd_kernel(k_ref, v_ref, lb_ref, sb_ref, st_ref, *, npair, vp, nsub):
    g = LA_GROUP
    c = CHUNK
    gc = g * c

    @pl.when(pl.program_id(1) == 0)
    def _():
        st_ref[...] = jnp.zeros_like(st_ref)

    km, vm, bd = _head_masks(npair, vp)

    def sub_block(t, carry):
        sub = nsub - 1 - t
        base = pl.multiple_of(sub * gc, gc)
        kblk = k_ref[pl.ds(base, gc), :].astype(F32)
        vblk = v_ref[pl.ds(base, gc), :]
        lblk = lb_ref[pl.ds(base, gc), :]
        cblk = _cumsum_chunks(lblk)
        kb2blk = (kblk * jnp.exp(cblk - lblk)).astype(BF16)
        states = [st_ref[p] for p in range(npair)]
        for gi in reversed(range(g)):
            rows = slice(gi * c, (gi + 1) * c)
            kb2 = kb2blk[rows, :]
            dec = jnp.exp(cblk[gi * c + c - 1:(gi + 1) * c, :])
            vv = vblk[rows, :]
            for p in range(npair):
                sb_ref[sub * g + gi, p] = states[p].astype(BF16)
                upd = lax.dot_general(vv[:, p * vp:(p + 1) * vp], kb2[:, p * LANES:(p + 1) * LANES],
                                      (((0,), (0,)), ((), ())), preferred_element_type=F32)
                states[p] = states[p] * dec[:, p * LANES:(p + 1) * LANES] + jnp.where(bd, upd, 0.0)
        for p in range(npair):
            st_ref[p] = states[p]
        return carry

    lax.fori_loop(0, nsub, sub_block, 0)


def _head_rms_gate(o, gate, bdv, dv):
    ms = _split_dot(o * o, bdv) * (1.0 / dv)
    return (o * lax.rsqrt(ms + NORM_EPS) * _silu(gate.astype(F32))).astype(BF16)


def _la_fwd_kernel(q_ref, k_ref, v_ref, lf_ref, lb_ref, sb_ref, gate_ref, bdv_ref, o_ref, st_ref,
                   *, npair, vp, dv, nsub):
    g = LA_GROUP
    c = CHUNK
    gc = g * c

    @pl.when(pl.program_id(1) == 0)
    def _():
        st_ref[...] = jnp.zeros_like(st_ref)

    km, vm, bd = _head_masks(npair, vp)
    ri = lax.broadcasted_iota(jnp.int32, (c, LANES), 0)
    cj = lax.broadcasted_iota(jnp.int32, (c, LANES), 1) & (c - 1)
    lower = ri >= cj
    upper = ri <= cj
    mid = c // 2
    nt = (((1,), (1,)), ((), ()))
    tn = (((0,), (0,)), ((), ()))

    def sub_block(sub, carry):
        base = pl.multiple_of(sub * gc, gc)
        qblk = q_ref[pl.ds(base, gc), :].astype(F32)
        kblk = k_ref[pl.ds(base, gc), :].astype(F32)
        vblk = v_ref[pl.ds(base, gc), :]
        lfblk = lf_ref[pl.ds(base, gc), :]
        lbblk = lb_ref[pl.ds(base, gc), :]
        w = lfblk.shape[1]
        csblk = _cumsum_chunks(jnp.concatenate([lfblk, lbblk], axis=1))
        states = [st_ref[p] for p in range(npair)]
        outs = []
        for gi in range(g):
            rows = slice(gi * c, (gi + 1) * c)
            qq = qblk[rows, :]
            kk = kblk[rows, :]
            vv = vblk[rows, :]
            cf = csblk[rows, :w]
            cbi = csblk[rows, w:]
            cb = cbi - lbblk[rows, :]
            mf = cf[mid:mid + 1, :]
            mb = cb[mid:mid + 1, :]
            tf = cf[c - 1:c, :]
            tb = cbi[c - 1:c, :]
            qf = (qq * jnp.exp(cf - mf)).astype(BF16)
            kf = (kk * jnp.exp(mf - cf)).astype(BF16)
            qb = (qq * jnp.exp(mb - cb)).astype(BF16)
            kb = (kk * jnp.exp(cb - mb)).astype(BF16)
            zk = jnp.zeros((c, LANES), BF16)
            q2 = jnp.concatenate([qq * jnp.exp(cf), qq * jnp.exp(tb - cb)], axis=0).astype(BF16)
            kf2 = (kk * jnp.exp(tf - cf)).astype(BF16)
            dec = jnp.exp(tf)
            pair_out = []
            for p in range(npair):
                ks = slice(p * LANES, (p + 1) * LANES)
                vsl = slice(p * vp, (p + 1) * vp)
                kfp = kf[:, ks]
                kbp = kb[:, ks]
                k2f = jnp.concatenate([jnp.where(km[0], kfp, zk), jnp.where(km[1], kfp, zk)], axis=0)
                k2b = jnp.concatenate([jnp.where(km[0], kbp, zk), jnp.where(km[1], kbp, zk)], axis=0)
                sf = lax.dot_general(qf[:, ks], k2f, nt, preferred_element_type=F32)
                sb = lax.dot_general(qb[:, ks], k2b, nt, preferred_element_type=F32)
                sc = (jnp.where(lower, sf, 0.0) + jnp.where(upper, sb, 0.0)).astype(BF16)
                vp_ = vv[:, vsl]
                zero = jnp.zeros_like(vp_)
                v2 = jnp.concatenate([jnp.where(vm[0], vp_, zero), jnp.where(vm[1], vp_, zero)], axis=0)
                o = jnp.dot(sc, v2, preferred_element_type=F32)
                st = states[p]
                o = o + lax.dot_general(q2[0:c, ks], st.astype(BF16), nt, preferred_element_type=F32)
                o = o + lax.dot_general(q2[c:2 * c, ks], sb_ref[sub * g + gi, p], nt, preferred_element_type=F32)
                upd = lax.dot_general(vp_, kf2[:, ks], tn, preferred_element_type=F32)
                states[p] = st * dec[:, ks] + jnp.where(bd, upd, 0.0)
                pair_out.append(o)
            outs.append(jnp.concatenate(pair_out, axis=1) if npair > 1 else pair_out[0])
        for p in range(npair):
            st_ref[p] = states[p]
        o = jnp.concatenate(outs, axis=0)
        o_ref[pl.ds(base, gc), :] = _head_rms_gate(o, gate_ref[pl.ds(base, gc), :], bdv_ref[...], dv)
        return carry

    lax.fori_loop(0, nsub, sub_block, 0)


def _head_block_ones(wv, seg):
    vr = lax.broadcasted_iota(jnp.int32, (wv, wv), 0) // seg
    vc = lax.broadcasted_iota(jnp.int32, (wv, wv), 1) // seg
    return (vr == vc).astype(BF16)


def _sub_blocks(lp, rows):
    n = lp // rows
    for cand in (5, 4, 3, 2):
        if n % cand == 0:
            return cand
    return 1


def _linear_attention(q, k, v, lf, lb, gate, batch, lp, *, npair, vp, dv):
    gc = LA_GROUP * CHUNK
    nsub = _sub_blocks(lp, gc)
    rows = nsub * gc
    nblk = lp // rows
    nc = lp // CHUNK
    wk = npair * LANES
    wv = npair * vp
    q3 = q.reshape(batch, lp, wk)
    k3 = k.reshape(batch, lp, wk)
    v3 = v.reshape(batch, lp, wv)
    g3 = gate.reshape(batch, lp, wv)
    lf3 = lf.reshape(batch, lp, wk)
    lb3 = lb.reshape(batch, lp, wk)
    fwd = lambda b, j: (b, j, 0)
    rev = lambda b, j: (b, nblk - 1 - j, 0)
    sblk = (None, nsub * LA_GROUP, npair, vp, LANES)

    sb = pl.pallas_call(
        functools.partial(_la_bwd_kernel, npair=npair, vp=vp, nsub=nsub),
        grid=(batch, nblk),
        in_specs=[pl.BlockSpec((None, rows, wk), rev), pl.BlockSpec((None, rows, wv), rev),
                  pl.BlockSpec((None, rows, wk), rev)],
        out_specs=pl.BlockSpec(sblk, lambda b, j: (b, nblk - 1 - j, 0, 0, 0)),
        out_shape=jax.ShapeDtypeStruct((batch, nc, npair, vp, LANES), BF16),
        scratch_shapes=[pltpu.VMEM((npair, vp, LANES), F32)],
        compiler_params=_params("parallel", "arbitrary"),
        name="la_bwd_states",
    )(k3, v3, lb3)

    o = pl.pallas_call(
        functools.partial(_la_fwd_kernel, npair=npair, vp=vp, dv=dv, nsub=nsub),
        grid=(batch, nblk),
        in_specs=[pl.BlockSpec((None, rows, wk), fwd), pl.BlockSpec((None, rows, wk), fwd),
                  pl.BlockSpec((None, rows, wv), fwd), pl.BlockSpec((None, rows, wk), fwd),
                  pl.BlockSpec((None, rows, wk), fwd),
                  pl.BlockSpec(sblk, lambda b, j: (b, j, 0, 0, 0)),
                  pl.BlockSpec((None, rows, wv), fwd),
                  pl.BlockSpec((wv, wv), lambda b, j: (0, 0))],
        out_specs=pl.BlockSpec((None, rows, wv), fwd),
        out_shape=jax.ShapeDtypeStruct((batch, lp, wv), BF16),
        scratch_shapes=[pltpu.VMEM((npair, vp, LANES), F32)],
        compiler_params=_params("parallel", "arbitrary"),
        name="la_fwd",
    )(q3, k3, v3, lf3, lb3, sb, g3, _head_block_ones(wv, vp // 2))
    return o.reshape(batch * lp, wv)


RET_CHUNK = 256


def _ret_decay_terms(lg_ref):
    c = RET_CHUNK
    lg = lg_ref[...]
    i = lax.broadcasted_iota(jnp.int32, (c, 1), 0).astype(F32)
    return dict(q_f=jnp.exp((i + 1.0) * lg), q_b=jnp.exp((float(c) - i) * lg),
                k_f=jnp.exp((float(c - 1) - i) * lg), k_b=jnp.exp(i * lg), dec=jnp.exp(float(c) * lg))


def _ret_bwd_kernel(k_ref, v_ref, lg_ref, sb_ref, st_ref, *, npair, nsub):
    c = RET_CHUNK
    tn = (((0,), (0,)), ((), ()))

    @pl.when(pl.program_id(1) == 0)
    def _():
        st_ref[...] = jnp.zeros_like(st_ref)

    _, _, bd = _head_masks(npair, LANES)
    t = _ret_decay_terms(lg_ref)

    def sub_block(it, carry):
        sub = nsub - 1 - it
        base = pl.multiple_of(sub * c, c)
        kb = (k_ref[pl.ds(base, c), :].astype(F32) * t["k_b"]).astype(BF16)
        vv = v_ref[pl.ds(base, c), :]
        for p in range(npair):
            ks = slice(p * LANES, (p + 1) * LANES)
            st = st_ref[p]
            sb_ref[sub, p] = st.astype(BF16)
            upd = lax.dot_general(vv[:, ks], kb[:, ks], tn, preferred_element_type=F32)
            st_ref[p] = st * t["dec"][:, ks] + jnp.where(bd, upd, 0.0)
        return carry

    lax.fori_loop(0, nsub, sub_block, 0)


def _ret_fwd_kernel(q_ref, k_ref, v_ref, lg_ref, dmask_ref, sb_ref, gate_ref, bdv_ref, o_ref, st_ref,
                    *, npair, nsub, dv):
    c = RET_CHUNK
    nt = (((1,), (1,)), ((), ()))
    tn = (((0,), (0,)), ((), ()))

    @pl.when(pl.program_id(1) == 0)
    def _():
        st_ref[...] = jnp.zeros_like(st_ref)

    km, vm, bd = _head_masks(npair, LANES)
    t = _ret_decay_terms(lg_ref)

    def sub_block(sub, carry):
        base = pl.multiple_of(sub * c, c)
        qb16 = q_ref[pl.ds(base, c), :]
        kb16 = k_ref[pl.ds(base, c), :]
        vv = v_ref[pl.ds(base, c), :]
        qq = qb16.astype(F32)
        kk = kb16.astype(F32)
        q2 = jnp.concatenate([qq * t["q_f"], qq * t["q_b"]], axis=0).astype(BF16)
        kf2 = (kk * t["k_f"]).astype(BF16)
        zero = jnp.zeros((c, LANES), BF16)
        pair_out = []
        for p in range(npair):
            ks = slice(p * LANES, (p + 1) * LANES)
            kp = kb16[:, ks]
            vp_ = vv[:, ks]
            k2 = jnp.concatenate([jnp.where(km[0], kp, zero), jnp.where(km[1], kp, zero)], axis=0)
            v2 = jnp.concatenate([jnp.where(vm[0], vp_, zero), jnp.where(vm[1], vp_, zero)], axis=0)
            s = lax.dot_general(qb16[:, ks], k2, nt, preferred_element_type=F32)
            o = jnp.dot((s * dmask_ref[p]).astype(BF16), v2, preferred_element_type=F32)
            st = st_ref[p]
            o = o + lax.dot_general(q2[0:c, ks], st.astype(BF16), nt, preferred_element_type=F32)
            o = o + lax.dot_general(q2[c:2 * c, ks], sb_ref[sub, p], nt, preferred_element_type=F32)
            upd = lax.dot_general(vp_, kf2[:, ks], tn, preferred_element_type=F32)
            st_ref[p] = st * t["dec"][:, ks] + jnp.where(bd, upd, 0.0)
            pair_out.append(o)
        o = jnp.concatenate(pair_out, axis=1)
        o_ref[pl.ds(base, c), :] = _head_rms_gate(o, gate_ref[pl.ds(base, c), :], bdv_ref[...], dv)
        return carry

    lax.fori_loop(0, nsub, sub_block, 0)


def _retention(q, k, v, gate, batch, lp):
    c = RET_CHUNK
    npair = RET_HEADS // 2
    w = npair * LANES
    nsub = _sub_blocks(lp, c)
    rows = nsub * c
    nblk = lp // rows
    log_gamma = jnp.log1p(-jnp.exp2(-5.0 - jnp.arange(RET_HEADS, dtype=F32)))
    lg = jnp.repeat(log_gamma, RET_DIM)[None]
    dist = jnp.abs(jnp.arange(c)[:, None] - (jnp.arange(2 * c) % c)[None, :]).astype(F32)
    dmask = jnp.exp(dist[None] * jnp.repeat(log_gamma, c).reshape(npair, 1, 2 * c))
    q3, k3, v3, g3 = (a.reshape(batch, lp, w) for a in (q, k, v, gate))
    fwd = lambda b, j: (b, j, 0)
    rev = lambda b, j: (b, nblk - 1 - j, 0)
    sblk = (None, nsub, npair, LANES, LANES)
    lgspec = pl.BlockSpec((1, w), lambda b, j: (0, 0))

    sb = pl.pallas_call(
        functools.partial(_ret_bwd_kernel, npair=npair, nsub=nsub),
        grid=(batch, nblk),
        in_specs=[pl.BlockSpec((None, rows, w), rev), pl.BlockSpec((None, rows, w), rev), lgspec],
        out_specs=pl.BlockSpec(sblk, lambda b, j: (b, nblk - 1 - j, 0, 0, 0)),
        out_shape=jax.ShapeDtypeStruct((batch, lp // c, npair, LANES, LANES), BF16),
        scratch_shapes=[pltpu.VMEM((npair, LANES, LANES), F32)],
        compiler_params=_params("parallel", "arbitrary"),
        name="ret_bwd_states",
    )(k3, v3, lg)

    o = pl.pallas_call(
        functools.partial(_ret_fwd_kernel, npair=npair, nsub=nsub, dv=RET_DIM),
        grid=(batch, nblk),
        in_specs=[pl.BlockSpec((None, rows, w), fwd), pl.BlockSpec((None, rows, w), fwd),
                  pl.BlockSpec((None, rows, w), fwd), lgspec,
                  pl.BlockSpec((npair, c, 2 * c), lambda b, j: (0, 0, 0)),
                  pl.BlockSpec(sblk, lambda b, j: (b, j, 0, 0, 0)),
                  pl.BlockSpec((None, rows, w), fwd),
                  pl.BlockSpec((w, w), lambda b, j: (0, 0))],
        out_specs=pl.BlockSpec((None, rows, w), fwd),
        out_shape=jax.ShapeDtypeStruct((batch, lp, w), BF16),
        scratch_shapes=[pltpu.VMEM((npair, LANES, LANES), F32)],
        compiler_params=_params("parallel", "arbitrary"),
        name="ret_fwd",
    )(q3, k3, v3, lg, dmask, sb, g3, _head_block_ones(w, RET_DIM))
    return o.reshape(batch * lp, w)


PACKED = D_MODEL // 2


def _pack_rows(x):
    bits = lax.bitcast_convert_type(x.astype(BF16).astype(F32), jnp.int32)
    return bits[:, :PACKED] | lax.shift_right_logical(bits[:, PACKED:], 16)


def _unpack_rows(p):
    hi = lax.bitcast_convert_type(p & jnp.int32(-65536), F32)
    lo = lax.bitcast_convert_type(lax.shift_left(p, jnp.int32(16)), F32)
    return jnp.concatenate([hi, lo], axis=1)


_RT_E1, _RT_E2, _RT_W1, _RT_W2, _RT_R1, _RT_R2 = range(6)


def _mix_out(ya_ref, yb_ref, yc_ref, h_ref, wglu_ref, wout_ref, g_ref):
    ya = _gelu_tanh(ya_ref[...].astype(F32))
    gl = jnp.dot(ya.astype(BF16), wglu_ref[...], preferred_element_type=F32)
    ya = (ya * _sigmoid(gl)).astype(BF16)
    y = jnp.concatenate([ya, yb_ref[...], yc_ref[...]], axis=1)
    h = h_ref[...] + jnp.dot(y, wout_ref[...], preferred_element_type=F32)
    ms = jnp.mean(h * h, axis=-1, keepdims=True)
    return h, h * lax.rsqrt(ms + NORM_EPS) * g_ref[...]


def _outproj_kernel(ya_ref, yb_ref, yc_ref, h_ref, wglu_ref, wout_ref, g_ref, hout_ref, hn_ref):
    h, hn = _mix_out(ya_ref, yb_ref, yc_ref, h_ref, wglu_ref, wout_ref, g_ref)
    hout_ref[...] = h
    hn_ref[...] = hn.astype(BF16)


def _outproj_router_kernel(ya_ref, yb_ref, yc_ref, h_ref, wglu_ref, wout_ref, g_ref, wrh_ref, wrl_ref, before_ref,
                           hout_ref, hn_ref, route_ref, cnt_ref, carry_ref):
    @pl.when(pl.program_id(0) == 0)
    def _():
        carry_ref[...] = jnp.zeros_like(carry_ref)

    h, hn = _mix_out(ya_ref, yb_ref, yc_ref, h_ref, wglu_ref, wout_ref, g_ref)
    hout_ref[...] = h
    hn_ref[...] = _pack_rows(hn)
    hi = hn.astype(BF16)
    lo = (hn - hi.astype(F32)).astype(BF16)
    wrh = wrh_ref[...]
    logits = (jnp.dot(hi, wrh, preferred_element_type=F32) + jnp.dot(lo, wrh, preferred_element_type=F32)
              + jnp.dot(hi, wrl_ref[...], preferred_element_type=F32))
    lane = lax.broadcasted_iota(jnp.int32, logits.shape, 1)
    neg = jnp.float32(-jnp.inf)
    logits = jnp.where(lane < N_EXPERTS, logits, neg)
    v1 = jnp.max(logits, axis=-1, keepdims=True)
    i1 = jnp.min(jnp.where(logits == v1, lane, LANES), axis=-1, keepdims=True)
    m1 = lane == i1
    l2 = jnp.where(m1, neg, logits)
    v2 = jnp.max(l2, axis=-1, keepdims=True)
    i2 = jnp.min(jnp.where(l2 == v2, lane, LANES), axis=-1, keepdims=True)
    m2 = lane == i2
    e = jnp.exp(v2 - v1)
    w1 = 1.0 / (1.0 + e)
    w2 = e * w1
    chosen = jnp.where(m1 | m2, 1.0, 0.0)
    prefix = jnp.dot(before_ref[...], chosen.astype(BF16), preferred_element_type=F32) + carry_ref[...]
    r1 = jnp.sum(jnp.where(m1, prefix, 0.0), axis=-1, keepdims=True)
    r2 = jnp.sum(jnp.where(m2, prefix, 0.0), axis=-1, keepdims=True)
    total = carry_ref[...] + jnp.sum(chosen, axis=0, keepdims=True)
    carry_ref[...] = total
    cnt_ref[...] = total
    rec = jnp.zeros_like(logits)
    for ln, val in ((_RT_E1, i1.astype(F32)), (_RT_E2, i2.astype(F32)), (_RT_W1, w1), (_RT_W2, w2),
                    (_RT_R1, r1), (_RT_R2, r2)):
        rec = jnp.where(lane == ln, val, rec)
    route_ref[...] = rec


def _outproj(ya, yb, yc, h, wglu, wout, gamma, wr=None):
    r = h.shape[0]
    tm = ROW_TILE
    with_router = wr is not None

    def rows(wd):
        return pl.BlockSpec((tm, wd), lambda i: (i, 0))

    in_specs = [rows(256), rows(384), rows(512), rows(D_MODEL), _const_spec((256, 256)),
                _const_spec((1152, D_MODEL)), _const_spec((1, D_MODEL))]
    args = [ya, yb, yc, h, wglu, wout, gamma]
    if not with_router:
        return pl.pallas_call(
            _outproj_kernel,
            grid=(r // tm,),
            in_specs=in_specs, out_specs=[rows(D_MODEL), rows(D_MODEL)],
            out_shape=[jax.ShapeDtypeStruct((r, D_MODEL), F32), jax.ShapeDtypeStruct((r, D_MODEL), BF16)],
            input_output_aliases={3: 0},
            compiler_params=_params("parallel"),
            name="outproj",
        )(*args)
    earlier = (lax.broadcasted_iota(jnp.int32, (tm, tm), 1) < lax.broadcasted_iota(jnp.int32, (tm, tm), 0)).astype(BF16)
    return pl.pallas_call(
        _outproj_router_kernel,
        grid=(r // tm,),
        in_specs=in_specs + [_const_spec((D_MODEL, LANES)), _const_spec((D_MODEL, LANES)), _const_spec((tm, tm))],
        out_specs=[rows(D_MODEL), rows(PACKED), rows(LANES), _const_spec((1, LANES))],
        out_shape=[jax.ShapeDtypeStruct((r, D_MODEL), F32), jax.ShapeDtypeStruct((r, PACKED), jnp.int32),
                   jax.ShapeDtypeStruct((r, LANES), F32), jax.ShapeDtypeStruct((1, LANES), F32)],
        scratch_shapes=[pltpu.VMEM((1, LANES), F32)],
        input_output_aliases={3: 0},
        compiler_params=_params("arbitrary"),
        name="outproj_router",
    )(*args, wr[0], wr[1], earlier)


FF_TILE = 2816


def _ffn_kernel(hn_ref, h_ref, wg_ref, wu_ref, wd_ref, o_ref, acc_ref):
    j = pl.program_id(1)
    hn = hn_ref[...]
    a = _silu(jnp.dot(hn, wg_ref[...], preferred_element_type=F32)) * jnp.dot(hn, wu_ref[...], preferred_element_type=F32)
    y = jnp.dot(a.astype(BF16), wd_ref[...], preferred_element_type=F32)

    @pl.when(j == 0)
    def _():
        acc_ref[...] = h_ref[...] + y

    @pl.when(j != 0)
    def _():
        acc_ref[...] += y

    @pl.when(j == pl.num_programs(1) - 1)
    def _():
        o_ref[...] = acc_ref[...]


def _ffn(hn, h, wg, wu, wd):
    r = h.shape[0]
    tm = ROW_TILE
    nf = D_FF // FF_TILE
    return pl.pallas_call(
        _ffn_kernel,
        grid=(r // tm, nf),
        in_specs=[pl.BlockSpec((tm, D_MODEL), lambda i, j: (i, 0)),
                  pl.BlockSpec((tm, D_MODEL), lambda i, j: (i, 0)),
                  pl.BlockSpec((D_MODEL, FF_TILE), lambda i, j: (0, j)),
                  pl.BlockSpec((D_MODEL, FF_TILE), lambda i, j: (0, j)),
                  pl.BlockSpec((FF_TILE, D_MODEL), lambda i, j: (j, 0))],
        out_specs=pl.BlockSpec((tm, D_MODEL), lambda i, j: (i, 0)),
        out_shape=jax.ShapeDtypeStruct((r, D_MODEL), F32),
        scratch_shapes=[pltpu.VMEM((tm, D_MODEL), F32)],
        input_output_aliases={1: 0},
        compiler_params=_params("parallel", "arbitrary"),
        name="ffn",
    )(hn, h, wg, wu, wd)


MOE_BLOCK = 512
GATHER_TILE = 256


def _route_meta(route, cnt, r):
    bm = MOE_BLOCK
    nb = 2 * r // bm + N_EXPERTS
    e1 = route[:, _RT_E1].astype(jnp.int32)
    e2 = route[:, _RT_E2].astype(jnp.int32)
    counts = cnt[0, :N_EXPERTS].astype(jnp.int32)
    padded = ((counts + bm - 1) // bm) * bm
    ends = jnp.cumsum(padded)
    starts = ends - padded
    pos1 = starts[e1] + route[:, _RT_R1].astype(jnp.int32)
    pos2 = starts[e2] + route[:, _RT_R2].astype(jnp.int32)
    n_used = (ends[-1] // bm).astype(jnp.int32)
    blk = jnp.arange(nb, dtype=jnp.int32)
    blk = jnp.minimum(blk, n_used - 1)
    block_expert = jnp.sum((blk[:, None] * bm >= ends[None, :]).astype(jnp.int32), axis=1)
    block_expert = jnp.minimum(block_expert, N_EXPERTS - 1).astype(jnp.int32)
    return pos1, pos2, block_expert, n_used.reshape(1), nb


def _dispatch_kernel(p1_ref, p2_ref, hn_ref, xs_in_ref, xs_ref, sem):
    del xs_in_ref
    n = hn_ref.shape[0]

    def issue(r, c):
        src = hn_ref.at[pl.ds(r, 1), :]
        pltpu.make_async_copy(src, xs_ref.at[pl.ds(p1_ref[0, 0, r], 1), :], sem).start()
        pltpu.make_async_copy(src, xs_ref.at[pl.ds(p2_ref[0, 0, r], 1), :], sem).start()
        return c

    for r in range(n):
        issue(r, 0)

    def drain(r, c):
        row = pltpu.make_async_copy(hn_ref.at[pl.ds(0, 1), :], xs_ref.at[pl.ds(0, 1), :], sem)
        row.wait()
        row.wait()
        return c

    lax.fori_loop(0, n, drain, 0, unroll=8)


def _dispatch(hn, pos1, pos2, nb):
    r = hn.shape[0]
    t = GATHER_TILE
    p = nb * MOE_BLOCK
    idx = pl.BlockSpec((1, 1, t), lambda i: (i, 0, 0), memory_space=pltpu.SMEM)
    return pl.pallas_call(
        _dispatch_kernel,
        grid=(r // t,),
        in_specs=[idx, idx, pl.BlockSpec((t, PACKED), lambda i: (i, 0)), pl.BlockSpec(memory_space=pl.ANY)],
        out_specs=pl.BlockSpec(memory_space=pl.ANY),
        out_shape=jax.ShapeDtypeStruct((p, PACKED), jnp.int32),
        scratch_shapes=[pltpu.SemaphoreType.DMA(())],
        input_output_aliases={3: 0},
        compiler_params=_params("arbitrary"),
        name="moe_dispatch",
    )(pos1.reshape(r // t, 1, t), pos2.reshape(r // t, 1, t), hn, jnp.zeros((p, PACKED), jnp.int32))


def _moe_ffn_kernel(be_ref, nu_ref, x_ref, wg_ref, wu_ref, wd_ref, o_ref, acc_ref):
    del be_ref
    b = pl.program_id(0)
    j = pl.program_id(1)
    last = j == pl.num_programs(1) - 1
    used = b < nu_ref[0]

    @pl.when(used)
    def _():
        x = _unpack_rows(x_ref[...]).astype(BF16)
        a = _silu(jnp.dot(x, wg_ref[...], preferred_element_type=F32)) * jnp.dot(x, wu_ref[...], preferred_element_type=F32)
        y = jnp.dot(a.astype(BF16), wd_ref[...], preferred_element_type=F32)

        @pl.when(j == 0)
        def _():
            acc_ref[...] = y

        @pl.when(j != 0)
        def _():
            acc_ref[...] += y

        @pl.when(last)
        def _():
            o_ref[...] = _pack_rows(acc_ref[...])

    @pl.when(jnp.logical_not(used) & last)
    def _():
        o_ref[...] = jnp.zeros_like(o_ref)


def _moe_ffn(xs, block_expert, n_used, wg, wu, wd):
    p = xs.shape[0]
    bm = MOE_BLOCK
    nf = D_FF // FF_TILE
    grid_spec = pltpu.PrefetchScalarGridSpec(
        num_scalar_prefetch=2,
        grid=(p // bm, nf),
        in_specs=[pl.BlockSpec((bm, PACKED), lambda b, j, be, nu: (b, 0)),
                  pl.BlockSpec((None, D_MODEL, FF_TILE), lambda b, j, be, nu: (be[b], 0, j)),
                  pl.BlockSpec((None, D_MODEL, FF_TILE), lambda b, j, be, nu: (be[b], 0, j)),
                  pl.BlockSpec((None, FF_TILE, D_MODEL), lambda b, j, be, nu: (be[b], j, 0))],
        out_specs=pl.BlockSpec((bm, PACKED), lambda b, j, be, nu: (b, 0)),
        scratch_shapes=[pltpu.VMEM((bm, D_MODEL), F32)])
    return pl.pallas_call(
        _moe_ffn_kernel,
        grid_spec=grid_spec,
        out_shape=jax.ShapeDtypeStruct((p, PACKED), jnp.int32),
        compiler_params=_params("arbitrary", "arbitrary"),
        name="moe_ffn",
    )(block_expert, n_used, xs, wg, wu, wd)


def _combine_kernel(p1_ref, p2_ref, route_ref, h_ref, g_ref, ys_ref, o_ref, buf1, buf2, sems, *, final):
    n = h_ref.shape[0]
    half = n // 2

    def issue(k):
        def body(r, c):
            pltpu.make_async_copy(ys_ref.at[pl.ds(p1_ref[0, 0, r], 1), :], buf1.at[pl.ds(r, 1), :], sems.at[k]).start()
            pltpu.make_async_copy(ys_ref.at[pl.ds(p2_ref[0, 0, r], 1), :], buf2.at[pl.ds(r, 1), :], sems.at[k]).start()
            return c
        for r in range(k * half, (k + 1) * half):
            body(r, 0)

    def drain(k):
        def body(r, c):
            row = pltpu.make_async_copy(ys_ref.at[pl.ds(0, 1), :], buf1.at[pl.ds(0, 1), :], sems.at[k])
            row.wait()
            row.wait()
            return c
        lax.fori_loop(0, half, body, 0, unroll=8)

    issue(0)
    issue(1)
    for k in range(2):
        drain(k)
        rows = slice(k * half, (k + 1) * half)
        rt = route_ref[rows, :]
        lane = lax.broadcasted_iota(jnp.int32, rt.shape, 1)
        w1 = jnp.sum(jnp.where(lane == _RT_W1, rt, 0.0), axis=-1, keepdims=True)
        w2 = jnp.sum(jnp.where(lane == _RT_W2, rt, 0.0), axis=-1, keepdims=True)
        h = h_ref[rows, :] + w1 * _unpack_rows(buf1[rows, :]) + w2 * _unpack_rows(buf2[rows, :])
        if final:
            ms = jnp.mean(h * h, axis=-1, keepdims=True)
            h = h * lax.rsqrt(ms + NORM_EPS) * g_ref[...]
        o_ref[rows, :] = h


def _combine(ys, pos1, pos2, route, h, gamma, batch, lp, final):
    r = h.shape[0]
    t = GATHER_TILE
    p1 = pos1.reshape(r // t, 1, t)
    p2 = pos2.reshape(r // t, 1, t)
    scratch = [pltpu.VMEM((t, PACKED), jnp.int32), pltpu.VMEM((t, PACKED), jnp.int32), pltpu.SemaphoreType.DMA((2,))]
    if final:
        per_seq = lp // t
        skip = FRONT // t
        rb = lambda b, i: b * per_seq + skip + i
        grid = (batch, per_seq - skip)
        idx = pl.BlockSpec((1, 1, t), lambda b, i: (rb(b, i), 0, 0), memory_space=pltpu.SMEM)
        in_specs = [idx, idx, pl.BlockSpec((t, LANES), lambda b, i: (rb(b, i), 0)),
                    pl.BlockSpec((t, D_MODEL), lambda b, i: (rb(b, i), 0)),
                    pl.BlockSpec((1, D_MODEL), lambda b, i: (0, 0)), pl.BlockSpec(memory_space=pl.ANY)]
        out_specs = pl.BlockSpec((None, t, D_MODEL), lambda b, i: (b, i, 0))
        out_shape = jax.ShapeDtypeStruct((batch, lp - FRONT, D_MODEL), F32)
        sem = ("arbitrary", "arbitrary")
    else:
        grid = (r // t,)
        idx = pl.BlockSpec((1, 1, t), lambda i: (i, 0, 0), memory_space=pltpu.SMEM)
        in_specs = [idx, idx, pl.BlockSpec((t, LANES), lambda i: (i, 0)), pl.BlockSpec((t, D_MODEL), lambda i: (i, 0)),
                    pl.BlockSpec((1, D_MODEL), lambda i: (0, 0)), pl.BlockSpec(memory_space=pl.ANY)]
        out_specs = pl.BlockSpec((t, D_MODEL), lambda i: (i, 0))
        out_shape = jax.ShapeDtypeStruct((r, D_MODEL), F32)
        sem = ("arbitrary",)
    return pl.pallas_call(
        functools.partial(_combine_kernel, final=final),
        grid=grid, in_specs=in_specs, out_specs=out_specs, out_shape=out_shape, scratch_shapes=scratch,
        compiler_params=_params(*sem),
        name="moe_combine_final" if final else "moe_combine",
    )(p1, p2, route, h, gamma, ys)


def _final_norm_kernel(h_ref, g_ref, o_ref):
    h = h_ref[...]
    ms = jnp.mean(h * h, axis=-1, keepdims=True)
    o_ref[...] = h * lax.rsqrt(ms + NORM_EPS) * g_ref[...]


def _final_norm(h, gamma, batch, lp):
    t = GATHER_TILE
    per_seq = lp // t
    skip = FRONT // t
    return pl.pallas_call(
        _final_norm_kernel,
        grid=(batch, per_seq - skip),
        in_specs=[pl.BlockSpec((t, D_MODEL), lambda b, i: (b * per_seq + skip + i, 0)),
                  pl.BlockSpec((1, D_MODEL), lambda b, i: (0, 0))],
        out_specs=pl.BlockSpec((None, t, D_MODEL), lambda b, i: (b, i, 0)),
        out_shape=jax.ShapeDtypeStruct((batch, lp - FRONT, D_MODEL), F32),
        compiler_params=_params("parallel", "parallel"),
        name="final_norm",
    )(h, gamma)


def _rope_tables(lp):
    half = RET_DIM // 2
    pos = jnp.arange(lp, dtype=F32) - float(PAD)
    inv = ROPE_BASE ** (-jnp.arange(half, dtype=F32) / half)
    ang = pos[:, None] * inv[None, :]
    cos = jnp.tile(jnp.cos(ang), (1, 2 * RET_HEADS))
    sin = jnp.tile(jnp.sin(ang), (1, 2 * RET_HEADS))
    return cos, sin


def _prep_layer(li, p):
    w_out = p["w_out"][li].astype(F32)
    wc = w_out[640:1024].reshape(GLA_HEADS, GLA_DV, D_MODEL)
    wc = jnp.pad(wc, ((0, 0), (0, GLA_DV_PAD - GLA_DV), (0, 0))).reshape(GLA_HEADS * GLA_DV_PAD, D_MODEL)
    wgate, bgate = _pack_gate(p["gla_w_gate_f"][li], p["gla_b_gate_f"][li], p["gla_w_gate_b"][li], p["gla_b_gate_b"][li])
    return dict(
        norm_mix=p["norm_mix"][li].astype(F32)[None],
        w_in=_pack_w_in(p["w_in"][li].astype(F32)),
        wgate=wgate, bgate=bgate,
        s5=_s5_tables(p["s5_lambda_re"][li], p["s5_lambda_im"][li], p["s5_log_dt"][li], p["s5_b_re"][li],
                      p["s5_b_im"][li], p["s5_c_re"][li], p["s5_c_im"][li], p["s5_d"][li]),
        w_glu=p["s5_w_glu"][li].astype(BF16),
        w_out=jnp.concatenate([w_out[0:640], wc], axis=0).astype(BF16),
        norm_ffn=p["norm_ffn"][li].astype(F32)[None],
    )


def _trunk(x, meta_tokens, layers, ffn, moe, norm_final, depth):
    batch, seq, _ = x.shape
    lp = seq + FRONT
    r = batch * lp
    meta = jnp.broadcast_to(meta_tokens.astype(F32)[None], (batch, N_META, D_MODEL))
    h = jnp.concatenate([jnp.zeros((batch, PAD, D_MODEL), F32), meta, x.astype(F32)], axis=1).reshape(r, D_MODEL)
    cos, sin = _rope_tables(lp)
    cos = jnp.tile(cos, (batch, 1))
    sin = jnp.tile(sin, (batch, 1))
    gamma_final = norm_final.astype(F32)[None]
    for li in range(depth):
        lw = layers[li]
        u, rq, rk, rv, rg, gq, gk, gv, gg, lf, lb = _inproj(
            h, lw["norm_mix"], lw["w_in"], cos, sin, lw["wgate"], lw["bgate"], batch, lp)
        ya = _s5_mixer(u, lw["s5"], batch, lp)
        yb = _retention(rq, rk, rv, rg, batch, lp)
        yc = _linear_attention(gq, gk, gv, lf, lb, gg, batch, lp, npair=GLA_HEADS // 2, vp=2 * GLA_DV_PAD, dv=GLA_DV)
        j = li // 2
        last = li == depth - 1
        if li % 2 == 0:
            h, hn = _outproj(ya, yb, yc, h, lw["w_glu"], lw["w_out"], lw["norm_ffn"])
            h = _ffn(hn, h, ffn["wg"][j], ffn["wu"][j], ffn["wd"][j])
        else:
            h, hn, route, cnt = _outproj(ya, yb, yc, h, lw["w_glu"], lw["w_out"], lw["norm_ffn"], moe["wr"][j])
            pos1, pos2, block_expert, n_used, nb = _route_meta(route, cnt, r)
            xs = _dispatch(hn, pos1, pos2, nb)
            ys = _moe_ffn(xs, block_expert, n_used, moe["wg"][j], moe["wu"][j], moe["wd"][j])
            h = _combine(ys, pos1, pos2, route, h, gamma_final, batch, lp, final=last)
            if last:
                return h
    return _final_norm(h, gamma_final, batch, lp)


def kernel(x_prompt, x_sample, meta_tokens, norm_mix, w_in, s5_lambda_re, s5_lambda_im, s5_log_dt, s5_b_re, s5_b_im, s5_c_re, s5_c_im, s5_d, s5_w_glu, gla_w_gate_f, gla_b_gate_f, gla_w_gate_b, gla_b_gate_b, w_out, norm_ffn, ffn_w_gate, ffn_w_up, ffn_w_down, router_w, moe_w_gate, moe_w_up, moe_w_down, norm_final):
    depth = w_in.shape[0]
    p = dict(norm_mix=norm_mix, w_in=w_in, s5_lambda_re=s5_lambda_re, s5_lambda_im=s5_lambda_im,
             s5_log_dt=s5_log_dt, s5_b_re=s5_b_re, s5_b_im=s5_b_im, s5_c_re=s5_c_re, s5_c_im=s5_c_im, s5_d=s5_d,
             s5_w_glu=s5_w_glu, gla_w_gate_f=gla_w_gate_f, gla_b_gate_f=gla_b_gate_f, gla_w_gate_b=gla_w_gate_b,
             gla_b_gate_b=gla_b_gate_b, w_out=w_out, norm_ffn=norm_ffn)
    layers = [_prep_layer(li, p) for li in range(depth)]
    ffn = dict(wg=ffn_w_gate.astype(BF16), wu=ffn_w_up.astype(BF16), wd=ffn_w_down.astype(BF16))
    wr = jnp.pad(router_w.astype(F32), ((0, 0), (0, 0), (0, LANES - N_EXPERTS)))
    wr_hi = wr.astype(BF16)
    wr_lo = (wr - wr_hi.astype(F32)).astype(BF16)
    wr = [(wr_hi[j], wr_lo[j]) for j in range(wr.shape[0])]
    moe = dict(wr=wr, wg=moe_w_gate.astype(BF16), wu=moe_w_up.astype(BF16), wd=moe_w_down.astype(BF16))
    y_prompt = _trunk(x_prompt, meta_tokens, layers, ffn, moe, norm_final, depth)
    y_sample = _trunk(x_sample, meta_tokens, layers, ffn, moe, norm_final, depth)
    return (y_prompt, y_sample)
```

```python
import functools
import math

import jax
import jax.numpy as jnp
from jax import lax
from jax.experimental import pallas as pl
from jax.experimental.pallas import tpu as pltpu

F32 = jnp.float32
BF16 = jnp.bfloat16

D_MODEL = 1024
N_META = 16
S5_P = 16
S5_WIDTH = 256
S5_GROUPS = 16
S5_N = 64
RET_HEADS = 6
RET_DIM = 64
RET_WIDTH = 384
GLA_HEADS = 4
GLA_DV = 96
GLA_DK = 48
GLA_QK = 192
GLA_WIDTH = 384
GLA_GATE_RANK = 16
GLA_TAU = 16.0
ROPE_BASE = 10000.0
D_FF = 2816
N_EXPERTS = 8
NORM_EPS = 1e-5

CHUNK = 64
FRONT = 256
PAD = FRONT - N_META
ROW_TILE = 512
LA_GROUP = 4
LANES = 128
GLA_DK_PAD = 64
GLA_DV_PAD = 128
VMEM_LIMIT = 56 * 1024 * 1024


def _params(*sem):
    return pltpu.CompilerParams(dimension_semantics=sem, vmem_limit_bytes=VMEM_LIMIT)


def _const_spec(shape):
    nd = len(shape)
    return pl.BlockSpec(shape, lambda *_: (0,) * nd)


def _sigmoid(x):
    return 1.0 / (1.0 + jnp.exp(-x))


def _silu(x):
    return x * _sigmoid(x)


def _gelu_tanh(x):
    c = math.sqrt(2.0 / math.pi)
    return 0.5 * x * (1.0 + jnp.tanh(c * (x + 0.044715 * (x * x * x))))


def _log_sigmoid(z):
    return jnp.minimum(z, 0.0) - jnp.log(1.0 + jnp.exp(-jnp.abs(z)))


def _split_dot(a, b_bf16, dims=None):
    hi = a.astype(BF16)
    lo = (a - hi.astype(F32)).astype(BF16)
    if dims is None:
        return (jnp.dot(hi, b_bf16, preferred_element_type=F32)
                + jnp.dot(lo, b_bf16, preferred_element_type=F32))
    return (lax.dot_general(hi, b_bf16, dims, preferred_element_type=F32)
            + lax.dot_general(lo, b_bf16, dims, preferred_element_type=F32))


_C_U = 0
_C_RQ = 256
_C_RK = 640
_C_RV = 1024
_C_RG = 1408
_C_GQ = 1792
_C_GK = 2048
_C_GV = 2304
_C_GG = 2816
_C_GL = 3328
_C_END = 3456


def _pack_w_in(w):
    o = 0
    u = w[:, o:o + 256]; o += 256
    rq = w[:, o:o + 384]; o += 384
    rk = w[:, o:o + 384]; o += 384
    rv = w[:, o:o + 384]; o += 384
    rg = w[:, o:o + 384]; o += 384
    gq = w[:, o:o + 192]; o += 192
    gk = w[:, o:o + 192]; o += 192
    gv = w[:, o:o + 384]; o += 384
    gg = w[:, o:o + 384]; o += 384
    glf = w[:, o:o + 16]; o += 16
    glb = w[:, o:o + 16]; o += 16

    def padh(m, d, dp):
        m = m.reshape(D_MODEL, GLA_HEADS, d)
        return jnp.pad(m, ((0, 0), (0, 0), (0, dp - d))).reshape(D_MODEL, GLA_HEADS * dp)

    gl = jnp.pad(jnp.concatenate([glf, glb], axis=1), ((0, 0), (0, LANES - 2 * GLA_GATE_RANK)))
    cat = jnp.concatenate([
        u, rq, rk, rv, rg,
        padh(gq, GLA_DK, GLA_DK_PAD), padh(gk, GLA_DK, GLA_DK_PAD),
        padh(gv, GLA_DV, GLA_DV_PAD), padh(gg, GLA_DV, GLA_DV_PAD), gl], axis=1)
    return cat.astype(BF16)


def _pack_gate(w_f, b_f, w_b, b_b):
    def padh(m):
        m = m.reshape(m.shape[0], GLA_HEADS, GLA_DK)
        return jnp.pad(m, ((0, 0), (0, 0), (0, GLA_DK_PAD - GLA_DK))).reshape(m.shape[0], GLA_HEADS * GLA_DK_PAD)
    r = GLA_GATE_RANK
    w = jnp.zeros((LANES, 2 * GLA_HEADS * GLA_DK_PAD), F32)
    w = w.at[0:r, 0:256].set(padh(w_f.astype(F32)))
    w = w.at[r:2 * r, 256:512].set(padh(w_b.astype(F32)))
    b = jnp.concatenate([padh(b_f.astype(F32)[None]), padh(b_b.astype(F32)[None])], axis=1)
    return w.astype(BF16), b


def _inproj_kernel(h_ref, g_ref, w_ref, cos_ref, sin_ref, wgate_ref, bgate_ref,
                   u_ref, rq_ref, rk_ref, rv_ref, rg_ref, gq_ref, gk_ref, gv_ref, gg_ref, lf_ref, lb_ref,
                   *, batch, lp):
    tm = h_ref.shape[0]
    x = h_ref[...]
    ms = jnp.mean(x * x, axis=-1, keepdims=True)
    row = pl.program_id(0) * tm + lax.broadcasted_iota(jnp.int32, (tm, 1), 0)
    valid = jnp.ones((tm, 1), F32)
    for b in range(batch):
        valid = jnp.where((row >= b * lp) & (row < b * lp + PAD), 0.0, valid)
    hn = (x * (lax.rsqrt(ms + NORM_EPS) * valid) * g_ref[...]).astype(BF16)

    full = jnp.dot(hn, w_ref[...], preferred_element_type=F32)

    def proj(lo, hi):
        return full[:, lo:hi]

    u_ref[...] = proj(_C_U, _C_RQ)
    cos = cos_ref[...]
    sin = sin_ref[...]
    half = RET_DIM // 2
    first_half = (lax.broadcasted_iota(jnp.int32, (1, RET_WIDTH), 1) & (RET_DIM - 1)) < half

    def rope(x):
        rot = jnp.where(first_half, -pltpu.roll(x, RET_WIDTH - half, 1), pltpu.roll(x, half, 1))
        return x * cos + rot * sin

    rq_ref[...] = rope(proj(_C_RQ, _C_RK)).astype(BF16)
    rk_ref[...] = (rope(proj(_C_RK, _C_RV)) * (RET_DIM ** -0.5)).astype(BF16)
    rv_ref[...] = proj(_C_RV, _C_RG).astype(BF16)
    rg_ref[...] = proj(_C_RG, _C_GQ).astype(BF16)
    gq_ref[...] = (proj(_C_GQ, _C_GK) * (GLA_DK ** -0.5)).astype(BF16)
    gk_ref[...] = proj(_C_GK, _C_GV).astype(BF16)
    gv_ref[...] = proj(_C_GV, _C_GG).astype(BF16)
    gg_ref[...] = proj(_C_GG, _C_GL).astype(BF16)
    codes = proj(_C_GL, _C_END).astype(BF16)
    z = jnp.dot(codes, wgate_ref[...], preferred_element_type=F32) + bgate_ref[...]
    ls = _log_sigmoid(z) * (1.0 / GLA_TAU)
    lf_ref[...] = ls[:, 0:256]
    lb_ref[...] = ls[:, 256:512]


def _inproj(h, gamma, w, cos, sin, wgate, bgate, batch, lp):
    r = h.shape[0]
    tm = ROW_TILE
    widths = (256, 384, 384, 384, 384, 256, 256, 512, 512, 256, 256)
    dtypes = (F32,) + (BF16,) * 8 + (F32, F32)

    def rows(wd):
        return pl.BlockSpec((tm, wd), lambda i: (i, 0))

    return pl.pallas_call(
        functools.partial(_inproj_kernel, batch=batch, lp=lp),
        grid=(r // tm,),
        in_specs=[rows(D_MODEL), _const_spec((1, D_MODEL)), _const_spec((D_MODEL, _C_END)),
                  rows(RET_WIDTH), rows(RET_WIDTH), _const_spec((LANES, 512)), _const_spec((1, 512))],
        out_specs=[rows(wd) for wd in widths],
        out_shape=[jax.ShapeDtypeStruct((r, wd), dt) for wd, dt in zip(widths, dtypes)],
        compiler_params=_params("parallel"),
        name="inproj",
    )(h, gamma, w, cos, sin, wgate, bgate)


def _s5_toeplitz_kernel(pwr_ref, pwi_ref, cer_ref, cei_ref, bbr_ref, bbi_ref, d_ref, tt_ref):
    hp = lax.Precision.HIGHEST
    w = CHUNK * S5_P
    lane = lax.broadcasted_iota(jnp.int32, (S5_P, w), 1)
    krow = []
    for dr in range(2):
        pr, pi, cr, ci = pwr_ref[dr], pwi_ref[dr], cer_ref[dr], cei_ref[dr]
        zr = pr * cr - pi * ci
        zi = pr * ci + pi * cr
        krow.append(jnp.dot(bbr_ref[dr], zr, precision=hp, preferred_element_type=F32)
                    - jnp.dot(bbi_ref[dr], zi, precision=hp, preferred_element_type=F32))
    kf = krow[0] + d_ref[...]
    kb = krow[1]
    for s in range(CHUNK):
        right = S5_P * s
        left = S5_P * (CHUNK - 1 - s)
        a = kf if right == 0 else jnp.where(lane >= right, pltpu.roll(kf, right, 1), 0.0)
        b = kb if left == 0 else jnp.where(lane < w - left, pltpu.roll(kb, w - left, 1), 0.0)
        tt_ref[S5_P * s:S5_P * (s + 1), :] = (a + b).astype(BF16)


def _s5_toeplitz(pwr, pwi, cer, cei, bbr, bbi, d_e):
    w = CHUNK * S5_P
    big = pl.BlockSpec((None, 2, S5_N, w), lambda g: (g, 0, 0, 0))
    small = pl.BlockSpec((None, 2, S5_P, S5_N), lambda g: (g, 0, 0, 0))
    return pl.pallas_call(
        _s5_toeplitz_kernel,
        grid=(S5_GROUPS,),
        in_specs=[big, big, big, big, small, small, pl.BlockSpec((None, S5_P, w), lambda g: (g, 0, 0))],
        out_specs=pl.BlockSpec((None, w, w), lambda g: (g, 0, 0)),
        out_shape=jax.ShapeDtypeStruct((S5_GROUPS, w, w), BF16),
        compiler_params=_params("parallel"),
        name="s5_toeplitz",
    )(pwr, pwi, cer, cei, bbr, bbi, d_e)


def _s5_tables(lam_re, lam_im, log_dt, b_re, b_im, c_re, c_im, d):
    c = CHUNK
    g_, n_, p_ = S5_GROUPS, S5_N, S5_P
    dt = jnp.exp(log_dt.astype(F32))[..., None]
    lr = lam_re.astype(F32)
    li = lam_im.astype(F32)
    e = lr * dt
    th = li * dt
    mag = jnp.exp(e)
    a_re = mag * jnp.cos(th)
    a_im = mag * jnp.sin(th)
    den = lr * lr + li * li
    nr = a_re - 1.0
    ni = a_im
    coef_re = (nr * lr + ni * li) / den
    coef_im = (ni * lr - nr * li) / den
    br = b_re.astype(F32)
    bi = b_im.astype(F32)
    bb_re = coef_re[..., None] * br - coef_im[..., None] * bi
    bb_im = coef_re[..., None] * bi + coef_im[..., None] * br
    cr = c_re.astype(F32)
    ci = c_im.astype(F32)
    tau = jnp.arange(c + 1, dtype=F32)[:, None, None, None]
    pw_mag = jnp.exp(tau * e[None])
    pw_re = pw_mag * jnp.cos(tau * th[None])
    pw_im = pw_mag * jnp.sin(tau * th[None])

    z_re = cr[None] * pw_re[:, :, :, None, :] - ci[None] * pw_im[:, :, :, None, :]
    z_im = cr[None] * pw_im[:, :, :, None, :] + ci[None] * pw_re[:, :, :, None, :]
    idx = jnp.arange(c)

    def expand_pw(pw):
        both = jnp.stack([pw[:c, 0], pw[c - 1 - idx, 1]], axis=0)
        both = jnp.transpose(both, (2, 0, 3, 1))
        return jnp.broadcast_to(both[..., None], (g_, 2, n_, c, p_)).reshape(g_, 2, n_, c * p_)

    def expand_c(cm):
        cm = jnp.transpose(cm, (1, 0, 3, 2))
        return jnp.broadcast_to(cm[:, :, :, None, :], (g_, 2, n_, c, p_)).reshape(g_, 2, n_, c * p_)

    bbt_re = jnp.transpose(bb_re, (1, 0, 3, 2))
    bbt_im = jnp.transpose(bb_im, (1, 0, 3, 2))
    d_e = jnp.zeros((g_, p_, c * p_), F32).at[:, :, :p_].set(d.astype(F32)[:, :, None] * jnp.eye(p_, dtype=F32))
    tt = _s5_toeplitz(expand_pw(pw_re), expand_pw(pw_im), expand_c(cr), expand_c(ci), bbt_re, bbt_im, d_e)

    pf_re = pw_re[c - 1 - idx, 0]
    pf_im = pw_im[c - 1 - idx, 0]
    pb_re = pw_re[idx, 1]
    pb_im = pw_im[idx, 1]

    def m_of(p_re_, p_im_, dr):
        m_re = p_re_[:, :, :, None] * bb_re[dr][None] - p_im_[:, :, :, None] * bb_im[dr][None]
        m_im = p_re_[:, :, :, None] * bb_im[dr][None] + p_im_[:, :, :, None] * bb_re[dr][None]
        to = lambda m: jnp.transpose(m, (1, 0, 3, 2)).reshape(g_, c * p_, n_)
        return to(m_re), to(m_im)

    mf_re, mf_im = m_of(pf_re, pf_im, 0)
    mb_re, mb_im = m_of(pb_re, pb_im, 1)
    m4 = jnp.stack([mf_re, mf_im, mb_re, mb_im], axis=1)
    m4 = m4.reshape(g_ // 2, 2, 4, c * p_, n_)
    mz = jnp.zeros_like(m4[:, 0])

    def m_rows(blocks):
        return jnp.transpose(jnp.concatenate(blocks, axis=-1), (0, 2, 1, 3)).reshape(g_ // 2, c * p_, 8 * n_)

    mpair = jnp.concatenate([m_rows([m4[:, 0], mz]), m_rows([mz, m4[:, 1]])], axis=1).astype(BF16)

    def n_of(tsel, dr):
        w_re = z_re[tsel, dr]
        w_im = z_im[tsel, dr]
        to = lambda m: jnp.transpose(m, (1, 3, 0, 2)).reshape(g_, n_, c * p_)
        return to(w_re), to(-w_im)

    nf_re, nf_im = n_of(idx + 1, 0)
    nb_re, nb_im = n_of(c - idx, 1)
    n4 = jnp.stack([nf_re, nf_im, nb_re, nb_im], axis=1)
    n4 = n4.reshape(g_ // 2, 2, 4, n_, c * p_)
    nz = jnp.zeros_like(n4[:, 0])
    npair = jnp.stack([jnp.concatenate([n4[:, 0], nz], axis=-1), jnp.concatenate([nz, n4[:, 1]], axis=-1)], axis=2)
    npair = npair.reshape(g_ // 2, 4 * 2 * n_, 2 * c * p_).astype(BF16)

    dec = jnp.stack([pw_re[c, 0], pw_im[c, 0], pw_re[c, 1], pw_im[c, 1]], axis=0)
    dec = dec.reshape(4, 1, g_ * n_)
    return tt, mpair, npair, dec


S5_PITCH = CHUNK + 4
S5_HALVES = S5_WIDTH // LANES
S5_GPH = LANES // S5_P


def _s5_chunk_block(nch):
    best = 8
    for cand in range(8, 113, 8):
        if nch % cand == 0:
            best = cand
    return best


def _block_transpose8(tiles):
    lane = lax.broadcasted_iota(jnp.int32, tiles[0].shape, 1)
    tiles = list(tiles)
    for dist in (4, 2, 1):
        width = S5_P * dist
        low = (lane & (2 * width - 1)) < width
        for k in range(S5_GPH):
            if k & dist:
                continue
            a, b = tiles[k], tiles[k + dist]
            tiles[k] = jnp.where(low, a, pltpu.roll(b, width, 1))
            tiles[k + dist] = jnp.where(low, pltpu.roll(a, LANES - width, 1), b)
    return tiles


def _s5_in_kernel(u_hbm, m_ref, ucat_ref, o0, o1, o2, o3, xpad, uall, sem):
    i = pl.program_id(0)
    j = pl.program_id(1)
    cb = uall.shape[1]

    @pl.when(j == 0)
    def _():
        def chunk_copy(c, h):
            return pltpu.make_async_copy(
                u_hbm.at[pl.ds((i * cb + c) * CHUNK, CHUNK), pl.ds(h * LANES, LANES)],
                xpad.at[h, pl.ds(c * S5_PITCH, CHUNK), :], sem)

        def start(c, carry):
            for h in range(S5_HALVES):
                chunk_copy(c, h).start()
            return carry

        def wait(c, carry):
            for h in range(S5_HALVES):
                chunk_copy(c, h).wait()
            return carry

        lax.fori_loop(0, cb, start, 0)
        lax.fori_loop(0, cb, wait, 0)

        def dest_tile(jt, carry):
            col = pl.multiple_of(jt * LANES, LANES)
            for h in range(S5_HALVES):
                by_token = [xpad[h, pl.ds(jt * S5_GPH + k, cb, stride=S5_PITCH), :] for k in range(S5_GPH)]
                for go, tile in enumerate(_block_transpose8(by_token)):
                    uall[h * S5_GPH + go, :, pl.ds(col, LANES)] = tile
            return carry

        lax.fori_loop(0, CHUNK // S5_GPH, dest_tile, 0)

    u2 = jnp.concatenate([uall[2 * j], uall[2 * j + 1]], axis=1)
    ucat_ref[...] = u2
    s = jnp.dot(u2.astype(BF16), m_ref[...], preferred_element_type=F32)
    o0[...] = s[:, 0:128]
    o1[...] = s[:, 128:256]
    o2[...] = s[:, 256:384]
    o3[...] = s[:, 384:512]


def _s5_in(u, mpair):
    nch = u.shape[0] // CHUNK
    cb = _s5_chunk_block(nch)
    w = CHUNK * S5_P
    return pl.pallas_call(
        _s5_in_kernel,
        grid=(nch // cb, S5_GROUPS // 2),
        in_specs=[pl.BlockSpec(memory_space=pl.ANY),
                  pl.BlockSpec((None, 2 * w, 512), lambda i, j: (j, 0, 0))],
        out_specs=[pl.BlockSpec((cb, 2 * w), lambda i, j: (i, j))] + [pl.BlockSpec((cb, LANES), lambda i, j: (i, j))] * 4,
        out_shape=[jax.ShapeDtypeStruct((nch, S5_GROUPS * w), F32)] + [jax.ShapeDtypeStruct((nch, 1024), F32)] * 4,
        scratch_shapes=[pltpu.VMEM((S5_HALVES, cb * S5_PITCH, LANES), F32), pltpu.VMEM((S5_GROUPS, cb, w), F32),
                        pltpu.SemaphoreType.DMA(())],
        compiler_params=_params("arbitrary", "arbitrary"),
        name="s5_in",
    )(u, mpair)


def _s5_scan_kernel(sfr, sfi, sbr, sbi, dfr, dfi, dbr, dbi, hfr, hfi, hbr, hbi):
    nc, b, _ = sfr.shape
    a_fr = dfr[...]
    a_fi = dfi[...]
    a_br = dbr[...]
    a_bi = dbi[...]
    zero = jnp.zeros((b, LANES), F32)

    def fwd(c, carry):
        hr, hi = carry
        hfr[c] = hr
        hfi[c] = hi
        return (a_fr * hr - a_fi * hi + sfr[c], a_fr * hi + a_fi * hr + sfi[c])

    def bwd(i, carry):
        c = nc - 1 - i
        hr, hi = carry
        hbr[c] = hr
        hbi[c] = hi
        return (a_br * hr - a_bi * hi + sbr[c], a_br * hi + a_bi * hr + sbi[c])

    lax.fori_loop(0, nc, fwd, (zero, zero))
    lax.fori_loop(0, nc, bwd, (zero, zero))


def _s5_scan(s4, dec, nc, b):
    blk = pl.BlockSpec((nc, b, LANES), lambda j: (0, 0, j))
    dspecs = [pl.BlockSpec((None, 1, LANES), functools.partial(lambda j, k: (k, 0, j), k=k)) for k in range(4)]
    return pl.pallas_call(
        _s5_scan_kernel,
        grid=(S5_GROUPS // 2,),
        in_specs=[blk] * 4 + dspecs,
        out_specs=[blk] * 4,
        out_shape=[jax.ShapeDtypeStruct((nc, b, 1024), F32)] * 4,
        compiler_params=_params("parallel"),
        name="s5_scan",
    )(*s4, dec, dec, dec, dec)


def _s5_out_kernel(u_ref, tt_ref, h0, h1, h2, h3, n_ref, y_hbm, yall, ypad, sem):
    i = pl.program_id(0)
    j = pl.program_id(1)
    cb = yall.shape[1]
    w = CHUNK * S5_P
    u = u_ref[...].astype(BF16)
    y0 = jnp.dot(u[:, 0:w], tt_ref[0], preferred_element_type=F32)
    y1 = jnp.dot(u[:, w:2 * w], tt_ref[1], preferred_element_type=F32)
    hcat = jnp.concatenate([h0[...], h1[...], h2[...], h3[...]], axis=1).astype(BF16)
    yh = jnp.dot(hcat, n_ref[...], preferred_element_type=F32)
    yall[2 * j] = y0 + yh[:, 0:w]
    yall[2 * j + 1] = y1 + yh[:, w:2 * w]

    @pl.when(j == pl.num_programs(1) - 1)
    def _():
        def src_tile(jt, carry):
            col = pl.multiple_of(jt * LANES, LANES)
            for h in range(S5_HALVES):
                by_group = [yall[h * S5_GPH + go, :, pl.ds(col, LANES)] for go in range(S5_GPH)]
                for k, tile in enumerate(_block_transpose8(by_group)):
                    ypad[h, pl.ds(jt * S5_GPH + k, cb, stride=S5_PITCH), :] = tile
            return carry

        lax.fori_loop(0, CHUNK // S5_GPH, src_tile, 0)

        def chunk_copy(c, h):
            return pltpu.make_async_copy(
                ypad.at[h, pl.ds(c * S5_PITCH, CHUNK), :],
                y_hbm.at[pl.ds((i * cb + c) * CHUNK, CHUNK), pl.ds(h * LANES, LANES)], sem)

        def start(c, carry):
            for h in range(S5_HALVES):
                chunk_copy(c, h).start()
            return carry

        def wait(c, carry):
            for h in range(S5_HALVES):
                chunk_copy(c, h).wait()
            return carry

        lax.fori_loop(0, cb, start, 0)
        lax.fori_loop(0, cb, wait, 0)


def _s5_out(ucat, tt, h4, npair):
    nch = ucat.shape[0]
    cb = _s5_chunk_block(nch)
    w = CHUNK * S5_P
    hblk = pl.BlockSpec((cb, LANES), lambda i, j: (i, j))
    return pl.pallas_call(
        _s5_out_kernel,
        grid=(nch // cb, S5_GROUPS // 2),
        in_specs=[pl.BlockSpec((cb, 2 * w), lambda i, j: (i, j)),
                  pl.BlockSpec((2, w, w), lambda i, j: (j, 0, 0)),
                  hblk, hblk, hblk, hblk,
                  pl.BlockSpec((None, 512, 2 * w), lambda i, j: (j, 0, 0))],
        out_specs=pl.BlockSpec(memory_space=pl.ANY),
        out_shape=jax.ShapeDtypeStruct((nch * CHUNK, S5_WIDTH), F32),
        scratch_shapes=[pltpu.VMEM((S5_GROUPS, cb, w), F32), pltpu.VMEM((S5_HALVES, cb * S5_PITCH, LANES), F32),
                        pltpu.SemaphoreType.DMA(())],
        compiler_params=_params("arbitrary", "arbitrary"),
        name="s5_out",
    )(ucat, tt, *h4, npair)


def _s5_mixer(u, tables, batch, lp):
    tt, mpair, npair, dec = tables
    nc = lp // CHUNK
    ucat, *s4 = _s5_in(u, mpair)
    s4 = [jnp.transpose(s.reshape(batch, nc, 1024), (1, 0, 2)) for s in s4]
    h4 = _s5_scan(s4, dec, nc, batch)
    h4 = [jnp.transpose(h, (1, 0, 2)).reshape(batch * nc, 1024) for h in h4]
    return _s5_out(ucat, tt, h4, npair)


def _cumsum_chunks(x):
    n = x.shape[0]
    r = lax.broadcasted_iota(jnp.int32, (n, n), 0)
    s = lax.broadcasted_iota(jnp.int32, (n, n), 1)
    tri = jnp.where((s <= r) & ((s // CHUNK) == (r // CHUNK)), 1.0, 0.0).astype(BF16)
    hi = x.astype(BF16)
    lo = (x - hi.astype(F32)).astype(BF16)
    return jnp.dot(tri, hi, preferred_element_type=F32) + jnp.dot(tri, lo, preferred_element_type=F32)


def _head_masks(npair, vp):
    kl = lax.broadcasted_iota(jnp.int32, (1, LANES), 1)
    km = [(kl < 64), (kl >= 64)]
    vl = lax.broadcasted_iota(jnp.int32, (1, vp), 1)
    vm = [(vl < vp // 2), (vl >= vp // 2)]
    vrow = lax.broadcasted_iota(jnp.int32, (vp, LANES), 0)
    kcol = lax.broadcasted_iota(jnp.int32, (vp, LANES), 1)
    bd = (vrow >= vp // 2) == (kcol >= 64)
    return km, vm, bd


def _la_bwd_kernel(k_all, v_all, lb_all, sb_all, st_all, *, npair, vp, nsub, nseq):
    g = LA_GROUP
    c = CHUNK
    gc = g * c

    @pl.when(pl.program_id(1) == 0)
    def _():
        st_all[...] = jnp.zeros_like(st_all)

    km, vm, bd = _head_masks(npair, vp)

    def sub_block(t, carry):
        for s in range(nseq):
            one_seq(k_all.at[s], v_all.at[s], lb_all.at[s], sb_all.at[s], st_all.at[s], nsub - 1 - t)
        return carry

    def one_seq(k_ref, v_ref, lb_ref, sb_ref, st_ref, sub):
        base = pl.multiple_of(sub * gc, gc)
        kblk = k_ref[pl.ds(base, gc), :].astype(F32)
        vblk = v_ref[pl.ds(base, gc), :]
        lblk = lb_ref[pl.ds(base, gc), :]
        cblk = _cumsum_chunks(lblk)
        kb2blk = (kblk * jnp.exp(cblk - lblk)).astype(BF16)
        states = [st_ref[p] for p in range(npair)]
        for gi in reversed(range(g)):
            rows = slice(gi * c, (gi + 1) * c)
            kb2 = kb2blk[rows, :]
            dec = jnp.exp(cblk[gi * c + c - 1:(gi + 1) * c, :])
            vv = vblk[rows, :]
            for p in range(npair):
                sb_ref[sub * g + gi, p] = states[p].astype(BF16)
                upd = lax.dot_general(vv[:, p * vp:(p + 1) * vp], kb2[:, p * LANES:(p + 1) * LANES],
                                      (((0,), (0,)), ((), ())), preferred_element_type=F32)
                states[p] = states[p] * dec[:, p * LANES:(p + 1) * LANES] + jnp.where(bd, upd, 0.0)
        for p in range(npair):
            st_ref[p] = states[p]

    lax.fori_loop(0, nsub, sub_block, 0)


def _head_rms_gate(o, gate, bdv, dv):
    ms = _split_dot(o * o, bdv) * (1.0 / dv)
    return (o * lax.rsqrt(ms + NORM_EPS) * _silu(gate.astype(F32))).astype(BF16)


def _la_fwd_kernel(q_all, k_all, v_all, lf_all, lb_all, sb_all, gate_all, bdv_ref, o_all, st_all,
                   *, npair, vp, dv, nsub, nseq):
    g = LA_GROUP
    c = CHUNK
    gc = g * c

    @pl.when(pl.program_id(1) == 0)
    def _():
        st_all[...] = jnp.zeros_like(st_all)

    km, vm, bd = _head_masks(npair, vp)
    ri = lax.broadcasted_iota(jnp.int32, (c, LANES), 0)
    cj = lax.broadcasted_iota(jnp.int32, (c, LANES), 1) & (c - 1)
    lower = ri >= cj
    upper = ri <= cj
    mid = c // 2
    nt = (((1,), (1,)), ((), ()))
    tn = (((0,), (0,)), ((), ()))

    def sub_block(sub, carry):
        for s in range(nseq):
            one_seq(q_all.at[s], k_all.at[s], v_all.at[s], lf_all.at[s], lb_all.at[s], sb_all.at[s],
                    gate_all.at[s], o_all.at[s], st_all.at[s], sub)
        return carry

    def one_seq(q_ref, k_ref, v_ref, lf_ref, lb_ref, sb_ref, gate_ref, o_ref, st_ref, sub):
        base = pl.multiple_of(sub * gc, gc)
        qblk = q_ref[pl.ds(base, gc), :].astype(F32)
        kblk = k_ref[pl.ds(base, gc), :].astype(F32)
        vblk = v_ref[pl.ds(base, gc), :]
        lfblk = lf_ref[pl.ds(base, gc), :]
        lbblk = lb_ref[pl.ds(base, gc), :]
        w = lfblk.shape[1]
        csblk = _cumsum_chunks(jnp.concatenate([lfblk, lbblk], axis=1))
        states = [st_ref[p] for p in range(npair)]
        outs = []
        for gi in range(g):
            rows = slice(gi * c, (gi + 1) * c)
            qq = qblk[rows, :]
            kk = kblk[rows, :]
            vv = vblk[rows, :]
            cf = csblk[rows, :w]
            cbi = csblk[rows, w:]
            cb = cbi - lbblk[rows, :]
            mf = cf[mid:mid + 1, :]
            mb = cb[mid:mid + 1, :]
            tf = cf[c - 1:c, :]
            tb = cbi[c - 1:c, :]
            qf = (qq * jnp.exp(cf - mf)).astype(BF16)
            kf = (kk * jnp.exp(mf - cf)).astype(BF16)
            qb = (qq * jnp.exp(mb - cb)).astype(BF16)
            kb = (kk * jnp.exp(cb - mb)).astype(BF16)
            zk = jnp.zeros((c, LANES), BF16)
            q2 = jnp.concatenate([qq * jnp.exp(cf), qq * jnp.exp(tb - cb)], axis=0).astype(BF16)
            kf2 = (kk * jnp.exp(tf - cf)).astype(BF16)
            dec = jnp.exp(tf)
            pair_out = []
            for p in range(npair):
                ks = slice(p * LANES, (p + 1) * LANES)
                vsl = slice(p * vp, (p + 1) * vp)
                kfp = kf[:, ks]
                kbp = kb[:, ks]
                k2f = jnp.concatenate([jnp.where(km[0], kfp, zk), jnp.where(km[1], kfp, zk)], axis=0)
                k2b = jnp.concatenate([jnp.where(km[0], kbp, zk), jnp.where(km[1], kbp, zk)], axis=0)
                sf = lax.dot_general(qf[:, ks], k2f, nt, preferred_element_type=F32)
                sb = lax.dot_general(qb[:, ks], k2b, nt, preferred_element_type=F32)
                sc = (jnp.where(lower, sf, 0.0) + jnp.where(upper, sb, 0.0)).astype(BF16)
                vp_ = vv[:, vsl]
                zero = jnp.zeros_like(vp_)
                v2 = jnp.concatenate([jnp.where(vm[0], vp_, zero), jnp.where(vm[1], vp_, zero)], axis=0)
                o = jnp.dot(sc, v2, preferred_element_type=F32)
                st = states[p]
                o = o + lax.dot_general(q2[0:c, ks], st.astype(BF16), nt, preferred_element_type=F32)
                o = o + lax.dot_general(q2[c:2 * c, ks], sb_ref[sub * g + gi, p], nt, preferred_element_type=F32)
                upd = lax.dot_general(vp_, kf2[:, ks], tn, preferred_element_type=F32)
                states[p] = st * dec[:, ks] + jnp.where(bd, upd, 0.0)
                pair_out.append(o)
            outs.append(jnp.concatenate(pair_out, axis=1) if npair > 1 else pair_out[0])
        for p in range(npair):
            st_ref[p] = states[p]
        o = jnp.concatenate(outs, axis=0)
        o_ref[pl.ds(base, gc), :] = _head_rms_gate(o, gate_ref[pl.ds(base, gc), :], bdv_ref[...], dv)

    lax.fori_loop(0, nsub, sub_block, 0)


def _head_block_ones(wv, seg):
    vr = lax.broadcasted_iota(jnp.int32, (wv, wv), 0) // seg
    vc = lax.broadcasted_iota(jnp.int32, (wv, wv), 1) // seg
    return (vr == vc).astype(BF16)


def _sub_blocks(lp, rows):
    n = lp // rows
    for cand in (5, 4, 3, 2):
        if n % cand == 0:
            return cand
    return 1


def _seqs_per_step(batch):
    return 2 if batch % 2 == 0 else 1


def _linear_attention(q, k, v, lf, lb, gate, batch, lp, *, npair, vp, dv):
    gc = LA_GROUP * CHUNK
    nsub = _sub_blocks(lp, gc)
    rows = nsub * gc
    nblk = lp // rows
    nc = lp // CHUNK
    wk = npair * LANES
    wv = npair * vp
    q3 = q.reshape(batch, lp, wk)
    k3 = k.reshape(batch, lp, wk)
    v3 = v.reshape(batch, lp, wv)
    g3 = gate.reshape(batch, lp, wv)
    lf3 = lf.reshape(batch, lp, wk)
    lb3 = lb.reshape(batch, lp, wk)
    nseq = _seqs_per_step(batch)
    fwd = lambda b, j: (b, j, 0)
    rev = lambda b, j: (b, nblk - 1 - j, 0)
    sblk = (nseq, nsub * LA_GROUP, npair, vp, LANES)
    kblk = (nseq, rows, wk)
    vblk = (nseq, rows, wv)

    sb = pl.pallas_call(
        functools.partial(_la_bwd_kernel, npair=npair, vp=vp, nsub=nsub, nseq=nseq),
        grid=(batch // nseq, nblk),
        in_specs=[pl.BlockSpec(kblk, rev), pl.BlockSpec(vblk, rev), pl.BlockSpec(kblk, rev)],
        out_specs=pl.BlockSpec(sblk, lambda b, j: (b, nblk - 1 - j, 0, 0, 0)),
        out_shape=jax.ShapeDtypeStruct((batch, nc, npair, vp, LANES), BF16),
        scratch_shapes=[pltpu.VMEM((nseq, npair, vp, LANES), F32)],
        compiler_params=_params("parallel", "arbitrary"),
        name="la_bwd_states",
    )(k3, v3, lb3)

    o = pl.pallas_call(
        functools.partial(_la_fwd_kernel, npair=npair, vp=vp, dv=dv, nsub=nsub, nseq=nseq),
        grid=(batch // nseq, nblk),
        in_specs=[pl.BlockSpec(kblk, fwd), pl.BlockSpec(kblk, fwd), pl.BlockSpec(vblk, fwd),
                  pl.BlockSpec(kblk, fwd), pl.BlockSpec(kblk, fwd),
                  pl.BlockSpec(sblk, lambda b, j: (b, j, 0, 0, 0)),
                  pl.BlockSpec(vblk, fwd),
                  pl.BlockSpec((wv, wv), lambda b, j: (0, 0))],
        out_specs=pl.BlockSpec(vblk, fwd),
        out_shape=jax.ShapeDtypeStruct((batch, lp, wv), BF16),
        scratch_shapes=[pltpu.VMEM((nseq, npair, vp, LANES), F32)],
        compiler_params=_params("parallel", "arbitrary"),
        name="la_fwd",
    )(q3, k3, v3, lf3, lb3, sb, g3, _head_block_ones(wv, vp // 2))
    return o.reshape(batch * lp, wv)


RET_CHUNK = 256


def _ret_decay_terms(lg_ref):
    c = RET_CHUNK
    lg = lg_ref[...]
    i = lax.broadcasted_iota(jnp.int32, (c, 1), 0).astype(F32)
    return dict(q_f=jnp.exp((i + 1.0) * lg), q_b=jnp.exp((float(c) - i) * lg),
                k_f=jnp.exp((float(c - 1) - i) * lg), k_b=jnp.exp(i * lg), dec=jnp.exp(float(c) * lg))


def _ret_bwd_kernel(k_all, v_all, lg_ref, sb_all, st_all, *, npair, nsub, nseq):
    c = RET_CHUNK
    tn = (((0,), (0,)), ((), ()))

    @pl.when(pl.program_id(1) == 0)
    def _():
        st_all[...] = jnp.zeros_like(st_all)

    _, _, bd = _head_masks(npair, LANES)
    t = _ret_decay_terms(lg_ref)

    def sub_block(it, carry):
        for s in range(nseq):
            one_seq(k_all.at[s], v_all.at[s], sb_all.at[s], st_all.at[s], nsub - 1 - it)
        return carry

    def one_seq(k_ref, v_ref, sb_ref, st_ref, sub):
        base = pl.multiple_of(sub * c, c)
        kb = (k_ref[pl.ds(base, c), :].astype(F32) * t["k_b"]).astype(BF16)
        vv = v_ref[pl.ds(base, c), :]
        for p in range(npair):
            ks = slice(p * LANES, (p + 1) * LANES)
            st = st_ref[p]
            sb_ref[sub, p] = st.astype(BF16)
            upd = lax.dot_general(vv[:, ks], kb[:, ks], tn, preferred_element_type=F32)
            st_ref[p] = st * t["dec"][:, ks] + jnp.where(bd, upd, 0.0)

    lax.fori_loop(0, nsub, sub_block, 0)


def _ret_fwd_kernel(q_all, k_all, v_all, lg_ref, dmask_ref, sb_all, gate_all, bdv_ref, o_all, st_all,
                    *, npair, nsub, dv, nseq):
    c = RET_CHUNK
    nt = (((1,), (1,)), ((), ()))
    tn = (((0,), (0,)), ((), ()))

    @pl.when(pl.program_id(1) == 0)
    def _():
        st_all[...] = jnp.zeros_like(st_all)

    km, vm, bd = _head_masks(npair, LANES)
    t = _ret_decay_terms(lg_ref)

    def sub_block(sub, carry):
        for s in range(nseq):
            one_seq(q_all.at[s], k_all.at[s], v_all.at[s], sb_all.at[s], gate_all.at[s], o_all.at[s], st_all.at[s], sub)
        return carry

    def one_seq(q_ref, k_ref, v_ref, sb_ref, gate_ref, o_ref, st_ref, sub):
        base = pl.multiple_of(sub * c, c)
        qb16 = q_ref[pl.ds(base, c), :]
        kb16 = k_ref[pl.ds(base, c), :]
        vv = v_ref[pl.ds(base, c), :]
        qq = qb16.astype(F32)
        kk = kb16.astype(F32)
        q2 = jnp.concatenate([qq * t["q_f"], qq * t["q_b"]], axis=0).astype(BF16)
        kf2 = (kk * t["k_f"]).astype(BF16)
        zero = jnp.zeros((c, LANES), BF16)
        pair_out = []
        for p in range(npair):
            ks = slice(p * LANES, (p + 1) * LANES)
            kp = kb16[:, ks]
            vp_ = vv[:, ks]
            k2 = jnp.concatenate([jnp.where(km[0], kp, zero), jnp.where(km[1], kp, zero)], axis=0)
            v2 = jnp.concatenate([jnp.where(vm[0], vp_, zero), jnp.where(vm[1], vp_, zero)], axis=0)
            s = lax.dot_general(qb16[:, ks], k2, nt, preferred_element_type=F32)
            o = jnp.dot((s * dmask_ref[p]).astype(BF16), v2, preferred_element_type=F32)
            st = st_ref[p]
            o = o + lax.dot_general(q2[0:c, ks], st.astype(BF16), nt, preferred_element_type=F32)
            o = o + lax.dot_general(q2[c:2 * c, ks], sb_ref[sub, p], nt, preferred_element_type=F32)
            upd = lax.dot_general(vp_, kf2[:, ks], tn, preferred_element_type=F32)
            st_ref[p] = st * t["dec"][:, ks] + jnp.where(bd, upd, 0.0)
            pair_out.append(o)
        o = jnp.concatenate(pair_out, axis=1)
        o_ref[pl.ds(base, c), :] = _head_rms_gate(o, gate_ref[pl.ds(base, c), :], bdv_ref[...], dv)

    lax.fori_loop(0, nsub, sub_block, 0)


def _retention(q, k, v, gate, batch, lp):
    c = RET_CHUNK
    npair = RET_HEADS // 2
    w = npair * LANES
    nsub = _sub_blocks(lp, c)
    rows = nsub * c
    nblk = lp // rows
    log_gamma = jnp.log1p(-jnp.exp2(-5.0 - jnp.arange(RET_HEADS, dtype=F32)))
    lg = jnp.repeat(log_gamma, RET_DIM)[None]
    dist = jnp.abs(jnp.arange(c)[:, None] - (jnp.arange(2 * c) % c)[None, :]).astype(F32)
    dmask = jnp.exp(dist[None] * jnp.repeat(log_gamma, c).reshape(npair, 1, 2 * c))
    q3, k3, v3, g3 = (a.reshape(batch, lp, w) for a in (q, k, v, gate))
    nseq = _seqs_per_step(batch)
    fwd = lambda b, j: (b, j, 0)
    rev = lambda b, j: (b, nblk - 1 - j, 0)
    sblk = (nseq, nsub, npair, LANES, LANES)
    xblk = (nseq, rows, w)
    lgspec = pl.BlockSpec((1, w), lambda b, j: (0, 0))

    sb = pl.pallas_call(
        functools.partial(_ret_bwd_kernel, npair=npair, nsub=nsub, nseq=nseq),
        grid=(batch // nseq, nblk),
        in_specs=[pl.BlockSpec(xblk, rev), pl.BlockSpec(xblk, rev), lgspec],
        out_specs=pl.BlockSpec(sblk, lambda b, j: (b, nblk - 1 - j, 0, 0, 0)),
        out_shape=jax.ShapeDtypeStruct((batch, lp // c, npair, LANES, LANES), BF16),
        scratch_shapes=[pltpu.VMEM((nseq, npair, LANES, LANES), F32)],
        compiler_params=_params("parallel", "arbitrary"),
        name="ret_bwd_states",
    )(k3, v3, lg)

    o = pl.pallas_call(
        functools.partial(_ret_fwd_kernel, npair=npair, nsub=nsub, dv=RET_DIM, nseq=nseq),
        grid=(batch // nseq, nblk),
        in_specs=[pl.BlockSpec(xblk, fwd), pl.BlockSpec(xblk, fwd), pl.BlockSpec(xblk, fwd), lgspec,
                  pl.BlockSpec((npair, c, 2 * c), lambda b, j: (0, 0, 0)),
                  pl.BlockSpec(sblk, lambda b, j: (b, j, 0, 0, 0)),
                  pl.BlockSpec(xblk, fwd),
                  pl.BlockSpec((w, w), lambda b, j: (0, 0))],
        out_specs=pl.BlockSpec(xblk, fwd),
        out_shape=jax.ShapeDtypeStruct((batch, lp, w), BF16),
        scratch_shapes=[pltpu.VMEM((nseq, npair, LANES, LANES), F32)],
        compiler_params=_params("parallel", "arbitrary"),
        name="ret_fwd",
    )(q3, k3, v3, lg, dmask, sb, g3, _head_block_ones(w, RET_DIM))
    return o.reshape(batch * lp, w)


PACKED = D_MODEL // 2


def _pack_rows(x):
    bits = lax.bitcast_convert_type(x.astype(BF16).astype(F32), jnp.int32)
    return bits[:, :PACKED] | lax.shift_right_logical(bits[:, PACKED:], 16)


def _unpack_rows(p):
    hi = lax.bitcast_convert_type(p & jnp.int32(-65536), F32)
    lo = lax.bitcast_convert_type(lax.shift_left(p, jnp.int32(16)), F32)
    return jnp.concatenate([hi, lo], axis=1)


_RT_E1, _RT_E2, _RT_W1, _RT_W2, _RT_R1, _RT_R2 = range(6)


def _mix_out(ya_ref, yb_ref, yc_ref, h_ref, wglu_ref, wout_ref, g_ref):
    ya = _gelu_tanh(ya_ref[...].astype(F32))
    gl = jnp.dot(ya.astype(BF16), wglu_ref[...], preferred_element_type=F32)
    ya = (ya * _sigmoid(gl)).astype(BF16)
    y = jnp.concatenate([ya, yb_ref[...], yc_ref[...]], axis=1)
    h = h_ref[...] + jnp.dot(y, wout_ref[...], preferred_element_type=F32)
    ms = jnp.mean(h * h, axis=-1, keepdims=True)
    return h, h * lax.rsqrt(ms + NORM_EPS) * g_ref[...]


def _outproj_kernel(ya_ref, yb_ref, yc_ref, h_ref, wglu_ref, wout_ref, g_ref, hout_ref, hn_ref):
    h, hn = _mix_out(ya_ref, yb_ref, yc_ref, h_ref, wglu_ref, wout_ref, g_ref)
    hout_ref[...] = h
    hn_ref[...] = hn.astype(BF16)


def _outproj_router_kernel(ya_ref, yb_ref, yc_ref, h_ref, wglu_ref, wout_ref, g_ref, wrh_ref, wrl_ref, before_ref,
                           hout_ref, hn_ref, route_ref, cnt_ref, carry_ref):
    @pl.when(pl.program_id(0) == 0)
    def _():
        carry_ref[...] = jnp.zeros_like(carry_ref)

    h, hn = _mix_out(ya_ref, yb_ref, yc_ref, h_ref, wglu_ref, wout_ref, g_ref)
    hout_ref[...] = h
    hn_ref[...] = _pack_rows(hn)
    hi = hn.astype(BF16)
    lo = (hn - hi.astype(F32)).astype(BF16)
    wrh = wrh_ref[...]
    logits = (jnp.dot(hi, wrh, preferred_element_type=F32) + jnp.dot(lo, wrh, preferred_element_type=F32)
              + jnp.dot(hi, wrl_ref[...], preferred_element_type=F32))
    lane = lax.broadcasted_iota(jnp.int32, logits.shape, 1)
    neg = jnp.float32(-jnp.inf)
    logits = jnp.where(lane < N_EXPERTS, logits, neg)
    v1 = jnp.max(logits, axis=-1, keepdims=True)
    i1 = jnp.min(jnp.where(logits == v1, lane, LANES), axis=-1, keepdims=True)
    m1 = lane == i1
    l2 = jnp.where(m1, neg, logits)
    v2 = jnp.max(l2, axis=-1, keepdims=True)
    i2 = jnp.min(jnp.where(l2 == v2, lane, LANES), axis=-1, keepdims=True)
    m2 = lane == i2
    e = jnp.exp(v2 - v1)
    w1 = 1.0 / (1.0 + e)
    w2 = e * w1
    chosen = jnp.where(m1 | m2, 1.0, 0.0)
    prefix = jnp.dot(before_ref[...], chosen.astype(BF16), preferred_element_type=F32) + carry_ref[...]
    r1 = jnp.sum(jnp.where(m1, prefix, 0.0), axis=-1, keepdims=True)
    r2 = jnp.sum(jnp.where(m2, prefix, 0.0), axis=-1, keepdims=True)
    total = carry_ref[...] + jnp.sum(chosen, axis=0, keepdims=True)
    carry_ref[...] = total
    cnt_ref[...] = total
    rec = jnp.zeros_like(logits)
    for ln, val in ((_RT_E1, i1.astype(F32)), (_RT_E2, i2.astype(F32)), (_RT_W1, w1), (_RT_W2, w2),
                    (_RT_R1, r1), (_RT_R2, r2)):
        rec = jnp.where(lane == ln, val, rec)
    route_ref[...] = rec


def _outproj(ya, yb, yc, h, wglu, wout, gamma, wr=None):
    r = h.shape[0]
    tm = ROW_TILE
    with_router = wr is not None

    def rows(wd):
        return pl.BlockSpec((tm, wd), lambda i: (i, 0))

    in_specs = [rows(256), rows(384), rows(512), rows(D_MODEL), _const_spec((256, 256)),
                _const_spec((1152, D_MODEL)), _const_spec((1, D_MODEL))]
    args = [ya, yb, yc, h, wglu, wout, gamma]
    if not with_router:
        return pl.pallas_call(
            _outproj_kernel,
            grid=(r // tm,),
            in_specs=in_specs, out_specs=[rows(D_MODEL), rows(D_MODEL)],
            out_shape=[jax.ShapeDtypeStruct((r, D_MODEL), F32), jax.ShapeDtypeStruct((r, D_MODEL), BF16)],
            input_output_aliases={3: 0},
            compiler_params=_params("parallel"),
            name="outproj",
        )(*args)
    earlier = (lax.broadcasted_iota(jnp.int32, (tm, tm), 1) < lax.broadcasted_iota(jnp.int32, (tm, tm), 0)).astype(BF16)
    return pl.pallas_call(
        _outproj_router_kernel,
        grid=(r // tm,),
        in_specs=in_specs + [_const_spec((D_MODEL, LANES)), _const_spec((D_MODEL, LANES)), _const_spec((tm, tm))],
        out_specs=[rows(D_MODEL), rows(PACKED), rows(LANES), _const_spec((1, LANES))],
        out_shape=[jax.ShapeDtypeStruct((r, D_MODEL), F32), jax.ShapeDtypeStruct((r, PACKED), jnp.int32),
                   jax.ShapeDtypeStruct((r, LANES), F32), jax.ShapeDtypeStruct((1, LANES), F32)],
        scratch_shapes=[pltpu.VMEM((1, LANES), F32)],
        input_output_aliases={3: 0},
        compiler_params=_params("arbitrary"),
        name="outproj_router",
    )(*args, wr[0], wr[1], earlier)


FF_TILE = 2816


def _ffn_kernel(hn_ref, h_ref, wg_ref, wu_ref, wd_ref, o_ref, acc_ref):
    j = pl.program_id(1)
    hn = hn_ref[...]
    a = _silu(jnp.dot(hn, wg_ref[...], preferred_element_type=F32)) * jnp.dot(hn, wu_ref[...], preferred_element_type=F32)
    y = jnp.dot(a.astype(BF16), wd_ref[...], preferred_element_type=F32)

    @pl.when(j == 0)
    def _():
        acc_ref[...] = h_ref[...] + y

    @pl.when(j != 0)
    def _():
        acc_ref[...] += y

    @pl.when(j == pl.num_programs(1) - 1)
    def _():
        o_ref[...] = acc_ref[...]


def _ffn(hn, h, wg, wu, wd):
    r = h.shape[0]
    tm = ROW_TILE
    nf = D_FF // FF_TILE
    return pl.pallas_call(
        _ffn_kernel,
        grid=(r // tm, nf),
        in_specs=[pl.BlockSpec((tm, D_MODEL), lambda i, j: (i, 0)),
                  pl.BlockSpec((tm, D_MODEL), lambda i, j: (i, 0)),
                  pl.BlockSpec((D_MODEL, FF_TILE), lambda i, j: (0, j)),
                  pl.BlockSpec((D_MODEL, FF_TILE), lambda i, j: (0, j)),
                  pl.BlockSpec((FF_TILE, D_MODEL), lambda i, j: (j, 0))],
        out_specs=pl.BlockSpec((tm, D_MODEL), lambda i, j: (i, 0)),
        out_shape=jax.ShapeDtypeStruct((r, D_MODEL), F32),
        scratch_shapes=[pltpu.VMEM((tm, D_MODEL), F32)],
        input_output_aliases={1: 0},
        compiler_params=_params("parallel", "arbitrary"),
        name="ffn",
    )(hn, h, wg, wu, wd)


MOE_BLOCK = 512
GATHER_TILE = 256


def _route_meta(route, cnt, r):
    bm = MOE_BLOCK
    nb = 2 * r // bm + N_EXPERTS
    e1 = route[:, _RT_E1].astype(jnp.int32)
    e2 = route[:, _RT_E2].astype(jnp.int32)
    counts = cnt[0, :N_EXPERTS].astype(jnp.int32)
    padded = ((counts + bm - 1) // bm) * bm
    ends = jnp.cumsum(padded)
    starts = ends - padded
    pos1 = starts[e1] + route[:, _RT_R1].astype(jnp.int32)
    pos2 = starts[e2] + route[:, _RT_R2].astype(jnp.int32)
    n_used = (ends[-1] // bm).astype(jnp.int32)
    blk = jnp.arange(nb, dtype=jnp.int32)
    blk = jnp.minimum(blk, n_used - 1)
    block_expert = jnp.sum((blk[:, None] * bm >= ends[None, :]).astype(jnp.int32), axis=1)
    block_expert = jnp.minimum(block_expert, N_EXPERTS - 1).astype(jnp.int32)
    return pos1, pos2, block_expert, n_used.reshape(1), nb


def _dispatch_kernel(p1_ref, p2_ref, hn_ref, xs_in_ref, xs_ref, stage, sems):
    del xs_in_ref
    n = hn_ref.shape[0]
    s = pl.program_id(0)
    last = pl.num_programs(0) - 1
    slot = lax.rem(s, 2)

    def drain(sl):
        def body(r, c):
            row = pltpu.make_async_copy(stage.at[sl, pl.ds(0, 1), :], xs_ref.at[pl.ds(0, 1), :], sems.at[sl])
            row.wait()
            row.wait()
            return c
        lax.fori_loop(0, n, body, 0, unroll=8)

    @pl.when(s >= 2)
    def _():
        drain(slot)

    stage[slot] = hn_ref[...]
    for r in range(n):
        src = stage.at[slot, pl.ds(r, 1), :]
        pltpu.make_async_copy(src, xs_ref.at[pl.ds(p1_ref[0, 0, r], 1), :], sems.at[slot]).start()
        pltpu.make_async_copy(src, xs_ref.at[pl.ds(p2_ref[0, 0, r], 1), :], sems.at[slot]).start()

    @pl.when(s == last)
    def _():
        @pl.when(s >= 1)
        def _():
            drain(1 - slot)
        drain(slot)


def _dispatch(hn, pos1, pos2, nb):
    r = hn.shape[0]
    t = GATHER_TILE
    p = nb * MOE_BLOCK
    idx = pl.BlockSpec((1, 1, t), lambda i: (i, 0, 0), memory_space=pltpu.SMEM)
    return pl.pallas_call(
        _dispatch_kernel,
        grid=(r // t,),
        in_specs=[idx, idx, pl.BlockSpec((t, PACKED), lambda i: (i, 0)), pl.BlockSpec(memory_space=pl.ANY)],
        out_specs=pl.BlockSpec(memory_space=pl.ANY),
        out_shape=jax.ShapeDtypeStruct((p, PACKED), jnp.int32),
        scratch_shapes=[pltpu.VMEM((2, t, PACKED), jnp.int32), pltpu.SemaphoreType.DMA((2,))],
        input_output_aliases={3: 0},
        compiler_params=_params("arbitrary"),
        name="moe_dispatch",
    )(pos1.reshape(r // t, 1, t), pos2.reshape(r // t, 1, t), hn, jnp.zeros((p, PACKED), jnp.int32))


def _moe_ffn_kernel(be_ref, nu_ref, x_ref, wg_ref, wu_ref, wd_ref, o_ref, acc_ref):
    del be_ref
    b = pl.program_id(0)
    j = pl.program_id(1)
    last = j == pl.num_programs(1) - 1
    used = b < nu_ref[0]

    @pl.when(used)
    def _():
        x = _unpack_rows(x_ref[...]).astype(BF16)
        a = _silu(jnp.dot(x, wg_ref[...], preferred_element_type=F32)) * jnp.dot(x, wu_ref[...], preferred_element_type=F32)
        y = jnp.dot(a.astype(BF16), wd_ref[...], preferred_element_type=F32)

        @pl.when(j == 0)
        def _():
            acc_ref[...] = y

        @pl.when(j != 0)
        def _():
            acc_ref[...] += y

        @pl.when(last)
        def _():
            o_ref[...] = _pack_rows(acc_ref[...])

    @pl.when(jnp.logical_not(used) & last)
    def _():
        o_ref[...] = jnp.zeros_like(o_ref)


def _moe_ffn(xs, block_expert, n_used, wg, wu, wd):
    p = xs.shape[0]
    bm = MOE_BLOCK
    nf = D_FF // FF_TILE
    grid_spec = pltpu.PrefetchScalarGridSpec(
        num_scalar_prefetch=2,
        grid=(p // bm, nf),
        in_specs=[pl.BlockSpec((bm, PACKED), lambda b, j, be, nu: (b, 0)),
                  pl.BlockSpec((None, D_MODEL, FF_TILE), lambda b, j, be, nu: (be[b], 0, j)),
                  pl.BlockSpec((None, D_MODEL, FF_TILE), lambda b, j, be, nu: (be[b], 0, j)),
                  pl.BlockSpec((None, FF_TILE, D_MODEL), lambda b, j, be, nu: (be[b], j, 0))],
        out_specs=pl.BlockSpec((bm, PACKED), lambda b, j, be, nu: (b, 0)),
        scratch_shapes=[pltpu.VMEM((bm, D_MODEL), F32)])
    return pl.pallas_call(
        _moe_ffn_kernel,
        grid_spec=grid_spec,
        out_shape=jax.ShapeDtypeStruct((p, PACKED), jnp.int32),
        compiler_params=_params("arbitrary", "arbitrary"),
        name="moe_ffn",
    )(block_expert, n_used, xs, wg, wu, wd)


def _combine_kernel(p1_ref, p2_ref, p1n_ref, p2n_ref, route_ref, h_ref, g_ref, ys_ref, o_ref, buf1, buf2, sems,
                    *, final):
    n = h_ref.shape[0]
    s = pl.program_id(0)
    last = pl.num_programs(0) - 1
    slot = lax.rem(s, 2)

    def gather(pa, pb, sl):
        for r in range(n):
            pltpu.make_async_copy(ys_ref.at[pl.ds(pa[0, 0, r], 1), :], buf1.at[sl, pl.ds(r, 1), :], sems.at[sl]).start()
            pltpu.make_async_copy(ys_ref.at[pl.ds(pb[0, 0, r], 1), :], buf2.at[sl, pl.ds(r, 1), :], sems.at[sl]).start()

    @pl.when(s == 0)
    def _():
        gather(p1_ref, p2_ref, 0)

    @pl.when(s < last)
    def _():
        gather(p1n_ref, p2n_ref, 1 - slot)

    def drain(r, c):
        row = pltpu.make_async_copy(ys_ref.at[pl.ds(0, 1), :], buf1.at[slot, pl.ds(0, 1), :], sems.at[slot])
        row.wait()
        row.wait()
        return c

    lax.fori_loop(0, n, drain, 0, unroll=8)
    rt = route_ref[...]
    lane = lax.broadcasted_iota(jnp.int32, rt.shape, 1)
    w1 = jnp.sum(jnp.where(lane == _RT_W1, rt, 0.0), axis=-1, keepdims=True)
    w2 = jnp.sum(jnp.where(lane == _RT_W2, rt, 0.0), axis=-1, keepdims=True)
    h = h_ref[...] + w1 * _unpack_rows(buf1[slot]) + w2 * _unpack_rows(buf2[slot])
    if final:
        ms = jnp.mean(h * h, axis=-1, keepdims=True)
        h = h * lax.rsqrt(ms + NORM_EPS) * g_ref[...]
    o_ref[...] = h


def _combine(ys, pos1, pos2, route, h, gamma, batch, lp, final):
    r = h.shape[0]
    t = GATHER_TILE
    p1 = pos1.reshape(r // t, 1, t)
    p2 = pos2.reshape(r // t, 1, t)
    scratch = [pltpu.VMEM((2, t, PACKED), jnp.int32), pltpu.VMEM((2, t, PACKED), jnp.int32),
               pltpu.SemaphoreType.DMA((2,))]
    if final:
        per_seq = lp // t
        skip = FRONT // t
        live = per_seq - skip
        steps = batch * live
        rb = lambda s: (s // live) * per_seq + skip + s % live
        out_specs = pl.BlockSpec((None, t, D_MODEL), lambda s: (s // live, s % live, 0))
        out_shape = jax.ShapeDtypeStruct((batch, lp - FRONT, D_MODEL), F32)
    else:
        steps = r // t
        rb = lambda s: s
        out_specs = pl.BlockSpec((t, D_MODEL), lambda s: (s, 0))
        out_shape = jax.ShapeDtypeStruct((r, D_MODEL), F32)
    nxt = lambda s: rb(jnp.minimum(s + 1, steps - 1))
    idx = pl.BlockSpec((1, 1, t), lambda s: (rb(s), 0, 0), memory_space=pltpu.SMEM)
    idx_next = pl.BlockSpec((1, 1, t), lambda s: (nxt(s), 0, 0), memory_space=pltpu.SMEM)
    in_specs = [idx, idx, idx_next, idx_next,
                pl.BlockSpec((t, LANES), lambda s: (rb(s), 0)), pl.BlockSpec((t, D_MODEL), lambda s: (rb(s), 0)),
                pl.BlockSpec((1, D_MODEL), lambda s: (0, 0)), pl.BlockSpec(memory_space=pl.ANY)]
    return pl.pallas_call(
        functools.partial(_combine_kernel, final=final),
        grid=(steps,), in_specs=in_specs, out_specs=out_specs, out_shape=out_shape, scratch_shapes=scratch,
        compiler_params=_params("arbitrary"),
        name="moe_combine_final" if final else "moe_combine",
    )(p1, p2, p1, p2, route, h, gamma, ys)


def _final_norm_kernel(h_ref, g_ref, o_ref):
    h = h_ref[...]
    ms = jnp.mean(h * h, axis=-1, keepdims=True)
    o_ref[...] = h * lax.rsqrt(ms + NORM_EPS) * g_ref[...]


def _final_norm(h, gamma, batch, lp):
    t = GATHER_TILE
    per_seq = lp // t
    skip = FRONT // t
    return pl.pallas_call(
        _final_norm_kernel,
        grid=(batch, per_seq - skip),
        in_specs=[pl.BlockSpec((t, D_MODEL), lambda b, i: (b * per_seq + skip + i, 0)),
                  pl.BlockSpec((1, D_MODEL), lambda b, i: (0, 0))],
        out_specs=pl.BlockSpec((None, t, D_MODEL), lambda b, i: (b, i, 0)),
        out_shape=jax.ShapeDtypeStruct((batch, lp - FRONT, D_MODEL), F32),
        compiler_params=_params("parallel", "parallel"),
        name="final_norm",
    )(h, gamma)


def _rope_tables(lp):
    half = RET_DIM // 2
    pos = jnp.arange(lp, dtype=F32) - float(PAD)
    inv = ROPE_BASE ** (-jnp.arange(half, dtype=F32) / half)
    ang = pos[:, None] * inv[None, :]
    cos = jnp.tile(jnp.cos(ang), (1, 2 * RET_HEADS))
    sin = jnp.tile(jnp.sin(ang), (1, 2 * RET_HEADS))
    return cos, sin


def _prep_layer(li, p):
    w_out = p["w_out"][li].astype(F32)
    wc = w_out[640:1024].reshape(GLA_HEADS, GLA_DV, D_MODEL)
    wc = jnp.pad(wc, ((0, 0), (0, GLA_DV_PAD - GLA_DV), (0, 0))).reshape(GLA_HEADS * GLA_DV_PAD, D_MODEL)
    wgate, bgate = _pack_gate(p["gla_w_gate_f"][li], p["gla_b_gate_f"][li], p["gla_w_gate_b"][li], p["gla_b_gate_b"][li])
    return dict(
        norm_mix=p["norm_mix"][li].astype(F32)[None],
        w_in=_pack_w_in(p["w_in"][li].astype(F32)),
        wgate=wgate, bgate=bgate,
        s5=_s5_tables(p["s5_lambda_re"][li], p["s5_lambda_im"][li], p["s5_log_dt"][li], p["s5_b_re"][li],
                      p["s5_b_im"][li], p["s5_c_re"][li], p["s5_c_im"][li], p["s5_d"][li]),
        w_glu=p["s5_w_glu"][li].astype(BF16),
        w_out=jnp.concatenate([w_out[0:640], wc], axis=0).astype(BF16),
        norm_ffn=p["norm_ffn"][li].astype(F32)[None],
    )


def _trunk(x, meta_tokens, layers, ffn, moe, norm_final, depth):
    batch, seq, _ = x.shape
    lp = seq + FRONT
    r = batch * lp
    meta = jnp.broadcast_to(meta_tokens.astype(F32)[None], (batch, N_META, D_MODEL))
    h = jnp.concatenate([jnp.zeros((batch, PAD, D_MODEL), F32), meta, x.astype(F32)], axis=1).reshape(r, D_MODEL)
    cos, sin = _rope_tables(lp)
    cos = jnp.tile(cos, (batch, 1))
    sin = jnp.tile(sin, (batch, 1))
    gamma_final = norm_final.astype(F32)[None]
    for li in range(depth):
        lw = layers[li]
        u, rq, rk, rv, rg, gq, gk, gv, gg, lf, lb = _inproj(
            h, lw["norm_mix"], lw["w_in"], cos, sin, lw["wgate"], lw["bgate"], batch, lp)
        ya = _s5_mixer(u, lw["s5"], batch, lp)
        yb = _retention(rq, rk, rv, rg, batch, lp)
        yc = _linear_attention(gq, gk, gv, lf, lb, gg, batch, lp, npair=GLA_HEADS // 2, vp=2 * GLA_DV_PAD, dv=GLA_DV)
        j = li // 2
        last = li == depth - 1
        if li % 2 == 0:
            h, hn = _outproj(ya, yb, yc, h, lw["w_glu"], lw["w_out"], lw["norm_ffn"])
            h = _ffn(hn, h, ffn["wg"][j], ffn["wu"][j], ffn["wd"][j])
        else:
            h, hn, route, cnt = _outproj(ya, yb, yc, h, lw["w_glu"], lw["w_out"], lw["norm_ffn"], moe["wr"][j])
            pos1, pos2, block_expert, n_used, nb = _route_meta(route, cnt, r)
            xs = _dispatch(hn, pos1, pos2, nb)
            ys = _moe_ffn(xs, block_expert, n_used, moe["wg"][j], moe["wu"][j], moe["wd"][j])
            h = _combine(ys, pos1, pos2, route, h, gamma_final, batch, lp, final=last)
            if last:
                return h
    return _final_norm(h, gamma_final, batch, lp)


def kernel(x_prompt, x_sample, meta_tokens, norm_mix, w_in, s5_lambda_re, s5_lambda_im, s5_log_dt, s5_b_re, s5_b_im, s5_c_re, s5_c_im, s5_d, s5_w_glu, gla_w_gate_f, gla_b_gate_f, gla_w_gate_b, gla_b_gate_b, w_out, norm_ffn, ffn_w_gate, ffn_w_up, ffn_w_down, router_w, moe_w_gate, moe_w_up, moe_w_down, norm_final):
    depth = w_in.shape[0]
    p = dict(norm_mix=norm_mix, w_in=w_in, s5_lambda_re=s5_lambda_re, s5_lambda_im=s5_lambda_im,
             s5_log_dt=s5_log_dt, s5_b_re=s5_b_re, s5_b_im=s5_b_im, s5_c_re=s5_c_re, s5_c_im=s5_c_im, s5_d=s5_d,
             s5_w_glu=s5_w_glu, gla_w_gate_f=gla_w_gate_f, gla_b_gate_f=gla_b_gate_f, gla_w_gate_b=gla_w_gate_b,
             gla_b_gate_b=gla_b_gate_b, w_out=w_out, norm_ffn=norm_ffn)
    layers = [_prep_layer(li, p) for li in range(depth)]
    ffn = dict(wg=ffn_w_gate.astype(BF16), wu=ffn_w_up.astype(BF16), wd=ffn_w_down.astype(BF16))
    wr = jnp.pad(router_w.astype(F32), ((0, 0), (0, 0), (0, LANES - N_EXPERTS)))
    wr_hi = wr.astype(BF16)
    wr_lo = (wr - wr_hi.astype(F32)).astype(BF16)
    wr = [(wr_hi[j], wr_lo[j]) for j in range(wr.shape[0])]
    moe = dict(wr=wr, wg=moe_w_gate.astype(BF16), wu=moe_w_up.astype(BF16), wd=moe_w_down.astype(BF16))
    y_prompt = _trunk(x_prompt, meta_tokens, layers, ffn, moe, norm_final, depth)
    y_sample = _trunk(x_sample, meta_tokens, layers, ffn, moe, norm_final, depth)
    return (y_prompt, y_sample)
```

```python
import functools
import math

import jax
import jax.numpy as jnp
from jax import lax
from jax.experimental import pallas as pl
from jax.experimental.pallas import tpu as pltpu

F32 = jnp.float32
BF16 = jnp.bfloat16

D_MODEL = 1024
N_META = 16
S5_P = 16
S5_WIDTH = 256
S5_GROUPS = 16
S5_N = 64
RET_HEADS = 6
RET_DIM = 64
RET_WIDTH = 384
GLA_HEADS = 4
GLA_DV = 96
GLA_DK = 48
GLA_QK = 192
GLA_WIDTH = 384
GLA_GATE_RANK = 16
GLA_TAU = 16.0
ROPE_BASE = 10000.0
D_FF = 2816
N_EXPERTS = 8
NORM_EPS = 1e-5

CHUNK = 64
FRONT = 256
PAD = FRONT - N_META
ROW_TILE = 512
LA_GROUP = 4
LANES = 128
GLA_DK_PAD = 64
GLA_DV_PAD = 128
VMEM_LIMIT = 56 * 1024 * 1024


def _params(*sem):
    return pltpu.CompilerParams(dimension_semantics=sem, vmem_limit_bytes=VMEM_LIMIT)


def _const_spec(shape):
    nd = len(shape)
    return pl.BlockSpec(shape, lambda *_: (0,) * nd)


def _sigmoid(x):
    return 1.0 / (1.0 + jnp.exp(-x))


def _silu(x):
    return x * _sigmoid(x)


def _gelu_tanh(x):
    c = math.sqrt(2.0 / math.pi)
    return 0.5 * x * (1.0 + jnp.tanh(c * (x + 0.044715 * (x * x * x))))


def _log_sigmoid(z):
    return jnp.minimum(z, 0.0) - jnp.log(1.0 + jnp.exp(-jnp.abs(z)))


def _split_dot(a, b_bf16, dims=None):
    hi = a.astype(BF16)
    lo = (a - hi.astype(F32)).astype(BF16)
    if dims is None:
        return (jnp.dot(hi, b_bf16, preferred_element_type=F32)
                + jnp.dot(lo, b_bf16, preferred_element_type=F32))
    return (lax.dot_general(hi, b_bf16, dims, preferred_element_type=F32)
            + lax.dot_general(lo, b_bf16, dims, preferred_element_type=F32))


_C_U = 0
_C_RQ = 256
_C_RK = 640
_C_RV = 1024
_C_RG = 1408
_C_GQ = 1792
_C_GK = 2048
_C_GV = 2304
_C_GG = 2816
_C_GL = 3328
_C_END = 3456


def _pack_w_in(w):
    o = 0
    u = w[:, o:o + 256]; o += 256
    rq = w[:, o:o + 384]; o += 384
    rk = w[:, o:o + 384]; o += 384
    rv = w[:, o:o + 384]; o += 384
    rg = w[:, o:o + 384]; o += 384
    gq = w[:, o:o + 192]; o += 192
    gk = w[:, o:o + 192]; o += 192
    gv = w[:, o:o + 384]; o += 384
    gg = w[:, o:o + 384]; o += 384
    glf = w[:, o:o + 16]; o += 16
    glb = w[:, o:o + 16]; o += 16

    def padh(m, d, dp):
        m = m.reshape(D_MODEL, GLA_HEADS, d)
        return jnp.pad(m, ((0, 0), (0, 0), (0, dp - d))).reshape(D_MODEL, GLA_HEADS * dp)

    gl = jnp.pad(jnp.concatenate([glf, glb], axis=1), ((0, 0), (0, LANES - 2 * GLA_GATE_RANK)))
    cat = jnp.concatenate([
        u, rq, rk, rv, rg,
        padh(gq, GLA_DK, GLA_DK_PAD), padh(gk, GLA_DK, GLA_DK_PAD),
        padh(gv, GLA_DV, GLA_DV_PAD), padh(gg, GLA_DV, GLA_DV_PAD), gl], axis=1)
    return cat.astype(BF16)


def _pack_gate(w_f, b_f, w_b, b_b):
    def padh(m):
        m = m.reshape(m.shape[0], GLA_HEADS, GLA_DK)
        return jnp.pad(m, ((0, 0), (0, 0), (0, GLA_DK_PAD - GLA_DK))).reshape(m.shape[0], GLA_HEADS * GLA_DK_PAD)
    r = GLA_GATE_RANK
    w = jnp.zeros((LANES, 2 * GLA_HEADS * GLA_DK_PAD), F32)
    w = w.at[0:r, 0:256].set(padh(w_f.astype(F32)))
    w = w.at[r:2 * r, 256:512].set(padh(w_b.astype(F32)))
    b = jnp.concatenate([padh(b_f.astype(F32)[None]), padh(b_b.astype(F32)[None])], axis=1)
    return w.astype(BF16), b


def _inproj_kernel(h_ref, g_ref, w_ref, cos_ref, sin_ref, wgate_ref, bgate_ref,
                   u_ref, rq_ref, rk_ref, rv_ref, rg_ref, gq_ref, gk_ref, gv_ref, gg_ref, lf_ref, lb_ref,
                   *, batch, lp):
    tm = h_ref.shape[0]
    x = h_ref[...]
    ms = jnp.mean(x * x, axis=-1, keepdims=True)
    row = pl.program_id(0) * tm + lax.broadcasted_iota(jnp.int32, (tm, 1), 0)
    valid = jnp.ones((tm, 1), F32)
    for b in range(batch):
        valid = jnp.where((row >= b * lp) & (row < b * lp + PAD), 0.0, valid)
    hn = (x * (lax.rsqrt(ms + NORM_EPS) * valid) * g_ref[...]).astype(BF16)

    full = jnp.dot(hn, w_ref[...], preferred_element_type=F32)

    def proj(lo, hi):
        return full[:, lo:hi]

    u_ref[...] = proj(_C_U, _C_RQ)
    cos = cos_ref[...]
    sin = sin_ref[...]
    half = RET_DIM // 2
    first_half = (lax.broadcasted_iota(jnp.int32, (1, RET_WIDTH), 1) & (RET_DIM - 1)) < half

    def rope(x):
        rot = jnp.where(first_half, -pltpu.roll(x, RET_WIDTH - half, 1), pltpu.roll(x, half, 1))
        return x * cos + rot * sin

    rq_ref[...] = rope(proj(_C_RQ, _C_RK)).astype(BF16)
    rk_ref[...] = (rope(proj(_C_RK, _C_RV)) * (RET_DIM ** -0.5)).astype(BF16)
    rv_ref[...] = proj(_C_RV, _C_RG).astype(BF16)
    rg_ref[...] = proj(_C_RG, _C_GQ).astype(BF16)
    gq_ref[...] = (proj(_C_GQ, _C_GK) * (GLA_DK ** -0.5)).astype(BF16)
    gk_ref[...] = proj(_C_GK, _C_GV).astype(BF16)
    gv_ref[...] = proj(_C_GV, _C_GG).astype(BF16)
    gg_ref[...] = proj(_C_GG, _C_GL).astype(BF16)
    codes = proj(_C_GL, _C_END).astype(BF16)
    z = jnp.dot(codes, wgate_ref[...], preferred_element_type=F32) + bgate_ref[...]
    ls = _log_sigmoid(z) * (1.0 / GLA_TAU)
    lf_ref[...] = ls[:, 0:256]
    lb_ref[...] = ls[:, 256:512]


def _inproj(h, gamma, w, cos, sin, wgate, bgate, batch, lp):
    r = h.shape[0]
    tm = ROW_TILE
    widths = (256, 384, 384, 384, 384, 256, 256, 512, 512, 256, 256)
    dtypes = (F32,) + (BF16,) * 8 + (F32, F32)

    def rows(wd):
        return pl.BlockSpec((tm, wd), lambda i: (i, 0))

    return pl.pallas_call(
        functools.partial(_inproj_kernel, batch=batch, lp=lp),
        grid=(r // tm,),
        in_specs=[rows(D_MODEL), _const_spec((1, D_MODEL)), _const_spec((D_MODEL, _C_END)),
                  rows(RET_WIDTH), rows(RET_WIDTH), _const_spec((LANES, 512)), _const_spec((1, 512))],
        out_specs=[rows(wd) for wd in widths],
        out_shape=[jax.ShapeDtypeStruct((r, wd), dt) for wd, dt in zip(widths, dtypes)],
        compiler_params=_params("parallel"),
        name="inproj",
    )(h, gamma, w, cos, sin, wgate, bgate)


def _s5_toeplitz_kernel(pwr_ref, pwi_ref, cer_ref, cei_ref, bbr_ref, bbi_ref, d_ref, tt_ref):
    hp = lax.Precision.HIGHEST
    w = CHUNK * S5_P
    lane = lax.broadcasted_iota(jnp.int32, (S5_P, w), 1)
    krow = []
    for dr in range(2):
        pr, pi, cr, ci = pwr_ref[dr], pwi_ref[dr], cer_ref[dr], cei_ref[dr]
        zr = pr * cr - pi * ci
        zi = pr * ci + pi * cr
        krow.append(jnp.dot(bbr_ref[dr], zr, precision=hp, preferred_element_type=F32)
                    - jnp.dot(bbi_ref[dr], zi, precision=hp, preferred_element_type=F32))
    kf = krow[0] + d_ref[...]
    kb = krow[1]
    for s in range(CHUNK):
        right = S5_P * s
        left = S5_P * (CHUNK - 1 - s)
        a = kf if right == 0 else jnp.where(lane >= right, pltpu.roll(kf, right, 1), 0.0)
        b = kb if left == 0 else jnp.where(lane < w - left, pltpu.roll(kb, w - left, 1), 0.0)
        tt_ref[S5_P * s:S5_P * (s + 1), :] = (a + b).astype(BF16)


def _s5_toeplitz(pwr, pwi, cer, cei, bbr, bbi, d_e):
    w = CHUNK * S5_P
    big = pl.BlockSpec((None, 2, S5_N, w), lambda g: (g, 0, 0, 0))
    small = pl.BlockSpec((None, 2, S5_P, S5_N), lambda g: (g, 0, 0, 0))
    return pl.pallas_call(
        _s5_toeplitz_kernel,
        grid=(S5_GROUPS,),
        in_specs=[big, big, big, big, small, small, pl.BlockSpec((None, S5_P, w), lambda g: (g, 0, 0))],
        out_specs=pl.BlockSpec((None, w, w), lambda g: (g, 0, 0)),
        out_shape=jax.ShapeDtypeStruct((S5_GROUPS, w, w), BF16),
        compiler_params=_params("parallel"),
        name="s5_toeplitz",
    )(pwr, pwi, cer, cei, bbr, bbi, d_e)


def _s5_tables(lam_re, lam_im, log_dt, b_re, b_im, c_re, c_im, d):
    c = CHUNK
    g_, n_, p_ = S5_GROUPS, S5_N, S5_P
    dt = jnp.exp(log_dt.astype(F32))[..., None]
    lr = lam_re.astype(F32)
    li = lam_im.astype(F32)
    e = lr * dt
    th = li * dt
    mag = jnp.exp(e)
    a_re = mag * jnp.cos(th)
    a_im = mag * jnp.sin(th)
    den = lr * lr + li * li
    nr = a_re - 1.0
    ni = a_im
    coef_re = (nr * lr + ni * li) / den
    coef_im = (ni * lr - nr * li) / den
    br = b_re.astype(F32)
    bi = b_im.astype(F32)
    bb_re = coef_re[..., None] * br - coef_im[..., None] * bi
    bb_im = coef_re[..., None] * bi + coef_im[..., None] * br
    cr = c_re.astype(F32)
    ci = c_im.astype(F32)
    tau = jnp.arange(c + 1, dtype=F32)[:, None, None, None]
    pw_mag = jnp.exp(tau * e[None])
    pw_re = pw_mag * jnp.cos(tau * th[None])
    pw_im = pw_mag * jnp.sin(tau * th[None])

    z_re = cr[None] * pw_re[:, :, :, None, :] - ci[None] * pw_im[:, :, :, None, :]
    z_im = cr[None] * pw_im[:, :, :, None, :] + ci[None] * pw_re[:, :, :, None, :]
    idx = jnp.arange(c)

    def expand_pw(pw):
        both = jnp.stack([pw[:c, 0], pw[c - 1 - idx, 1]], axis=0)
        both = jnp.transpose(both, (2, 0, 3, 1))
        return jnp.broadcast_to(both[..., None], (g_, 2, n_, c, p_)).reshape(g_, 2, n_, c * p_)

    def expand_c(cm):
        cm = jnp.transpose(cm, (1, 0, 3, 2))
        return jnp.broadcast_to(cm[:, :, :, None, :], (g_, 2, n_, c, p_)).reshape(g_, 2, n_, c * p_)

    bbt_re = jnp.transpose(bb_re, (1, 0, 3, 2))
    bbt_im = jnp.transpose(bb_im, (1, 0, 3, 2))
    d_e = jnp.zeros((g_, p_, c * p_), F32).at[:, :, :p_].set(d.astype(F32)[:, :, None] * jnp.eye(p_, dtype=F32))
    tt = _s5_toeplitz(expand_pw(pw_re), expand_pw(pw_im), expand_c(cr), expand_c(ci), bbt_re, bbt_im, d_e)

    pf_re = pw_re[c - 1 - idx, 0]
    pf_im = pw_im[c - 1 - idx, 0]
    pb_re = pw_re[idx, 1]
    pb_im = pw_im[idx, 1]

    def m_of(p_re_, p_im_, dr):
        m_re = p_re_[:, :, :, None] * bb_re[dr][None] - p_im_[:, :, :, None] * bb_im[dr][None]
        m_im = p_re_[:, :, :, None] * bb_im[dr][None] + p_im_[:, :, :, None] * bb_re[dr][None]
        to = lambda m: jnp.transpose(m, (1, 0, 3, 2)).reshape(g_, c * p_, n_)
        return to(m_re), to(m_im)

    mf_re, mf_im = m_of(pf_re, pf_im, 0)
    mb_re, mb_im = m_of(pb_re, pb_im, 1)
    m4 = jnp.stack([mf_re, mf_im, mb_re, mb_im], axis=1)
    m4 = m4.reshape(g_ // 2, 2, 4, c * p_, n_)
    mz = jnp.zeros_like(m4[:, 0])

    def m_rows(blocks):
        return jnp.transpose(jnp.concatenate(blocks, axis=-1), (0, 2, 1, 3)).reshape(g_ // 2, c * p_, 8 * n_)

    mpair = jnp.concatenate([m_rows([m4[:, 0], mz]), m_rows([mz, m4[:, 1]])], axis=1).astype(BF16)

    def n_of(tsel, dr):
        w_re = z_re[tsel, dr]
        w_im = z_im[tsel, dr]
        to = lambda m: jnp.transpose(m, (1, 3, 0, 2)).reshape(g_, n_, c * p_)
        return to(w_re), to(-w_im)

    nf_re, nf_im = n_of(idx + 1, 0)
    nb_re, nb_im = n_of(c - idx, 1)
    n4 = jnp.stack([nf_re, nf_im, nb_re, nb_im], axis=1)
    n4 = n4.reshape(g_ // 2, 2, 4, n_, c * p_)
    nz = jnp.zeros_like(n4[:, 0])
    npair = jnp.stack([jnp.concatenate([n4[:, 0], nz], axis=-1), jnp.concatenate([nz, n4[:, 1]], axis=-1)], axis=2)
    npair = npair.reshape(g_ // 2, 4 * 2 * n_, 2 * c * p_).astype(BF16)

    dec = jnp.stack([pw_re[c, 0], pw_im[c, 0], pw_re[c, 1], pw_im[c, 1]], axis=0)
    dec = dec.reshape(4, 1, g_ * n_)
    return tt, mpair, npair, dec


S5_PITCH = CHUNK + 4
S5_HALVES = S5_WIDTH // LANES
S5_GPH = LANES // S5_P


def _s5_chunk_block(nch):
    best = 8
    for cand in range(8, 113, 8):
        if nch % cand == 0:
            best = cand
    return best


def _block_transpose8(tiles):
    lane = lax.broadcasted_iota(jnp.int32, tiles[0].shape, 1)
    tiles = list(tiles)
    for dist in (4, 2, 1):
        width = S5_P * dist
        low = (lane & (2 * width - 1)) < width
        for k in range(S5_GPH):
            if k & dist:
                continue
            a, b = tiles[k], tiles[k + dist]
            tiles[k] = jnp.where(low, a, pltpu.roll(b, width, 1))
            tiles[k + dist] = jnp.where(low, pltpu.roll(a, LANES - width, 1), b)
    return tiles


def _s5_in_kernel(u_hbm, m_ref, ucat_ref, o0, o1, o2, o3, xpad, uall, sem):
    i = pl.program_id(0)
    j = pl.program_id(1)
    cb = uall.shape[1]

    @pl.when(j == 0)
    def _():
        def chunk_copy(c, h):
            return pltpu.make_async_copy(
                u_hbm.at[pl.ds((i * cb + c) * CHUNK, CHUNK), pl.ds(h * LANES, LANES)],
                xpad.at[h, pl.ds(c * S5_PITCH, CHUNK), :], sem)

        def start(c, carry):
            for h in range(S5_HALVES):
                chunk_copy(c, h).start()
            return carry

        def wait(c, carry):
            for h in range(S5_HALVES):
                chunk_copy(c, h).wait()
            return carry

        lax.fori_loop(0, cb, start, 0)
        lax.fori_loop(0, cb, wait, 0)

        def dest_tile(jt, carry):
            col = pl.multiple_of(jt * LANES, LANES)
            for h in range(S5_HALVES):
                by_token = [xpad[h, pl.ds(jt * S5_GPH + k, cb, stride=S5_PITCH), :] for k in range(S5_GPH)]
                for go, tile in enumerate(_block_transpose8(by_token)):
                    uall[h * S5_GPH + go, :, pl.ds(col, LANES)] = tile
            return carry

        lax.fori_loop(0, CHUNK // S5_GPH, dest_tile, 0)

    u2 = jnp.concatenate([uall[2 * j], uall[2 * j + 1]], axis=1)
    ucat_ref[...] = u2
    s = jnp.dot(u2.astype(BF16), m_ref[...], preferred_element_type=F32)
    o0[...] = s[:, 0:128]
    o1[...] = s[:, 128:256]
    o2[...] = s[:, 256:384]
    o3[...] = s[:, 384:512]


def _s5_in(u, mpair):
    nch = u.shape[0] // CHUNK
    cb = _s5_chunk_block(nch)
    w = CHUNK * S5_P
    return pl.pallas_call(
        _s5_in_kernel,
        grid=(nch // cb, S5_GROUPS // 2),
        in_specs=[pl.BlockSpec(memory_space=pl.ANY),
                  pl.BlockSpec((None, 2 * w, 512), lambda i, j: (j, 0, 0))],
        out_specs=[pl.BlockSpec((cb, 2 * w), lambda i, j: (i, j))] + [pl.BlockSpec((cb, LANES), lambda i, j: (i, j))] * 4,
        out_shape=[jax.ShapeDtypeStruct((nch, S5_GROUPS * w), F32)] + [jax.ShapeDtypeStruct((nch, 1024), F32)] * 4,
        scratch_shapes=[pltpu.VMEM((S5_HALVES, cb * S5_PITCH, LANES), F32), pltpu.VMEM((S5_GROUPS, cb, w), F32),
                        pltpu.SemaphoreType.DMA(())],
        compiler_params=_params("arbitrary", "arbitrary"),
        name="s5_in",
    )(u, mpair)


def _s5_scan_kernel(sfr, sfi, sbr, sbi, dfr, dfi, dbr, dbi, hfr, hfi, hbr, hbi):
    nc, b, _ = sfr.shape
    a_fr = dfr[...]
    a_fi = dfi[...]
    a_br = dbr[...]
    a_bi = dbi[...]
    zero = jnp.zeros((b, LANES), F32)

    def fwd(c, carry):
        hr, hi = carry
        hfr[c] = hr
        hfi[c] = hi
        return (a_fr * hr - a_fi * hi + sfr[c], a_fr * hi + a_fi * hr + sfi[c])

    def bwd(i, carry):
        c = nc - 1 - i
        hr, hi = carry
        hbr[c] = hr
        hbi[c] = hi
        return (a_br * hr - a_bi * hi + sbr[c], a_br * hi + a_bi * hr + sbi[c])

    lax.fori_loop(0, nc, fwd, (zero, zero))
    lax.fori_loop(0, nc, bwd, (zero, zero))


def _s5_scan(s4, dec, nc, b):
    blk = pl.BlockSpec((nc, b, LANES), lambda j: (0, 0, j))
    dspecs = [pl.BlockSpec((None, 1, LANES), functools.partial(lambda j, k: (k, 0, j), k=k)) for k in range(4)]
    return pl.pallas_call(
        _s5_scan_kernel,
        grid=(S5_GROUPS // 2,),
        in_specs=[blk] * 4 + dspecs,
        out_specs=[blk] * 4,
        out_shape=[jax.ShapeDtypeStruct((nc, b, 1024), F32)] * 4,
        compiler_params=_params("parallel"),
        name="s5_scan",
    )(*s4, dec, dec, dec, dec)


def _s5_out_kernel(u_ref, tt_ref, h0, h1, h2, h3, n_ref, y_hbm, yall, ypad, sem):
    i = pl.program_id(0)
    j = pl.program_id(1)
    cb = yall.shape[1]
    w = CHUNK * S5_P
    u = u_ref[...].astype(BF16)
    y0 = jnp.dot(u[:, 0:w], tt_ref[0], preferred_element_type=F32)
    y1 = jnp.dot(u[:, w:2 * w], tt_ref[1], preferred_element_type=F32)
    hcat = jnp.concatenate([h0[...], h1[...], h2[...], h3[...]], axis=1).astype(BF16)
    yh = jnp.dot(hcat, n_ref[...], preferred_element_type=F32)
    yall[2 * j] = y0 + yh[:, 0:w]
    yall[2 * j + 1] = y1 + yh[:, w:2 * w]

    @pl.when(j == pl.num_programs(1) - 1)
    def _():
        def src_tile(jt, carry):
            col = pl.multiple_of(jt * LANES, LANES)
            for h in range(S5_HALVES):
                by_group = [yall[h * S5_GPH + go, :, pl.ds(col, LANES)] for go in range(S5_GPH)]
                for k, tile in enumerate(_block_transpose8(by_group)):
                    ypad[h, pl.ds(jt * S5_GPH + k, cb, stride=S5_PITCH), :] = tile
            return carry

        lax.fori_loop(0, CHUNK // S5_GPH, src_tile, 0)

        def chunk_copy(c, h):
            return pltpu.make_async_copy(
                ypad.at[h, pl.ds(c * S5_PITCH, CHUNK), :],
                y_hbm.at[pl.ds((i * cb + c) * CHUNK, CHUNK), pl.ds(h * LANES, LANES)], sem)

        def start(c, carry):
            for h in range(S5_HALVES):
                chunk_copy(c, h).start()
            return carry

        def wait(c, carry):
            for h in range(S5_HALVES):
                chunk_copy(c, h).wait()
            return carry

        lax.fori_loop(0, cb, start, 0)
        lax.fori_loop(0, cb, wait, 0)


def _s5_out(ucat, tt, h4, npair):
    nch = ucat.shape[0]
    cb = _s5_chunk_block(nch)
    w = CHUNK * S5_P
    hblk = pl.BlockSpec((cb, LANES), lambda i, j: (i, j))
    return pl.pallas_call(
        _s5_out_kernel,
        grid=(nch // cb, S5_GROUPS // 2),
        in_specs=[pl.BlockSpec((cb, 2 * w), lambda i, j: (i, j)),
                  pl.BlockSpec((2, w, w), lambda i, j: (j, 0, 0)),
                  hblk, hblk, hblk, hblk,
                  pl.BlockSpec((None, 512, 2 * w), lambda i, j: (j, 0, 0))],
        out_specs=pl.BlockSpec(memory_space=pl.ANY),
        out_shape=jax.ShapeDtypeStruct((nch * CHUNK, S5_WIDTH), F32),
        scratch_shapes=[pltpu.VMEM((S5_GROUPS, cb, w), F32), pltpu.VMEM((S5_HALVES, cb * S5_PITCH, LANES), F32),
                        pltpu.SemaphoreType.DMA(())],
        compiler_params=_params("arbitrary", "arbitrary"),
        name="s5_out",
    )(ucat, tt, *h4, npair)


def _s5_mixer(u, tables, batch, lp):
    tt, mpair, npair, dec = tables
    nc = lp // CHUNK
    ucat, *s4 = _s5_in(u, mpair)
    s4 = [jnp.transpose(s.reshape(batch, nc, 1024), (1, 0, 2)) for s in s4]
    h4 = _s5_scan(s4, dec, nc, batch)
    h4 = [jnp.transpose(h, (1, 0, 2)).reshape(batch * nc, 1024) for h in h4]
    return _s5_out(ucat, tt, h4, npair)


def _cumsum_chunks(x):
    n = x.shape[0]
    r = lax.broadcasted_iota(jnp.int32, (n, n), 0)
    s = lax.broadcasted_iota(jnp.int32, (n, n), 1)
    tri = jnp.where((s <= r) & ((s // CHUNK) == (r // CHUNK)), 1.0, 0.0).astype(BF16)
    hi = x.astype(BF16)
    lo = (x - hi.astype(F32)).astype(BF16)
    return jnp.dot(tri, hi, preferred_element_type=F32) + jnp.dot(tri, lo, preferred_element_type=F32)


def _head_masks(npair, vp):
    kl = lax.broadcasted_iota(jnp.int32, (1, LANES), 1)
    km = [(kl < 64), (kl >= 64)]
    vl = lax.broadcasted_iota(jnp.int32, (1, vp), 1)
    vm = [(vl < vp // 2), (vl >= vp // 2)]
    vrow = lax.broadcasted_iota(jnp.int32, (vp, LANES), 0)
    kcol = lax.broadcasted_iota(jnp.int32, (vp, LANES), 1)
    bd = (vrow >= vp // 2) == (kcol >= 64)
    return km, vm, bd


def _la_bwd_kernel(k_all, v_all, lb_all, sb_all, st_all, *, npair, vp, nsub, nseq):
    g = LA_GROUP
    c = CHUNK
    gc = g * c

    @pl.when(pl.program_id(1) == 0)
    def _():
        st_all[...] = jnp.zeros_like(st_all)

    km, vm, bd = _head_masks(npair, vp)

    def sub_block(t, carry):
        for s in range(nseq):
            one_seq(k_all.at[s], v_all.at[s], lb_all.at[s], sb_all.at[s], st_all.at[s], nsub - 1 - t)
        return carry

    def one_seq(k_ref, v_ref, lb_ref, sb_ref, st_ref, sub):
        base = pl.multiple_of(sub * gc, gc)
        kblk = k_ref[pl.ds(base, gc), :].astype(F32)
        vblk = v_ref[pl.ds(base, gc), :]
        lblk = lb_ref[pl.ds(base, gc), :]
        cblk = _cumsum_chunks(lblk)
        kb2blk = (kblk * jnp.exp(cblk - lblk)).astype(BF16)
        states = [st_ref[p] for p in range(npair)]
        for gi in reversed(range(g)):
            rows = slice(gi * c, (gi + 1) * c)
            kb2 = kb2blk[rows, :]
            dec = jnp.exp(cblk[gi * c + c - 1:(gi + 1) * c, :])
            vv = vblk[rows, :]
            for p in range(npair):
                sb_ref[sub * g + gi, p] = states[p].astype(BF16)
                upd = lax.dot_general(vv[:, p * vp:(p + 1) * vp], kb2[:, p * LANES:(p + 1) * LANES],
                                      (((0,), (0,)), ((), ())), preferred_element_type=F32)
                states[p] = states[p] * dec[:, p * LANES:(p + 1) * LANES] + jnp.where(bd, upd, 0.0)
        for p in range(npair):
            st_ref[p] = states[p]

    lax.fori_loop(0, nsub, sub_block, 0)


def _head_rms_gate(o, gate, bdv, dv):
    ms = _split_dot(o * o, bdv) * (1.0 / dv)
    return (o * lax.rsqrt(ms + NORM_EPS) * _silu(gate.astype(F32))).astype(BF16)


def _la_fwd_kernel(q_all, k_all, v_all, lf_all, lb_all, sb_all, gate_all, bdv_ref, o_all, st_all,
                   *, npair, vp, dv, nsub, nseq):
    g = LA_GROUP
    c = CHUNK
    gc = g * c

    @pl.when(pl.program_id(1) == 0)
    def _():
        st_all[...] = jnp.zeros_like(st_all)

    km, vm, bd = _head_masks(npair, vp)
    ri = lax.broadcasted_iota(jnp.int32, (c, LANES), 0)
    cj = lax.broadcasted_iota(jnp.int32, (c, LANES), 1) & (c - 1)
    lower = ri >= cj
    upper = ri <= cj
    mid = c // 2
    nt = (((1,), (1,)), ((), ()))
    tn = (((0,), (0,)), ((), ()))

    def sub_block(sub, carry):
        for s in range(nseq):
            one_seq(q_all.at[s], k_all.at[s], v_all.at[s], lf_all.at[s], lb_all.at[s], sb_all.at[s],
                    gate_all.at[s], o_all.at[s], st_all.at[s], sub)
        return carry

    def one_seq(q_ref, k_ref, v_ref, lf_ref, lb_ref, sb_ref, gate_ref, o_ref, st_ref, sub):
        base = pl.multiple_of(sub * gc, gc)
        qblk = q_ref[pl.ds(base, gc), :].astype(F32)
        kblk = k_ref[pl.ds(base, gc), :].astype(F32)
        vblk = v_ref[pl.ds(base, gc), :]
        lfblk = lf_ref[pl.ds(base, gc), :]
        lbblk = lb_ref[pl.ds(base, gc), :]
        w = lfblk.shape[1]
        csblk = _cumsum_chunks(jnp.concatenate([lfblk, lbblk], axis=1))
        states = [st_ref[p] for p in range(npair)]
        outs = []
        for gi in range(g):
            rows = slice(gi * c, (gi + 1) * c)
            qq = qblk[rows, :]
            kk = kblk[rows, :]
            vv = vblk[rows, :]
            cf = csblk[rows, :w]
            cbi = csblk[rows, w:]
            cb = cbi - lbblk[rows, :]
            mf = cf[mid:mid + 1, :]
            mb = cb[mid:mid + 1, :]
            tf = cf[c - 1:c, :]
            tb = cbi[c - 1:c, :]
            qf = (qq * jnp.exp(cf - mf)).astype(BF16)
            kf = (kk * jnp.exp(mf - cf)).astype(BF16)
            qb = (qq * jnp.exp(mb - cb)).astype(BF16)
            kb = (kk * jnp.exp(cb - mb)).astype(BF16)
            zk = jnp.zeros((c, LANES), BF16)
            q2 = jnp.concatenate([qq * jnp.exp(cf), qq * jnp.exp(tb - cb)], axis=0).astype(BF16)
            kf2 = (kk * jnp.exp(tf - cf)).astype(BF16)
            dec = jnp.exp(tf)
            pair_out = []
            for p in range(npair):
                ks = slice(p * LANES, (p + 1) * LANES)
                vsl = slice(p * vp, (p + 1) * vp)
                kfp = kf[:, ks]
                kbp = kb[:, ks]
                k2f = jnp.concatenate([jnp.where(km[0], kfp, zk), jnp.where(km[1], kfp, zk)], axis=0)
                k2b = jnp.concatenate([jnp.where(km[0], kbp, zk), jnp.where(km[1], kbp, zk)], axis=0)
                sf = lax.dot_general(qf[:, ks], k2f, nt, preferred_element_type=F32)
                sb = lax.dot_general(qb[:, ks], k2b, nt, preferred_element_type=F32)
                sc = (jnp.where(lower, sf, 0.0) + jnp.where(upper, sb, 0.0)).astype(BF16)
                vp_ = vv[:, vsl]
                zero = jnp.zeros_like(vp_)
                v2 = jnp.concatenate([jnp.where(vm[0], vp_, zero), jnp.where(vm[1], vp_, zero)], axis=0)
                o = jnp.dot(sc, v2, preferred_element_type=F32)
                st = states[p]
                o = o + lax.dot_general(q2[0:c, ks], st.astype(BF16), nt, preferred_element_type=F32)
                o = o + lax.dot_general(q2[c:2 * c, ks], sb_ref[sub * g + gi, p], nt, preferred_element_type=F32)
                upd = lax.dot_general(vp_, kf2[:, ks], tn, preferred_element_type=F32)
                states[p] = st * dec[:, ks] + jnp.where(bd, upd, 0.0)
                pair_out.append(o)
            outs.append(jnp.concatenate(pair_out, axis=1) if npair > 1 else pair_out[0])
        for p in range(npair):
            st_ref[p] = states[p]
        o = jnp.concatenate(outs, axis=0)
        o_ref[pl.ds(base, gc), :] = _head_rms_gate(o, gate_ref[pl.ds(base, gc), :], bdv_ref[...], dv)

    lax.fori_loop(0, nsub, sub_block, 0)


def _head_block_ones(wv, seg):
    vr = lax.broadcasted_iota(jnp.int32, (wv, wv), 0) // seg
    vc = lax.broadcasted_iota(jnp.int32, (wv, wv), 1) // seg
    return (vr == vc).astype(BF16)


def _sub_blocks(lp, rows):
    n = lp // rows
    for cand in (5, 4, 3, 2):
        if n % cand == 0:
            return cand
    return 1


def _seqs_per_step(batch):
    for n in (4, 2):
        if batch % n == 0:
            return n
    return 1


def _linear_attention(q, k, v, lf, lb, gate, batch, lp, *, npair, vp, dv):
    gc = LA_GROUP * CHUNK
    nsub = _sub_blocks(lp, gc)
    rows = nsub * gc
    nblk = lp // rows
    nc = lp // CHUNK
    wk = npair * LANES
    wv = npair * vp
    q3 = q.reshape(batch, lp, wk)
    k3 = k.reshape(batch, lp, wk)
    v3 = v.reshape(batch, lp, wv)
    g3 = gate.reshape(batch, lp, wv)
    lf3 = lf.reshape(batch, lp, wk)
    lb3 = lb.reshape(batch, lp, wk)
    nseq = _seqs_per_step(batch)
    fwd = lambda b, j: (b, j, 0)
    rev = lambda b, j: (b, nblk - 1 - j, 0)
    sblk = (nseq, nsub * LA_GROUP, npair, vp, LANES)
    kblk = (nseq, rows, wk)
    vblk = (nseq, rows, wv)

    sb = pl.pallas_call(
        functools.partial(_la_bwd_kernel, npair=npair, vp=vp, nsub=nsub, nseq=nseq),
        grid=(batch // nseq, nblk),
        in_specs=[pl.BlockSpec(kblk, rev), pl.BlockSpec(vblk, rev), pl.BlockSpec(kblk, rev)],
        out_specs=pl.BlockSpec(sblk, lambda b, j: (b, nblk - 1 - j, 0, 0, 0)),
        out_shape=jax.ShapeDtypeStruct((batch, nc, npair, vp, LANES), BF16),
        scratch_shapes=[pltpu.VMEM((nseq, npair, vp, LANES), F32)],
        compiler_params=_params("parallel", "arbitrary"),
        name="la_bwd_states",
    )(k3, v3, lb3)

    o = pl.pallas_call(
        functools.partial(_la_fwd_kernel, npair=npair, vp=vp, dv=dv, nsub=nsub, nseq=nseq),
        grid=(batch // nseq, nblk),
        in_specs=[pl.BlockSpec(kblk, fwd), pl.BlockSpec(kblk, fwd), pl.BlockSpec(vblk, fwd),
                  pl.BlockSpec(kblk, fwd), pl.BlockSpec(kblk, fwd),
                  pl.BlockSpec(sblk, lambda b, j: (b, j, 0, 0, 0)),
                  pl.BlockSpec(vblk, fwd),
                  pl.BlockSpec((wv, wv), lambda b, j: (0, 0))],
        out_specs=pl.BlockSpec(vblk, fwd),
        out_shape=jax.ShapeDtypeStruct((batch, lp, wv), BF16),
        scratch_shapes=[pltpu.VMEM((nseq, npair, vp, LANES), F32)],
        compiler_params=_params("parallel", "arbitrary"),
        name="la_fwd",
    )(q3, k3, v3, lf3, lb3, sb, g3, _head_block_ones(wv, vp // 2))
    return o.reshape(batch * lp, wv)


RET_CHUNK = 256


def _ret_decay_terms(lg_ref):
    c = RET_CHUNK
    lg = lg_ref[...]
    i = lax.broadcasted_iota(jnp.int32, (c, 1), 0).astype(F32)
    return dict(q_f=jnp.exp((i + 1.0) * lg), q_b=jnp.exp((float(c) - i) * lg),
                k_f=jnp.exp((float(c - 1) - i) * lg), k_b=jnp.exp(i * lg), dec=jnp.exp(float(c) * lg))


def _ret_bwd_kernel(k_all, v_all, lg_ref, sb_all, st_all, *, npair, nsub, nseq):
    c = RET_CHUNK
    tn = (((0,), (0,)), ((), ()))

    @pl.when(pl.program_id(1) == 0)
    def _():
        st_all[...] = jnp.zeros_like(st_all)

    _, _, bd = _head_masks(npair, LANES)
    t = _ret_decay_terms(lg_ref)

    def sub_block(it, carry):
        for s in range(nseq):
            one_seq(k_all.at[s], v_all.at[s], sb_all.at[s], st_all.at[s], nsub - 1 - it)
        return carry

    def one_seq(k_ref, v_ref, sb_ref, st_ref, sub):
        base = pl.multiple_of(sub * c, c)
        kb = (k_ref[pl.ds(base, c), :].astype(F32) * t["k_b"]).astype(BF16)
        vv = v_ref[pl.ds(base, c), :]
        for p in range(npair):
            ks = slice(p * LANES, (p + 1) * LANES)
            st = st_ref[p]
            sb_ref[sub, p] = st.astype(BF16)
            upd = lax.dot_general(vv[:, ks], kb[:, ks], tn, preferred_element_type=F32)
            st_ref[p] = st * t["dec"][:, ks] + jnp.where(bd, upd, 0.0)

    lax.fori_loop(0, nsub, sub_block, 0)


def _ret_fwd_kernel(q_all, k_all, v_all, lg_ref, dmask_ref, sb_all, gate_all, bdv_ref, o_all, st_all,
                    *, npair, nsub, dv, nseq):
    c = RET_CHUNK
    nt = (((1,), (1,)), ((), ()))
    tn = (((0,), (0,)), ((), ()))

    @pl.when(pl.program_id(1) == 0)
    def _():
        st_all[...] = jnp.zeros_like(st_all)

    km, vm, bd = _head_masks(npair, LANES)
    t = _ret_decay_terms(lg_ref)

    def sub_block(sub, carry):
        for s in range(nseq):
            one_seq(q_all.at[s], k_all.at[s], v_all.at[s], sb_all.at[s], gate_all.at[s], o_all.at[s], st_all.at[s], sub)
        return carry

    def one_seq(q_ref, k_ref, v_ref, sb_ref, gate_ref, o_ref, st_ref, sub):
        base = pl.multiple_of(sub * c, c)
        qb16 = q_ref[pl.ds(base, c), :]
        kb16 = k_ref[pl.ds(base, c), :]
        vv = v_ref[pl.ds(base, c), :]
        qq = qb16.astype(F32)
        kk = kb16.astype(F32)
        q2 = jnp.concatenate([qq * t["q_f"], qq * t["q_b"]], axis=0).astype(BF16)
        kf2 = (kk * t["k_f"]).astype(BF16)
        zero = jnp.zeros((c, LANES), BF16)
        pair_out = []
        for p in range(npair):
            ks = slice(p * LANES, (p + 1) * LANES)
            kp = kb16[:, ks]
            vp_ = vv[:, ks]
            k2 = jnp.concatenate([jnp.where(km[0], kp, zero), jnp.where(km[1], kp, zero)], axis=0)
            v2 = jnp.concatenate([jnp.where(vm[0], vp_, zero), jnp.where(vm[1], vp_, zero)], axis=0)
            s = lax.dot_general(qb16[:, ks], k2, nt, preferred_element_type=F32)
            o = jnp.dot((s * dmask_ref[p]).astype(BF16), v2, preferred_element_type=F32)
            st = st_ref[p]
            o = o + lax.dot_general(q2[0:c, ks], st.astype(BF16), nt, preferred_element_type=F32)
            o = o + lax.dot_general(q2[c:2 * c, ks], sb_ref[sub, p], nt, preferred_element_type=F32)
            upd = lax.dot_general(vp_, kf2[:, ks], tn, preferred_element_type=F32)
            st_ref[p] = st * t["dec"][:, ks] + jnp.where(bd, upd, 0.0)
            pair_out.append(o)
        o = jnp.concatenate(pair_out, axis=1)
        o_ref[pl.ds(base, c), :] = _head_rms_gate(o, gate_ref[pl.ds(base, c), :], bdv_ref[...], dv)

    lax.fori_loop(0, nsub, sub_block, 0)


def _retention(q, k, v, gate, batch, lp):
    c = RET_CHUNK
    npair = RET_HEADS // 2
    w = npair * LANES
    nsub = _sub_blocks(lp, c)
    rows = nsub * c
    nblk = lp // rows
    log_gamma = jnp.log1p(-jnp.exp2(-5.0 - jnp.arange(RET_HEADS, dtype=F32)))
    lg = jnp.repeat(log_gamma, RET_DIM)[None]
    dist = jnp.abs(jnp.arange(c)[:, None] - (jnp.arange(2 * c) % c)[None, :]).astype(F32)
    dmask = jnp.exp(dist[None] * jnp.repeat(log_gamma, c).reshape(npair, 1, 2 * c))
    q3, k3, v3, g3 = (a.reshape(batch, lp, w) for a in (q, k, v, gate))
    nseq = _seqs_per_step(batch)
    fwd = lambda b, j: (b, j, 0)
    rev = lambda b, j: (b, nblk - 1 - j, 0)
    sblk = (nseq, nsub, npair, LANES, LANES)
    xblk = (nseq, rows, w)
    lgspec = pl.BlockSpec((1, w), lambda b, j: (0, 0))

    sb = pl.pallas_call(
        functools.partial(_ret_bwd_kernel, npair=npair, nsub=nsub, nseq=nseq),
        grid=(batch // nseq, nblk),
        in_specs=[pl.BlockSpec(xblk, rev), pl.BlockSpec(xblk, rev), lgspec],
        out_specs=pl.BlockSpec(sblk, lambda b, j: (b, nblk - 1 - j, 0, 0, 0)),
        out_shape=jax.ShapeDtypeStruct((batch, lp // c, npair, LANES, LANES), BF16),
        scratch_shapes=[pltpu.VMEM((nseq, npair, LANES, LANES), F32)],
        compiler_params=_params("parallel", "arbitrary"),
        name="ret_bwd_states",
    )(k3, v3, lg)

    o = pl.pallas_call(
        functools.partial(_ret_fwd_kernel, npair=npair, nsub=nsub, dv=RET_DIM, nseq=nseq),
        grid=(batch // nseq, nblk),
        in_specs=[pl.BlockSpec(xblk, fwd), pl.BlockSpec(xblk, fwd), pl.BlockSpec(xblk, fwd), lgspec,
                  pl.BlockSpec((npair, c, 2 * c), lambda b, j: (0, 0, 0)),
                  pl.BlockSpec(sblk, lambda b, j: (b, j, 0, 0, 0)),
                  pl.BlockSpec(xblk, fwd),
                  pl.BlockSpec((w, w), lambda b, j: (0, 0))],
        out_specs=pl.BlockSpec(xblk, fwd),
        out_shape=jax.ShapeDtypeStruct((batch, lp, w), BF16),
        scratch_shapes=[pltpu.VMEM((nseq, npair, LANES, LANES), F32)],
        compiler_params=_params("parallel", "arbitrary"),
        name="ret_fwd",
    )(q3, k3, v3, lg, dmask, sb, g3, _head_block_ones(w, RET_DIM))
    return o.reshape(batch * lp, w)


PACKED = D_MODEL // 2


def _pack_rows(x):
    bits = lax.bitcast_convert_type(x.astype(BF16).astype(F32), jnp.int32)
    return bits[:, :PACKED] | lax.shift_right_logical(bits[:, PACKED:], 16)


def _unpack_rows(p):
    hi = lax.bitcast_convert_type(p & jnp.int32(-65536), F32)
    lo = lax.bitcast_convert_type(lax.shift_left(p, jnp.int32(16)), F32)
    return jnp.concatenate([hi, lo], axis=1)


_RT_E1, _RT_E2, _RT_W1, _RT_W2, _RT_R1, _RT_R2 = range(6)


def _mix_out(ya_ref, yb_ref, yc_ref, h_ref, wglu_ref, wout_ref, g_ref):
    ya = _gelu_tanh(ya_ref[...].astype(F32))
    gl = jnp.dot(ya.astype(BF16), wglu_ref[...], preferred_element_type=F32)
    ya = (ya * _sigmoid(gl)).astype(BF16)
    y = jnp.concatenate([ya, yb_ref[...], yc_ref[...]], axis=1)
    h = h_ref[...] + jnp.dot(y, wout_ref[...], preferred_element_type=F32)
    ms = jnp.mean(h * h, axis=-1, keepdims=True)
    return h, h * lax.rsqrt(ms + NORM_EPS) * g_ref[...]


def _outproj_kernel(ya_ref, yb_ref, yc_ref, h_ref, wglu_ref, wout_ref, g_ref, hout_ref, hn_ref):
    h, hn = _mix_out(ya_ref, yb_ref, yc_ref, h_ref, wglu_ref, wout_ref, g_ref)
    hout_ref[...] = h
    hn_ref[...] = hn.astype(BF16)


def _outproj_router_kernel(ya_ref, yb_ref, yc_ref, h_ref, wglu_ref, wout_ref, g_ref, wrh_ref, wrl_ref, before_ref,
                           hout_ref, hn_ref, route_ref, cnt_ref, carry_ref):
    @pl.when(pl.program_id(0) == 0)
    def _():
        carry_ref[...] = jnp.zeros_like(carry_ref)

    h, hn = _mix_out(ya_ref, yb_ref, yc_ref, h_ref, wglu_ref, wout_ref, g_ref)
    hout_ref[...] = h
    hn_ref[...] = _pack_rows(hn)
    hi = hn.astype(BF16)
    lo = (hn - hi.astype(F32)).astype(BF16)
    wrh = wrh_ref[...]
    logits = (jnp.dot(hi, wrh, preferred_element_type=F32) + jnp.dot(lo, wrh, preferred_element_type=F32)
              + jnp.dot(hi, wrl_ref[...], preferred_element_type=F32))
    lane = lax.broadcasted_iota(jnp.int32, logits.shape, 1)
    neg = jnp.float32(-jnp.inf)
    logits = jnp.where(lane < N_EXPERTS, logits, neg)
    v1 = jnp.max(logits, axis=-1, keepdims=True)
    i1 = jnp.min(jnp.where(logits == v1, lane, LANES), axis=-1, keepdims=True)
    m1 = lane == i1
    l2 = jnp.where(m1, neg, logits)
    v2 = jnp.max(l2, axis=-1, keepdims=True)
    i2 = jnp.min(jnp.where(l2 == v2, lane, LANES), axis=-1, keepdims=True)
    m2 = lane == i2
    e = jnp.exp(v2 - v1)
    w1 = 1.0 / (1.0 + e)
    w2 = e * w1
    chosen = jnp.where(m1 | m2, 1.0, 0.0)
    prefix = jnp.dot(before_ref[...], chosen.astype(BF16), preferred_element_type=F32) + carry_ref[...]
    r1 = jnp.sum(jnp.where(m1, prefix, 0.0), axis=-1, keepdims=True)
    r2 = jnp.sum(jnp.where(m2, prefix, 0.0), axis=-1, keepdims=True)
    total = carry_ref[...] + jnp.sum(chosen, axis=0, keepdims=True)
    carry_ref[...] = total
    cnt_ref[...] = total
    rec = jnp.zeros_like(logits)
    for ln, val in ((_RT_E1, i1.astype(F32)), (_RT_E2, i2.astype(F32)), (_RT_W1, w1), (_RT_W2, w2),
                    (_RT_R1, r1), (_RT_R2, r2)):
        rec = jnp.where(lane == ln, val, rec)
    route_ref[...] = rec


def _outproj(ya, yb, yc, h, wglu, wout, gamma, wr=None):
    r = h.shape[0]
    tm = ROW_TILE
    with_router = wr is not None

    def rows(wd):
        return pl.BlockSpec((tm, wd), lambda i: (i, 0))

    in_specs = [rows(256), rows(384), rows(512), rows(D_MODEL), _const_spec((256, 256)),
                _const_spec((1152, D_MODEL)), _const_spec((1, D_MODEL))]
    args = [ya, yb, yc, h, wglu, wout, gamma]
    if not with_router:
        return pl.pallas_call(
            _outproj_kernel,
            grid=(r // tm,),
            in_specs=in_specs, out_specs=[rows(D_MODEL), rows(D_MODEL)],
            out_shape=[jax.ShapeDtypeStruct((r, D_MODEL), F32), jax.ShapeDtypeStruct((r, D_MODEL), BF16)],
            input_output_aliases={3: 0},
            compiler_params=_params("parallel"),
            name="outproj",
        )(*args)
    earlier = (lax.broadcasted_iota(jnp.int32, (tm, tm), 1) < lax.broadcasted_iota(jnp.int32, (tm, tm), 0)).astype(BF16)
    return pl.pallas_call(
        _outproj_router_kernel,
        grid=(r // tm,),
        in_specs=in_specs + [_const_spec((D_MODEL, LANES)), _const_spec((D_MODEL, LANES)), _const_spec((tm, tm))],
        out_specs=[rows(D_MODEL), rows(PACKED), rows(LANES), _const_spec((1, LANES))],
        out_shape=[jax.ShapeDtypeStruct((r, D_MODEL), F32), jax.ShapeDtypeStruct((r, PACKED), jnp.int32),
                   jax.ShapeDtypeStruct((r, LANES), F32), jax.ShapeDtypeStruct((1, LANES), F32)],
        scratch_shapes=[pltpu.VMEM((1, LANES), F32)],
        input_output_aliases={3: 0},
        compiler_params=_params("arbitrary"),
        name="outproj_router",
    )(*args, wr[0], wr[1], earlier)


FF_TILE = 2816


def _ffn_kernel(hn_ref, h_ref, wg_ref, wu_ref, wd_ref, o_ref, acc_ref):
    j = pl.program_id(1)
    hn = hn_ref[...]
    a = _silu(jnp.dot(hn, wg_ref[...], preferred_element_type=F32)) * jnp.dot(hn, wu_ref[...], preferred_element_type=F32)
    y = jnp.dot(a.astype(BF16), wd_ref[...], preferred_element_type=F32)

    @pl.when(j == 0)
    def _():
        acc_ref[...] = h_ref[...] + y

    @pl.when(j != 0)
    def _():
        acc_ref[...] += y

    @pl.when(j == pl.num_programs(1) - 1)
    def _():
        o_ref[...] = acc_ref[...]


def _ffn(hn, h, wg, wu, wd):
    r = h.shape[0]
    tm = ROW_TILE
    nf = D_FF // FF_TILE
    return pl.pallas_call(
        _ffn_kernel,
        grid=(r // tm, nf),
        in_specs=[pl.BlockSpec((tm, D_MODEL), lambda i, j: (i, 0)),
                  pl.BlockSpec((tm, D_MODEL), lambda i, j: (i, 0)),
                  pl.BlockSpec((D_MODEL, FF_TILE), lambda i, j: (0, j)),
                  pl.BlockSpec((D_MODEL, FF_TILE), lambda i, j: (0, j)),
                  pl.BlockSpec((FF_TILE, D_MODEL), lambda i, j: (j, 0))],
        out_specs=pl.BlockSpec((tm, D_MODEL), lambda i, j: (i, 0)),
        out_shape=jax.ShapeDtypeStruct((r, D_MODEL), F32),
        scratch_shapes=[pltpu.VMEM((tm, D_MODEL), F32)],
        input_output_aliases={1: 0},
        compiler_params=_params("parallel", "arbitrary"),
        name="ffn",
    )(hn, h, wg, wu, wd)


MOE_BLOCK = 512
GATHER_TILE = 256


def _route_meta(route, cnt, r):
    bm = MOE_BLOCK
    nb = 2 * r // bm + N_EXPERTS
    e1 = route[:, _RT_E1].astype(jnp.int32)
    e2 = route[:, _RT_E2].astype(jnp.int32)
    counts = cnt[0, :N_EXPERTS].astype(jnp.int32)
    padded = ((counts + bm - 1) // bm) * bm
    ends = jnp.cumsum(padded)
    starts = ends - padded
    pos1 = starts[e1] + route[:, _RT_R1].astype(jnp.int32)
    pos2 = starts[e2] + route[:, _RT_R2].astype(jnp.int32)
    n_used = (ends[-1] // bm).astype(jnp.int32)
    blk = jnp.arange(nb, dtype=jnp.int32)
    blk = jnp.minimum(blk, n_used - 1)
    block_expert = jnp.sum((blk[:, None] * bm >= ends[None, :]).astype(jnp.int32), axis=1)
    block_expert = jnp.minimum(block_expert, N_EXPERTS - 1).astype(jnp.int32)
    return pos1, pos2, block_expert, n_used.reshape(1), nb


def _dispatch_kernel(p1_ref, p2_ref, hn_ref, xs_in_ref, xs_ref, stage, sems):
    del xs_in_ref
    n = hn_ref.shape[0]
    s = pl.program_id(0)
    last = pl.num_programs(0) - 1
    slot = lax.rem(s, 2)

    def drain(sl):
        def body(r, c):
            row = pltpu.make_async_copy(stage.at[sl, pl.ds(0, 1), :], xs_ref.at[pl.ds(0, 1), :], sems.at[sl])
            row.wait()
            row.wait()
            return c
        lax.fori_loop(0, n, body, 0, unroll=8)

    @pl.when(s >= 2)
    def _():
        drain(slot)

    stage[slot] = hn_ref[...]
    for r in range(n):
        src = stage.at[slot, pl.ds(r, 1), :]
        pltpu.make_async_copy(src, xs_ref.at[pl.ds(p1_ref[0, 0, r], 1), :], sems.at[slot]).start()
        pltpu.make_async_copy(src, xs_ref.at[pl.ds(p2_ref[0, 0, r], 1), :], sems.at[slot]).start()

    @pl.when(s == last)
    def _():
        @pl.when(s >= 1)
        def _():
            drain(1 - slot)
        drain(slot)


def _dispatch(hn, pos1, pos2, nb):
    r = hn.shape[0]
    t = GATHER_TILE
    p = nb * MOE_BLOCK
    idx = pl.BlockSpec((1, 1, t), lambda i: (i, 0, 0), memory_space=pltpu.SMEM)
    return pl.pallas_call(
        _dispatch_kernel,
        grid=(r // t,),
        in_specs=[idx, idx, pl.BlockSpec((t, PACKED), lambda i: (i, 0)), pl.BlockSpec(memory_space=pl.ANY)],
        out_specs=pl.BlockSpec(memory_space=pl.ANY),
        out_shape=jax.ShapeDtypeStruct((p, PACKED), jnp.int32),
        scratch_shapes=[pltpu.VMEM((2, t, PACKED), jnp.int32), pltpu.SemaphoreType.DMA((2,))],
        input_output_aliases={3: 0},
        compiler_params=_params("arbitrary"),
        name="moe_dispatch",
    )(pos1.reshape(r // t, 1, t), pos2.reshape(r // t, 1, t), hn, jnp.zeros((p, PACKED), jnp.int32))


def _moe_ffn_kernel(be_ref, nu_ref, x_ref, wg_ref, wu_ref, wd_ref, o_ref, acc_ref):
    del be_ref
    b = pl.program_id(0)
    j = pl.program_id(1)
    last = j == pl.num_programs(1) - 1
    used = b < nu_ref[0]

    @pl.when(used)
    def _():
        x = _unpack_rows(x_ref[...]).astype(BF16)
        a = _silu(jnp.dot(x, wg_ref[...], preferred_element_type=F32)) * jnp.dot(x, wu_ref[...], preferred_element_type=F32)
        y = jnp.dot(a.astype(BF16), wd_ref[...], preferred_element_type=F32)

        @pl.when(j == 0)
        def _():
            acc_ref[...] = y

        @pl.when(j != 0)
        def _():
            acc_ref[...] += y

        @pl.when(last)
        def _():
            o_ref[...] = _pack_rows(acc_ref[...])

    @pl.when(jnp.logical_not(used) & last)
    def _():
        o_ref[...] = jnp.zeros_like(o_ref)


def _moe_ffn(xs, block_expert, n_used, wg, wu, wd):
    p = xs.shape[0]
    bm = MOE_BLOCK
    nf = D_FF // FF_TILE
    grid_spec = pltpu.PrefetchScalarGridSpec(
        num_scalar_prefetch=2,
        grid=(p // bm, nf),
        in_specs=[pl.BlockSpec((bm, PACKED), lambda b, j, be, nu: (b, 0)),
                  pl.BlockSpec((None, D_MODEL, FF_TILE), lambda b, j, be, nu: (be[b], 0, j)),
                  pl.BlockSpec((None, D_MODEL, FF_TILE), lambda b, j, be, nu: (be[b], 0, j)),
                  pl.BlockSpec((None, FF_TILE, D_MODEL), lambda b, j, be, nu: (be[b], j, 0))],
        out_specs=pl.BlockSpec((bm, PACKED), lambda b, j, be, nu: (b, 0)),
        scratch_shapes=[pltpu.VMEM((bm, D_MODEL), F32)])
    return pl.pallas_call(
        _moe_ffn_kernel,
        grid_spec=grid_spec,
        out_shape=jax.ShapeDtypeStruct((p, PACKED), jnp.int32),
        compiler_params=_params("arbitrary", "arbitrary"),
        name="moe_ffn",
    )(block_expert, n_used, xs, wg, wu, wd)


def _combine_kernel(p1_ref, p2_ref, p1n_ref, p2n_ref, route_ref, h_ref, g_ref, ys_ref, o_ref, buf1, buf2, sems,
                    *, final):
    n = h_ref.shape[0]
    s = pl.program_id(0)
    last = pl.num_programs(0) - 1
    slot = lax.rem(s, 2)

    def gather(pa, pb, sl):
        for r in range(n):
            pltpu.make_async_copy(ys_ref.at[pl.ds(pa[0, 0, r], 1), :], buf1.at[sl, pl.ds(r, 1), :], sems.at[sl]).start()
            pltpu.make_async_copy(ys_ref.at[pl.ds(pb[0, 0, r], 1), :], buf2.at[sl, pl.ds(r, 1), :], sems.at[sl]).start()

    @pl.when(s == 0)
    def _():
        gather(p1_ref, p2_ref, 0)

    @pl.when(s < last)
    def _():
        gather(p1n_ref, p2n_ref, 1 - slot)

    def drain(r, c):
        row = pltpu.make_async_copy(ys_ref.at[pl.ds(0, 1), :], buf1.at[slot, pl.ds(0, 1), :], sems.at[slot])
        row.wait()
        row.wait()
        return c

    lax.fori_loop(0, n, drain, 0, unroll=8)
    rt = route_ref[...]
    lane = lax.broadcasted_iota(jnp.int32, rt.shape, 1)
    w1 = jnp.sum(jnp.where(lane == _RT_W1, rt, 0.0), axis=-1, keepdims=True)
    w2 = jnp.sum(jnp.where(lane == _RT_W2, rt, 0.0), axis=-1, keepdims=True)
    h = h_ref[...] + w1 * _unpack_rows(buf1[slot]) + w2 * _unpack_rows(buf2[slot])
    if final:
        ms = jnp.mean(h * h, axis=-1, keepdims=True)
        h = h * lax.rsqrt(ms + NORM_EPS) * g_ref[...]
    o_ref[...] = h


def _combine(ys, pos1, pos2, route, h, gamma, batch, lp, final):
    r = h.shape[0]
    t = GATHER_TILE
    p1 = pos1.reshape(r // t, 1, t)
    p2 = pos2.reshape(r // t, 1, t)
    scratch = [pltpu.VMEM((2, t, PACKED), jnp.int32), pltpu.VMEM((2, t, PACKED), jnp.int32),
               pltpu.SemaphoreType.DMA((2,))]
    if final:
        per_seq = lp // t
        skip = FRONT // t
        live = per_seq - skip
        steps = batch * live
        rb = lambda s: (s // live) * per_seq + skip + s % live
        out_specs = pl.BlockSpec((None, t, D_MODEL), lambda s: (s // live, s % live, 0))
        out_shape = jax.ShapeDtypeStruct((batch, lp - FRONT, D_MODEL), F32)
    else:
        steps = r // t
        rb = lambda s: s
        out_specs = pl.BlockSpec((t, D_MODEL), lambda s: (s, 0))
        out_shape = jax.ShapeDtypeStruct((r, D_MODEL), F32)
    nxt = lambda s: rb(jnp.minimum(s + 1, steps - 1))
    idx = pl.BlockSpec((1, 1, t), lambda s: (rb(s), 0, 0), memory_space=pltpu.SMEM)
    idx_next = pl.BlockSpec((1, 1, t), lambda s: (nxt(s), 0, 0), memory_space=pltpu.SMEM)
    in_specs = [idx, idx, idx_next, idx_next,
                pl.BlockSpec((t, LANES), lambda s: (rb(s), 0)), pl.BlockSpec((t, D_MODEL), lambda s: (rb(s), 0)),
                pl.BlockSpec((1, D_MODEL), lambda s: (0, 0)), pl.BlockSpec(memory_space=pl.ANY)]
    return pl.pallas_call(
        functools.partial(_combine_kernel, final=final),
        grid=(steps,), in_specs=in_specs, out_specs=out_specs, out_shape=out_shape, scratch_shapes=scratch,
        compiler_params=_params("arbitrary"),
        name="moe_combine_final" if final else "moe_combine",
    )(p1, p2, p1, p2, route, h, gamma, ys)


def _final_norm_kernel(h_ref, g_ref, o_ref):
    h = h_ref[...]
    ms = jnp.mean(h * h, axis=-1, keepdims=True)
    o_ref[...] = h * lax.rsqrt(ms + NORM_EPS) * g_ref[...]


def _final_norm(h, gamma, batch, lp):
    t = GATHER_TILE
    per_seq = lp // t
    skip = FRONT // t
    return pl.pallas_call(
        _final_norm_kernel,
        grid=(batch, per_seq - skip),
        in_specs=[pl.BlockSpec((t, D_MODEL), lambda b, i: (b * per_seq + skip + i, 0)),
                  pl.BlockSpec((1, D_MODEL), lambda b, i: (0, 0))],
        out_specs=pl.BlockSpec((None, t, D_MODEL), lambda b, i: (b, i, 0)),
        out_shape=jax.ShapeDtypeStruct((batch, lp - FRONT, D_MODEL), F32),
        compiler_params=_params("parallel", "parallel"),
        name="final_norm",
    )(h, gamma)


def _rope_tables(lp):
    half = RET_DIM // 2
    pos = jnp.arange(lp, dtype=F32) - float(PAD)
    inv = ROPE_BASE ** (-jnp.arange(half, dtype=F32) / half)
    ang = pos[:, None] * inv[None, :]
    cos = jnp.tile(jnp.cos(ang), (1, 2 * RET_HEADS))
    sin = jnp.tile(jnp.sin(ang), (1, 2 * RET_HEADS))
    return cos, sin


def _prep_layer(li, p):
    w_out = p["w_out"][li].astype(F32)
    wc = w_out[640:1024].reshape(GLA_HEADS, GLA_DV, D_MODEL)
    wc = jnp.pad(wc, ((0, 0), (0, GLA_DV_PAD - GLA_DV), (0, 0))).reshape(GLA_HEADS * GLA_DV_PAD, D_MODEL)
    wgate, bgate = _pack_gate(p["gla_w_gate_f"][li], p["gla_b_gate_f"][li], p["gla_w_gate_b"][li], p["gla_b_gate_b"][li])
    return dict(
        norm_mix=p["norm_mix"][li].astype(F32)[None],
        w_in=_pack_w_in(p["w_in"][li].astype(F32)),
        wgate=wgate, bgate=bgate,
        s5=_s5_tables(p["s5_lambda_re"][li], p["s5_lambda_im"][li], p["s5_log_dt"][li], p["s5_b_re"][li],
                      p["s5_b_im"][li], p["s5_c_re"][li], p["s5_c_im"][li], p["s5_d"][li]),
        w_glu=p["s5_w_glu"][li].astype(BF16),
        w_out=jnp.concatenate([w_out[0:640], wc], axis=0).astype(BF16),
        norm_ffn=p["norm_ffn"][li].astype(F32)[None],
    )


def _trunk(x, meta_tokens, layers, ffn, moe, norm_final, depth):
    batch, seq, _ = x.shape
    lp = seq + FRONT
    r = batch * lp
    meta = jnp.broadcast_to(meta_tokens.astype(F32)[None], (batch, N_META, D_MODEL))
    h = jnp.concatenate([jnp.zeros((batch, PAD, D_MODEL), F32), meta, x.astype(F32)], axis=1).reshape(r, D_MODEL)
    cos, sin = _rope_tables(lp)
    cos = jnp.tile(cos, (batch, 1))
    sin = jnp.tile(sin, (batch, 1))
    gamma_final = norm_final.astype(F32)[None]
    for li in range(depth):
        lw = layers[li]
        u, rq, rk, rv, rg, gq, gk, gv, gg, lf, lb = _inproj(
            h, lw["norm_mix"], lw["w_in"], cos, sin, lw["wgate"], lw["bgate"], batch, lp)
        ya = _s5_mixer(u, lw["s5"], batch, lp)
        yb = _retention(rq, rk, rv, rg, batch, lp)
        yc = _linear_attention(gq, gk, gv, lf, lb, gg, batch, lp, npair=GLA_HEADS // 2, vp=2 * GLA_DV_PAD, dv=GLA_DV)
        j = li // 2
        last = li == depth - 1
        if li % 2 == 0:
            h, hn = _outproj(ya, yb, yc, h, lw["w_glu"], lw["w_out"], lw["norm_ffn"])
            h = _ffn(hn, h, ffn["wg"][j], ffn["wu"][j], ffn["wd"][j])
        else:
            h, hn, route, cnt = _outproj(ya, yb, yc, h, lw["w_glu"], lw["w_out"], lw["norm_ffn"], moe["wr"][j])
            pos1, pos2, block_expert, n_used, nb = _route_meta(route, cnt, r)
            xs = _dispatch(hn, pos1, pos2, nb)
            ys = _moe_ffn(xs, block_expert, n_used, moe["wg"][j], moe["wu"][j], moe["wd"][j])
            h = _combine(ys, pos1, pos2, route, h, gamma_final, batch, lp, final=last)
            if last:
                return h
    return _final_norm(h, gamma_final, batch, lp)


def kernel(x_prompt, x_sample, meta_tokens, norm_mix, w_in, s5_lambda_re, s5_lambda_im, s5_log_dt, s5_b_re, s5_b_im, s5_c_re, s5_c_im, s5_d, s5_w_glu, gla_w_gate_f, gla_b_gate_f, gla_w_gate_b, gla_b_gate_b, w_out, norm_ffn, ffn_w_gate, ffn_w_up, ffn_w_down, router_w, moe_w_gate, moe_w_up, moe_w_down, norm_final):
    depth = w_in.shape[0]
    p = dict(norm_mix=norm_mix, w_in=w_in, s5_lambda_re=s5_lambda_re, s5_lambda_im=s5_lambda_im,
             s5_log_dt=s5_log_dt, s5_b_re=s5_b_re, s5_b_im=s5_b_im, s5_c_re=s5_c_re, s5_c_im=s5_c_im, s5_d=s5_d,
             s5_w_glu=s5_w_glu, gla_w_gate_f=gla_w_gate_f, gla_b_gate_f=gla_b_gate_f, gla_w_gate_b=gla_w_gate_b,
             gla_b_gate_b=gla_b_gate_b, w_out=w_out, norm_ffn=norm_ffn)
    layers = [_prep_layer(li, p) for li in range(depth)]
    ffn = dict(wg=ffn_w_gate.astype(BF16), wu=ffn_w_up.astype(BF16), wd=ffn_w_down.astype(BF16))
    wr = jnp.pad(router_w.astype(F32), ((0, 0), (0, 0), (0, LANES - N_EXPERTS)))
    wr_hi = wr.astype(BF16)
    wr_lo = (wr - wr_hi.astype(F32)).astype(BF16)
    wr = [(wr_hi[j], wr_lo[j]) for j in range(wr.shape[0])]
    moe = dict(wr=wr, wg=moe_w_gate.astype(BF16), wu=moe_w_up.astype(BF16), wd=moe_w_down.astype(BF16))
    y_prompt = _trunk(x_prompt, meta_tokens, layers, ffn, moe, norm_final, depth)
    y_sample = _trunk(x_sample, meta_tokens, layers, ffn, moe, norm_final, depth)
    return (y_prompt, y_sample)
```

```python
import functools
import math

import jax
import jax.numpy as jnp
from jax import lax
from jax.experimental import pallas as pl
from jax.experimental.pallas import tpu as pltpu

F32 = jnp.float32
BF16 = jnp.bfloat16

D_MODEL = 1024
N_META = 16
S5_P = 16
S5_WIDTH = 256
S5_GROUPS = 16
S5_N = 64
RET_HEADS = 6
RET_DIM = 64
RET_WIDTH = 384
GLA_HEADS = 4
GLA_DV = 96
GLA_DK = 48
GLA_QK = 192
GLA_WIDTH = 384
GLA_GATE_RANK = 16
GLA_TAU = 16.0
ROPE_BASE = 10000.0
D_FF = 2816
N_EXPERTS = 8
NORM_EPS = 1e-5

CHUNK = 64
FRONT = 256
PAD = FRONT - N_META
ROW_TILE = 512
LA_GROUP = 4
LANES = 128
GLA_DK_PAD = 64
GLA_DV_PAD = 128
VMEM_LIMIT = 56 * 1024 * 1024


def _params(*sem):
    return pltpu.CompilerParams(dimension_semantics=sem, vmem_limit_bytes=VMEM_LIMIT)


def _const_spec(shape):
    nd = len(shape)
    return pl.BlockSpec(shape, lambda *_: (0,) * nd)


def _sigmoid(x):
    return 1.0 / (1.0 + jnp.exp(-x))


def _silu(x):
    return x * _sigmoid(x)


def _gelu_tanh(x):
    c = math.sqrt(2.0 / math.pi)
    return 0.5 * x * (1.0 + jnp.tanh(c * (x + 0.044715 * (x * x * x))))


def _log_sigmoid(z):
    return jnp.minimum(z, 0.0) - jnp.log(1.0 + jnp.exp(-jnp.abs(z)))


def _split_dot(a, b_bf16, dims=None):
    hi = a.astype(BF16)
    lo = (a - hi.astype(F32)).astype(BF16)
    if dims is None:
        return (jnp.dot(hi, b_bf16, preferred_element_type=F32)
                + jnp.dot(lo, b_bf16, preferred_element_type=F32))
    return (lax.dot_general(hi, b_bf16, dims, preferred_element_type=F32)
            + lax.dot_general(lo, b_bf16, dims, preferred_element_type=F32))


_C_U = 0
_C_RQ = 256
_C_RK = 640
_C_RV = 1024
_C_RG = 1408
_C_GQ = 1792
_C_GK = 2048
_C_GV = 2304
_C_GG = 2816
_C_GL = 3328
_C_END = 3456


def _pack_w_in(w):
    o = 0
    u = w[:, o:o + 256]; o += 256
    rq = w[:, o:o + 384]; o += 384
    rk = w[:, o:o + 384]; o += 384
    rv = w[:, o:o + 384]; o += 384
    rg = w[:, o:o + 384]; o += 384
    gq = w[:, o:o + 192]; o += 192
    gk = w[:, o:o + 192]; o += 192
    gv = w[:, o:o + 384]; o += 384
    gg = w[:, o:o + 384]; o += 384
    glf = w[:, o:o + 16]; o += 16
    glb = w[:, o:o + 16]; o += 16

    def padh(m, d, dp):
        m = m.reshape(D_MODEL, GLA_HEADS, d)
        return jnp.pad(m, ((0, 0), (0, 0), (0, dp - d))).reshape(D_MODEL, GLA_HEADS * dp)

    gl = jnp.pad(jnp.concatenate([glf, glb], axis=1), ((0, 0), (0, LANES - 2 * GLA_GATE_RANK)))
    cat = jnp.concatenate([
        u, rq, rk, rv, rg,
        padh(gq, GLA_DK, GLA_DK_PAD), padh(gk, GLA_DK, GLA_DK_PAD),
        padh(gv, GLA_DV, GLA_DV_PAD), padh(gg, GLA_DV, GLA_DV_PAD), gl], axis=1)
    return cat.astype(BF16)


def _pack_gate(w_f, b_f, w_b, b_b):
    def padh(m):
        m = m.reshape(m.shape[0], GLA_HEADS, GLA_DK)
        return jnp.pad(m, ((0, 0), (0, 0), (0, GLA_DK_PAD - GLA_DK))).reshape(m.shape[0], GLA_HEADS * GLA_DK_PAD)
    r = GLA_GATE_RANK
    w = jnp.zeros((LANES, 2 * GLA_HEADS * GLA_DK_PAD), F32)
    w = w.at[0:r, 0:256].set(padh(w_f.astype(F32)))
    w = w.at[r:2 * r, 256:512].set(padh(w_b.astype(F32)))
    b = jnp.concatenate([padh(b_f.astype(F32)[None]), padh(b_b.astype(F32)[None])], axis=1)
    return w.astype(BF16), b


def _inproj_kernel(h_ref, g_ref, w_ref, cos_ref, sin_ref, wgate_ref, bgate_ref,
                   u_ref, rq_ref, rk_ref, rv_ref, rg_ref, gq_ref, gk_ref, gv_ref, gg_ref, lf_ref, lb_ref,
                   *, batch, lp):
    tm = h_ref.shape[0]
    half = RET_DIM // 2
    first_half = (lax.broadcasted_iota(jnp.int32, (1, RET_WIDTH), 1) & (RET_DIM - 1)) < half
    parts = 2
    tp = tm // parts
    for part in range(parts):
        rows = slice(part * tp, (part + 1) * tp)
        x = h_ref[rows, :]
        ms = jnp.mean(x * x, axis=-1, keepdims=True)
        row = pl.program_id(0) * tm + part * tp + lax.broadcasted_iota(jnp.int32, (tp, 1), 0)
        valid = jnp.ones((tp, 1), F32)
        for b in range(batch):
            valid = jnp.where((row >= b * lp) & (row < b * lp + PAD), 0.0, valid)
        hn = (x * (lax.rsqrt(ms + NORM_EPS) * valid) * g_ref[...]).astype(BF16)

        full = jnp.dot(hn, w_ref[...], preferred_element_type=F32)

        def proj(lo, hi):
            return full[:, lo:hi]

        cos = cos_ref[rows, :]
        sin = sin_ref[rows, :]

        def rope(v):
            rot = jnp.where(first_half, -pltpu.roll(v, RET_WIDTH - half, 1), pltpu.roll(v, half, 1))
            return v * cos + rot * sin

        u_ref[rows, :] = proj(_C_U, _C_RQ)
        rq_ref[rows, :] = rope(proj(_C_RQ, _C_RK)).astype(BF16)
        rk_ref[rows, :] = (rope(proj(_C_RK, _C_RV)) * (RET_DIM ** -0.5)).astype(BF16)
        rv_ref[rows, :] = proj(_C_RV, _C_RG).astype(BF16)
        rg_ref[rows, :] = proj(_C_RG, _C_GQ).astype(BF16)
        gq_ref[rows, :] = (proj(_C_GQ, _C_GK) * (GLA_DK ** -0.5)).astype(BF16)
        gk_ref[rows, :] = proj(_C_GK, _C_GV).astype(BF16)
        gv_ref[rows, :] = proj(_C_GV, _C_GG).astype(BF16)
        gg_ref[rows, :] = proj(_C_GG, _C_GL).astype(BF16)
        codes = proj(_C_GL, _C_END).astype(BF16)
        z = jnp.dot(codes, wgate_ref[...], preferred_element_type=F32) + bgate_ref[...]
        ls = _log_sigmoid(z) * (1.0 / GLA_TAU)
        lf_ref[rows, :] = ls[:, 0:256]
        lb_ref[rows, :] = ls[:, 256:512]


def _inproj(h, gamma, w, cos, sin, wgate, bgate, batch, lp):
    r = h.shape[0]
    tm = ROW_TILE
    widths = (256, 384, 384, 384, 384, 256, 256, 512, 512, 256, 256)
    dtypes = (F32,) + (BF16,) * 8 + (F32, F32)

    def rows(wd):
        return pl.BlockSpec((tm, wd), lambda i: (i, 0))

    return pl.pallas_call(
        functools.partial(_inproj_kernel, batch=batch, lp=lp),
        grid=(r // tm,),
        in_specs=[rows(D_MODEL), _const_spec((1, D_MODEL)), _const_spec((D_MODEL, _C_END)),
                  rows(RET_WIDTH), rows(RET_WIDTH), _const_spec((LANES, 512)), _const_spec((1, 512))],
        out_specs=[rows(wd) for wd in widths],
        out_shape=[jax.ShapeDtypeStruct((r, wd), dt) for wd, dt in zip(widths, dtypes)],
        compiler_params=_params("parallel"),
        name="inproj",
    )(h, gamma, w, cos, sin, wgate, bgate)


def _s5_toeplitz_kernel(pwr_ref, pwi_ref, cer_ref, cei_ref, bbr_ref, bbi_ref, d_ref, tt_ref):
    hp = lax.Precision.HIGHEST
    w = CHUNK * S5_P
    lane = lax.broadcasted_iota(jnp.int32, (S5_P, w), 1)
    krow = []
    for dr in range(2):
        pr, pi, cr, ci = pwr_ref[dr], pwi_ref[dr], cer_ref[dr], cei_ref[dr]
        zr = pr * cr - pi * ci
        zi = pr * ci + pi * cr
        krow.append(jnp.dot(bbr_ref[dr], zr, precision=hp, preferred_element_type=F32)
                    - jnp.dot(bbi_ref[dr], zi, precision=hp, preferred_element_type=F32))
    kf = krow[0] + d_ref[...]
    kb = krow[1]
    for s in range(CHUNK):
        right = S5_P * s
        left = S5_P * (CHUNK - 1 - s)
        a = kf if right == 0 else jnp.where(lane >= right, pltpu.roll(kf, right, 1), 0.0)
        b = kb if left == 0 else jnp.where(lane < w - left, pltpu.roll(kb, w - left, 1), 0.0)
        tt_ref[S5_P * s:S5_P * (s + 1), :] = (a + b).astype(BF16)


def _s5_toeplitz(pwr, pwi, cer, cei, bbr, bbi, d_e):
    w = CHUNK * S5_P
    big = pl.BlockSpec((None, 2, S5_N, w), lambda g: (g, 0, 0, 0))
    small = pl.BlockSpec((None, 2, S5_P, S5_N), lambda g: (g, 0, 0, 0))
    return pl.pallas_call(
        _s5_toeplitz_kernel,
        grid=(S5_GROUPS,),
        in_specs=[big, big, big, big, small, small, pl.BlockSpec((None, S5_P, w), lambda g: (g, 0, 0))],
        out_specs=pl.BlockSpec((None, w, w), lambda g: (g, 0, 0)),
        out_shape=jax.ShapeDtypeStruct((S5_GROUPS, w, w), BF16),
        compiler_params=_params("parallel"),
        name="s5_toeplitz",
    )(pwr, pwi, cer, cei, bbr, bbi, d_e)


def _s5_tables(lam_re, lam_im, log_dt, b_re, b_im, c_re, c_im, d):
    c = CHUNK
    g_, n_, p_ = S5_GROUPS, S5_N, S5_P
    dt = jnp.exp(log_dt.astype(F32))[..., None]
    lr = lam_re.astype(F32)
    li = lam_im.astype(F32)
    e = lr * dt
    th = li * dt
    mag = jnp.exp(e)
    a_re = mag * jnp.cos(th)
    a_im = mag * jnp.sin(th)
    den = lr * lr + li * li
    nr = a_re - 1.0
    ni = a_im
    coef_re = (nr * lr + ni * li) / den
    coef_im = (ni * lr - nr * li) / den
    br = b_re.astype(F32)
    bi = b_im.astype(F32)
    bb_re = coef_re[..., None] * br - coef_im[..., None] * bi
    bb_im = coef_re[..., None] * bi + coef_im[..., None] * br
    cr = c_re.astype(F32)
    ci = c_im.astype(F32)
    tau = jnp.arange(c + 1, dtype=F32)[:, None, None, None]
    pw_mag = jnp.exp(tau * e[None])
    pw_re = pw_mag * jnp.cos(tau * th[None])
    pw_im = pw_mag * jnp.sin(tau * th[None])

    z_re = cr[None] * pw_re[:, :, :, None, :] - ci[None] * pw_im[:, :, :, None, :]
    z_im = cr[None] * pw_im[:, :, :, None, :] + ci[None] * pw_re[:, :, :, None, :]
    idx = jnp.arange(c)

    def expand_pw(pw):
        both = jnp.stack([pw[:c, 0], pw[c - 1 - idx, 1]], axis=0)
        both = jnp.transpose(both, (2, 0, 3, 1))
        return jnp.broadcast_to(both[..., None], (g_, 2, n_, c, p_)).reshape(g_, 2, n_, c * p_)

    def expand_c(cm):
        cm = jnp.transpose(cm, (1, 0, 3, 2))
        return jnp.broadcast_to(cm[:, :, :, None, :], (g_, 2, n_, c, p_)).reshape(g_, 2, n_, c * p_)

    bbt_re = jnp.transpose(bb_re, (1, 0, 3, 2))
    bbt_im = jnp.transpose(bb_im, (1, 0, 3, 2))
    d_e = jnp.zeros((g_, p_, c * p_), F32).at[:, :, :p_].set(d.astype(F32)[:, :, None] * jnp.eye(p_, dtype=F32))
    tt = _s5_toeplitz(expand_pw(pw_re), expand_pw(pw_im), expand_c(cr), expand_c(ci), bbt_re, bbt_im, d_e)

    pf_re = pw_re[c - 1 - idx, 0]
    pf_im = pw_im[c - 1 - idx, 0]
    pb_re = pw_re[idx, 1]
    pb_im = pw_im[idx, 1]

    def m_of(p_re_, p_im_, dr):
        m_re = p_re_[:, :, :, None] * bb_re[dr][None] - p_im_[:, :, :, None] * bb_im[dr][None]
        m_im = p_re_[:, :, :, None] * bb_im[dr][None] + p_im_[:, :, :, None] * bb_re[dr][None]
        to = lambda m: jnp.transpose(m, (1, 0, 3, 2)).reshape(g_, c * p_, n_)
        return to(m_re), to(m_im)

    mf_re, mf_im = m_of(pf_re, pf_im, 0)
    mb_re, mb_im = m_of(pb_re, pb_im, 1)
    m4 = jnp.stack([mf_re, mf_im, mb_re, mb_im], axis=1)
    m4 = m4.reshape(g_ // 2, 2, 4, c * p_, n_)
    mz = jnp.zeros_like(m4[:, 0])

    def m_rows(blocks):
        return jnp.transpose(jnp.concatenate(blocks, axis=-1), (0, 2, 1, 3)).reshape(g_ // 2, c * p_, 8 * n_)

    mpair = jnp.concatenate([m_rows([m4[:, 0], mz]), m_rows([mz, m4[:, 1]])], axis=1).astype(BF16)

    def n_of(tsel, dr):
        w_re = z_re[tsel, dr]
        w_im = z_im[tsel, dr]
        to = lambda m: jnp.transpose(m, (1, 3, 0, 2)).reshape(g_, n_, c * p_)
        return to(w_re), to(-w_im)

    nf_re, nf_im = n_of(idx + 1, 0)
    nb_re, nb_im = n_of(c - idx, 1)
    n4 = jnp.stack([nf_re, nf_im, nb_re, nb_im], axis=1)
    n4 = n4.reshape(g_ // 2, 2, 4, n_, c * p_)
    nz = jnp.zeros_like(n4[:, 0])
    npair = jnp.stack([jnp.concatenate([n4[:, 0], nz], axis=-1), jnp.concatenate([nz, n4[:, 1]], axis=-1)], axis=2)
    npair = npair.reshape(g_ // 2, 4 * 2 * n_, 2 * c * p_).astype(BF16)

    dec = jnp.stack([pw_re[c, 0], pw_im[c, 0], pw_re[c, 1], pw_im[c, 1]], axis=0)
    dec = dec.reshape(4, 1, g_ * n_)
    return tt, mpair, npair, dec


S5_PITCH = CHUNK + 4
S5_HALVES = S5_WIDTH // LANES
S5_GPH = LANES // S5_P


def _s5_chunk_block(nch):
    best = 8
    for cand in range(8, 113, 8):
        if nch % cand == 0:
            best = cand
    return best


def _block_transpose8(tiles):
    lane = lax.broadcasted_iota(jnp.int32, tiles[0].shape, 1)
    tiles = list(tiles)
    for dist in (4, 2, 1):
        width = S5_P * dist
        low = (lane & (2 * width - 1)) < width
        for k in range(S5_GPH):
            if k & dist:
                continue
            a, b = tiles[k], tiles[k + dist]
            tiles[k] = jnp.where(low, a, pltpu.roll(b, width, 1))
            tiles[k + dist] = jnp.where(low, pltpu.roll(a, LANES - width, 1), b)
    return tiles


def _s5_in_kernel(u_hbm, m_ref, ucat_ref, o0, o1, o2, o3, xpad, uall, sem):
    i = pl.program_id(0)
    j = pl.program_id(1)
    cb = uall.shape[1]

    @pl.when(j == 0)
    def _():
        def chunk_copy(c, h):
            return pltpu.make_async_copy(
                u_hbm.at[pl.ds((i * cb + c) * CHUNK, CHUNK), pl.ds(h * LANES, LANES)],
                xpad.at[h, pl.ds(c * S5_PITCH, CHUNK), :], sem)

        def start(c, carry):
            for h in range(S5_HALVES):
                chunk_copy(c, h).start()
            return carry

        def wait(c, carry):
            for h in range(S5_HALVES):
                chunk_copy(c, h).wait()
            return carry

        lax.fori_loop(0, cb, start, 0)
        lax.fori_loop(0, cb, wait, 0)

        def dest_tile(jt, carry):
            col = pl.multiple_of(jt * LANES, LANES)
            for h in range(S5_HALVES):
                by_token = [xpad[h, pl.ds(jt * S5_GPH + k, cb, stride=S5_PITCH), :] for k in range(S5_GPH)]
                for go, tile in enumerate(_block_transpose8(by_token)):
                    uall[h * S5_GPH + go, :, pl.ds(col, LANES)] = tile
            return carry

        lax.fori_loop(0, CHUNK // S5_GPH, dest_tile, 0)

    u2 = jnp.concatenate([uall[2 * j], uall[2 * j + 1]], axis=1)
    ucat_ref[...] = u2
    s = jnp.dot(u2.astype(BF16), m_ref[...], preferred_element_type=F32)
    o0[...] = s[:, 0:128]
    o1[...] = s[:, 128:256]
    o2[...] = s[:, 256:384]
    o3[...] = s[:, 384:512]


def _s5_in(u, mpair):
    nch = u.shape[0] // CHUNK
    cb = _s5_chunk_block(nch)
    w = CHUNK * S5_P
    return pl.pallas_call(
        _s5_in_kernel,
        grid=(nch // cb, S5_GROUPS // 2),
        in_specs=[pl.BlockSpec(memory_space=pl.ANY),
                  pl.BlockSpec((None, 2 * w, 512), lambda i, j: (j, 0, 0))],
        out_specs=[pl.BlockSpec((cb, 2 * w), lambda i, j: (i, j))] + [pl.BlockSpec((cb, LANES), lambda i, j: (i, j))] * 4,
        out_shape=[jax.ShapeDtypeStruct((nch, S5_GROUPS * w), F32)] + [jax.ShapeDtypeStruct((nch, 1024), F32)] * 4,
        scratch_shapes=[pltpu.VMEM((S5_HALVES, cb * S5_PITCH, LANES), F32), pltpu.VMEM((S5_GROUPS, cb, w), F32),
                        pltpu.SemaphoreType.DMA(())],
        compiler_params=_params("arbitrary", "arbitrary"),
        name="s5_in",
    )(u, mpair)


def _s5_scan_kernel(sfr, sfi, sbr, sbi, dfr, dfi, dbr, dbi, hfr, hfi, hbr, hbi):
    nc, b, _ = sfr.shape
    a_fr = dfr[...]
    a_fi = dfi[...]
    a_br = dbr[...]
    a_bi = dbi[...]
    zero = jnp.zeros((b, LANES), F32)

    def fwd(c, carry):
        hr, hi = carry
        hfr[c] = hr
        hfi[c] = hi
        return (a_fr * hr - a_fi * hi + sfr[c], a_fr * hi + a_fi * hr + sfi[c])

    def bwd(i, carry):
        c = nc - 1 - i
        hr, hi = carry
        hbr[c] = hr
        hbi[c] = hi
        return (a_br * hr - a_bi * hi + sbr[c], a_br * hi + a_bi * hr + sbi[c])

    lax.fori_loop(0, nc, fwd, (zero, zero))
    lax.fori_loop(0, nc, bwd, (zero, zero))


def _s5_scan(s4, dec, nc, b):
    blk = pl.BlockSpec((nc, b, LANES), lambda j: (0, 0, j))
    dspecs = [pl.BlockSpec((None, 1, LANES), functools.partial(lambda j, k: (k, 0, j), k=k)) for k in range(4)]
    return pl.pallas_call(
        _s5_scan_kernel,
        grid=(S5_GROUPS // 2,),
        in_specs=[blk] * 4 + dspecs,
        out_specs=[blk] * 4,
        out_shape=[jax.ShapeDtypeStruct((nc, b, 1024), F32)] * 4,
        compiler_params=_params("parallel"),
        name="s5_scan",
    )(*s4, dec, dec, dec, dec)


def _s5_out_kernel(u_ref, tt_ref, h0, h1, h2, h3, n_ref, y_hbm, yall, ypad, sem):
    i = pl.program_id(0)
    j = pl.program_id(1)
    cb = yall.shape[1]
    w = CHUNK * S5_P
    u = u_ref[...].astype(BF16)
    y0 = jnp.dot(u[:, 0:w], tt_ref[0], preferred_element_type=F32)
    y1 = jnp.dot(u[:, w:2 * w], tt_ref[1], preferred_element_type=F32)
    hcat = jnp.concatenate([h0[...], h1[...], h2[...], h3[...]], axis=1).astype(BF16)
    yh = jnp.dot(hcat, n_ref[...], preferred_element_type=F32)
    yall[2 * j] = y0 + yh[:, 0:w]
    yall[2 * j + 1] = y1 + yh[:, w:2 * w]

    @pl.when(j == pl.num_programs(1) - 1)
    def _():
        def src_tile(jt, carry):
            col = pl.multiple_of(jt * LANES, LANES)
            for h in range(S5_HALVES):
                by_group = [yall[h * S5_GPH + go, :, pl.ds(col, LANES)] for go in range(S5_GPH)]
                for k, tile in enumerate(_block_transpose8(by_group)):
                    ypad[h, pl.ds(jt * S5_GPH + k, cb, stride=S5_PITCH), :] = tile
            return carry

        lax.fori_loop(0, CHUNK // S5_GPH, src_tile, 0)

        def chunk_copy(c, h):
            return pltpu.make_async_copy(
                ypad.at[h, pl.ds(c * S5_PITCH, CHUNK), :],
                y_hbm.at[pl.ds((i * cb + c) * CHUNK, CHUNK), pl.ds(h * LANES, LANES)], sem)

        def start(c, carry):
            for h in range(S5_HALVES):
                chunk_copy(c, h).start()
            return carry

        def wait(c, carry):
            for h in range(S5_HALVES):
                chunk_copy(c, h).wait()
            return carry

        lax.fori_loop(0, cb, start, 0)
        lax.fori_loop(0, cb, wait, 0)


def _s5_out(ucat, tt, h4, npair):
    nch = ucat.shape[0]
    cb = _s5_chunk_block(nch)
    w = CHUNK * S5_P
    hblk = pl.BlockSpec((cb, LANES), lambda i, j: (i, j))
    return pl.pallas_call(
        _s5_out_kernel,
        grid=(nch // cb, S5_GROUPS // 2),
        in_specs=[pl.BlockSpec((cb, 2 * w), lambda i, j: (i, j)),
                  pl.BlockSpec((2, w, w), lambda i, j: (j, 0, 0)),
                  hblk, hblk, hblk, hblk,
                  pl.BlockSpec((None, 512, 2 * w), lambda i, j: (j, 0, 0))],
        out_specs=pl.BlockSpec(memory_space=pl.ANY),
        out_shape=jax.ShapeDtypeStruct((nch * CHUNK, S5_WIDTH), F32),
        scratch_shapes=[pltpu.VMEM((S5_GROUPS, cb, w), F32), pltpu.VMEM((S5_HALVES, cb * S5_PITCH, LANES), F32),
                        pltpu.SemaphoreType.DMA(())],
        compiler_params=_params("arbitrary", "arbitrary"),
        name="s5_out",
    )(ucat, tt, *h4, npair)


def _s5_mixer(u, tables, batch, lp):
    tt, mpair, npair, dec = tables
    nc = lp // CHUNK
    ucat, *s4 = _s5_in(u, mpair)
    s4 = [jnp.transpose(s.reshape(batch, nc, 1024), (1, 0, 2)) for s in s4]
    h4 = _s5_scan(s4, dec, nc, batch)
    h4 = [jnp.transpose(h, (1, 0, 2)).reshape(batch * nc, 1024) for h in h4]
    return _s5_out(ucat, tt, h4, npair)


def _cumsum_chunks(x):
    n = x.shape[0]
    r = lax.broadcasted_iota(jnp.int32, (n, n), 0)
    s = lax.broadcasted_iota(jnp.int32, (n, n), 1)
    tri = jnp.where((s <= r) & ((s // CHUNK) == (r // CHUNK)), 1.0, 0.0).astype(BF16)
    hi = x.astype(BF16)
    lo = (x - hi.astype(F32)).astype(BF16)
    return jnp.dot(tri, hi, preferred_element_type=F32) + jnp.dot(tri, lo, preferred_element_type=F32)


def _head_masks(npair, vp):
    kl = lax.broadcasted_iota(jnp.int32, (1, LANES), 1)
    km = [(kl < 64), (kl >= 64)]
    vl = lax.broadcasted_iota(jnp.int32, (1, vp), 1)
    vm = [(vl < vp // 2), (vl >= vp // 2)]
    vrow = lax.broadcasted_iota(jnp.int32, (vp, LANES), 0)
    kcol = lax.broadcasted_iota(jnp.int32, (vp, LANES), 1)
    bd = (vrow >= vp // 2) == (kcol >= 64)
    return km, vm, bd


def _la_bwd_kernel(k_all, v_all, lb_all, sb_all, st_all, *, npair, vp, nsub, nseq):
    g = LA_GROUP
    c = CHUNK
    gc = g * c

    @pl.when(pl.program_id(1) == 0)
    def _():
        st_all[...] = jnp.zeros_like(st_all)

    km, vm, bd = _head_masks(npair, vp)

    def sub_block(t, carry):
        for s in range(nseq):
            one_seq(k_all.at[s], v_all.at[s], lb_all.at[s], sb_all.at[s], st_all.at[s], nsub - 1 - t)
        return carry

    def one_seq(k_ref, v_ref, lb_ref, sb_ref, st_ref, sub):
        base = pl.multiple_of(sub * gc, gc)
        kblk = k_ref[pl.ds(base, gc), :].astype(F32)
        vblk = v_ref[pl.ds(base, gc), :]
        lblk = lb_ref[pl.ds(base, gc), :]
        cblk = _cumsum_chunks(lblk)
        kb2blk = (kblk * jnp.exp(cblk - lblk)).astype(BF16)
        states = [st_ref[p] for p in range(npair)]
        for gi in reversed(range(g)):
            rows = slice(gi * c, (gi + 1) * c)
            kb2 = kb2blk[rows, :]
            dec = jnp.exp(cblk[gi * c + c - 1:(gi + 1) * c, :])
            vv = vblk[rows, :]
            for p in range(npair):
                sb_ref[sub * g + gi, p] = states[p].astype(BF16)
                upd = lax.dot_general(vv[:, p * vp:(p + 1) * vp], kb2[:, p * LANES:(p + 1) * LANES],
                                      (((0,), (0,)), ((), ())), preferred_element_type=F32)
                states[p] = states[p] * dec[:, p * LANES:(p + 1) * LANES] + jnp.where(bd, upd, 0.0)
        for p in range(npair):
            st_ref[p] = states[p]

    lax.fori_loop(0, nsub, sub_block, 0)


def _head_rms_gate(o, gate, bdv, dv):
    ms = _split_dot(o * o, bdv) * (1.0 / dv)
    return (o * lax.rsqrt(ms + NORM_EPS) * _silu(gate.astype(F32))).astype(BF16)


def _la_fwd_kernel(q_all, k_all, v_all, lf_all, lb_all, sb_all, gate_all, bdv_ref, o_all, st_all,
                   *, npair, vp, dv, nsub, nseq):
    g = LA_GROUP
    c = CHUNK
    gc = g * c

    @pl.when(pl.program_id(1) == 0)
    def _():
        st_all[...] = jnp.zeros_like(st_all)

    km, vm, bd = _head_masks(npair, vp)
    ri = lax.broadcasted_iota(jnp.int32, (c, LANES), 0)
    cj = lax.broadcasted_iota(jnp.int32, (c, LANES), 1) & (c - 1)
    lower = ri >= cj
    upper = ri <= cj
    mid = c // 2
    nt = (((1,), (1,)), ((), ()))
    tn = (((0,), (0,)), ((), ()))

    def sub_block(sub, carry):
        for s in range(nseq):
            one_seq(q_all.at[s], k_all.at[s], v_all.at[s], lf_all.at[s], lb_all.at[s], sb_all.at[s],
                    gate_all.at[s], o_all.at[s], st_all.at[s], sub)
        return carry

    def one_seq(q_ref, k_ref, v_ref, lf_ref, lb_ref, sb_ref, gate_ref, o_ref, st_ref, sub):
        base = pl.multiple_of(sub * gc, gc)
        qblk = q_ref[pl.ds(base, gc), :].astype(F32)
        kblk = k_ref[pl.ds(base, gc), :].astype(F32)
        vblk = v_ref[pl.ds(base, gc), :]
        lfblk = lf_ref[pl.ds(base, gc), :]
        lbblk = lb_ref[pl.ds(base, gc), :]
        w = lfblk.shape[1]
        csblk = _cumsum_chunks(jnp.concatenate([lfblk, lbblk], axis=1))
        states = [st_ref[p] for p in range(npair)]
        outs = []
        for gi in range(g):
            rows = slice(gi * c, (gi + 1) * c)
            qq = qblk[rows, :]
            kk = kblk[rows, :]
            vv = vblk[rows, :]
            cf = csblk[rows, :w]
            cbi = csblk[rows, w:]
            cb = cbi - lbblk[rows, :]
            mf = cf[mid:mid + 1, :]
            mb = cb[mid:mid + 1, :]
            tf = cf[c - 1:c, :]
            tb = cbi[c - 1:c, :]
            qf = (qq * jnp.exp(cf - mf)).astype(BF16)
            kf = (kk * jnp.exp(mf - cf)).astype(BF16)
            qb = (qq * jnp.exp(mb - cb)).astype(BF16)
            kb = (kk * jnp.exp(cb - mb)).astype(BF16)
            zk = jnp.zeros((c, LANES), BF16)
            q2 = jnp.concatenate([qq * jnp.exp(cf), qq * jnp.exp(tb - cb)], axis=0).astype(BF16)
            kf2 = (kk * jnp.exp(tf - cf)).astype(BF16)
            dec = jnp.exp(tf)
            pair_out = []
            for p in range(npair):
                ks = slice(p * LANES, (p + 1) * LANES)
                vsl = slice(p * vp, (p + 1) * vp)
                kfp = kf[:, ks]
                kbp = kb[:, ks]
                k2f = jnp.concatenate([jnp.where(km[0], kfp, zk), jnp.where(km[1], kfp, zk)], axis=0)
                k2b = jnp.concatenate([jnp.where(km[0], kbp, zk), jnp.where(km[1], kbp, zk)], axis=0)
                sf = lax.dot_general(qf[:, ks], k2f, nt, preferred_element_type=F32)
                sb = lax.dot_general(qb[:, ks], k2b, nt, preferred_element_type=F32)
                sc = (jnp.where(lower, sf, 0.0) + jnp.where(upper, sb, 0.0)).astype(BF16)
                vp_ = vv[:, vsl]
                zero = jnp.zeros_like(vp_)
                v2 = jnp.concatenate([jnp.where(vm[0], vp_, zero), jnp.where(vm[1], vp_, zero)], axis=0)
                o = jnp.dot(sc, v2, preferred_element_type=F32)
                st = states[p]
                o = o + lax.dot_general(q2[0:c, ks], st.astype(BF16), nt, preferred_element_type=F32)
                o = o + lax.dot_general(q2[c:2 * c, ks], sb_ref[sub * g + gi, p], nt, preferred_element_type=F32)
                upd = lax.dot_general(vp_, kf2[:, ks], tn, preferred_element_type=F32)
                states[p] = st * dec[:, ks] + jnp.where(bd, upd, 0.0)
                pair_out.append(o)
            outs.append(jnp.concatenate(pair_out, axis=1) if npair > 1 else pair_out[0])
        for p in range(npair):
            st_ref[p] = states[p]
        o = jnp.concatenate(outs, axis=0)
        o_ref[pl.ds(base, gc), :] = _head_rms_gate(o, gate_ref[pl.ds(base, gc), :], bdv_ref[...], dv)

    lax.fori_loop(0, nsub, sub_block, 0)


def _head_block_ones(wv, seg):
    vr = lax.broadcasted_iota(jnp.int32, (wv, wv), 0) // seg
    vc = lax.broadcasted_iota(jnp.int32, (wv, wv), 1) // seg
    return (vr == vc).astype(BF16)


def _sub_blocks(lp, rows):
    n = lp // rows
    for cand in (5, 4, 3, 2):
        if n % cand == 0:
            return cand
    return 1


def _seqs_per_step(batch):
    for n in (4, 2):
        if batch % n == 0:
            return n
    return 1


def _linear_attention(q, k, v, lf, lb, gate, batch, lp, *, npair, vp, dv):
    gc = LA_GROUP * CHUNK
    nsub = _sub_blocks(lp, gc)
    rows = nsub * gc
    nblk = lp // rows
    nc = lp // CHUNK
    wk = npair * LANES
    wv = npair * vp
    q3 = q.reshape(batch, lp, wk)
    k3 = k.reshape(batch, lp, wk)
    v3 = v.reshape(batch, lp, wv)
    g3 = gate.reshape(batch, lp, wv)
    lf3 = lf.reshape(batch, lp, wk)
    lb3 = lb.reshape(batch, lp, wk)
    nseq = _seqs_per_step(batch)
    fwd = lambda b, j: (b, j, 0)
    rev = lambda b, j: (b, nblk - 1 - j, 0)
    sblk = (nseq, nsub * LA_GROUP, npair, vp, LANES)
    kblk = (nseq, rows, wk)
    vblk = (nseq, rows, wv)

    sb = pl.pallas_call(
        functools.partial(_la_bwd_kernel, npair=npair, vp=vp, nsub=nsub, nseq=nseq),
        grid=(batch // nseq, nblk),
        in_specs=[pl.BlockSpec(kblk, rev), pl.BlockSpec(vblk, rev), pl.BlockSpec(kblk, rev)],
        out_specs=pl.BlockSpec(sblk, lambda b, j: (b, nblk - 1 - j, 0, 0, 0)),
        out_shape=jax.ShapeDtypeStruct((batch, nc, npair, vp, LANES), BF16),
        scratch_shapes=[pltpu.VMEM((nseq, npair, vp, LANES), F32)],
        compiler_params=_params("parallel", "arbitrary"),
        name="la_bwd_states",
    )(k3, v3, lb3)

    o = pl.pallas_call(
        functools.partial(_la_fwd_kernel, npair=npair, vp=vp, dv=dv, nsub=nsub, nseq=nseq),
        grid=(batch // nseq, nblk),
        in_specs=[pl.BlockSpec(kblk, fwd), pl.BlockSpec(kblk, fwd), pl.BlockSpec(vblk, fwd),
                  pl.BlockSpec(kblk, fwd), pl.BlockSpec(kblk, fwd),
                  pl.BlockSpec(sblk, lambda b, j: (b, j, 0, 0, 0)),
                  pl.BlockSpec(vblk, fwd),
                  pl.BlockSpec((wv, wv), lambda b, j: (0, 0))],
        out_specs=pl.BlockSpec(vblk, fwd),
        out_shape=jax.ShapeDtypeStruct((batch, lp, wv), BF16),
        scratch_shapes=[pltpu.VMEM((nseq, npair, vp, LANES), F32)],
        compiler_params=_params("parallel", "arbitrary"),
        name="la_fwd",
    )(q3, k3, v3, lf3, lb3, sb, g3, _head_block_ones(wv, vp // 2))
    return o.reshape(batch * lp, wv)


RET_CHUNK = 256


def _ret_decay_terms(lg_ref):
    c = RET_CHUNK
    lg = lg_ref[...]
    i = lax.broadcasted_iota(jnp.int32, (c, 1), 0).astype(F32)
    return dict(q_f=jnp.exp((i + 1.0) * lg), q_b=jnp.exp((float(c) - i) * lg),
                k_f=jnp.exp((float(c - 1) - i) * lg), k_b=jnp.exp(i * lg), dec=jnp.exp(float(c) * lg))


def _ret_bwd_kernel(k_all, v_all, lg_ref, sb_all, st_all, *, npair, nsub, nseq):
    c = RET_CHUNK
    tn = (((0,), (0,)), ((), ()))

    @pl.when(pl.program_id(1) == 0)
    def _():
        st_all[...] = jnp.zeros_like(st_all)

    _, _, bd = _head_masks(npair, LANES)
    t = _ret_decay_terms(lg_ref)

    def sub_block(it, carry):
        for s in range(nseq):
            one_seq(k_all.at[s], v_all.at[s], sb_all.at[s], st_all.at[s], nsub - 1 - it)
        return carry

    def one_seq(k_ref, v_ref, sb_ref, st_ref, sub):
        base = pl.multiple_of(sub * c, c)
        kb = (k_ref[pl.ds(base, c), :].astype(F32) * t["k_b"]).astype(BF16)
        vv = v_ref[pl.ds(base, c), :]
        for p in range(npair):
            ks = slice(p * LANES, (p + 1) * LANES)
            st = st_ref[p]
            sb_ref[sub, p] = st.astype(BF16)
            upd = lax.dot_general(vv[:, ks], kb[:, ks], tn, preferred_element_type=F32)
            st_ref[p] = st * t["dec"][:, ks] + jnp.where(bd, upd, 0.0)

    lax.fori_loop(0, nsub, sub_block, 0)


def _ret_fwd_kernel(q_all, k_all, v_all, lg_ref, dmask_ref, sb_all, gate_all, bdv_ref, o_all, st_all,
                    *, npair, nsub, dv, nseq):
    c = RET_CHUNK
    nt = (((1,), (1,)), ((), ()))
    tn = (((0,), (0,)), ((), ()))

    @pl.when(pl.program_id(1) == 0)
    def _():
        st_all[...] = jnp.zeros_like(st_all)

    km, vm, bd = _head_masks(npair, LANES)
    t = _ret_decay_terms(lg_ref)

    def sub_block(sub, carry):
        for s in range(nseq):
            one_seq(q_all.at[s], k_all.at[s], v_all.at[s], sb_all.at[s], gate_all.at[s], o_all.at[s], st_all.at[s], sub)
        return carry

    def one_seq(q_ref, k_ref, v_ref, sb_ref, gate_ref, o_ref, st_ref, sub):
        base = pl.multiple_of(sub * c, c)
        qb16 = q_ref[pl.ds(base, c), :]
        kb16 = k_ref[pl.ds(base, c), :]
        vv = v_ref[pl.ds(base, c), :]
        qq = qb16.astype(F32)
        kk = kb16.astype(F32)
        q2 = jnp.concatenate([qq * t["q_f"], qq * t["q_b"]], axis=0).astype(BF16)
        kf2 = (kk * t["k_f"]).astype(BF16)
        zero = jnp.zeros((c, LANES), BF16)
        pair_out = []
        for p in range(npair):
            ks = slice(p * LANES, (p + 1) * LANES)
            kp = kb16[:, ks]
            vp_ = vv[:, ks]
            k2 = jnp.concatenate([jnp.where(km[0], kp, zero), jnp.where(km[1], kp, zero)], axis=0)
            v2 = jnp.concatenate([jnp.where(vm[0], vp_, zero), jnp.where(vm[1], vp_, zero)], axis=0)
            s = lax.dot_general(qb16[:, ks], k2, nt, preferred_element_type=F32)
            o = jnp.dot((s * dmask_ref[p]).astype(BF16), v2, preferred_element_type=F32)
            st = st_ref[p]
            o = o + lax.dot_general(q2[0:c, ks], st.astype(BF16), nt, preferred_element_type=F32)
            o = o + lax.dot_general(q2[c:2 * c, ks], sb_ref[sub, p], nt, preferred_element_type=F32)
            upd = lax.dot_general(vp_, kf2[:, ks], tn, preferred_element_type=F32)
            st_ref[p] = st * t["dec"][:, ks] + jnp.where(bd, upd, 0.0)
            pair_out.append(o)
        o = jnp.concatenate(pair_out, axis=1)
        o_ref[pl.ds(base, c), :] = _head_rms_gate(o, gate_ref[pl.ds(base, c), :], bdv_ref[...], dv)

    lax.fori_loop(0, nsub, sub_block, 0)


def _retention(q, k, v, gate, batch, lp):
    c = RET_CHUNK
    npair = RET_HEADS // 2
    w = npair * LANES
    nsub = _sub_blocks(lp, c)
    rows = nsub * c
    nblk = lp // rows
    log_gamma = jnp.log1p(-jnp.exp2(-5.0 - jnp.arange(RET_HEADS, dtype=F32)))
    lg = jnp.repeat(log_gamma, RET_DIM)[None]
    dist = jnp.abs(jnp.arange(c)[:, None] - (jnp.arange(2 * c) % c)[None, :]).astype(F32)
    dmask = jnp.exp(dist[None] * jnp.repeat(log_gamma, c).reshape(npair, 1, 2 * c))
    q3, k3, v3, g3 = (a.reshape(batch, lp, w) for a in (q, k, v, gate))
    nseq = _seqs_per_step(batch)
    fwd = lambda b, j: (b, j, 0)
    rev = lambda b, j: (b, nblk - 1 - j, 0)
    sblk = (nseq, nsub, npair, LANES, LANES)
    xblk = (nseq, rows, w)
    lgspec = pl.BlockSpec((1, w), lambda b, j: (0, 0))

    sb = pl.pallas_call(
        functools.partial(_ret_bwd_kernel, npair=npair, nsub=nsub, nseq=nseq),
        grid=(batch // nseq, nblk),
        in_specs=[pl.BlockSpec(xblk, rev), pl.BlockSpec(xblk, rev), lgspec],
        out_specs=pl.BlockSpec(sblk, lambda b, j: (b, nblk - 1 - j, 0, 0, 0)),
        out_shape=jax.ShapeDtypeStruct((batch, lp // c, npair, LANES, LANES), BF16),
        scratch_shapes=[pltpu.VMEM((nseq, npair, LANES, LANES), F32)],
        compiler_params=_params("parallel", "arbitrary"),
        name="ret_bwd_states",
    )(k3, v3, lg)

    o = pl.pallas_call(
        functools.partial(_ret_fwd_kernel, npair=npair, nsub=nsub, dv=RET_DIM, nseq=nseq),
        grid=(batch // nseq, nblk),
        in_specs=[pl.BlockSpec(xblk, fwd), pl.BlockSpec(xblk, fwd), pl.BlockSpec(xblk, fwd), lgspec,
                  pl.BlockSpec((npair, c, 2 * c), lambda b, j: (0, 0, 0)),
                  pl.BlockSpec(sblk, lambda b, j: (b, j, 0, 0, 0)),
                  pl.BlockSpec(xblk, fwd),
                  pl.BlockSpec((w, w), lambda b, j: (0, 0))],
        out_specs=pl.BlockSpec(xblk, fwd),
        out_shape=jax.ShapeDtypeStruct((batch, lp, w), BF16),
        scratch_shapes=[pltpu.VMEM((nseq, npair, LANES, LANES), F32)],
        compiler_params=_params("parallel", "arbitrary"),
        name="ret_fwd",
    )(q3, k3, v3, lg, dmask, sb, g3, _head_block_ones(w, RET_DIM))
    return o.reshape(batch * lp, w)


PACKED = D_MODEL // 2


def _pack_rows(x):
    bits = lax.bitcast_convert_type(x.astype(BF16).astype(F32), jnp.int32)
    return bits[:, :PACKED] | lax.shift_right_logical(bits[:, PACKED:], 16)


def _unpack_rows(p):
    hi = lax.bitcast_convert_type(p & jnp.int32(-65536), F32)
    lo = lax.bitcast_convert_type(lax.shift_left(p, jnp.int32(16)), F32)
    return jnp.concatenate([hi, lo], axis=1)


_RT_E1, _RT_E2, _RT_W1, _RT_W2, _RT_R1, _RT_R2 = range(6)


ROW_PARTS = 2


def _mix_out(rows, ya_ref, yb_ref, yc_ref, h_ref, wglu_ref, wout_ref, g_ref):
    ya = _gelu_tanh(ya_ref[rows, :].astype(F32))
    gl = jnp.dot(ya.astype(BF16), wglu_ref[...], preferred_element_type=F32)
    ya = (ya * _sigmoid(gl)).astype(BF16)
    y = jnp.concatenate([ya, yb_ref[rows, :], yc_ref[rows, :]], axis=1)
    h = h_ref[rows, :] + jnp.dot(y, wout_ref[...], preferred_element_type=F32)
    ms = jnp.mean(h * h, axis=-1, keepdims=True)
    return h, h * lax.rsqrt(ms + NORM_EPS) * g_ref[...]


def _outproj_kernel(ya_ref, yb_ref, yc_ref, h_ref, wglu_ref, wout_ref, g_ref, hout_ref, hn_ref):
    h, hn = _mix_out(slice(None), ya_ref, yb_ref, yc_ref, h_ref, wglu_ref, wout_ref, g_ref)
    hout_ref[...] = h
    hn_ref[...] = hn.astype(BF16)


def _outproj_router_kernel(ya_ref, yb_ref, yc_ref, h_ref, wglu_ref, wout_ref, g_ref, wrh_ref, wrl_ref, before_ref,
                           hout_ref, hn_ref, route_ref, cnt_ref, carry_ref):
    @pl.when(pl.program_id(0) == 0)
    def _():
        carry_ref[...] = jnp.zeros_like(carry_ref)

    tp = h_ref.shape[0] // ROW_PARTS
    total = carry_ref[...]
    for part in range(ROW_PARTS):
        rows = slice(part * tp, (part + 1) * tp)
        h, hn = _mix_out(rows, ya_ref, yb_ref, yc_ref, h_ref, wglu_ref, wout_ref, g_ref)
        hout_ref[rows, :] = h
        hn_ref[rows, :] = _pack_rows(hn)
        hi = hn.astype(BF16)
        lo = (hn - hi.astype(F32)).astype(BF16)
        wrh = wrh_ref[...]
        logits = (jnp.dot(hi, wrh, preferred_element_type=F32) + jnp.dot(lo, wrh, preferred_element_type=F32)
                  + jnp.dot(hi, wrl_ref[...], preferred_element_type=F32))
        lane = lax.broadcasted_iota(jnp.int32, logits.shape, 1)
        neg = jnp.float32(-jnp.inf)
        logits = jnp.where(lane < N_EXPERTS, logits, neg)
        v1 = jnp.max(logits, axis=-1, keepdims=True)
        i1 = jnp.min(jnp.where(logits == v1, lane, LANES), axis=-1, keepdims=True)
        m1 = lane == i1
        l2 = jnp.where(m1, neg, logits)
        v2 = jnp.max(l2, axis=-1, keepdims=True)
        i2 = jnp.min(jnp.where(l2 == v2, lane, LANES), axis=-1, keepdims=True)
        m2 = lane == i2
        e = jnp.exp(v2 - v1)
        w1 = 1.0 / (1.0 + e)
        w2 = e * w1
        chosen = jnp.where(m1 | m2, 1.0, 0.0)
        prefix = jnp.dot(before_ref[...], chosen.astype(BF16), preferred_element_type=F32) + total
        r1 = jnp.sum(jnp.where(m1, prefix, 0.0), axis=-1, keepdims=True)
        r2 = jnp.sum(jnp.where(m2, prefix, 0.0), axis=-1, keepdims=True)
        total = total + jnp.sum(chosen, axis=0, keepdims=True)
        rec = jnp.zeros_like(logits)
        for ln, val in ((_RT_E1, i1.astype(F32)), (_RT_E2, i2.astype(F32)), (_RT_W1, w1), (_RT_W2, w2),
                        (_RT_R1, r1), (_RT_R2, r2)):
            rec = jnp.where(lane == ln, val, rec)
        route_ref[rows, :] = rec
    carry_ref[...] = total
    cnt_ref[...] = total


def _outproj(ya, yb, yc, h, wglu, wout, gamma, wr=None):
    r = h.shape[0]
    tm = ROW_TILE
    with_router = wr is not None

    def rows(wd):
        return pl.BlockSpec((tm, wd), lambda i: (i, 0))

    in_specs = [rows(256), rows(384), rows(512), rows(D_MODEL), _const_spec((256, 256)),
                _const_spec((1152, D_MODEL)), _const_spec((1, D_MODEL))]
    args = [ya, yb, yc, h, wglu, wout, gamma]
    if not with_router:
        return pl.pallas_call(
            _outproj_kernel,
            grid=(r // tm,),
            in_specs=in_specs, out_specs=[rows(D_MODEL), rows(D_MODEL)],
            out_shape=[jax.ShapeDtypeStruct((r, D_MODEL), F32), jax.ShapeDtypeStruct((r, D_MODEL), BF16)],
            input_output_aliases={3: 0},
            compiler_params=_params("parallel"),
            name="outproj",
        )(*args)
    tp = tm // ROW_PARTS
    earlier = (lax.broadcasted_iota(jnp.int32, (tp, tp), 1) < lax.broadcasted_iota(jnp.int32, (tp, tp), 0)).astype(BF16)
    return pl.pallas_call(
        _outproj_router_kernel,
        grid=(r // tm,),
        in_specs=in_specs + [_const_spec((D_MODEL, LANES)), _const_spec((D_MODEL, LANES)), _const_spec((tp, tp))],
        out_specs=[rows(D_MODEL), rows(PACKED), rows(LANES), _const_spec((1, LANES))],
        out_shape=[jax.ShapeDtypeStruct((r, D_MODEL), F32), jax.ShapeDtypeStruct((r, PACKED), jnp.int32),
                   jax.ShapeDtypeStruct((r, LANES), F32), jax.ShapeDtypeStruct((1, LANES), F32)],
        scratch_shapes=[pltpu.VMEM((1, LANES), F32)],
        input_output_aliases={3: 0},
        compiler_params=_params("arbitrary"),
        name="outproj_router",
    )(*args, wr[0], wr[1], earlier)


FF_TILE = 2816


def _ffn_kernel(hn_ref, h_ref, wg_ref, wu_ref, wd_ref, o_ref, acc_ref):
    j = pl.program_id(1)
    hn = hn_ref[...]
    a = _silu(jnp.dot(hn, wg_ref[...], preferred_element_type=F32)) * jnp.dot(hn, wu_ref[...], preferred_element_type=F32)
    y = jnp.dot(a.astype(BF16), wd_ref[...], preferred_element_type=F32)

    @pl.when(j == 0)
    def _():
        acc_ref[...] = h_ref[...] + y

    @pl.when(j != 0)
    def _():
        acc_ref[...] += y

    @pl.when(j == pl.num_programs(1) - 1)
    def _():
        o_ref[...] = acc_ref[...]


def _ffn(hn, h, wg, wu, wd):
    r = h.shape[0]
    tm = ROW_TILE
    nf = D_FF // FF_TILE
    return pl.pallas_call(
        _ffn_kernel,
        grid=(r // tm, nf),
        in_specs=[pl.BlockSpec((tm, D_MODEL), lambda i, j: (i, 0)),
                  pl.BlockSpec((tm, D_MODEL), lambda i, j: (i, 0)),
                  pl.BlockSpec((D_MODEL, FF_TILE), lambda i, j: (0, j)),
                  pl.BlockSpec((D_MODEL, FF_TILE), lambda i, j: (0, j)),
                  pl.BlockSpec((FF_TILE, D_MODEL), lambda i, j: (j, 0))],
        out_specs=pl.BlockSpec((tm, D_MODEL), lambda i, j: (i, 0)),
        out_shape=jax.ShapeDtypeStruct((r, D_MODEL), F32),
        scratch_shapes=[pltpu.VMEM((tm, D_MODEL), F32)],
        input_output_aliases={1: 0},
        compiler_params=_params("parallel", "arbitrary"),
        name="ffn",
    )(hn, h, wg, wu, wd)


MOE_BLOCK = 512
GATHER_TILE = 256


def _route_meta(route, cnt, r):
    bm = MOE_BLOCK
    nb = 2 * r // bm + N_EXPERTS
    e1 = route[:, _RT_E1].astype(jnp.int32)
    e2 = route[:, _RT_E2].astype(jnp.int32)
    counts = cnt[0, :N_EXPERTS].astype(jnp.int32)
    padded = ((counts + bm - 1) // bm) * bm
    ends = jnp.cumsum(padded)
    starts = ends - padded
    pos1 = starts[e1] + route[:, _RT_R1].astype(jnp.int32)
    pos2 = starts[e2] + route[:, _RT_R2].astype(jnp.int32)
    n_used = (ends[-1] // bm).astype(jnp.int32)
    blk = jnp.arange(nb, dtype=jnp.int32)
    blk = jnp.minimum(blk, n_used - 1)
    block_expert = jnp.sum((blk[:, None] * bm >= ends[None, :]).astype(jnp.int32), axis=1)
    block_expert = jnp.minimum(block_expert, N_EXPERTS - 1).astype(jnp.int32)
    return pos1, pos2, block_expert, n_used.reshape(1), nb


def _dispatch_kernel(p1_ref, p2_ref, hn_ref, xs_in_ref, xs_ref, stage, sems):
    del xs_in_ref
    n = hn_ref.shape[0]
    s = pl.program_id(0)
    last = pl.num_programs(0) - 1
    slot = lax.rem(s, 2)

    def drain(sl):
        def body(r, c):
            row = pltpu.make_async_copy(stage.at[sl, pl.ds(0, 1), :], xs_ref.at[pl.ds(0, 1), :], sems.at[sl])
            row.wait()
            row.wait()
            return c
        lax.fori_loop(0, n, body, 0, unroll=8)

    @pl.when(s >= 2)
    def _():
        drain(slot)

    stage[slot] = hn_ref[...]
    for r in range(n):
        src = stage.at[slot, pl.ds(r, 1), :]
        pltpu.make_async_copy(src, xs_ref.at[pl.ds(p1_ref[0, 0, r], 1), :], sems.at[slot]).start()
        pltpu.make_async_copy(src, xs_ref.at[pl.ds(p2_ref[0, 0, r], 1), :], sems.at[slot]).start()

    @pl.when(s == last)
    def _():
        @pl.when(s >= 1)
        def _():
            drain(1 - slot)
        drain(slot)


def _dispatch(hn, pos1, pos2, nb):
    r = hn.shape[0]
    t = GATHER_TILE
    p = nb * MOE_BLOCK
    idx = pl.BlockSpec((1, 1, t), lambda i: (i, 0, 0), memory_space=pltpu.SMEM)
    return pl.pallas_call(
        _dispatch_kernel,
        grid=(r // t,),
        in_specs=[idx, idx, pl.BlockSpec((t, PACKED), lambda i: (i, 0)), pl.BlockSpec(memory_space=pl.ANY)],
        out_specs=pl.BlockSpec(memory_space=pl.ANY),
        out_shape=jax.ShapeDtypeStruct((p, PACKED), jnp.int32),
        scratch_shapes=[pltpu.VMEM((2, t, PACKED), jnp.int32), pltpu.SemaphoreType.DMA((2,))],
        input_output_aliases={3: 0},
        compiler_params=_params("arbitrary"),
        name="moe_dispatch",
    )(pos1.reshape(r // t, 1, t), pos2.reshape(r // t, 1, t), hn, jnp.zeros((p, PACKED), jnp.int32))


def _moe_ffn_kernel(be_ref, nu_ref, x_ref, wg_ref, wu_ref, wd_ref, o_ref, acc_ref):
    del be_ref
    b = pl.program_id(0)
    j = pl.program_id(1)
    last = j == pl.num_programs(1) - 1
    used = b < nu_ref[0]

    @pl.when(used)
    def _():
        x = _unpack_rows(x_ref[...]).astype(BF16)
        a = _silu(jnp.dot(x, wg_ref[...], preferred_element_type=F32)) * jnp.dot(x, wu_ref[...], preferred_element_type=F32)
        y = jnp.dot(a.astype(BF16), wd_ref[...], preferred_element_type=F32)

        @pl.when(j == 0)
        def _():
            acc_ref[...] = y

        @pl.when(j != 0)
        def _():
            acc_ref[...] += y

        @pl.when(last)
        def _():
            o_ref[...] = _pack_rows(acc_ref[...])

    @pl.when(jnp.logical_not(used) & last)
    def _():
        o_ref[...] = jnp.zeros_like(o_ref)


def _moe_ffn(xs, block_expert, n_used, wg, wu, wd):
    p = xs.shape[0]
    bm = MOE_BLOCK
    nf = D_FF // FF_TILE
    grid_spec = pltpu.PrefetchScalarGridSpec(
        num_scalar_prefetch=2,
        grid=(p // bm, nf),
        in_specs=[pl.BlockSpec((bm, PACKED), lambda b, j, be, nu: (b, 0)),
                  pl.BlockSpec((None, D_MODEL, FF_TILE), lambda b, j, be, nu: (be[b], 0, j)),
                  pl.BlockSpec((None, D_MODEL, FF_TILE), lambda b, j, be, nu: (be[b], 0, j)),
                  pl.BlockSpec((None, FF_TILE, D_MODEL), lambda b, j, be, nu: (be[b], j, 0))],
        out_specs=pl.BlockSpec((bm, PACKED), lambda b, j, be, nu: (b, 0)),
        scratch_shapes=[pltpu.VMEM((bm, D_MODEL), F32)])
    return pl.pallas_call(
        _moe_ffn_kernel,
        grid_spec=grid_spec,
        out_shape=jax.ShapeDtypeStruct((p, PACKED), jnp.int32),
        compiler_params=_params("arbitrary", "arbitrary"),
        name="moe_ffn",
    )(block_expert, n_used, xs, wg, wu, wd)


def _combine_kernel(p1_ref, p2_ref, p1n_ref, p2n_ref, route_ref, h_ref, g_ref, ys_ref, o_ref, buf1, buf2, sems,
                    *, final):
    n = h_ref.shape[0]
    s = pl.program_id(0)
    last = pl.num_programs(0) - 1
    slot = lax.rem(s, 2)

    def gather(pa, pb, sl):
        for r in range(n):
            pltpu.make_async_copy(ys_ref.at[pl.ds(pa[0, 0, r], 1), :], buf1.at[sl, pl.ds(r, 1), :], sems.at[sl]).start()
            pltpu.make_async_copy(ys_ref.at[pl.ds(pb[0, 0, r], 1), :], buf2.at[sl, pl.ds(r, 1), :], sems.at[sl]).start()

    @pl.when(s == 0)
    def _():
        gather(p1_ref, p2_ref, 0)

    @pl.when(s < last)
    def _():
        gather(p1n_ref, p2n_ref, 1 - slot)

    def drain(r, c):
        row = pltpu.make_async_copy(ys_ref.at[pl.ds(0, 1), :], buf1.at[slot, pl.ds(0, 1), :], sems.at[slot])
        row.wait()
        row.wait()
        return c

    lax.fori_loop(0, n, drain, 0, unroll=8)
    rt = route_ref[...]
    lane = lax.broadcasted_iota(jnp.int32, rt.shape, 1)
    w1 = jnp.sum(jnp.where(lane == _RT_W1, rt, 0.0), axis=-1, keepdims=True)
    w2 = jnp.sum(jnp.where(lane == _RT_W2, rt, 0.0), axis=-1, keepdims=True)
    h = h_ref[...] + w1 * _unpack_rows(buf1[slot]) + w2 * _unpack_rows(buf2[slot])
    if final:
        ms = jnp.mean(h * h, axis=-1, keepdims=True)
        h = h * lax.rsqrt(ms + NORM_EPS) * g_ref[...]
    o_ref[...] = h


def _combine(ys, pos1, pos2, route, h, gamma, batch, lp, final):
    r = h.shape[0]
    t = GATHER_TILE
    p1 = pos1.reshape(r // t, 1, t)
    p2 = pos2.reshape(r // t, 1, t)
    scratch = [pltpu.VMEM((2, t, PACKED), jnp.int32), pltpu.VMEM((2, t, PACKED), jnp.int32),
               pltpu.SemaphoreType.DMA((2,))]
    if final:
        per_seq = lp // t
        skip = FRONT // t
        live = per_seq - skip
        steps = batch * live
        rb = lambda s: (s // live) * per_seq + skip + s % live
        out_specs = pl.BlockSpec((None, t, D_MODEL), lambda s: (s // live, s % live, 0))
        out_shape = jax.ShapeDtypeStruct((batch, lp - FRONT, D_MODEL), F32)
    else:
        steps = r // t
        rb = lambda s: s
        out_specs = pl.BlockSpec((t, D_MODEL), lambda s: (s, 0))
        out_shape = jax.ShapeDtypeStruct((r, D_MODEL), F32)
    nxt = lambda s: rb(jnp.minimum(s + 1, steps - 1))
    idx = pl.BlockSpec((1, 1, t), lambda s: (rb(s), 0, 0), memory_space=pltpu.SMEM)
    idx_next = pl.BlockSpec((1, 1, t), lambda s: (nxt(s), 0, 0), memory_space=pltpu.SMEM)
    in_specs = [idx, idx, idx_next, idx_next,
                pl.BlockSpec((t, LANES), lambda s: (rb(s), 0)), pl.BlockSpec((t, D_MODEL), lambda s: (rb(s), 0)),
                pl.BlockSpec((1, D_MODEL), lambda s: (0, 0)), pl.BlockSpec(memory_space=pl.ANY)]
    return pl.pallas_call(
        functools.partial(_combine_kernel, final=final),
        grid=(steps,), in_specs=in_specs, out_specs=out_specs, out_shape=out_shape, scratch_shapes=scratch,
        compiler_params=_params("arbitrary"),
        name="moe_combine_final" if final else "moe_combine",
    )(p1, p2, p1, p2, route, h, gamma, ys)


def _final_norm_kernel(h_ref, g_ref, o_ref):
    h = h_ref[...]
    ms = jnp.mean(h * h, axis=-1, keepdims=True)
    o_ref[...] = h * lax.rsqrt(ms + NORM_EPS) * g_ref[...]


def _final_norm(h, gamma, batch, lp):
    t = GATHER_TILE
    per_seq = lp // t
    skip = FRONT // t
    return pl.pallas_call(
        _final_norm_kernel,
        grid=(batch, per_seq - skip),
        in_specs=[pl.BlockSpec((t, D_MODEL), lambda b, i: (b * per_seq + skip + i, 0)),
                  pl.BlockSpec((1, D_MODEL), lambda b, i: (0, 0))],
        out_specs=pl.BlockSpec((None, t, D_MODEL), lambda b, i: (b, i, 0)),
        out_shape=jax.ShapeDtypeStruct((batch, lp - FRONT, D_MODEL), F32),
        compiler_params=_params("parallel", "parallel"),
        name="final_norm",
    )(h, gamma)


def _rope_tables(lp):
    half = RET_DIM // 2
    pos = jnp.arange(lp, dtype=F32) - float(PAD)
    inv = ROPE_BASE ** (-jnp.arange(half, dtype=F32) / half)
    ang = pos[:, None] * inv[None, :]
    cos = jnp.tile(jnp.cos(ang), (1, 2 * RET_HEADS))
    sin = jnp.tile(jnp.sin(ang), (1, 2 * RET_HEADS))
    return cos, sin


def _prep_layer(li, p):
    w_out = p["w_out"][li].astype(F32)
    wc = w_out[640:1024].reshape(GLA_HEADS, GLA_DV, D_MODEL)
    wc = jnp.pad(wc, ((0, 0), (0, GLA_DV_PAD - GLA_DV), (0, 0))).reshape(GLA_HEADS * GLA_DV_PAD, D_MODEL)
    wgate, bgate = _pack_gate(p["gla_w_gate_f"][li], p["gla_b_gate_f"][li], p["gla_w_gate_b"][li], p["gla_b_gate_b"][li])
    return dict(
        norm_mix=p["norm_mix"][li].astype(F32)[None],
        w_in=_pack_w_in(p["w_in"][li].astype(F32)),
        wgate=wgate, bgate=bgate,
        s5=_s5_tables(p["s5_lambda_re"][li], p["s5_lambda_im"][li], p["s5_log_dt"][li], p["s5_b_re"][li],
                      p["s5_b_im"][li], p["s5_c_re"][li], p["s5_c_im"][li], p["s5_d"][li]),
        w_glu=p["s5_w_glu"][li].astype(BF16),
        w_out=jnp.concatenate([w_out[0:640], wc], axis=0).astype(BF16),
        norm_ffn=p["norm_ffn"][li].astype(F32)[None],
    )


def _trunk(x, meta_tokens, layers, ffn, moe, norm_final, depth):
    batch, seq, _ = x.shape
    lp = seq + FRONT
    r = batch * lp
    meta = jnp.broadcast_to(meta_tokens.astype(F32)[None], (batch, N_META, D_MODEL))
    h = jnp.concatenate([jnp.zeros((batch, PAD, D_MODEL), F32), meta, x.astype(F32)], axis=1).reshape(r, D_MODEL)
    cos, sin = _rope_tables(lp)
    cos = jnp.tile(cos, (batch, 1))
    sin = jnp.tile(sin, (batch, 1))
    gamma_final = norm_final.astype(F32)[None]
    for li in range(depth):
        lw = layers[li]
        u, rq, rk, rv, rg, gq, gk, gv, gg, lf, lb = _inproj(
            h, lw["norm_mix"], lw["w_in"], cos, sin, lw["wgate"], lw["bgate"], batch, lp)
        ya = _s5_mixer(u, lw["s5"], batch, lp)
        yb = _retention(rq, rk, rv, rg, batch, lp)
        yc = _linear_attention(gq, gk, gv, lf, lb, gg, batch, lp, npair=GLA_HEADS // 2, vp=2 * GLA_DV_PAD, dv=GLA_DV)
        j = li // 2
        last = li == depth - 1
        if li % 2 == 0:
            h, hn = _outproj(ya, yb, yc, h, lw["w_glu"], lw["w_out"], lw["norm_ffn"])
            h = _ffn(hn, h, ffn["wg"][j], ffn["wu"][j], ffn["wd"][j])
        else:
            h, hn, route, cnt = _outproj(ya, yb, yc, h, lw["w_glu"], lw["w_out"], lw["norm_ffn"], moe["wr"][j])
            pos1, pos2, block_expert, n_used, nb = _route_meta(route, cnt, r)
            xs = _dispatch(hn, pos1, pos2, nb)
            ys = _moe_ffn(xs, block_expert, n_used, moe["wg"][j], moe["wu"][j], moe["wd"][j])
            h = _combine(ys, pos1, pos2, route, h, gamma_final, batch, lp, final=last)
            if last:
                return h
    return _final_norm(h, gamma_final, batch, lp)


def kernel(x_prompt, x_sample, meta_tokens, norm_mix, w_in, s5_lambda_re, s5_lambda_im, s5_log_dt, s5_b_re, s5_b_im, s5_c_re, s5_c_im, s5_d, s5_w_glu, gla_w_gate_f, gla_b_gate_f, gla_w_gate_b, gla_b_gate_b, w_out, norm_ffn, ffn_w_gate, ffn_w_up, ffn_w_down, router_w, moe_w_gate, moe_w_up, moe_w_down, norm_final):
    depth = w_in.shape[0]
    p = dict(norm_mix=norm_mix, w_in=w_in, s5_lambda_re=s5_lambda_re, s5_lambda_im=s5_lambda_im,
             s5_log_dt=s5_log_dt, s5_b_re=s5_b_re, s5_b_im=s5_b_im, s5_c_re=s5_c_re, s5_c_im=s5_c_im, s5_d=s5_d,
             s5_w_glu=s5_w_glu, gla_w_gate_f=gla_w_gate_f, gla_b_gate_f=gla_b_gate_f, gla_w_gate_b=gla_w_gate_b,
             gla_b_gate_b=gla_b_gate_b, w_out=w_out, norm_ffn=norm_ffn)
    layers = [_prep_layer(li, p) for li in range(depth)]
    ffn = dict(wg=ffn_w_gate.astype(BF16), wu=ffn_w_up.astype(BF16), wd=ffn_w_down.astype(BF16))
    wr = jnp.pad(router_w.astype(F32), ((0, 0), (0, 0), (0, LANES - N_EXPERTS)))
    wr_hi = wr.astype(BF16)
    wr_lo = (wr - wr_hi.astype(F32)).astype(BF16)
    wr = [(wr_hi[j], wr_lo[j]) for j in range(wr.shape[0])]
    moe = dict(wr=wr, wg=moe_w_gate.astype(BF16), wu=moe_w_up.astype(BF16), wd=moe_w_down.astype(BF16))
    y_prompt = _trunk(x_prompt, meta_tokens, layers, ffn, moe, norm_final, depth)
    y_sample = _trunk(x_sample, meta_tokens, layers, ffn, moe, norm_final, depth)
    return (y_prompt, y_sample)
```

```python
import functools
import math

import jax
import jax.numpy as jnp
from jax import lax
from jax.experimental import pallas as pl
from jax.experimental.pallas import tpu as pltpu

F32 = jnp.float32
BF16 = jnp.bfloat16

D_MODEL = 1024
N_META = 16
S5_P = 16
S5_WIDTH = 256
S5_GROUPS = 16
S5_N = 64
RET_HEADS = 6
RET_DIM = 64
RET_WIDTH = 384
GLA_HEADS = 4
GLA_DV = 96
GLA_DK = 48
GLA_QK = 192
GLA_WIDTH = 384
GLA_GATE_RANK = 16
GLA_TAU = 16.0
ROPE_BASE = 10000.0
D_FF = 2816
N_EXPERTS = 8
NORM_EPS = 1e-5

CHUNK = 64
FRONT = 256
PAD = FRONT - N_META
ROW_TILE = 512
LA_GROUP = 4
LANES = 128
GLA_DK_PAD = 64
GLA_DV_PAD = 128
VMEM_LIMIT = 56 * 1024 * 1024


def _params(*sem):
    return pltpu.CompilerParams(dimension_semantics=sem, vmem_limit_bytes=VMEM_LIMIT)


def _const_spec(shape):
    nd = len(shape)
    return pl.BlockSpec(shape, lambda *_: (0,) * nd)


def _sigmoid(x):
    return 1.0 / (1.0 + jnp.exp(-x))


def _silu(x):
    return x * _sigmoid(x)


def _gelu_tanh(x):
    c = math.sqrt(2.0 / math.pi)
    return 0.5 * x * (1.0 + jnp.tanh(c * (x + 0.044715 * (x * x * x))))


def _log_sigmoid(z):
    return jnp.minimum(z, 0.0) - jnp.log(1.0 + jnp.exp(-jnp.abs(z)))


def _split_dot(a, b_bf16, dims=None):
    hi = a.astype(BF16)
    lo = (a - hi.astype(F32)).astype(BF16)
    if dims is None:
        return (jnp.dot(hi, b_bf16, preferred_element_type=F32)
                + jnp.dot(lo, b_bf16, preferred_element_type=F32))
    return (lax.dot_general(hi, b_bf16, dims, preferred_element_type=F32)
            + lax.dot_general(lo, b_bf16, dims, preferred_element_type=F32))


_C_U = 0
_C_RQ = 256
_C_RK = 640
_C_RV = 1024
_C_RG = 1408
_C_GQ = 1792
_C_GK = 2048
_C_GV = 2304
_C_GG = 2816
_C_GL = 3328
_C_END = 3456


def _pack_w_in(w):
    o = 0
    u = w[:, o:o + 256]; o += 256
    rq = w[:, o:o + 384]; o += 384
    rk = w[:, o:o + 384]; o += 384
    rv = w[:, o:o + 384]; o += 384
    rg = w[:, o:o + 384]; o += 384
    gq = w[:, o:o + 192]; o += 192
    gk = w[:, o:o + 192]; o += 192
    gv = w[:, o:o + 384]; o += 384
    gg = w[:, o:o + 384]; o += 384
    glf = w[:, o:o + 16]; o += 16
    glb = w[:, o:o + 16]; o += 16

    def padh(m, d, dp):
        m = m.reshape(D_MODEL, GLA_HEADS, d)
        return jnp.pad(m, ((0, 0), (0, 0), (0, dp - d))).reshape(D_MODEL, GLA_HEADS * dp)

    gl = jnp.pad(jnp.concatenate([glf, glb], axis=1), ((0, 0), (0, LANES - 2 * GLA_GATE_RANK)))
    cat = jnp.concatenate([
        u, rq, rk, rv, rg,
        padh(gq, GLA_DK, GLA_DK_PAD), padh(gk, GLA_DK, GLA_DK_PAD),
        padh(gv, GLA_DV, GLA_DV_PAD), padh(gg, GLA_DV, GLA_DV_PAD), gl], axis=1)
    return cat.astype(BF16)


def _pack_gate(w_f, b_f, w_b, b_b):
    def padh(m):
        m = m.reshape(m.shape[0], GLA_HEADS, GLA_DK)
        return jnp.pad(m, ((0, 0), (0, 0), (0, GLA_DK_PAD - GLA_DK))).reshape(m.shape[0], GLA_HEADS * GLA_DK_PAD)
    r = GLA_GATE_RANK
    w = jnp.zeros((LANES, 2 * GLA_HEADS * GLA_DK_PAD), F32)
    w = w.at[0:r, 0:256].set(padh(w_f.astype(F32)))
    w = w.at[r:2 * r, 256:512].set(padh(w_b.astype(F32)))
    b = jnp.concatenate([padh(b_f.astype(F32)[None]), padh(b_b.astype(F32)[None])], axis=1)
    return w.astype(BF16), b


def _inproj_kernel(h_ref, g_ref, w_ref, cos_ref, sin_ref, wgate_ref, bgate_ref,
                   u_ref, rq_ref, rk_ref, rv_ref, rg_ref, gq_ref, gk_ref, gv_ref, gg_ref, lf_ref, lb_ref,
                   *, batch, lp):
    tm = h_ref.shape[0]
    half = RET_DIM // 2
    first_half = (lax.broadcasted_iota(jnp.int32, (1, RET_WIDTH), 1) & (RET_DIM - 1)) < half
    parts = 2
    tp = tm // parts
    for part in range(parts):
        rows = slice(part * tp, (part + 1) * tp)
        x = h_ref[rows, :]
        ms = jnp.mean(x * x, axis=-1, keepdims=True)
        row = pl.program_id(0) * tm + part * tp + lax.broadcasted_iota(jnp.int32, (tp, 1), 0)
        valid = jnp.ones((tp, 1), F32)
        for b in range(batch):
            valid = jnp.where((row >= b * lp) & (row < b * lp + PAD), 0.0, valid)
        hn = (x * (lax.rsqrt(ms + NORM_EPS) * valid) * g_ref[...]).astype(BF16)

        full = jnp.dot(hn, w_ref[...], preferred_element_type=F32)

        def proj(lo, hi):
            return full[:, lo:hi]

        cos = cos_ref[rows, :]
        sin = sin_ref[rows, :]

        def rope(v):
            rot = jnp.where(first_half, -pltpu.roll(v, RET_WIDTH - half, 1), pltpu.roll(v, half, 1))
            return v * cos + rot * sin

        u_ref[rows, :] = proj(_C_U, _C_RQ)
        rq_ref[rows, :] = rope(proj(_C_RQ, _C_RK)).astype(BF16)
        rk_ref[rows, :] = (rope(proj(_C_RK, _C_RV)) * (RET_DIM ** -0.5)).astype(BF16)
        rv_ref[rows, :] = proj(_C_RV, _C_RG).astype(BF16)
        rg_ref[rows, :] = proj(_C_RG, _C_GQ).astype(BF16)
        gq_ref[rows, :] = (proj(_C_GQ, _C_GK) * (GLA_DK ** -0.5)).astype(BF16)
        gk_ref[rows, :] = proj(_C_GK, _C_GV).astype(BF16)
        gv_ref[rows, :] = proj(_C_GV, _C_GG).astype(BF16)
        gg_ref[rows, :] = proj(_C_GG, _C_GL).astype(BF16)
        codes = proj(_C_GL, _C_END).astype(BF16)
        z = jnp.dot(codes, wgate_ref[...], preferred_element_type=F32) + bgate_ref[...]
        ls = _log_sigmoid(z) * (1.0 / GLA_TAU)
        lf_ref[rows, :] = ls[:, 0:256]
        lb_ref[rows, :] = ls[:, 256:512]


def _inproj(h, gamma, w, cos, sin, wgate, bgate, batch, lp):
    r = h.shape[0]
    tm = ROW_TILE
    widths = (256, 384, 384, 384, 384, 256, 256, 512, 512, 256, 256)
    dtypes = (F32,) + (BF16,) * 8 + (F32, F32)

    def rows(wd):
        return pl.BlockSpec((tm, wd), lambda i: (i, 0))

    return pl.pallas_call(
        functools.partial(_inproj_kernel, batch=batch, lp=lp),
        grid=(r // tm,),
        in_specs=[rows(D_MODEL), _const_spec((1, D_MODEL)), _const_spec((D_MODEL, _C_END)),
                  rows(RET_WIDTH), rows(RET_WIDTH), _const_spec((LANES, 512)), _const_spec((1, 512))],
        out_specs=[rows(wd) for wd in widths],
        out_shape=[jax.ShapeDtypeStruct((r, wd), dt) for wd, dt in zip(widths, dtypes)],
        compiler_params=_params("parallel"),
        name="inproj",
    )(h, gamma, w, cos, sin, wgate, bgate)


def _s5_toeplitz_kernel(pwr_ref, pwi_ref, cer_ref, cei_ref, bbr_ref, bbi_ref, d_ref, tt_ref):
    hp = lax.Precision.HIGHEST
    w = CHUNK * S5_P
    lane = lax.broadcasted_iota(jnp.int32, (S5_P, w), 1)
    krow = []
    for dr in range(2):
        pr, pi, cr, ci = pwr_ref[dr], pwi_ref[dr], cer_ref[dr], cei_ref[dr]
        zr = pr * cr - pi * ci
        zi = pr * ci + pi * cr
        krow.append(jnp.dot(bbr_ref[dr], zr, precision=hp, preferred_element_type=F32)
                    - jnp.dot(bbi_ref[dr], zi, precision=hp, preferred_element_type=F32))
    kf = krow[0] + d_ref[...]
    kb = krow[1]
    for s in range(CHUNK):
        right = S5_P * s
        left = S5_P * (CHUNK - 1 - s)
        a = kf if right == 0 else jnp.where(lane >= right, pltpu.roll(kf, right, 1), 0.0)
        b = kb if left == 0 else jnp.where(lane < w - left, pltpu.roll(kb, w - left, 1), 0.0)
        tt_ref[S5_P * s:S5_P * (s + 1), :] = (a + b).astype(BF16)


def _s5_toeplitz(pwr, pwi, cer, cei, bbr, bbi, d_e):
    w = CHUNK * S5_P
    big = pl.BlockSpec((None, 2, S5_N, w), lambda g: (g, 0, 0, 0))
    small = pl.BlockSpec((None, 2, S5_P, S5_N), lambda g: (g, 0, 0, 0))
    return pl.pallas_call(
        _s5_toeplitz_kernel,
        grid=(S5_GROUPS,),
        in_specs=[big, big, big, big, small, small, pl.BlockSpec((None, S5_P, w), lambda g: (g, 0, 0))],
        out_specs=pl.BlockSpec((None, w, w), lambda g: (g, 0, 0)),
        out_shape=jax.ShapeDtypeStruct((S5_GROUPS, w, w), BF16),
        compiler_params=_params("parallel"),
        name="s5_toeplitz",
    )(pwr, pwi, cer, cei, bbr, bbi, d_e)


def _s5_tables(lam_re, lam_im, log_dt, b_re, b_im, c_re, c_im, d):
    c = CHUNK
    g_, n_, p_ = S5_GROUPS, S5_N, S5_P
    dt = jnp.exp(log_dt.astype(F32))[..., None]
    lr = lam_re.astype(F32)
    li = lam_im.astype(F32)
    e = lr * dt
    th = li * dt
    mag = jnp.exp(e)
    a_re = mag * jnp.cos(th)
    a_im = mag * jnp.sin(th)
    den = lr * lr + li * li
    nr = a_re - 1.0
    ni = a_im
    coef_re = (nr * lr + ni * li) / den
    coef_im = (ni * lr - nr * li) / den
    br = b_re.astype(F32)
    bi = b_im.astype(F32)
    bb_re = coef_re[..., None] * br - coef_im[..., None] * bi
    bb_im = coef_re[..., None] * bi + coef_im[..., None] * br
    cr = c_re.astype(F32)
    ci = c_im.astype(F32)
    tau = jnp.arange(c + 1, dtype=F32)[:, None, None, None]
    pw_mag = jnp.exp(tau * e[None])
    pw_re = pw_mag * jnp.cos(tau * th[None])
    pw_im = pw_mag * jnp.sin(tau * th[None])

    z_re = cr[None] * pw_re[:, :, :, None, :] - ci[None] * pw_im[:, :, :, None, :]
    z_im = cr[None] * pw_im[:, :, :, None, :] + ci[None] * pw_re[:, :, :, None, :]
    idx = jnp.arange(c)

    def expand_pw(pw):
        both = jnp.stack([pw[:c, 0], pw[c - 1 - idx, 1]], axis=0)
        both = jnp.transpose(both, (2, 0, 3, 1))
        return jnp.broadcast_to(both[..., None], (g_, 2, n_, c, p_)).reshape(g_, 2, n_, c * p_)

    def expand_c(cm):
        cm = jnp.transpose(cm, (1, 0, 3, 2))
        return jnp.broadcast_to(cm[:, :, :, None, :], (g_, 2, n_, c, p_)).reshape(g_, 2, n_, c * p_)

    bbt_re = jnp.transpose(bb_re, (1, 0, 3, 2))
    bbt_im = jnp.transpose(bb_im, (1, 0, 3, 2))
    d_e = jnp.zeros((g_, p_, c * p_), F32).at[:, :, :p_].set(d.astype(F32)[:, :, None] * jnp.eye(p_, dtype=F32))
    tt = _s5_toeplitz(expand_pw(pw_re), expand_pw(pw_im), expand_c(cr), expand_c(ci), bbt_re, bbt_im, d_e)

    pf_re = pw_re[c - 1 - idx, 0]
    pf_im = pw_im[c - 1 - idx, 0]
    pb_re = pw_re[idx, 1]
    pb_im = pw_im[idx, 1]

    def m_of(p_re_, p_im_, dr):
        m_re = p_re_[:, :, :, None] * bb_re[dr][None] - p_im_[:, :, :, None] * bb_im[dr][None]
        m_im = p_re_[:, :, :, None] * bb_im[dr][None] + p_im_[:, :, :, None] * bb_re[dr][None]
        to = lambda m: jnp.transpose(m, (1, 0, 3, 2)).reshape(g_, c * p_, n_)
        return to(m_re), to(m_im)

    mf_re, mf_im = m_of(pf_re, pf_im, 0)
    mb_re, mb_im = m_of(pb_re, pb_im, 1)
    m4 = jnp.stack([mf_re, mf_im, mb_re, mb_im], axis=1)
    m4 = m4.reshape(g_ // 2, 2, 4, c * p_, n_)
    mz = jnp.zeros_like(m4[:, 0])

    def m_rows(blocks):
        return jnp.transpose(jnp.concatenate(blocks, axis=-1), (0, 2, 1, 3)).reshape(g_ // 2, c * p_, 8 * n_)

    mpair = jnp.concatenate([m_rows([m4[:, 0], mz]), m_rows([mz, m4[:, 1]])], axis=1).astype(BF16)

    def n_of(tsel, dr):
        w_re = z_re[tsel, dr]
        w_im = z_im[tsel, dr]
        to = lambda m: jnp.transpose(m, (1, 3, 0, 2)).reshape(g_, n_, c * p_)
        return to(w_re), to(-w_im)

    nf_re, nf_im = n_of(idx + 1, 0)
    nb_re, nb_im = n_of(c - idx, 1)
    n4 = jnp.stack([nf_re, nf_im, nb_re, nb_im], axis=1)
    n4 = n4.reshape(g_ // 2, 2, 4, n_, c * p_)
    nz = jnp.zeros_like(n4[:, 0])
    npair = jnp.stack([jnp.concatenate([n4[:, 0], nz], axis=-1), jnp.concatenate([nz, n4[:, 1]], axis=-1)], axis=2)
    npair = npair.reshape(g_ // 2, 4 * 2 * n_, 2 * c * p_).astype(BF16)

    dec = jnp.stack([pw_re[c, 0], pw_im[c, 0], pw_re[c, 1], pw_im[c, 1]], axis=0)
    dec = dec.reshape(4, 1, g_ * n_)
    return tt, mpair, npair, dec


S5_PITCH = CHUNK + 4
S5_HALVES = S5_WIDTH // LANES
S5_GPH = LANES // S5_P


def _s5_chunk_block(nch):
    best = 8
    for cand in range(8, 113, 8):
        if nch % cand == 0:
            best = cand
    return best


def _block_transpose8(tiles):
    lane = lax.broadcasted_iota(jnp.int32, tiles[0].shape, 1)
    tiles = list(tiles)
    for dist in (4, 2, 1):
        width = S5_P * dist
        low = (lane & (2 * width - 1)) < width
        for k in range(S5_GPH):
            if k & dist:
                continue
            a, b = tiles[k], tiles[k + dist]
            tiles[k] = jnp.where(low, a, pltpu.roll(b, width, 1))
            tiles[k + dist] = jnp.where(low, pltpu.roll(a, LANES - width, 1), b)
    return tiles


def _s5_in_kernel(u_hbm, m_ref, ucat_ref, o0, o1, o2, o3, xpad, uall, sem):
    i = pl.program_id(0)
    j = pl.program_id(1)
    cb = uall.shape[1]

    @pl.when(j == 0)
    def _():
        def chunk_copy(c, h):
            return pltpu.make_async_copy(
                u_hbm.at[pl.ds((i * cb + c) * CHUNK, CHUNK), pl.ds(h * LANES, LANES)],
                xpad.at[h, pl.ds(c * S5_PITCH, CHUNK), :], sem)

        def start(c, carry):
            for h in range(S5_HALVES):
                chunk_copy(c, h).start()
            return carry

        def wait(c, carry):
            for h in range(S5_HALVES):
                chunk_copy(c, h).wait()
            return carry

        lax.fori_loop(0, cb, start, 0)
        lax.fori_loop(0, cb, wait, 0)

        def dest_tile(jt, carry):
            col = pl.multiple_of(jt * LANES, LANES)
            for h in range(S5_HALVES):
                by_token = [xpad[h, pl.ds(jt * S5_GPH + k, cb, stride=S5_PITCH), :] for k in range(S5_GPH)]
                for go, tile in enumerate(_block_transpose8(by_token)):
                    uall[h * S5_GPH + go, :, pl.ds(col, LANES)] = tile
            return carry

        lax.fori_loop(0, CHUNK // S5_GPH, dest_tile, 0)

    u2 = jnp.concatenate([uall[2 * j], uall[2 * j + 1]], axis=1)
    ucat_ref[...] = u2
    s = jnp.dot(u2.astype(BF16), m_ref[...], preferred_element_type=F32)
    o0[...] = s[:, 0:128]
    o1[...] = s[:, 128:256]
    o2[...] = s[:, 256:384]
    o3[...] = s[:, 384:512]


def _s5_in(u, mpair):
    nch = u.shape[0] // CHUNK
    cb = _s5_chunk_block(nch)
    w = CHUNK * S5_P
    return pl.pallas_call(
        _s5_in_kernel,
        grid=(nch // cb, S5_GROUPS // 2),
        in_specs=[pl.BlockSpec(memory_space=pl.ANY),
                  pl.BlockSpec((None, 2 * w, 512), lambda i, j: (j, 0, 0))],
        out_specs=[pl.BlockSpec((cb, 2 * w), lambda i, j: (i, j))] + [pl.BlockSpec((cb, LANES), lambda i, j: (i, j))] * 4,
        out_shape=[jax.ShapeDtypeStruct((nch, S5_GROUPS * w), F32)] + [jax.ShapeDtypeStruct((nch, 1024), F32)] * 4,
        scratch_shapes=[pltpu.VMEM((S5_HALVES, cb * S5_PITCH, LANES), F32), pltpu.VMEM((S5_GROUPS, cb, w), F32),
                        pltpu.SemaphoreType.DMA(())],
        compiler_params=_params("arbitrary", "arbitrary"),
        name="s5_in",
    )(u, mpair)


def _s5_scan_kernel(sfr, sfi, sbr, sbi, dfr, dfi, dbr, dbi, hfr, hfi, hbr, hbi):
    nc, b, _ = sfr.shape
    a_fr = dfr[...]
    a_fi = dfi[...]
    a_br = dbr[...]
    a_bi = dbi[...]
    zero = jnp.zeros((b, LANES), F32)

    def fwd(c, carry):
        hr, hi = carry
        hfr[c] = hr
        hfi[c] = hi
        return (a_fr * hr - a_fi * hi + sfr[c], a_fr * hi + a_fi * hr + sfi[c])

    def bwd(i, carry):
        c = nc - 1 - i
        hr, hi = carry
        hbr[c] = hr
        hbi[c] = hi
        return (a_br * hr - a_bi * hi + sbr[c], a_br * hi + a_bi * hr + sbi[c])

    lax.fori_loop(0, nc, fwd, (zero, zero))
    lax.fori_loop(0, nc, bwd, (zero, zero))


def _s5_scan(s4, dec, nc, b):
    blk = pl.BlockSpec((nc, b, LANES), lambda j: (0, 0, j))
    dspecs = [pl.BlockSpec((None, 1, LANES), functools.partial(lambda j, k: (k, 0, j), k=k)) for k in range(4)]
    return pl.pallas_call(
        _s5_scan_kernel,
        grid=(S5_GROUPS // 2,),
        in_specs=[blk] * 4 + dspecs,
        out_specs=[blk] * 4,
        out_shape=[jax.ShapeDtypeStruct((nc, b, 1024), F32)] * 4,
        compiler_params=_params("parallel"),
        name="s5_scan",
    )(*s4, dec, dec, dec, dec)


def _s5_out_kernel(u_ref, tt_ref, h0, h1, h2, h3, n_ref, y_hbm, yall, ypad, sem):
    i = pl.program_id(0)
    j = pl.program_id(1)
    cb = yall.shape[1]
    w = CHUNK * S5_P
    u = u_ref[...].astype(BF16)
    y0 = jnp.dot(u[:, 0:w], tt_ref[0], preferred_element_type=F32)
    y1 = jnp.dot(u[:, w:2 * w], tt_ref[1], preferred_element_type=F32)
    hcat = jnp.concatenate([h0[...], h1[...], h2[...], h3[...]], axis=1).astype(BF16)
    yh = jnp.dot(hcat, n_ref[...], preferred_element_type=F32)
    yall[2 * j] = y0 + yh[:, 0:w]
    yall[2 * j + 1] = y1 + yh[:, w:2 * w]

    @pl.when(j == pl.num_programs(1) - 1)
    def _():
        def src_tile(jt, carry):
            col = pl.multiple_of(jt * LANES, LANES)
            for h in range(S5_HALVES):
                by_group = [yall[h * S5_GPH + go, :, pl.ds(col, LANES)] for go in range(S5_GPH)]
                for k, tile in enumerate(_block_transpose8(by_group)):
                    ypad[h, pl.ds(jt * S5_GPH + k, cb, stride=S5_PITCH), :] = tile
            return carry

        lax.fori_loop(0, CHUNK // S5_GPH, src_tile, 0)

        def chunk_copy(c, h):
            return pltpu.make_async_copy(
                ypad.at[h, pl.ds(c * S5_PITCH, CHUNK), :],
                y_hbm.at[pl.ds((i * cb + c) * CHUNK, CHUNK), pl.ds(h * LANES, LANES)], sem)

        def start(c, carry):
            for h in range(S5_HALVES):
                chunk_copy(c, h).start()
            return carry

        def wait(c, carry):
            for h in range(S5_HALVES):
                chunk_copy(c, h).wait()
            return carry

        lax.fori_loop(0, cb, start, 0)
        lax.fori_loop(0, cb, wait, 0)


def _s5_out(ucat, tt, h4, npair):
    nch = ucat.shape[0]
    cb = _s5_chunk_block(nch)
    w = CHUNK * S5_P
    hblk = pl.BlockSpec((cb, LANES), lambda i, j: (i, j))
    return pl.pallas_call(
        _s5_out_kernel,
        grid=(nch // cb, S5_GROUPS // 2),
        in_specs=[pl.BlockSpec((cb, 2 * w), lambda i, j: (i, j)),
                  pl.BlockSpec((2, w, w), lambda i, j: (j, 0, 0)),
                  hblk, hblk, hblk, hblk,
                  pl.BlockSpec((None, 512, 2 * w), lambda i, j: (j, 0, 0))],
        out_specs=pl.BlockSpec(memory_space=pl.ANY),
        out_shape=jax.ShapeDtypeStruct((nch * CHUNK, S5_WIDTH), F32),
        scratch_shapes=[pltpu.VMEM((S5_GROUPS, cb, w), F32), pltpu.VMEM((S5_HALVES, cb * S5_PITCH, LANES), F32),
                        pltpu.SemaphoreType.DMA(())],
        compiler_params=_params("arbitrary", "arbitrary"),
        name="s5_out",
    )(ucat, tt, *h4, npair)


def _s5_mixer(u, tables, batch, lp):
    tt, mpair, npair, dec = tables
    nc = lp // CHUNK
    ucat, *s4 = _s5_in(u, mpair)
    s4 = [jnp.transpose(s.reshape(batch, nc, 1024), (1, 0, 2)) for s in s4]
    h4 = _s5_scan(s4, dec, nc, batch)
    h4 = [jnp.transpose(h, (1, 0, 2)).reshape(batch * nc, 1024) for h in h4]
    return _s5_out(ucat, tt, h4, npair)


def _cumsum_chunks(x):
    n = x.shape[0]
    r = lax.broadcasted_iota(jnp.int32, (n, n), 0)
    s = lax.broadcasted_iota(jnp.int32, (n, n), 1)
    tri = jnp.where((s <= r) & ((s // CHUNK) == (r // CHUNK)), 1.0, 0.0).astype(BF16)
    hi = x.astype(BF16)
    lo = (x - hi.astype(F32)).astype(BF16)
    return jnp.dot(tri, hi, preferred_element_type=F32) + jnp.dot(tri, lo, preferred_element_type=F32)


def _head_masks(npair, vp):
    kl = lax.broadcasted_iota(jnp.int32, (1, LANES), 1)
    km = [(kl < 64), (kl >= 64)]
    vl = lax.broadcasted_iota(jnp.int32, (1, vp), 1)
    vm = [(vl < vp // 2), (vl >= vp // 2)]
    vrow = lax.broadcasted_iota(jnp.int32, (vp, LANES), 0)
    kcol = lax.broadcasted_iota(jnp.int32, (vp, LANES), 1)
    bd = (vrow >= vp // 2) == (kcol >= 64)
    return km, vm, bd


def _la_bwd_kernel(k_all, v_all, lb_all, sb_all, st_all, *, npair, vp, nsub, nseq):
    g = LA_GROUP
    c = CHUNK
    gc = g * c

    @pl.when(pl.program_id(1) == 0)
    def _():
        st_all[...] = jnp.zeros_like(st_all)

    km, vm, bd = _head_masks(npair, vp)

    def sub_block(t, carry):
        for s in range(nseq):
            one_seq(k_all.at[s], v_all.at[s], lb_all.at[s], sb_all.at[s], st_all.at[s], nsub - 1 - t)
        return carry

    def one_seq(k_ref, v_ref, lb_ref, sb_ref, st_ref, sub):
        base = pl.multiple_of(sub * gc, gc)
        kblk = k_ref[pl.ds(base, gc), :].astype(F32)
        vblk = v_ref[pl.ds(base, gc), :]
        lblk = lb_ref[pl.ds(base, gc), :]
        cblk = _cumsum_chunks(lblk)
        kb2blk = (kblk * jnp.exp(cblk - lblk)).astype(BF16)
        states = [st_ref[p] for p in range(npair)]
        for gi in reversed(range(g)):
            rows = slice(gi * c, (gi + 1) * c)
            kb2 = kb2blk[rows, :]
            dec = jnp.exp(cblk[gi * c + c - 1:(gi + 1) * c, :])
            vv = vblk[rows, :]
            for p in range(npair):
                sb_ref[sub * g + gi, p] = states[p].astype(BF16)
                upd = lax.dot_general(vv[:, p * vp:(p + 1) * vp], kb2[:, p * LANES:(p + 1) * LANES],
                                      (((0,), (0,)), ((), ())), preferred_element_type=F32)
                states[p] = states[p] * dec[:, p * LANES:(p + 1) * LANES] + jnp.where(bd, upd, 0.0)
        for p in range(npair):
            st_ref[p] = states[p]

    lax.fori_loop(0, nsub, sub_block, 0)


def _head_rms_gate(o, gate, bdv, dv):
    ms = _split_dot(o * o, bdv) * (1.0 / dv)
    return (o * lax.rsqrt(ms + NORM_EPS) * _silu(gate.astype(F32))).astype(BF16)


def _la_fwd_kernel(q_all, k_all, v_all, lf_all, lb_all, sb_all, gate_all, bdv_ref, o_all, st_all,
                   *, npair, vp, dv, nsub, nseq):
    g = LA_GROUP
    c = CHUNK
    gc = g * c

    @pl.when(pl.program_id(1) == 0)
    def _():
        st_all[...] = jnp.zeros_like(st_all)

    km, vm, bd = _head_masks(npair, vp)
    ri = lax.broadcasted_iota(jnp.int32, (c, LANES), 0)
    cj = lax.broadcasted_iota(jnp.int32, (c, LANES), 1) & (c - 1)
    lower = ri >= cj
    upper = ri <= cj
    mid = c // 2
    nt = (((1,), (1,)), ((), ()))
    tn = (((0,), (0,)), ((), ()))

    def sub_block(sub, carry):
        for s in range(nseq):
            one_seq(q_all.at[s], k_all.at[s], v_all.at[s], lf_all.at[s], lb_all.at[s], sb_all.at[s],
                    gate_all.at[s], o_all.at[s], st_all.at[s], sub)
        return carry

    def one_seq(q_ref, k_ref, v_ref, lf_ref, lb_ref, sb_ref, gate_ref, o_ref, st_ref, sub):
        base = pl.multiple_of(sub * gc, gc)
        qblk = q_ref[pl.ds(base, gc), :].astype(F32)
        kblk = k_ref[pl.ds(base, gc), :].astype(F32)
        vblk = v_ref[pl.ds(base, gc), :]
        lfblk = lf_ref[pl.ds(base, gc), :]
        lbblk = lb_ref[pl.ds(base, gc), :]
        w = lfblk.shape[1]
        csblk = _cumsum_chunks(jnp.concatenate([lfblk, lbblk], axis=1))
        states = [st_ref[p] for p in range(npair)]
        outs = []
        for gi in range(g):
            rows = slice(gi * c, (gi + 1) * c)
            qq = qblk[rows, :]
            kk = kblk[rows, :]
            vv = vblk[rows, :]
            cf = csblk[rows, :w]
            cbi = csblk[rows, w:]
            cb = cbi - lbblk[rows, :]
            mf = cf[mid:mid + 1, :]
            mb = cb[mid:mid + 1, :]
            tf = cf[c - 1:c, :]
            tb = cbi[c - 1:c, :]
            qf = (qq * jnp.exp(cf - mf)).astype(BF16)
            kf = (kk * jnp.exp(mf - cf)).astype(BF16)
            qb = (qq * jnp.exp(mb - cb)).astype(BF16)
            kb = (kk * jnp.exp(cb - mb)).astype(BF16)
            zk = jnp.zeros((c, LANES), BF16)
            q2 = jnp.concatenate([qq * jnp.exp(cf), qq * jnp.exp(tb - cb)], axis=0).astype(BF16)
            kf2 = (kk * jnp.exp(tf - cf)).astype(BF16)
            dec = jnp.exp(tf)
            pair_out = []
            for p in range(npair):
                ks = slice(p * LANES, (p + 1) * LANES)
                vsl = slice(p * vp, (p + 1) * vp)
                kfp = kf[:, ks]
                kbp = kb[:, ks]
                k2f = jnp.concatenate([jnp.where(km[0], kfp, zk), jnp.where(km[1], kfp, zk)], axis=0)
                k2b = jnp.concatenate([jnp.where(km[0], kbp, zk), jnp.where(km[1], kbp, zk)], axis=0)
                sf = lax.dot_general(qf[:, ks], k2f, nt, preferred_element_type=F32)
                sb = lax.dot_general(qb[:, ks], k2b, nt, preferred_element_type=F32)
                sc = (jnp.where(lower, sf, 0.0) + jnp.where(upper, sb, 0.0)).astype(BF16)
                vp_ = vv[:, vsl]
                zero = jnp.zeros_like(vp_)
                v2 = jnp.concatenate([jnp.where(vm[0], vp_, zero), jnp.where(vm[1], vp_, zero)], axis=0)
                o = jnp.dot(sc, v2, preferred_element_type=F32)
                st = states[p]
                o = o + lax.dot_general(q2[0:c, ks], st.astype(BF16), nt, preferred_element_type=F32)
                o = o + lax.dot_general(q2[c:2 * c, ks], sb_ref[sub * g + gi, p], nt, preferred_element_type=F32)
                upd = lax.dot_general(vp_, kf2[:, ks], tn, preferred_element_type=F32)
                states[p] = st * dec[:, ks] + jnp.where(bd, upd, 0.0)
                pair_out.append(o)
            outs.append(jnp.concatenate(pair_out, axis=1) if npair > 1 else pair_out[0])
        for p in range(npair):
            st_ref[p] = states[p]
        o = jnp.concatenate(outs, axis=0)
        o_ref[pl.ds(base, gc), :] = _head_rms_gate(o, gate_ref[pl.ds(base, gc), :], bdv_ref[...], dv)

    lax.fori_loop(0, nsub, sub_block, 0)


def _head_block_ones(wv, seg):
    vr = lax.broadcasted_iota(jnp.int32, (wv, wv), 0) // seg
    vc = lax.broadcasted_iota(jnp.int32, (wv, wv), 1) // seg
    return (vr == vc).astype(BF16)


def _sub_blocks(lp, rows):
    n = lp // rows
    for cand in (5, 4, 3, 2):
        if n % cand == 0:
            return cand
    return 1


def _seqs_per_step(batch):
    for n in (4, 2):
        if batch % n == 0:
            return n
    return 1


def _linear_attention(q, k, v, lf, lb, gate, batch, lp, *, npair, vp, dv):
    gc = LA_GROUP * CHUNK
    nsub = _sub_blocks(lp, gc)
    rows = nsub * gc
    nblk = lp // rows
    nc = lp // CHUNK
    wk = npair * LANES
    wv = npair * vp
    q3 = q.reshape(batch, lp, wk)
    k3 = k.reshape(batch, lp, wk)
    v3 = v.reshape(batch, lp, wv)
    g3 = gate.reshape(batch, lp, wv)
    lf3 = lf.reshape(batch, lp, wk)
    lb3 = lb.reshape(batch, lp, wk)
    nseq = _seqs_per_step(batch)
    fwd = lambda b, j: (b, j, 0)
    rev = lambda b, j: (b, nblk - 1 - j, 0)
    sblk = (nseq, nsub * LA_GROUP, npair, vp, LANES)
    kblk = (nseq, rows, wk)
    vblk = (nseq, rows, wv)

    sb = pl.pallas_call(
        functools.partial(_la_bwd_kernel, npair=npair, vp=vp, nsub=nsub, nseq=nseq),
        grid=(batch // nseq, nblk),
        in_specs=[pl.BlockSpec(kblk, rev), pl.BlockSpec(vblk, rev), pl.BlockSpec(kblk, rev)],
        out_specs=pl.BlockSpec(sblk, lambda b, j: (b, nblk - 1 - j, 0, 0, 0)),
        out_shape=jax.ShapeDtypeStruct((batch, nc, npair, vp, LANES), BF16),
        scratch_shapes=[pltpu.VMEM((nseq, npair, vp, LANES), F32)],
        compiler_params=_params("parallel", "arbitrary"),
        name="la_bwd_states",
    )(k3, v3, lb3)

    o = pl.pallas_call(
        functools.partial(_la_fwd_kernel, npair=npair, vp=vp, dv=dv, nsub=nsub, nseq=nseq),
        grid=(batch // nseq, nblk),
        in_specs=[pl.BlockSpec(kblk, fwd), pl.BlockSpec(kblk, fwd), pl.BlockSpec(vblk, fwd),
                  pl.BlockSpec(kblk, fwd), pl.BlockSpec(kblk, fwd),
                  pl.BlockSpec(sblk, lambda b, j: (b, j, 0, 0, 0)),
                  pl.BlockSpec(vblk, fwd),
                  pl.BlockSpec((wv, wv), lambda b, j: (0, 0))],
        out_specs=pl.BlockSpec(vblk, fwd),
        out_shape=jax.ShapeDtypeStruct((batch, lp, wv), BF16),
        scratch_shapes=[pltpu.VMEM((nseq, npair, vp, LANES), F32)],
        compiler_params=_params("parallel", "arbitrary"),
        name="la_fwd",
    )(q3, k3, v3, lf3, lb3, sb, g3, _head_block_ones(wv, vp // 2))
    return o.reshape(batch * lp, wv)


RET_CHUNK = 256


def _ret_decay_terms(lg_ref):
    c = RET_CHUNK
    lg = lg_ref[...]
    i = lax.broadcasted_iota(jnp.int32, (c, 1), 0).astype(F32)
    return dict(q_f=jnp.exp((i + 1.0) * lg), q_b=jnp.exp((float(c) - i) * lg),
                k_f=jnp.exp((float(c - 1) - i) * lg), k_b=jnp.exp(i * lg), dec=jnp.exp(float(c) * lg))


def _ret_bwd_kernel(k_all, v_all, lg_ref, sb_all, st_all, *, npair, nsub, nseq):
    c = RET_CHUNK
    tn = (((0,), (0,)), ((), ()))

    @pl.when(pl.program_id(1) == 0)
    def _():
        st_all[...] = jnp.zeros_like(st_all)

    _, _, bd = _head_masks(npair, LANES)
    t = _ret_decay_terms(lg_ref)

    def sub_block(it, carry):
        for s in range(nseq):
            one_seq(k_all.at[s], v_all.at[s], sb_all.at[s], st_all.at[s], nsub - 1 - it)
        return carry

    def one_seq(k_ref, v_ref, sb_ref, st_ref, sub):
        base = pl.multiple_of(sub * c, c)
        kb = (k_ref[pl.ds(base, c), :].astype(F32) * t["k_b"]).astype(BF16)
        vv = v_ref[pl.ds(base, c), :]
        for p in range(npair):
            ks = slice(p * LANES, (p + 1) * LANES)
            st = st_ref[p]
            sb_ref[sub, p] = st.astype(BF16)
            upd = lax.dot_general(vv[:, ks], kb[:, ks], tn, preferred_element_type=F32)
            st_ref[p] = st * t["dec"][:, ks] + jnp.where(bd, upd, 0.0)

    lax.fori_loop(0, nsub, sub_block, 0)


def _ret_fwd_kernel(q_all, k_all, v_all, lg_ref, dmask_ref, sb_all, gate_all, bdv_ref, o_all, st_all,
                    *, npair, nsub, dv, nseq):
    c = RET_CHUNK
    nt = (((1,), (1,)), ((), ()))
    tn = (((0,), (0,)), ((), ()))

    @pl.when(pl.program_id(1) == 0)
    def _():
        st_all[...] = jnp.zeros_like(st_all)

    km, vm, bd = _head_masks(npair, LANES)
    t = _ret_decay_terms(lg_ref)

    def sub_block(sub, carry):
        for s in range(nseq):
            one_seq(q_all.at[s], k_all.at[s], v_all.at[s], sb_all.at[s], gate_all.at[s], o_all.at[s], st_all.at[s], sub)
        return carry

    def one_seq(q_ref, k_ref, v_ref, sb_ref, gate_ref, o_ref, st_ref, sub):
        base = pl.multiple_of(sub * c, c)
        qb16 = q_ref[pl.ds(base, c), :]
        kb16 = k_ref[pl.ds(base, c), :]
        vv = v_ref[pl.ds(base, c), :]
        qq = qb16.astype(F32)
        kk = kb16.astype(F32)
        q2 = jnp.concatenate([qq * t["q_f"], qq * t["q_b"]], axis=0).astype(BF16)
        kf2 = (kk * t["k_f"]).astype(BF16)
        zero = jnp.zeros((c, LANES), BF16)
        pair_out = []
        for p in range(npair):
            ks = slice(p * LANES, (p + 1) * LANES)
            kp = kb16[:, ks]
            vp_ = vv[:, ks]
            k2 = jnp.concatenate([jnp.where(km[0], kp, zero), jnp.where(km[1], kp, zero)], axis=0)
            v2 = jnp.concatenate([jnp.where(vm[0], vp_, zero), jnp.where(vm[1], vp_, zero)], axis=0)
            s = lax.dot_general(qb16[:, ks], k2, nt, preferred_element_type=F32)
            o = jnp.dot((s * dmask_ref[p]).astype(BF16), v2, preferred_element_type=F32)
            st = st_ref[p]
            o = o + lax.dot_general(q2[0:c, ks], st.astype(BF16), nt, preferred_element_type=F32)
            o = o + lax.dot_general(q2[c:2 * c, ks], sb_ref[sub, p], nt, preferred_element_type=F32)
            upd = lax.dot_general(vp_, kf2[:, ks], tn, preferred_element_type=F32)
            st_ref[p] = st * t["dec"][:, ks] + jnp.where(bd, upd, 0.0)
            pair_out.append(o)
        o = jnp.concatenate(pair_out, axis=1)
        o_ref[pl.ds(base, c), :] = _head_rms_gate(o, gate_ref[pl.ds(base, c), :], bdv_ref[...], dv)

    lax.fori_loop(0, nsub, sub_block, 0)


def _retention(q, k, v, gate, batch, lp):
    c = RET_CHUNK
    npair = RET_HEADS // 2
    w = npair * LANES
    nsub = _sub_blocks(lp, c)
    rows = nsub * c
    nblk = lp // rows
    log_gamma = jnp.log1p(-jnp.exp2(-5.0 - jnp.arange(RET_HEADS, dtype=F32)))
    lg = jnp.repeat(log_gamma, RET_DIM)[None]
    dist = jnp.abs(jnp.arange(c)[:, None] - (jnp.arange(2 * c) % c)[None, :]).astype(F32)
    dmask = jnp.exp(dist[None] * jnp.repeat(log_gamma, c).reshape(npair, 1, 2 * c))
    q3, k3, v3, g3 = (a.reshape(batch, lp, w) for a in (q, k, v, gate))
    nseq = _seqs_per_step(batch)
    fwd = lambda b, j: (b, j, 0)
    rev = lambda b, j: (b, nblk - 1 - j, 0)
    sblk = (nseq, nsub, npair, LANES, LANES)
    xblk = (nseq, rows, w)
    lgspec = pl.BlockSpec((1, w), lambda b, j: (0, 0))

    sb = pl.pallas_call(
        functools.partial(_ret_bwd_kernel, npair=npair, nsub=nsub, nseq=nseq),
        grid=(batch // nseq, nblk),
        in_specs=[pl.BlockSpec(xblk, rev), pl.BlockSpec(xblk, rev), lgspec],
        out_specs=pl.BlockSpec(sblk, lambda b, j: (b, nblk - 1 - j, 0, 0, 0)),
        out_shape=jax.ShapeDtypeStruct((batch, lp // c, npair, LANES, LANES), BF16),
        scratch_shapes=[pltpu.VMEM((nseq, npair, LANES, LANES), F32)],
        compiler_params=_params("parallel", "arbitrary"),
        name="ret_bwd_states",
    )(k3, v3, lg)

    o = pl.pallas_call(
        functools.partial(_ret_fwd_kernel, npair=npair, nsub=nsub, dv=RET_DIM, nseq=nseq),
        grid=(batch // nseq, nblk),
        in_specs=[pl.BlockSpec(xblk, fwd), pl.BlockSpec(xblk, fwd), pl.BlockSpec(xblk, fwd), lgspec,
                  pl.BlockSpec((npair, c, 2 * c), lambda b, j: (0, 0, 0)),
                  pl.BlockSpec(sblk, lambda b, j: (b, j, 0, 0, 0)),
                  pl.BlockSpec(xblk, fwd),
                  pl.BlockSpec((w, w), lambda b, j: (0, 0))],
        out_specs=pl.BlockSpec(xblk, fwd),
        out_shape=jax.ShapeDtypeStruct((batch, lp, w), BF16),
        scratch_shapes=[pltpu.VMEM((nseq, npair, LANES, LANES), F32)],
        compiler_params=_params("parallel", "arbitrary"),
        name="ret_fwd",
    )(q3, k3, v3, lg, dmask, sb, g3, _head_block_ones(w, RET_DIM))
    return o.reshape(batch * lp, w)


PACKED = D_MODEL // 2


def _pack_rows(x):
    bits = lax.bitcast_convert_type(x.astype(BF16).astype(F32), jnp.int32)
    return bits[:, :PACKED] | lax.shift_right_logical(bits[:, PACKED:], 16)


def _unpack_rows(p):
    hi = lax.bitcast_convert_type(p & jnp.int32(-65536), F32)
    lo = lax.bitcast_convert_type(lax.shift_left(p, jnp.int32(16)), F32)
    return jnp.concatenate([hi, lo], axis=1)


_RT_E1, _RT_E2, _RT_W1, _RT_W2, _RT_R1, _RT_R2 = range(6)


ROW_PARTS = 2


def _mix_out(rows, ya_ref, yb_ref, yc_ref, h_ref, wglu_ref, wout_ref, g_ref):
    ya = _gelu_tanh(ya_ref[rows, :].astype(F32))
    gl = jnp.dot(ya.astype(BF16), wglu_ref[...], preferred_element_type=F32)
    ya = (ya * _sigmoid(gl)).astype(BF16)
    y = jnp.concatenate([ya, yb_ref[rows, :], yc_ref[rows, :]], axis=1)
    h = h_ref[rows, :] + jnp.dot(y, wout_ref[...], preferred_element_type=F32)
    ms = jnp.mean(h * h, axis=-1, keepdims=True)
    return h, h * lax.rsqrt(ms + NORM_EPS) * g_ref[...]


def _outproj_kernel(ya_ref, yb_ref, yc_ref, h_ref, wglu_ref, wout_ref, g_ref, hout_ref, hn_ref):
    h, hn = _mix_out(slice(None), ya_ref, yb_ref, yc_ref, h_ref, wglu_ref, wout_ref, g_ref)
    hout_ref[...] = h
    hn_ref[...] = hn.astype(BF16)


def _outproj_router_kernel(ya_ref, yb_ref, yc_ref, h_ref, wglu_ref, wout_ref, g_ref, wrh_ref, wrl_ref, before_ref,
                           hout_ref, hn_ref, route_ref, cnt_ref, carry_ref):
    @pl.when(pl.program_id(0) == 0)
    def _():
        carry_ref[...] = jnp.zeros_like(carry_ref)

    tp = h_ref.shape[0] // ROW_PARTS
    total = carry_ref[...]
    for part in range(ROW_PARTS):
        rows = slice(part * tp, (part + 1) * tp)
        h, hn = _mix_out(rows, ya_ref, yb_ref, yc_ref, h_ref, wglu_ref, wout_ref, g_ref)
        hout_ref[rows, :] = h
        hn_ref[rows, :] = _pack_rows(hn)
        hi = hn.astype(BF16)
        lo = (hn - hi.astype(F32)).astype(BF16)
        wrh = wrh_ref[...]
        logits = (jnp.dot(hi, wrh, preferred_element_type=F32) + jnp.dot(lo, wrh, preferred_element_type=F32)
                  + jnp.dot(hi, wrl_ref[...], preferred_element_type=F32))
        lane = lax.broadcasted_iota(jnp.int32, logits.shape, 1)
        neg = jnp.float32(-jnp.inf)
        logits = jnp.where(lane < N_EXPERTS, logits, neg)
        v1 = jnp.max(logits, axis=-1, keepdims=True)
        i1 = jnp.min(jnp.where(logits == v1, lane, LANES), axis=-1, keepdims=True)
        m1 = lane == i1
        l2 = jnp.where(m1, neg, logits)
        v2 = jnp.max(l2, axis=-1, keepdims=True)
        i2 = jnp.min(jnp.where(l2 == v2, lane, LANES), axis=-1, keepdims=True)
        m2 = lane == i2
        e = jnp.exp(v2 - v1)
        w1 = 1.0 / (1.0 + e)
        w2 = e * w1
        chosen = jnp.where(m1 | m2, 1.0, 0.0)
        prefix = jnp.dot(before_ref[...], chosen.astype(BF16), preferred_element_type=F32) + total
        r1 = jnp.sum(jnp.where(m1, prefix, 0.0), axis=-1, keepdims=True)
        r2 = jnp.sum(jnp.where(m2, prefix, 0.0), axis=-1, keepdims=True)
        total = total + jnp.sum(chosen, axis=0, keepdims=True)
        rec = jnp.zeros_like(logits)
        for ln, val in ((_RT_E1, i1.astype(F32)), (_RT_E2, i2.astype(F32)), (_RT_W1, w1), (_RT_W2, w2),
                        (_RT_R1, r1), (_RT_R2, r2)):
            rec = jnp.where(lane == ln, val, rec)
        route_ref[rows, :] = rec
    carry_ref[...] = total
    cnt_ref[...] = total


def _outproj(ya, yb, yc, h, wglu, wout, gamma, wr=None):
    r = h.shape[0]
    tm = ROW_TILE
    with_router = wr is not None

    def rows(wd):
        return pl.BlockSpec((tm, wd), lambda i: (i, 0))

    in_specs = [rows(256), rows(384), rows(512), rows(D_MODEL), _const_spec((256, 256)),
                _const_spec((1152, D_MODEL)), _const_spec((1, D_MODEL))]
    args = [ya, yb, yc, h, wglu, wout, gamma]
    if not with_router:
        return pl.pallas_call(
            _outproj_kernel,
            grid=(r // tm,),
            in_specs=in_specs, out_specs=[rows(D_MODEL), rows(D_MODEL)],
            out_shape=[jax.ShapeDtypeStruct((r, D_MODEL), F32), jax.ShapeDtypeStruct((r, D_MODEL), BF16)],
            input_output_aliases={3: 0},
            compiler_params=_params("parallel"),
            name="outproj",
        )(*args)
    tp = tm // ROW_PARTS
    earlier = (lax.broadcasted_iota(jnp.int32, (tp, tp), 1) < lax.broadcasted_iota(jnp.int32, (tp, tp), 0)).astype(BF16)
    return pl.pallas_call(
        _outproj_router_kernel,
        grid=(r // tm,),
        in_specs=in_specs + [_const_spec((D_MODEL, LANES)), _const_spec((D_MODEL, LANES)), _const_spec((tp, tp))],
        out_specs=[rows(D_MODEL), rows(PACKED), rows(LANES), _const_spec((1, LANES))],
        out_shape=[jax.ShapeDtypeStruct((r, D_MODEL), F32), jax.ShapeDtypeStruct((r, PACKED), jnp.int32),
                   jax.ShapeDtypeStruct((r, LANES), F32), jax.ShapeDtypeStruct((1, LANES), F32)],
        scratch_shapes=[pltpu.VMEM((1, LANES), F32)],
        input_output_aliases={3: 0},
        compiler_params=_params("arbitrary"),
        name="outproj_router",
    )(*args, wr[0], wr[1], earlier)


FF_TILE = 2816


def _ffn_kernel(hn_ref, h_ref, wg_ref, wu_ref, wd_ref, o_ref, acc_ref):
    j = pl.program_id(1)
    hn = hn_ref[...]
    a = _silu(jnp.dot(hn, wg_ref[...], preferred_element_type=F32)) * jnp.dot(hn, wu_ref[...], preferred_element_type=F32)
    y = jnp.dot(a.astype(BF16), wd_ref[...], preferred_element_type=F32)

    @pl.when(j == 0)
    def _():
        acc_ref[...] = h_ref[...] + y

    @pl.when(j != 0)
    def _():
        acc_ref[...] += y

    @pl.when(j == pl.num_programs(1) - 1)
    def _():
        o_ref[...] = acc_ref[...]


def _ffn(hn, h, wg, wu, wd):
    r = h.shape[0]
    tm = ROW_TILE
    nf = D_FF // FF_TILE
    return pl.pallas_call(
        _ffn_kernel,
        grid=(r // tm, nf),
        in_specs=[pl.BlockSpec((tm, D_MODEL), lambda i, j: (i, 0)),
                  pl.BlockSpec((tm, D_MODEL), lambda i, j: (i, 0)),
                  pl.BlockSpec((D_MODEL, FF_TILE), lambda i, j: (0, j)),
                  pl.BlockSpec((D_MODEL, FF_TILE), lambda i, j: (0, j)),
                  pl.BlockSpec((FF_TILE, D_MODEL), lambda i, j: (j, 0))],
        out_specs=pl.BlockSpec((tm, D_MODEL), lambda i, j: (i, 0)),
        out_shape=jax.ShapeDtypeStruct((r, D_MODEL), F32),
        scratch_shapes=[pltpu.VMEM((tm, D_MODEL), F32)],
        input_output_aliases={1: 0},
        compiler_params=_params("parallel", "arbitrary"),
        name="ffn",
    )(hn, h, wg, wu, wd)


MOE_BLOCK = 512
GATHER_TILE = 256


def _route_meta(route, cnt, r):
    bm = MOE_BLOCK
    nb = 2 * r // bm + N_EXPERTS
    e1 = route[:, _RT_E1].astype(jnp.int32)
    e2 = route[:, _RT_E2].astype(jnp.int32)
    counts = cnt[0, :N_EXPERTS].astype(jnp.int32)
    padded = ((counts + bm - 1) // bm) * bm
    ends = jnp.cumsum(padded)
    starts = ends - padded
    pos1 = starts[e1] + route[:, _RT_R1].astype(jnp.int32)
    pos2 = starts[e2] + route[:, _RT_R2].astype(jnp.int32)
    n_used = (ends[-1] // bm).astype(jnp.int32)
    blk = jnp.arange(nb, dtype=jnp.int32)
    blk = jnp.minimum(blk, n_used - 1)
    block_expert = jnp.sum((blk[:, None] * bm >= ends[None, :]).astype(jnp.int32), axis=1)
    block_expert = jnp.minimum(block_expert, N_EXPERTS - 1).astype(jnp.int32)
    return pos1, pos2, block_expert, n_used.reshape(1), nb


def _dispatch_kernel(p1_ref, p2_ref, hn_ref, xs_in_ref, xs_ref, stage, sems):
    del xs_in_ref
    n = hn_ref.shape[0]
    s = pl.program_id(0)
    last = pl.num_programs(0) - 1
    slot = lax.rem(s, 2)

    def drain(sl):
        def body(r, c):
            row = pltpu.make_async_copy(stage.at[sl, pl.ds(0, 1), :], xs_ref.at[pl.ds(0, 1), :], sems.at[sl])
            row.wait()
            row.wait()
            return c
        lax.fori_loop(0, n, body, 0, unroll=8)

    @pl.when(s >= 2)
    def _():
        drain(slot)

    stage[slot] = hn_ref[...]
    for r in range(n):
        src = stage.at[slot, pl.ds(r, 1), :]
        pltpu.make_async_copy(src, xs_ref.at[pl.ds(p1_ref[0, 0, r], 1), :], sems.at[slot]).start(priority=0)
        pltpu.make_async_copy(src, xs_ref.at[pl.ds(p2_ref[0, 0, r], 1), :], sems.at[slot]).start(priority=1)

    @pl.when(s == last)
    def _():
        @pl.when(s >= 1)
        def _():
            drain(1 - slot)
        drain(slot)


def _dispatch(hn, pos1, pos2, nb):
    r = hn.shape[0]
    t = GATHER_TILE
    p = nb * MOE_BLOCK
    idx = pl.BlockSpec((1, 1, t), lambda i: (i, 0, 0), memory_space=pltpu.SMEM)
    return pl.pallas_call(
        _dispatch_kernel,
        grid=(r // t,),
        in_specs=[idx, idx, pl.BlockSpec((t, PACKED), lambda i: (i, 0)), pl.BlockSpec(memory_space=pl.ANY)],
        out_specs=pl.BlockSpec(memory_space=pl.ANY),
        out_shape=jax.ShapeDtypeStruct((p, PACKED), jnp.int32),
        scratch_shapes=[pltpu.VMEM((2, t, PACKED), jnp.int32), pltpu.SemaphoreType.DMA((2,))],
        input_output_aliases={3: 0},
        compiler_params=_params("arbitrary"),
        name="moe_dispatch",
    )(pos1.reshape(r // t, 1, t), pos2.reshape(r // t, 1, t), hn, jnp.zeros((p, PACKED), jnp.int32))


def _moe_ffn_kernel(be_ref, nu_ref, x_ref, wg_ref, wu_ref, wd_ref, o_ref, acc_ref):
    del be_ref
    b = pl.program_id(0)
    j = pl.program_id(1)
    last = j == pl.num_programs(1) - 1
    used = b < nu_ref[0]

    @pl.when(used)
    def _():
        x = _unpack_rows(x_ref[...]).astype(BF16)
        a = _silu(jnp.dot(x, wg_ref[...], preferred_element_type=F32)) * jnp.dot(x, wu_ref[...], preferred_element_type=F32)
        y = jnp.dot(a.astype(BF16), wd_ref[...], preferred_element_type=F32)

        @pl.when(j == 0)
        def _():
            acc_ref[...] = y

        @pl.when(j != 0)
        def _():
            acc_ref[...] += y

        @pl.when(last)
        def _():
            o_ref[...] = _pack_rows(acc_ref[...])

    @pl.when(jnp.logical_not(used) & last)
    def _():
        o_ref[...] = jnp.zeros_like(o_ref)


def _moe_ffn(xs, block_expert, n_used, wg, wu, wd):
    p = xs.shape[0]
    bm = MOE_BLOCK
    nf = D_FF // FF_TILE
    grid_spec = pltpu.PrefetchScalarGridSpec(
        num_scalar_prefetch=2,
        grid=(p // bm, nf),
        in_specs=[pl.BlockSpec((bm, PACKED), lambda b, j, be, nu: (b, 0)),
                  pl.BlockSpec((None, D_MODEL, FF_TILE), lambda b, j, be, nu: (be[b], 0, j)),
                  pl.BlockSpec((None, D_MODEL, FF_TILE), lambda b, j, be, nu: (be[b], 0, j)),
                  pl.BlockSpec((None, FF_TILE, D_MODEL), lambda b, j, be, nu: (be[b], j, 0))],
        out_specs=pl.BlockSpec((bm, PACKED), lambda b, j, be, nu: (b, 0)),
        scratch_shapes=[pltpu.VMEM((bm, D_MODEL), F32)])
    return pl.pallas_call(
        _moe_ffn_kernel,
        grid_spec=grid_spec,
        out_shape=jax.ShapeDtypeStruct((p, PACKED), jnp.int32),
        compiler_params=_params("arbitrary", "arbitrary"),
        name="moe_ffn",
    )(block_expert, n_used, xs, wg, wu, wd)


def _combine_kernel(p1_ref, p2_ref, p1n_ref, p2n_ref, route_ref, h_ref, g_ref, ys_ref, o_ref, buf1, buf2, sems,
                    *, final):
    n = h_ref.shape[0]
    s = pl.program_id(0)
    last = pl.num_programs(0) - 1
    slot = lax.rem(s, 2)

    def gather(pa, pb, sl):
        for r in range(n):
            pltpu.make_async_copy(ys_ref.at[pl.ds(pa[0, 0, r], 1), :], buf1.at[sl, pl.ds(r, 1), :],
                                  sems.at[sl]).start(priority=0)
            pltpu.make_async_copy(ys_ref.at[pl.ds(pb[0, 0, r], 1), :], buf2.at[sl, pl.ds(r, 1), :],
                                  sems.at[sl]).start(priority=1)

    @pl.when(s == 0)
    def _():
        gather(p1_ref, p2_ref, 0)

    @pl.when(s < last)
    def _():
        gather(p1n_ref, p2n_ref, 1 - slot)

    def drain(r, c):
        row = pltpu.make_async_copy(ys_ref.at[pl.ds(0, 1), :], buf1.at[slot, pl.ds(0, 1), :], sems.at[slot])
        row.wait()
        row.wait()
        return c

    lax.fori_loop(0, n, drain, 0, unroll=8)
    rt = route_ref[...]
    lane = lax.broadcasted_iota(jnp.int32, rt.shape, 1)
    w1 = jnp.sum(jnp.where(lane == _RT_W1, rt, 0.0), axis=-1, keepdims=True)
    w2 = jnp.sum(jnp.where(lane == _RT_W2, rt, 0.0), axis=-1, keepdims=True)
    h = h_ref[...] + w1 * _unpack_rows(buf1[slot]) + w2 * _unpack_rows(buf2[slot])
    if final:
        ms = jnp.mean(h * h, axis=-1, keepdims=True)
        h = h * lax.rsqrt(ms + NORM_EPS) * g_ref[...]
    o_ref[...] = h


def _combine(ys, pos1, pos2, route, h, gamma, batch, lp, final):
    r = h.shape[0]
    t = GATHER_TILE
    p1 = pos1.reshape(r // t, 1, t)
    p2 = pos2.reshape(r // t, 1, t)
    scratch = [pltpu.VMEM((2, t, PACKED), jnp.int32), pltpu.VMEM((2, t, PACKED), jnp.int32),
               pltpu.SemaphoreType.DMA((2,))]
    if final:
        per_seq = lp // t
        skip = FRONT // t
        live = per_seq - skip
        steps = batch * live
        rb = lambda s: (s // live) * per_seq + skip + s % live
        out_specs = pl.BlockSpec((None, t, D_MODEL), lambda s: (s // live, s % live, 0))
        out_shape = jax.ShapeDtypeStruct((batch, lp - FRONT, D_MODEL), F32)
    else:
        steps = r // t
        rb = lambda s: s
        out_specs = pl.BlockSpec((t, D_MODEL), lambda s: (s, 0))
        out_shape = jax.ShapeDtypeStruct((r, D_MODEL), F32)
    nxt = lambda s: rb(jnp.minimum(s + 1, steps - 1))
    idx = pl.BlockSpec((1, 1, t), lambda s: (rb(s), 0, 0), memory_space=pltpu.SMEM)
    idx_next = pl.BlockSpec((1, 1, t), lambda s: (nxt(s), 0, 0), memory_space=pltpu.SMEM)
    in_specs = [idx, idx, idx_next, idx_next,
                pl.BlockSpec((t, LANES), lambda s: (rb(s), 0)), pl.BlockSpec((t, D_MODEL), lambda s: (rb(s), 0)),
                pl.BlockSpec((1, D_MODEL), lambda s: (0, 0)), pl.BlockSpec(memory_space=pl.ANY)]
    return pl.pallas_call(
        functools.partial(_combine_kernel, final=final),
        grid=(steps,), in_specs=in_specs, out_specs=out_specs, out_shape=out_shape, scratch_shapes=scratch,
        compiler_params=_params("arbitrary"),
        name="moe_combine_final" if final else "moe_combine",
    )(p1, p2, p1, p2, route, h, gamma, ys)


def _final_norm_kernel(h_ref, g_ref, o_ref):
    h = h_ref[...]
    ms = jnp.mean(h * h, axis=-1, keepdims=True)
    o_ref[...] = h * lax.rsqrt(ms + NORM_EPS) * g_ref[...]


def _final_norm(h, gamma, batch, lp):
    t = GATHER_TILE
    per_seq = lp // t
    skip = FRONT // t
    return pl.pallas_call(
        _final_norm_kernel,
        grid=(batch, per_seq - skip),
        in_specs=[pl.BlockSpec((t, D_MODEL), lambda b, i: (b * per_seq + skip + i, 0)),
                  pl.BlockSpec((1, D_MODEL), lambda b, i: (0, 0))],
        out_specs=pl.BlockSpec((None, t, D_MODEL), lambda b, i: (b, i, 0)),
        out_shape=jax.ShapeDtypeStruct((batch, lp - FRONT, D_MODEL), F32),
        compiler_params=_params("parallel", "parallel"),
        name="final_norm",
    )(h, gamma)


def _rope_tables(lp):
    half = RET_DIM // 2
    pos = jnp.arange(lp, dtype=F32) - float(PAD)
    inv = ROPE_BASE ** (-jnp.arange(half, dtype=F32) / half)
    ang = pos[:, None] * inv[None, :]
    cos = jnp.tile(jnp.cos(ang), (1, 2 * RET_HEADS))
    sin = jnp.tile(jnp.sin(ang), (1, 2 * RET_HEADS))
    return cos, sin


def _prep_layer(li, p):
    w_out = p["w_out"][li].astype(F32)
    wc = w_out[640:1024].reshape(GLA_HEADS, GLA_DV, D_MODEL)
    wc = jnp.pad(wc, ((0, 0), (0, GLA_DV_PAD - GLA_DV), (0, 0))).reshape(GLA_HEADS * GLA_DV_PAD, D_MODEL)
    wgate, bgate = _pack_gate(p["gla_w_gate_f"][li], p["gla_b_gate_f"][li], p["gla_w_gate_b"][li], p["gla_b_gate_b"][li])
    return dict(
        norm_mix=p["norm_mix"][li].astype(F32)[None],
        w_in=_pack_w_in(p["w_in"][li].astype(F32)),
        wgate=wgate, bgate=bgate,
        s5=_s5_tables(p["s5_lambda_re"][li], p["s5_lambda_im"][li], p["s5_log_dt"][li], p["s5_b_re"][li],
                      p["s5_b_im"][li], p["s5_c_re"][li], p["s5_c_im"][li], p["s5_d"][li]),
        w_glu=p["s5_w_glu"][li].astype(BF16),
        w_out=jnp.concatenate([w_out[0:640], wc], axis=0).astype(BF16),
        norm_ffn=p["norm_ffn"][li].astype(F32)[None],
    )


def _trunk(x, meta_tokens, layers, ffn, moe, norm_final, depth):
    batch, seq, _ = x.shape
    lp = seq + FRONT
    r = batch * lp
    meta = jnp.broadcast_to(meta_tokens.astype(F32)[None], (batch, N_META, D_MODEL))
    h = jnp.concatenate([jnp.zeros((batch, PAD, D_MODEL), F32), meta, x.astype(F32)], axis=1).reshape(r, D_MODEL)
    cos, sin = _rope_tables(lp)
    cos = jnp.tile(cos, (batch, 1))
    sin = jnp.tile(sin, (batch, 1))
    gamma_final = norm_final.astype(F32)[None]
    for li in range(depth):
        lw = layers[li]
        u, rq, rk, rv, rg, gq, gk, gv, gg, lf, lb = _inproj(
            h, lw["norm_mix"], lw["w_in"], cos, sin, lw["wgate"], lw["bgate"], batch, lp)
        ya = _s5_mixer(u, lw["s5"], batch, lp)
        yb = _retention(rq, rk, rv, rg, batch, lp)
        yc = _linear_attention(gq, gk, gv, lf, lb, gg, batch, lp, npair=GLA_HEADS // 2, vp=2 * GLA_DV_PAD, dv=GLA_DV)
        j = li // 2
        last = li == depth - 1
        if li % 2 == 0:
            h, hn = _outproj(ya, yb, yc, h, lw["w_glu"], lw["w_out"], lw["norm_ffn"])
            h = _ffn(hn, h, ffn["wg"][j], ffn["wu"][j], ffn["wd"][j])
        else:
            h, hn, route, cnt = _outproj(ya, yb, yc, h, lw["w_glu"], lw["w_out"], lw["norm_ffn"], moe["wr"][j])
            pos1, pos2, block_expert, n_used, nb = _route_meta(route, cnt, r)
            xs = _dispatch(hn, pos1, pos2, nb)
            ys = _moe_ffn(xs, block_expert, n_used, moe["wg"][j], moe["wu"][j], moe["wd"][j])
            h = _combine(ys, pos1, pos2, route, h, gamma_final, batch, lp, final=last)
            if last:
                return h
    return _final_norm(h, gamma_final, batch, lp)


def kernel(x_prompt, x_sample, meta_tokens, norm_mix, w_in, s5_lambda_re, s5_lambda_im, s5_log_dt, s5_b_re, s5_b_im, s5_c_re, s5_c_im, s5_d, s5_w_glu, gla_w_gate_f, gla_b_gate_f, gla_w_gate_b, gla_b_gate_b, w_out, norm_ffn, ffn_w_gate, ffn_w_up, ffn_w_down, router_w, moe_w_gate, moe_w_up, moe_w_down, norm_final):
    depth = w_in.shape[0]
    p = dict(norm_mix=norm_mix, w_in=w_in, s5_lambda_re=s5_lambda_re, s5_lambda_im=s5_lambda_im,
             s5_log_dt=s5_log_dt, s5_b_re=s5_b_re, s5_b_im=s5_b_im, s5_c_re=s5_c_re, s5_c_im=s5_c_im, s5_d=s5_d,
             s5_w_glu=s5_w_glu, gla_w_gate_f=gla_w_gate_f, gla_b_gate_f=gla_b_gate_f, gla_w_gate_b=gla_w_gate_b,
             gla_b_gate_b=gla_b_gate_b, w_out=w_out, norm_ffn=norm_ffn)
    layers = [_prep_layer(li, p) for li in range(depth)]
    ffn = dict(wg=ffn_w_gate.astype(BF16), wu=ffn_w_up.astype(BF16), wd=ffn_w_down.astype(BF16))
    wr = jnp.pad(router_w.astype(F32), ((0, 0), (0, 0), (0, LANES - N_EXPERTS)))
    wr_hi = wr.astype(BF16)
    wr_lo = (wr - wr_hi.astype(F32)).astype(BF16)
    wr = [(wr_hi[j], wr_lo[j]) for j in range(wr.shape[0])]
    moe = dict(wr=wr, wg=moe_w_gate.astype(BF16), wu=moe_w_up.astype(BF16), wd=moe_w_down.astype(BF16))
    y_prompt = _trunk(x_prompt, meta_tokens, layers, ffn, moe, norm_final, depth)
    y_sample = _trunk(x_sample, meta_tokens, layers, ffn, moe, norm_final, depth)
    return (y_prompt, y_sample)
```
